```python
import math
import jax, jax.numpy as jnp
from jax import lax
import numpy as np

D_MODEL = 1024
BATCH = 32
SEQ = 2048
DEPTH = 4

HEAD_DIM = 64
D_SSM = D_MODEL
SSM_HEADS = D_SSM // HEAD_DIM
SSM_GROUPS = 4
SSM_STATE = 128
CONV_WIDTH = 4
CHUNK = 128
D_ATT = D_MODEL
ATT_HEADS = D_ATT // HEAD_DIM
Q_BLOCK = 128
D_MIX = D_SSM + D_ATT
D_CONV = D_SSM + 2 * SSM_GROUPS * SSM_STATE
D_IN_PROJ = D_SSM + D_CONV + SSM_HEADS + 3 * D_ATT
D_FF = 4 * D_MODEL
EPS = 1e-5
DT_MIN = 1e-3
DT_MAX = 1e-1

kernel_name = "hybrid_ssd_stickbreaking_trunk"


def rmsnorm(x, g):
    xf = x.astype(jnp.float32)
    r = lax.rsqrt(jnp.mean(xf * xf, axis=-1, keepdims=True) + EPS)
    return (xf * r * g.astype(jnp.float32)).astype(x.dtype)


def causal_depthwise_conv(u, w, b):
    c = u.shape[-1]
    out = lax.conv_general_dilated(
        u, w[:, None, :].astype(u.dtype), window_strides=(1,),
        padding=[(CONV_WIDTH - 1, 0)],
        dimension_numbers=("NWC", "WIO", "NWC"), feature_group_count=c)
    return out + b


def ssd_chunked(x, dt, a_neg, bm, cm):
    b, L = x.shape[:2]
    nc = L // CHUNK
    r = SSM_HEADS // SSM_GROUPS
    xdt = (x * dt[..., None]).reshape(b, nc, CHUNK, SSM_GROUPS, r, HEAD_DIM)
    a = (dt * a_neg).reshape(b, nc, CHUNK, SSM_GROUPS, r)
    a = jnp.moveaxis(a, 2, -1)
    a_cum = jnp.cumsum(a, axis=-1)
    bc = bm.reshape(b, nc, CHUNK, SSM_GROUPS, SSM_STATE)
    cc = cm.reshape(b, nc, CHUNK, SSM_GROUPS, SSM_STATE)
    causal = jnp.tril(jnp.ones((CHUNK, CHUNK), dtype=bool))
    seg = a_cum[..., :, None] - a_cum[..., None, :]
    lmat = jnp.exp(jnp.where(causal, seg, -jnp.inf))
    cb = jnp.einsum("bclgn,bcsgn->bcgls", cc, bc)
    w = cb[:, :, :, None] * lmat
    y_diag = jnp.einsum("bcgrls,bcsgrp->bclgrp", w, xdt)
    decay_states = jnp.exp(a_cum[..., -1:] - a_cum)
    states = jnp.einsum("bclgn,bcgrl,bclgrp->bcgrpn", bc, decay_states, xdt)
    chunk_decay = jnp.exp(a_cum[..., -1])

    def step(h, inp):
        s_c, d_c = inp
        h_new = h * d_c[..., None, None] + s_c
        return h_new, h

    h0 = jnp.zeros_like(states[:, 0])
    _, prev_states = lax.scan(step, h0, (jnp.moveaxis(states, 1, 0), jnp.moveaxis(chunk_decay, 1, 0)))
    prev_states = jnp.moveaxis(prev_states, 0, 1)
    state_decay = jnp.moveaxis(jnp.exp(a_cum), -1, 2)[..., None]
    y_off = jnp.einsum("bclgn,bcgrpn->bclgrp", cc, prev_states) * state_decay
    return (y_diag + y_off).reshape(b, L, SSM_HEADS, HEAD_DIM)


def stick_breaking_attention(q, k, v):
    L = q.shape[1]
    scale = HEAD_DIM ** -0.5
    outs = []
    for i in range(L // Q_BLOCK):
        q0 = i * Q_BLOCK
        kv_len = q0 + Q_BLOCK
        qb = q[:, q0:kv_len]
        kb = k[:, :kv_len]
        vb = v[:, :kv_len]
        logits = jnp.einsum("bthd,bshd->bhts", qb, kb).astype(jnp.float32) * scale
        t_pos = q0 + jnp.arange(Q_BLOCK)
        s_pos = jnp.arange(kv_len)
        mask = s_pos[None, :] < t_pos[:, None]
        log_beta = jax.nn.log_sigmoid(logits)
        log_1m_beta = jnp.where(mask, jax.nn.log_sigmoid(-logits), 0.0)
        suffix = lax.cumsum(log_1m_beta, axis=log_1m_beta.ndim - 1, reverse=True) - log_1m_beta
        att = jnp.where(mask, jnp.exp(log_beta + suffix), 0.0)
        outs.append(jnp.einsum("bhts,bshd->bthd", att.astype(vb.dtype), vb))
    return jnp.concatenate(outs, axis=1)


def hybrid_layer(x, norm_mix_g, w_in, conv_w, conv_b, dt_bias, a_log, d_skip,
                 ssd_norm_g, att_norm_g, w_out, norm_mlp_g, w_up, w_down):
    b, L, _ = x.shape
    h = rmsnorm(x, norm_mix_g)
    proj = h @ w_in
    splits = [D_SSM, D_SSM + D_CONV, D_SSM + D_CONV + SSM_HEADS,
              D_SSM + D_CONV + SSM_HEADS + D_ATT,
              D_SSM + D_CONV + SSM_HEADS + 2 * D_ATT]
    z, xbc, dt_raw, q, k, v = jnp.split(proj, splits, axis=-1)
    xbc = jax.nn.silu(causal_depthwise_conv(xbc, conv_w, conv_b))
    xs, bm, cm = jnp.split(xbc, [D_SSM, D_SSM + SSM_GROUPS * SSM_STATE], axis=-1)
    dt = jax.nn.softplus(dt_raw.astype(jnp.float32) + dt_bias.astype(jnp.float32))
    a_neg = -jnp.exp(a_log.astype(jnp.float32))
    xs = xs.reshape(b, L, SSM_HEADS, HEAD_DIM)
    y = ssd_chunked(xs, dt, a_neg,
                    bm.reshape(b, L, SSM_GROUPS, SSM_STATE),
                    cm.reshape(b, L, SSM_GROUPS, SSM_STATE))
    y = (y + xs * d_skip[:, None]).reshape(b, L, D_SSM)
    y_ssd = rmsnorm(y * jax.nn.silu(z), ssd_norm_g)
    y_att = stick_breaking_attention(q.reshape(b, L, ATT_HEADS, HEAD_DIM),
                                     k.reshape(b, L, ATT_HEADS, HEAD_DIM),
                                     v.reshape(b, L, ATT_HEADS, HEAD_DIM))
    y_att = rmsnorm(y_att.reshape(b, L, D_ATT), att_norm_g)
    x = x + jnp.concatenate([y_ssd, y_att], axis=-1) @ w_out
    h = rmsnorm(x, norm_mlp_g)
    x = x + jnp.square(jax.nn.relu(h @ w_up)) @ w_down
    return x


def _fwd_setup_inputs(seed: int = 0) -> dict:
    key = jax.random.key(seed)
    ks = jax.random.split(key, 16)
    f32 = jnp.float32
    nrm = lambda k, shape, s: jax.random.normal(k, shape, f32) * s
    x = jax.random.normal(ks[0], (BATCH, SEQ, D_MODEL), f32)
    norm_mix_g = 1.0 + nrm(ks[1], (DEPTH, D_MODEL), 0.02)
    w_in = nrm(ks[2], (DEPTH, D_MODEL, D_IN_PROJ), D_MODEL ** -0.5)
    conv_w = nrm(ks[3], (DEPTH, CONV_WIDTH, D_CONV), CONV_WIDTH ** -0.5)
    conv_b = nrm(ks[4], (DEPTH, D_CONV), 0.02)
    dt0 = jnp.exp(jax.random.uniform(ks[5], (DEPTH, SSM_HEADS), f32,
                                     math.log(DT_MIN), math.log(DT_MAX)))
    dt_bias = dt0 + jnp.log(-jnp.expm1(-dt0))
    a_log = jnp.log(jax.random.uniform(ks[6], (DEPTH, SSM_HEADS), f32, 1.0, 16.0))
    d_skip = 1.0 + nrm(ks[7], (DEPTH, SSM_HEADS), 0.02)
    ssd_norm_g = 1.0 + nrm(ks[8], (DEPTH, D_SSM), 0.02)
    att_norm_g = 1.0 + nrm(ks[9], (DEPTH, D_ATT), 0.02)
    w_out = nrm(ks[10], (DEPTH, D_MIX, D_MODEL), 0.5 * D_MIX ** -0.5)
    norm_mlp_g = 1.0 + nrm(ks[11], (DEPTH, D_MODEL), 0.02)
    w_up = nrm(ks[12], (DEPTH, D_MODEL, D_FF), D_MODEL ** -0.5)
    w_down = nrm(ks[13], (DEPTH, D_FF, D_MODEL), 0.5 * D_FF ** -0.5)
    final_norm_g = 1.0 + nrm(ks[14], (D_MODEL,), 0.02)
    return {"x": x, "norm_mix_g": norm_mix_g, "w_in": w_in, "conv_w": conv_w,
            "conv_b": conv_b, "dt_bias": dt_bias, "a_log": a_log, "d_skip": d_skip,
            "ssd_norm_g": ssd_norm_g, "att_norm_g": att_norm_g, "w_out": w_out,
            "norm_mlp_g": norm_mlp_g, "w_up": w_up, "w_down": w_down,
            "final_norm_g": final_norm_g}


def _fwd_reference(x, norm_mix_g, w_in, conv_w, conv_b, dt_bias, a_log, d_skip,
              ssd_norm_g, att_norm_g, w_out, norm_mlp_g, w_up, w_down, final_norm_g):
    for l in range(DEPTH):
        x = hybrid_layer(x, norm_mix_g[l], w_in[l], conv_w[l], conv_b[l], dt_bias[l],
                         a_log[l], d_skip[l], ssd_norm_g[l], att_norm_g[l], w_out[l],
                         norm_mlp_g[l], w_up[l], w_down[l])
    return rmsnorm(x, final_norm_g)


import jax as _jax
import jax.numpy as _jnp

TWIN_FORMAT = 'train_step'
FWD_PARAMS = ['x', 'norm_mix_g', 'w_in', 'conv_w', 'conv_b', 'dt_bias', 'a_log', 'd_skip', 'ssd_norm_g', 'att_norm_g', 'w_out', 'norm_mlp_g', 'w_up', 'w_down', 'final_norm_g']
TWIN_WEIGHTS = ['norm_mix_g', 'w_in', 'conv_w', 'conv_b', 'dt_bias', 'a_log', 'd_skip', 'ssd_norm_g', 'att_norm_g', 'w_out', 'norm_mlp_g', 'w_up', 'w_down', 'final_norm_g']
TWIN_DIFF_INPUT = 'x'
TWIN_INPUTS = ['x', 'norm_mix_g', 'w_in', 'conv_w', 'conv_b', 'dt_bias', 'a_log', 'd_skip', 'ssd_norm_g', 'att_norm_g', 'w_out', 'norm_mlp_g', 'w_up', 'w_down', 'final_norm_g', 'loss_target', 'm_norm_mix_g', 'm_w_in', 'm_conv_w', 'm_conv_b', 'm_dt_bias', 'm_a_log', 'm_d_skip', 'm_ssd_norm_g', 'm_att_norm_g', 'm_w_out', 'm_norm_mlp_g', 'm_w_up', 'm_w_down', 'm_final_norm_g', 'v_norm_mix_g', 'v_w_in', 'v_conv_w', 'v_conv_b', 'v_dt_bias', 'v_a_log', 'v_d_skip', 'v_ssd_norm_g', 'v_att_norm_g', 'v_w_out', 'v_norm_mlp_g', 'v_w_up', 'v_w_down', 'v_final_norm_g']
TWIN_OUTPUTS = ['loss', 'grad_x', 'grad_norm_mix_g', 'grad_w_in', 'grad_conv_w', 'grad_conv_b', 'grad_dt_bias', 'grad_a_log', 'grad_d_skip', 'grad_ssd_norm_g', 'grad_att_norm_g', 'grad_w_out', 'grad_norm_mlp_g', 'grad_w_up', 'grad_w_down', 'grad_final_norm_g', 'delta_norm_mix_g', 'delta_w_in', 'delta_conv_w', 'delta_conv_b', 'delta_dt_bias', 'delta_a_log', 'delta_d_skip', 'delta_ssd_norm_g', 'delta_att_norm_g', 'delta_w_out', 'delta_norm_mlp_g', 'delta_w_up', 'delta_w_down', 'delta_final_norm_g', 'new_m_norm_mix_g', 'new_m_w_in', 'new_m_conv_w', 'new_m_conv_b', 'new_m_dt_bias', 'new_m_a_log', 'new_m_d_skip', 'new_m_ssd_norm_g', 'new_m_att_norm_g', 'new_m_w_out', 'new_m_norm_mlp_g', 'new_m_w_up', 'new_m_w_down', 'new_m_final_norm_g', 'new_v_norm_mix_g', 'new_v_w_in', 'new_v_conv_w', 'new_v_conv_b', 'new_v_dt_bias', 'new_v_a_log', 'new_v_d_skip', 'new_v_ssd_norm_g', 'new_v_att_norm_g', 'new_v_w_out', 'new_v_norm_mlp_g', 'new_v_w_up', 'new_v_w_down', 'new_v_final_norm_g']
TWIN_LEAF_KINDS = {'loss': 'loss', 'grad_x': 'grad_x', 'grad_norm_mix_g': 'grad_w', 'grad_w_in': 'grad_w', 'grad_conv_w': 'grad_w', 'grad_conv_b': 'grad_w', 'grad_dt_bias': 'grad_w', 'grad_a_log': 'grad_w', 'grad_d_skip': 'grad_w', 'grad_ssd_norm_g': 'grad_w', 'grad_att_norm_g': 'grad_w', 'grad_w_out': 'grad_w', 'grad_norm_mlp_g': 'grad_w', 'grad_w_up': 'grad_w', 'grad_w_down': 'grad_w', 'grad_final_norm_g': 'grad_w', 'delta_norm_mix_g': 'delta_w', 'delta_w_in': 'delta_w', 'delta_conv_w': 'delta_w', 'delta_conv_b': 'delta_w', 'delta_dt_bias': 'delta_w', 'delta_a_log': 'delta_w', 'delta_d_skip': 'delta_w', 'delta_ssd_norm_g': 'delta_w', 'delta_att_norm_g': 'delta_w', 'delta_w_out': 'delta_w', 'delta_norm_mlp_g': 'delta_w', 'delta_w_up': 'delta_w', 'delta_w_down': 'delta_w', 'delta_final_norm_g': 'delta_w', 'new_m_norm_mix_g': 'new_m', 'new_m_w_in': 'new_m', 'new_m_conv_w': 'new_m', 'new_m_conv_b': 'new_m', 'new_m_dt_bias': 'new_m', 'new_m_a_log': 'new_m', 'new_m_d_skip': 'new_m', 'new_m_ssd_norm_g': 'new_m', 'new_m_att_norm_g': 'new_m', 'new_m_w_out': 'new_m', 'new_m_norm_mlp_g': 'new_m', 'new_m_w_up': 'new_m', 'new_m_w_down': 'new_m', 'new_m_final_norm_g': 'new_m', 'new_v_norm_mix_g': 'new_v', 'new_v_w_in': 'new_v', 'new_v_conv_w': 'new_v', 'new_v_conv_b': 'new_v', 'new_v_dt_bias': 'new_v', 'new_v_a_log': 'new_v', 'new_v_d_skip': 'new_v', 'new_v_ssd_norm_g': 'new_v', 'new_v_att_norm_g': 'new_v', 'new_v_w_out': 'new_v', 'new_v_norm_mlp_g': 'new_v', 'new_v_w_up': 'new_v', 'new_v_w_down': 'new_v', 'new_v_final_norm_g': 'new_v'}


def _forward(args):
    return _fwd_reference(*[args[k] for k in FWD_PARAMS])


def _output_shape():
    out = _jax.eval_shape(lambda: _forward(_fwd_setup_inputs(0)))
    return out.shape, out.dtype

N_MICROBATCH = 1
ADAM_LR = 0.001
ADAM_B1 = 0.9
ADAM_B2 = 0.999
ADAM_EPS = 1e-08
ADAM_WD = 0.01
ADAM_STEP = 10
PER_EXAMPLE_BATCH_AXIS = {'x': 0, 'loss_target': 0}
SHARED_INPUTS = []
_WEIGHT_DTYPES = {'norm_mix_g': _jnp.float32, 'w_in': _jnp.float32, 'conv_w': _jnp.float32, 'conv_b': _jnp.float32, 'dt_bias': _jnp.float32, 'a_log': _jnp.float32, 'd_skip': _jnp.float32, 'ssd_norm_g': _jnp.float32, 'att_norm_g': _jnp.float32, 'w_out': _jnp.float32, 'norm_mlp_g': _jnp.float32, 'w_up': _jnp.float32, 'w_down': _jnp.float32, 'final_norm_g': _jnp.float32}
MOMENT_SCALE = {'norm_mix_g': 1.416663e-01, 'w_in': 5.714446e-02, 'conv_w': 5.523335e-02, 'conv_b': 7.614368e-02, 'dt_bias': 1.688540e-01, 'a_log': 2.341907e-01, 'd_skip': 4.145402e-01, 'ssd_norm_g': 7.223291e-02, 'att_norm_g': 7.384778e-02, 'w_out': 2.139991e-01, 'norm_mlp_g': 1.331246e-01, 'w_up': 6.423449e-02, 'w_down': 2.495188e-01, 'final_norm_g': 6.479323e+01}


def _to_microbatches(a, axis):
    t = _jnp.moveaxis(a, axis, 0)
    t = t.reshape((N_MICROBATCH, t.shape[0] // N_MICROBATCH) + t.shape[1:])
    return _jnp.moveaxis(t, 1, axis + 1)


def setup_inputs(seed: int = 0) -> dict:
    inp = _fwd_setup_inputs(seed)
    key = _jax.random.fold_in(_jax.random.key(seed), 7919)
    shape, _ = _output_shape()
    out = dict(inp)
    out["loss_target"] = _jax.random.normal(_jax.random.fold_in(key, 0), shape, _jnp.float32)
    for i, name in enumerate(TWIN_WEIGHTS):
        w = inp[name].astype(_jnp.float32)
        if MOMENT_SCALE is None:
            s = _jnp.sqrt(_jnp.mean(_jnp.square(w)) + 1e-30)
        else:
            s = MOMENT_SCALE[name]
        km, kv = _jax.random.split(_jax.random.fold_in(key, i + 1))
        out[name] = w
        out["m_" + name] = s * _jax.random.normal(km, w.shape, _jnp.float32)
        out["v_" + name] = (s * s) * _jax.random.uniform(kv, w.shape, _jnp.float32, 0.5, 1.5)
    if N_MICROBATCH > 1:
        for name, axis in PER_EXAMPLE_BATCH_AXIS.items():
            out[name] = _to_microbatches(out[name], axis)
    return {'x': out['x'], 'norm_mix_g': out['norm_mix_g'], 'w_in': out['w_in'], 'conv_w': out['conv_w'], 'conv_b': out['conv_b'], 'dt_bias': out['dt_bias'], 'a_log': out['a_log'], 'd_skip': out['d_skip'], 'ssd_norm_g': out['ssd_norm_g'], 'att_norm_g': out['att_norm_g'], 'w_out': out['w_out'], 'norm_mlp_g': out['norm_mlp_g'], 'w_up': out['w_up'], 'w_down': out['w_down'], 'final_norm_g': out['final_norm_g'], 'loss_target': out['loss_target'], 'm_norm_mix_g': out['m_norm_mix_g'], 'm_w_in': out['m_w_in'], 'm_conv_w': out['m_conv_w'], 'm_conv_b': out['m_conv_b'], 'm_dt_bias': out['m_dt_bias'], 'm_a_log': out['m_a_log'], 'm_d_skip': out['m_d_skip'], 'm_ssd_norm_g': out['m_ssd_norm_g'], 'm_att_norm_g': out['m_att_norm_g'], 'm_w_out': out['m_w_out'], 'm_norm_mlp_g': out['m_norm_mlp_g'], 'm_w_up': out['m_w_up'], 'm_w_down': out['m_w_down'], 'm_final_norm_g': out['m_final_norm_g'], 'v_norm_mix_g': out['v_norm_mix_g'], 'v_w_in': out['v_w_in'], 'v_conv_w': out['v_conv_w'], 'v_conv_b': out['v_conv_b'], 'v_dt_bias': out['v_dt_bias'], 'v_a_log': out['v_a_log'], 'v_d_skip': out['v_d_skip'], 'v_ssd_norm_g': out['v_ssd_norm_g'], 'v_att_norm_g': out['v_att_norm_g'], 'v_w_out': out['v_w_out'], 'v_norm_mlp_g': out['v_norm_mlp_g'], 'v_w_up': out['v_w_up'], 'v_w_down': out['v_w_down'], 'v_final_norm_g': out['v_final_norm_g']}


def _loss(weights, diff, rest, loss_target):
    with _jax.named_scope("forward"):
        args = {**rest, TWIN_DIFF_INPUT: diff, **{k: w.astype(_WEIGHT_DTYPES[k]) for k, w in weights.items()}}
        y = _forward(args)
    with _jax.named_scope("loss_head"):
        err = _jnp.square(y.astype(_jnp.float32) - loss_target)
        return 0.5 * _jnp.sum(_jnp.mean(err, axis=-1)) if err.ndim else 0.5 * err


def _adamw(w, g, m, v):
    m = ADAM_B1 * m + (1.0 - ADAM_B1) * g
    v = ADAM_B2 * v + (1.0 - ADAM_B2) * _jnp.square(g)
    m_hat = m / (1.0 - ADAM_B1 ** ADAM_STEP)
    v_hat = v / (1.0 - ADAM_B2 ** ADAM_STEP)
    delta = -ADAM_LR * (m_hat / (_jnp.sqrt(v_hat) + ADAM_EPS) + ADAM_WD * w)
    return delta, m, v


def reference(x, norm_mix_g, w_in, conv_w, conv_b, dt_bias, a_log, d_skip, ssd_norm_g, att_norm_g, w_out, norm_mlp_g, w_up, w_down, final_norm_g, loss_target, m_norm_mix_g, m_w_in, m_conv_w, m_conv_b, m_dt_bias, m_a_log, m_d_skip, m_ssd_norm_g, m_att_norm_g, m_w_out, m_norm_mlp_g, m_w_up, m_w_down, m_final_norm_g, v_norm_mix_g, v_w_in, v_conv_w, v_conv_b, v_dt_bias, v_a_log, v_d_skip, v_ssd_norm_g, v_att_norm_g, v_w_out, v_norm_mlp_g, v_w_up, v_w_down, v_final_norm_g):
    given = dict(x=x, norm_mix_g=norm_mix_g, w_in=w_in, conv_w=conv_w, conv_b=conv_b, dt_bias=dt_bias, a_log=a_log, d_skip=d_skip, ssd_norm_g=ssd_norm_g, att_norm_g=att_norm_g, w_out=w_out, norm_mlp_g=norm_mlp_g, w_up=w_up, w_down=w_down, final_norm_g=final_norm_g, loss_target=loss_target, m_norm_mix_g=m_norm_mix_g, m_w_in=m_w_in, m_conv_w=m_conv_w, m_conv_b=m_conv_b, m_dt_bias=m_dt_bias, m_a_log=m_a_log, m_d_skip=m_d_skip, m_ssd_norm_g=m_ssd_norm_g, m_att_norm_g=m_att_norm_g, m_w_out=m_w_out, m_norm_mlp_g=m_norm_mlp_g, m_w_up=m_w_up, m_w_down=m_w_down, m_final_norm_g=m_final_norm_g, v_norm_mix_g=v_norm_mix_g, v_w_in=v_w_in, v_conv_w=v_conv_w, v_conv_b=v_conv_b, v_dt_bias=v_dt_bias, v_a_log=v_a_log, v_d_skip=v_d_skip, v_ssd_norm_g=v_ssd_norm_g, v_att_norm_g=v_att_norm_g, v_w_out=v_w_out, v_norm_mlp_g=v_norm_mlp_g, v_w_up=v_w_up, v_w_down=v_w_down, v_final_norm_g=v_final_norm_g)
    weights = {n: given[n] for n in TWIN_WEIGHTS}
    shared = {n: given[n] for n in SHARED_INPUTS}
    per_example = {n: given[n] for n in ['x']}
    grad_fn = _jax.value_and_grad(_loss, argnums=(0, 1))

    def one_microbatch(ex, loss_target):
        ex = dict(ex)
        diff = ex.pop(TWIN_DIFF_INPUT)
        return grad_fn(weights, diff, {**shared, **ex}, loss_target)

    if N_MICROBATCH == 1:
        loss, (grad_w, grad_x) = one_microbatch(per_example, given["loss_target"])
    else:
        def body(carry, xs):
            loss_sum, grad_sum = carry
            l_k, (gw_k, gx_k) = one_microbatch(xs[0], xs[1])
            with _jax.named_scope("update"):
                return (loss_sum + l_k, _jax.tree.map(_jnp.add, grad_sum, gw_k)), gx_k

        init = (_jnp.zeros((), _jnp.float32), _jax.tree.map(_jnp.zeros_like, weights))
        (loss, grad_w), grad_x = _jax.lax.scan(body, init, (per_example, given["loss_target"]))
    with _jax.named_scope("update"):
        delta_w, new_m, new_v = {}, {}, {}
        for n in TWIN_WEIGHTS:
            delta_w[n], new_m[n], new_v[n] = _adamw(weights[n], grad_w[n], given["m_" + n], given["v_" + n])
    return (loss, grad_x, *[grad_w[n] for n in TWIN_WEIGHTS], *[delta_w[n] for n in TWIN_WEIGHTS],
            *[new_m[n] for n in TWIN_WEIGHTS], *[new_v[n] for n in TWIN_WEIGHTS])
```

```python
import functools

import jax
import jax.numpy as jnp
from jax import lax
from jax.experimental import pallas as pl
from jax.experimental.pallas import tpu as pltpu

F32 = jnp.float32
BF16 = jnp.bfloat16

HEAD_DIM = 64
SSM_STATE = 128
SSM_GROUPS = 4
CONV_WIDTH = 4
CHUNK = 128
Q_BLOCK = 128
EPS = 1e-5
LANES = 128
N_DEV = 8
VMEM_LIMIT = 56 * 1024 * 1024

ADAM_LR = 0.001
ADAM_B1 = 0.9
ADAM_B2 = 0.999
ADAM_EPS = 1e-08
ADAM_WD = 0.01
ADAM_STEP = 10

_NT = (((1,), (1,)), ((), ()))
_TN = (((0,), (0,)), ((), ()))


def _blk(n, pref):
    if n <= pref:
        return n
    b = (pref // LANES) * LANES
    while b >= LANES:
        if n % b == 0:
            return b
        b -= LANES
    return n


def _cparams(sem):
    return pltpu.CompilerParams(dimension_semantics=sem, vmem_limit_bytes=VMEM_LIMIT)


def _dot(a, b):
    return jnp.dot(a, b, preferred_element_type=F32)


def _dg(a, b, dims):
    return lax.dot_general(a, b, dims, preferred_element_type=F32)


def _split3(x):
    h = x.astype(BF16)
    r = x - h.astype(F32)
    m = r.astype(BF16)
    l = (r - m.astype(F32)).astype(BF16)
    return h, m, l


def _dot_exact_rhs(x, c):
    h, m, l = _split3(x)
    return _dot(h, c) + _dot(m, c) + _dot(l, c)


def _dot_exact_lhs(c, x, dims):
    h, m, l = _split3(x)
    return _dg(c, h, dims) + _dg(c, m, dims) + _dg(c, l, dims)


def _mask(cond):
    return jnp.where(cond, 1.0, 0.0).astype(BF16)


def _sigmoid(x):
    return 1.0 / (1.0 + jnp.exp(-x))


def _softplus(x):
    return jnp.maximum(x, 0.0) + jnp.log(1.0 + jnp.exp(-jnp.abs(x)))


def _mm(a, b, *, name, out_dtypes, ta=False, epilogue=None, extras=(), tm=1024, tn=512, tk=2048):
    if ta:
        K, M = a.shape
    else:
        M, K = a.shape
    N = b.shape[1]
    assert b.shape[0] == K
    tm, tn, tk = _blk(M, tm), _blk(N, tn), _blk(K, tk)
    nk = K // tk
    n_ex, n_out = len(extras), len(out_dtypes)

    def kern(*refs):
        a_ref, b_ref = refs[0], refs[1]
        ex = refs[2:2 + n_ex]
        outs = refs[2 + n_ex:2 + n_ex + n_out]
        acc = refs[-1]
        k = pl.program_id(2)

        @pl.when(k == 0)
        def _():
            acc[...] = jnp.zeros_like(acc)

        av = a_ref[...].astype(BF16)
        bv = b_ref[...].astype(BF16)
        acc[...] += _dg(av, bv, _TN) if ta else _dot(av, bv)

        @pl.when(k == nk - 1)
        def _():
            r = acc[...]
            vals = epilogue(r, *[e[...] for e in ex]) if epilogue is not None else (r,)
            for o, v in zip(outs, vals):
                o[...] = v.astype(o.dtype)

    if ta:
        a_spec = pl.BlockSpec((tk, tm), lambda j, i, k: (k, i))
    else:
        a_spec = pl.BlockSpec((tm, tk), lambda j, i, k: (i, k))
    b_spec = pl.BlockSpec((tk, tn), lambda j, i, k: (k, j))
    o_spec = pl.BlockSpec((tm, tn), lambda j, i, k: (i, j))
    res = pl.pallas_call(
        kern, name=name,
        out_shape=[jax.ShapeDtypeStruct((M, N), d) for d in out_dtypes],
        grid=(N // tn, M // tm, nk),
        in_specs=[a_spec, b_spec] + [o_spec] * n_ex,
        out_specs=[o_spec] * n_out,
        scratch_shapes=[pltpu.VMEM((tm, tn), F32)],
        compiler_params=_cparams(("parallel", "parallel", "arbitrary")),
    )(a, b, *extras)
    return res


def _rmsnorm_fwd(x, g, *, name):
    T, D = x.shape
    tm = _blk(T, 512)

    def kern(x_ref, g_ref, h_ref):
        xv = x_ref[...]
        r = lax.rsqrt(jnp.mean(xv * xv, axis=-1, keepdims=True) + EPS)
        h_ref[...] = (xv * r * g_ref[...]).astype(h_ref.dtype)

    return pl.pallas_call(
        kern, name=name, out_shape=jax.ShapeDtypeStruct((T, D), BF16), grid=(T // tm,),
        in_specs=[pl.BlockSpec((tm, D), lambda i: (i, 0)), pl.BlockSpec((1, D), lambda i: (0, 0))],
        out_specs=pl.BlockSpec((tm, D), lambda i: (i, 0)),
        compiler_params=_cparams(("parallel",)),
    )(x, g.reshape(1, D))


def _rmsnorm_bwd(dh, x, g, dres, *, name):
    T, D = x.shape
    tm = _blk(T, 512)

    def kern(dh_ref, x_ref, g_ref, dres_ref, dx_ref, dg_ref):
        @pl.when(pl.program_id(0) == 0)
        def _():
            dg_ref[...] = jnp.zeros_like(dg_ref)

        xv = x_ref[...]
        r = lax.rsqrt(jnp.mean(xv * xv, axis=-1, keepdims=True) + EPS)
        xn = xv * r
        dy = dh_ref[...].astype(F32)
        dyg = dy * g_ref[...]
        dx = r * (dyg - xn * jnp.mean(dyg * xn, axis=-1, keepdims=True))
        dx_ref[...] = dres_ref[...] + dx
        dg_ref[...] += jnp.sum(dy * xn, axis=0, keepdims=True)

    row = pl.BlockSpec((tm, D), lambda i: (i, 0))
    vec = pl.BlockSpec((1, D), lambda i: (0, 0))
    return pl.pallas_call(
        kern, name=name,
        out_shape=[jax.ShapeDtypeStruct((T, D), F32), jax.ShapeDtypeStruct((1, D), F32)],
        grid=(T // tm,), in_specs=[row, row, vec, row], out_specs=[row, vec],
        compiler_params=_cparams(("arbitrary",)),
    )(dh, x, g.reshape(1, D), dres)


def _loss_head(x, g, target, *, name):
    T, D = x.shape
    tm = _blk(T, 512)

    def kern(x_ref, g_ref, t_ref, dx_ref, dg_ref, loss_ref):
        @pl.when(pl.program_id(0) == 0)
        def _():
            dg_ref[...] = jnp.zeros_like(dg_ref)
            loss_ref[...] = jnp.zeros_like(loss_ref)

        xv = x_ref[...]
        gv = g_ref[...]
        r = lax.rsqrt(jnp.mean(xv * xv, axis=-1, keepdims=True) + EPS)
        xn = xv * r
        err = xn * gv - t_ref[...]
        loss_ref[...] += jnp.sum(err * err, axis=0, keepdims=True) * (0.5 / D)
        dy = err * (1.0 / D)
        dyg = dy * gv
        dx_ref[...] = r * (dyg - xn * jnp.mean(dyg * xn, axis=-1, keepdims=True))
        dg_ref[...] += jnp.sum(dy * xn, axis=0, keepdims=True)

    row = pl.BlockSpec((tm, D), lambda i: (i, 0))
    vec = pl.BlockSpec((1, D), lambda i: (0, 0))
    return pl.pallas_call(
        kern, name=name,
        out_shape=[jax.ShapeDtypeStruct((T, D), F32), jax.ShapeDtypeStruct((1, D), F32),
                   jax.ShapeDtypeStruct((1, D), F32)],
        grid=(T // tm,), in_specs=[row, vec, row], out_specs=[row, vec, vec],
        compiler_params=_cparams(("arbitrary",)),
    )(x, g.reshape(1, D), target)


def _shift_down(u, s, L):
    t = lax.broadcasted_iota(jnp.int32, u.shape, 0)
    return jnp.where(t >= s, pltpu.roll(u, s, 0), 0.0)


def _shift_up(u, s, L):
    t = lax.broadcasted_iota(jnp.int32, u.shape, 0)
    return jnp.where(t < L - s, pltpu.roll(u, L - s, 0), 0.0)


def _conv_pre(u, w_ref, b_ref, L):
    pre = u * w_ref[CONV_WIDTH - 1:CONV_WIDTH, :] + b_ref[...]
    for s in range(1, CONV_WIDTH):
        pre = pre + _shift_down(u, s, L) * w_ref[CONV_WIDTH - 1 - s:CONV_WIDTH - s, :]
    return pre


def _conv_fwd(proj, w, b, Bl, L, col0, Dc, *, name):
    cb = LANES
    c0 = col0 // cb

    def kern(u_ref, w_ref, b_ref, o_ref):
        pre = _conv_pre(u_ref[...], w_ref, b_ref, L)
        o_ref[...] = pre * _sigmoid(pre)

    return pl.pallas_call(
        kern, name=name, out_shape=jax.ShapeDtypeStruct((Bl * L, Dc), F32), grid=(Bl, Dc // cb),
        in_specs=[pl.BlockSpec((L, cb), lambda bi, j: (bi, c0 + j)),
                  pl.BlockSpec((CONV_WIDTH, cb), lambda bi, j: (0, j)),
                  pl.BlockSpec((1, cb), lambda bi, j: (0, j))],
        out_specs=pl.BlockSpec((L, cb), lambda bi, j: (bi, j)),
        compiler_params=_cparams(("parallel", "parallel")),
    )(proj, w, b.reshape(1, Dc))


def _conv_bwd(dact, proj, w, b, Bl, L, col0, Dc, *, name):
    cb = LANES
    c0 = col0 // cb

    def kern(d_ref, u_ref, w_ref, b_ref, du_ref, dw_ref, db_ref):
        @pl.when(pl.program_id(1) == 0)
        def _():
            dw_ref[...] = jnp.zeros_like(dw_ref)
            db_ref[...] = jnp.zeros_like(db_ref)

        u = u_ref[...]
        pre = _conv_pre(u, w_ref, b_ref, L)
        sig = _sigmoid(pre)
        dpre = d_ref[...] * (sig * (1.0 + pre * (1.0 - sig)))
        du = dpre * w_ref[CONV_WIDTH - 1:CONV_WIDTH, :]
        dw_ref[CONV_WIDTH - 1:CONV_WIDTH, :] += jnp.sum(dpre * u, axis=0, keepdims=True)
        for s in range(1, CONV_WIDTH):
            k = CONV_WIDTH - 1 - s
            du = du + _shift_up(dpre, s, L) * w_ref[k:k + 1, :]
            dw_ref[k:k + 1, :] += jnp.sum(dpre * _shift_down(u, s, L), axis=0, keepdims=True)
        db_ref[...] += jnp.sum(dpre, axis=0, keepdims=True)
        du_ref[...] = du.astype(du_ref.dtype)

    return pl.pallas_call(
        kern, name=name,
        out_shape=[jax.ShapeDtypeStruct((Bl * L, Dc), BF16),
                   jax.ShapeDtypeStruct((CONV_WIDTH, Dc), F32), jax.ShapeDtypeStruct((1, Dc), F32)],
        grid=(Dc // cb, Bl),
        in_specs=[pl.BlockSpec((L, cb), lambda j, bi: (bi, j)),
                  pl.BlockSpec((L, cb), lambda j, bi: (bi, c0 + j)),
                  pl.BlockSpec((CONV_WIDTH, cb), lambda j, bi: (0, j)),
                  pl.BlockSpec((1, cb), lambda j, bi: (0, j))],
        out_specs=[pl.BlockSpec((L, cb), lambda j, bi: (bi, j)),
                   pl.BlockSpec((CONV_WIDTH, cb), lambda j, bi: (0, j)),
                   pl.BlockSpec((1, cb), lambda j, bi: (0, j))],
        compiler_params=_cparams(("parallel", "arbitrary")),
    )(dact, proj, w, b.reshape(1, Dc))


def _tri_consts():
    r = lax.broadcasted_iota(jnp.int32, (CHUNK, CHUNK), 0)
    c = lax.broadcasted_iota(jnp.int32, (CHUNK, CHUNK), 1)
    lane0 = c < HEAD_DIM
    return r, c, lane0


def _ssd_chunk_common(x, dtr, bias, alog):
    r, c, _ = _tri_consts()
    dt = _softplus(dtr + bias)
    a = dt * (-jnp.exp(alog))
    tril = _mask(c <= r)
    E = _dot_exact_lhs(tril, a, (((1,), (0,)), ((), ())))
    return dt, E, E.T


def _ssd_head_decay(E, ET, h):
    r, c, _ = _tri_consts()
    cm = jnp.broadcast_to(E[:, h * HEAD_DIM:h * HEAD_DIM + 1], (CHUNK, CHUNK))
    rm = jnp.broadcast_to(ET[h * HEAD_DIM:h * HEAD_DIM + 1, :], (CHUNK, CHUNK))
    return jnp.where(c <= r, jnp.exp(jnp.minimum(cm - rm, 0.0)), 0.0)


def _ssd_fwd(xbc, proj, bias_e, alog_e, Bl, L, D, dt_col0, *, name):
    nc = L // CHUNK
    npairs = D // LANES
    ppg = npairs // SSM_GROUPS
    gw = ppg * LANES
    b0 = D // LANES
    c0 = (D + SSM_GROUPS * SSM_STATE) // LANES
    d0 = dt_col0 // gw

    def kern(x_ref, b_ref, c_ref, dtr_ref, bias_ref, alog_ref, y_ref, st_ref, h_scr):
        _, _, lane0 = _tri_consts()
        h_scr[...] = jnp.zeros_like(h_scr)

        def chunk(ci, carry):
            rows = pl.ds(pl.multiple_of(ci * CHUNK, CHUNK), CHUNK)
            Bb = b_ref[rows, :].astype(BF16)
            Cb = c_ref[rows, :].astype(BF16)
            CB = _dg(Cb, Bb, _NT)
            for p in range(ppg):
                lanes = pl.ds(p * LANES, LANES)
                x = x_ref[rows, lanes]
                dt, E, ET = _ssd_chunk_common(x, dtr_ref[rows, lanes], bias_ref[:, lanes], alog_ref[:, lanes])
                xdt = x * dt
                xdtb = xdt.astype(BF16)
                ys = []
                for h in range(2):
                    Wh = (CB * _ssd_head_decay(E, ET, h)).astype(BF16)
                    ys.append(_dot(Wh, xdtb))
                hprev = h_scr[p]
                y = jnp.where(lane0, ys[0], ys[1]) + _dg(Cb, hprev.astype(BF16), _NT) * jnp.exp(E)
                ds = jnp.exp(jnp.broadcast_to(E[CHUNK - 1:CHUNK, :], (CHUNK, LANES)) - E)
                states = _dg((xdt * ds).astype(BF16), Bb, _TN)
                cd = jnp.exp(jnp.broadcast_to(ET[:, CHUNK - 1:CHUNK], (LANES, SSM_STATE)))
                st_ref[ci, p] = hprev
                h_scr[p] = hprev * cd + states
                y_ref[rows, lanes] = y
            return carry

        lax.fori_loop(0, nc, chunk, 0)

    return pl.pallas_call(
        kern, name=name,
        out_shape=[jax.ShapeDtypeStruct((Bl * L, D), F32),
                   jax.ShapeDtypeStruct((Bl, SSM_GROUPS, nc, ppg, LANES, SSM_STATE), F32)],
        grid=(Bl, SSM_GROUPS),
        in_specs=[pl.BlockSpec((L, gw), lambda bi, g: (bi, g)),
                  pl.BlockSpec((L, SSM_STATE), lambda bi, g: (bi, b0 + g)),
                  pl.BlockSpec((L, SSM_STATE), lambda bi, g: (bi, c0 + g)),
                  pl.BlockSpec((L, gw), lambda bi, g: (bi, d0 + g)),
                  pl.BlockSpec((1, gw), lambda bi, g: (0, g)),
                  pl.BlockSpec((1, gw), lambda bi, g: (0, g))],
        out_specs=[pl.BlockSpec((L, gw), lambda bi, g: (bi, g)),
                   pl.BlockSpec((None, None, nc, ppg, LANES, SSM_STATE), lambda bi, g: (bi, g, 0, 0, 0, 0))],
        scratch_shapes=[pltpu.VMEM((ppg, LANES, SSM_STATE), F32)],
        compiler_params=_cparams(("parallel", "parallel")),
    )(xbc, xbc, xbc, proj, bias_e, alog_e)


def _ssd_bwd(dy, states, xbc, proj, bias_e, alog_e, dskip_e, Bl, L, D, dt_col0, *, name):
    nc = L // CHUNK
    npairs = D // LANES
    ppg = npairs // SSM_GROUPS
    gw = ppg * LANES
    b0 = D // LANES
    c0 = (D + SSM_GROUPS * SSM_STATE) // LANES
    d0 = dt_col0 // gw

    def kern(dy_ref, st_ref, x_ref, b_ref, c_ref, dtr_ref, bias_ref, alog_ref, dsk_ref,
             dx_ref, db_ref, dc_ref, ddt_ref, gbias_ref, galog_ref, gdsk_ref, g_scr):
        r, c, lane0 = _tri_consts()
        m0 = lane0.astype(F32)
        masks = (m0, 1.0 - m0)
        triu = _mask(c >= r)
        trils = _mask(c < r)
        ones = jnp.ones((CHUNK, LANES), BF16)
        g_scr[...] = jnp.zeros_like(g_scr)

        @pl.when(pl.program_id(1) == 0)
        def _():
            gbias_ref[...] = jnp.zeros_like(gbias_ref)
            galog_ref[...] = jnp.zeros_like(galog_ref)
            gdsk_ref[...] = jnp.zeros_like(gdsk_ref)

        def chunk(i, carry):
            ci = nc - 1 - i
            rows = pl.ds(pl.multiple_of(ci * CHUNK, CHUNK), CHUNK)
            Bb = b_ref[rows, :].astype(BF16)
            Cb = c_ref[rows, :].astype(BF16)
            CB = _dg(Cb, Bb, _NT)
            gsum = jnp.zeros((CHUNK, CHUNK), F32)
            dB = jnp.zeros((CHUNK, SSM_STATE), F32)
            dC = jnp.zeros((CHUNK, SSM_STATE), F32)
            for p in range(ppg):
                lanes = pl.ds(p * LANES, LANES)
                x = x_ref[rows, lanes]
                dtr = dtr_ref[rows, lanes]
                bias = bias_ref[:, lanes]
                alog = alog_ref[:, lanes]
                dt, E, ET = _ssd_chunk_common(x, dtr, bias, alog)
                xdt = x * dt
                xdtb = xdt.astype(BF16)
                dyv = dy_ref[rows, lanes]
                dyb = dyv.astype(BF16)
                hprev = st_ref[ci, p]
                hpb = hprev.astype(BF16)
                g = g_scr[p]
                gb = g.astype(BF16)
                sd = jnp.exp(E)
                ds = jnp.exp(jnp.broadcast_to(E[CHUNK - 1:CHUNK, :], (CHUNK, LANES)) - E)
                t1 = []
                dE = dyv * (_dg(Cb, hpb, _NT) * sd)
                for h in range(2):
                    Lh = _ssd_head_decay(E, ET, h)
                    Wh = (CB * Lh).astype(BF16)
                    t1.append(_dg(Wh, dyb, _TN))
                    Gh = Lh * _dg((dyv * masks[h]).astype(BF16), xdtb, _NT)
                    gsum = gsum + Gh
                    Qh = Gh * CB
                    dE = dE + _dot_exact_rhs(Qh - Qh.T, _mask(c == h * HEAD_DIM))
                dxst = _dg(Bb, gb, _NT) * ds
                dxdt = jnp.where(lane0, t1[0], t1[1]) + dxst
                dysd = (dyv * sd).astype(BF16)
                dC = dC + _dot(dysd, hpb)
                dB = dB + _dot((xdt * ds).astype(BF16), gb)
                cd = jnp.exp(jnp.broadcast_to(ET[:, CHUNK - 1:CHUNK], (LANES, SSM_STATE)))
                g_scr[p] = g * cd + _dg(dysd, Cb, _TN)
                nn = (((1,), (0,)), ((), ()))
                da = (_dot_exact_lhs(triu, dE, nn) + _dot_exact_lhs(trils, xdt * dxst, nn)
                      + _dot_exact_lhs(ones, g * cd * hprev, _NT))
                aneg = -jnp.exp(alog)
                ddt = da * aneg + dxdt * x
                ddtr = ddt * _sigmoid(dtr + bias)
                dx_ref[rows, lanes] = dxdt * dt + dyv * dsk_ref[:, lanes]
                ddt_ref[rows, lanes] = ddtr.astype(ddt_ref.dtype)
                gbias_ref[:, lanes] += jnp.sum(ddtr, axis=0, keepdims=True)
                galog_ref[:, lanes] += jnp.sum(da * dt, axis=0, keepdims=True) * aneg
                gdsk_ref[:, lanes] += jnp.sum(dyv * x, axis=0, keepdims=True)
            gsb = gsum.astype(BF16)
            db_ref[rows, :] = dB + _dg(gsb, Cb, _TN)
            dc_ref[rows, :] = dC + _dot(gsb, Bb)
            return carry

        lax.fori_loop(0, nc, chunk, 0)

    T = Bl * L
    xspec = pl.BlockSpec((L, gw), lambda g, bi: (bi, g))
    nspec = pl.BlockSpec((L, SSM_STATE), lambda g, bi: (bi, g))
    vspec = pl.BlockSpec((1, gw), lambda g, bi: (0, g))
    return pl.pallas_call(
        kern, name=name,
        out_shape=[jax.ShapeDtypeStruct((T, D), F32),
                   jax.ShapeDtypeStruct((T, SSM_GROUPS * SSM_STATE), F32),
                   jax.ShapeDtypeStruct((T, SSM_GROUPS * SSM_STATE), F32),
                   jax.ShapeDtypeStruct((T, D), BF16),
                   jax.ShapeDtypeStruct((1, D), F32), jax.ShapeDtypeStruct((1, D), F32),
                   jax.ShapeDtypeStruct((1, D), F32)],
        grid=(SSM_GROUPS, Bl),
        in_specs=[xspec,
                  pl.BlockSpec((None, None, nc, ppg, LANES, SSM_STATE), lambda g, bi: (bi, g, 0, 0, 0, 0)),
                  xspec,
                  pl.BlockSpec((L, SSM_STATE), lambda g, bi: (bi, b0 + g)),
                  pl.BlockSpec((L, SSM_STATE), lambda g, bi: (bi, c0 + g)),
                  pl.BlockSpec((L, gw), lambda g, bi: (bi, d0 + g)),
                  vspec, vspec, vspec],
        out_specs=[xspec, nspec, nspec, xspec, vspec, vspec, vspec],
        scratch_shapes=[pltpu.VMEM((ppg, LANES, SSM_STATE), F32)],
        compiler_params=_cparams(("parallel", "arbitrary")),
    )(dy, states, xbc, xbc, xbc, proj, bias_e, alog_e, dskip_e)


def _att_consts():
    r = lax.broadcasted_iota(jnp.int32, (Q_BLOCK, Q_BLOCK), 0)
    c = lax.broadcasted_iota(jnp.int32, (Q_BLOCK, Q_BLOCK), 1)
    return r, c


def _att_logs(z):
    lp = jnp.log(1.0 + jnp.exp(-jnp.abs(z)))
    lb = jnp.minimum(z, 0.0) - lp
    l1m = -(jnp.maximum(z, 0.0) + lp)
    return lb, l1m


def _cum2(x, mat2):
    hi = x.astype(BF16)
    lo = (x - hi.astype(F32)).astype(BF16)
    r = _dot(hi, mat2) + _dot(lo, mat2)
    return r[:, :Q_BLOCK], r[:, Q_BLOCK:]


def _attn_fwd(proj, Bl, L, D, qc, kc, vc, *, name):
    npairs = D // LANES
    nq = L // Q_BLOCK
    scale = HEAD_DIM ** -0.5

    def kern(q_ref, k_ref, v_ref, o_ref, tot_ref, qm, kb, vb):
        r, c = _att_consts()
        tri = r > c
        lane0 = c < HEAD_DIM
        lrow = lax.broadcasted_iota(jnp.int32, (1, LANES), 1) < HEAD_DIM
        q = q_ref[...] * scale
        qm[0] = jnp.where(lrow, q, 0.0).astype(BF16)
        qm[1] = jnp.where(lrow, 0.0, q).astype(BF16)
        kb[...] = k_ref[...].astype(BF16)
        vb[...] = v_ref[...].astype(BF16)
        upper = _mask(r > c)
        mat2 = jnp.concatenate([upper, jnp.ones((Q_BLOCK, Q_BLOCK), BF16)], axis=1)

        def tile(qs, ks, st, diag):
            new = []
            for h in range(2):
                carry, acc = st[2 * h], st[2 * h + 1]
                z = _dg(qm[h, qs, :], kb[ks, :], _NT)
                lb, l1m = _att_logs(z)
                if diag:
                    l1m = jnp.where(tri, l1m, 0.0)
                suf, rs = _cum2(l1m, mat2)
                att = jnp.exp(lb + suf + carry)
                if diag:
                    att = jnp.where(tri, att, 0.0)
                acc = acc + _dot(att.astype(BF16), vb[ks, :])
                new += [carry + rs, acc]
            return tuple(new)

        def qblock(qi, carry):
            qs = pl.ds(pl.multiple_of(qi * Q_BLOCK, Q_BLOCK), Q_BLOCK)
            zero = jnp.zeros((Q_BLOCK, LANES), F32)
            st = tile(qs, qs, (zero, zero, zero, zero), True)

            def kblock(j, st):
                kj = qi - 1 - j
                ks = pl.ds(pl.multiple_of(kj * Q_BLOCK, Q_BLOCK), Q_BLOCK)
                return tile(qs, ks, st, False)

            st = lax.fori_loop(0, qi, kblock, st)
            o_ref[qs, :] = jnp.where(lane0, st[1], st[3])
            tot_ref[0, qs, :] = st[0]
            tot_ref[1, qs, :] = st[2]
            return carry

        lax.fori_loop(0, nq, qblock, 0)

    spec = lambda col: pl.BlockSpec((L, LANES), lambda bi, p: (bi, col // LANES + p))
    return pl.pallas_call(
        kern, name=name,
        out_shape=[jax.ShapeDtypeStruct((Bl * L, D), F32),
                   jax.ShapeDtypeStruct((Bl, npairs, 2, L, LANES), F32)],
        grid=(Bl, npairs),
        in_specs=[spec(qc), spec(kc), spec(vc)],
        out_specs=[pl.BlockSpec((L, LANES), lambda bi, p: (bi, p)),
                   pl.BlockSpec((None, None, 2, L, LANES), lambda bi, p: (bi, p, 0, 0, 0))],
        scratch_shapes=[pltpu.VMEM((2, L, LANES), BF16), pltpu.VMEM((L, LANES), BF16),
                        pltpu.VMEM((L, LANES), BF16)],
        compiler_params=_cparams(("parallel", "parallel")),
    )(proj, proj, proj)


def _attn_bwd(dout, tot, proj, Bl, L, D, qc, kc, vc, *, name):
    npairs = D // LANES
    nq = L // Q_BLOCK
    scale = HEAD_DIM ** -0.5

    def kern(do_ref, tot_ref, q_ref, k_ref, v_ref, dq_ref, dk_ref, dv_ref, qm, km, kb, vb, dom, dkacc, dvacc):
        r, c = _att_consts()
        tri = r > c
        lrow = lax.broadcasted_iota(jnp.int32, (1, LANES), 1) < HEAD_DIM
        q = q_ref[...] * scale
        qm[0] = jnp.where(lrow, q, 0.0).astype(BF16)
        qm[1] = jnp.where(lrow, 0.0, q).astype(BF16)
        k = k_ref[...]
        kb[...] = k.astype(BF16)
        km[0] = jnp.where(lrow, k, 0.0).astype(BF16)
        km[1] = jnp.where(lrow, 0.0, k).astype(BF16)
        vb[...] = v_ref[...].astype(BF16)
        do = do_ref[...]
        dom[0] = jnp.where(lrow, do, 0.0).astype(BF16)
        dom[1] = jnp.where(lrow, 0.0, do).astype(BF16)
        dkacc[...] = jnp.zeros_like(dkacc)
        dvacc[...] = jnp.zeros_like(dvacc)
        ones = jnp.ones((Q_BLOCK, Q_BLOCK), BF16)
        incl = jnp.concatenate([_mask(r <= c), ones], axis=1)
        strict = jnp.concatenate([_mask(r < c), ones], axis=1)

        def tile(qs, ks, st, diag):
            dq = st[4]
            new = []
            for h in range(2):
                pl_, pe = st[2 * h], st[2 * h + 1]
                z = _dg(qm[h, qs, :], kb[ks, :], _NT)
                lb, l1m = _att_logs(z)
                if diag:
                    l1m = jnp.where(tri, l1m, 0.0)
                pin, rs = _cum2(l1m, incl)
                att = jnp.exp(lb + (tot_ref[h, qs, :] - (pl_ + pin)))
                if diag:
                    att = jnp.where(tri, att, 0.0)
                attb = att.astype(BF16)
                dE = att * _dg(dom[h, qs, :], vb[ks, :], _NT)
                r2 = _dot(dE.astype(BF16), strict)
                P = pe + r2[:, :Q_BLOCK]
                dz = dE - jnp.exp(lb) * (dE + P)
                if diag:
                    dz = jnp.where(tri, dz, 0.0)
                dzb = dz.astype(BF16)
                dq = dq + _dot(dzb, km[h, ks, :])
                dkacc[ks, :] += _dg(dzb, qm[h, qs, :], _TN)
                dvacc[ks, :] += _dg(attb, dom[h, qs, :], _TN)
                new += [pl_ + rs, pe + r2[:, Q_BLOCK:]]
            return tuple(new) + (dq,)

        def qblock(qi, carry):
            qs = pl.ds(pl.multiple_of(qi * Q_BLOCK, Q_BLOCK), Q_BLOCK)
            zero = jnp.zeros((Q_BLOCK, LANES), F32)

            def kblock(kj, st):
                ks = pl.ds(pl.multiple_of(kj * Q_BLOCK, Q_BLOCK), Q_BLOCK)
                return tile(qs, ks, st, False)

            st = lax.fori_loop(0, qi, kblock, (zero, zero, zero, zero, zero))
            st = tile(qs, qs, st, True)
            dq_ref[qs, :] = (st[4] * scale).astype(dq_ref.dtype)
            return carry

        lax.fori_loop(0, nq, qblock, 0)
        dk_ref[...] = dkacc[...].astype(dk_ref.dtype)
        dv_ref[...] = dvacc[...].astype(dv_ref.dtype)

    spec = lambda col: pl.BlockSpec((L, LANES), lambda bi, p: (bi, col // LANES + p))
    ospec = pl.BlockSpec((L, LANES), lambda bi, p: (bi, p))
    T = Bl * L
    return pl.pallas_call(
        kern, name=name,
        out_shape=[jax.ShapeDtypeStruct((T, D), BF16)] * 3,
        grid=(Bl, npairs),
        in_specs=[ospec, pl.BlockSpec((None, None, 2, L, LANES), lambda bi, p: (bi, p, 0, 0, 0)),
                  spec(qc), spec(kc), spec(vc)],
        out_specs=[ospec, ospec, ospec],
        scratch_shapes=[pltpu.VMEM((2, L, LANES), BF16), pltpu.VMEM((2, L, LANES), BF16),
                        pltpu.VMEM((L, LANES), BF16), pltpu.VMEM((L, LANES), BF16),
                        pltpu.VMEM((2, L, LANES), BF16), pltpu.VMEM((L, LANES), F32),
                        pltpu.VMEM((L, LANES), F32)],
        compiler_params=_cparams(("parallel", "parallel")),
    )(dout, tot, proj, proj, proj)


def _mix_fwd(y, xbc, proj, yatt, dskip_e, g_ssd, g_att, D, *, name):
    T = y.shape[0]
    tm = _blk(T, 256)

    def kern(y_ref, xs_ref, z_ref, ya_ref, dsk_ref, gs_ref, ga_ref, o_ref):
        z = z_ref[...]
        u = (y_ref[...] + xs_ref[...] * dsk_ref[...]) * (z * _sigmoid(z))
        rs = lax.rsqrt(jnp.mean(u * u, axis=-1, keepdims=True) + EPS)
        o_ref[:, :D] = (u * rs * gs_ref[...]).astype(o_ref.dtype)
        ya = ya_ref[...]
        ra = lax.rsqrt(jnp.mean(ya * ya, axis=-1, keepdims=True) + EPS)
        o_ref[:, D:] = (ya * ra * ga_ref[...]).astype(o_ref.dtype)

    row = pl.BlockSpec((tm, D), lambda i: (i, 0))
    vec = pl.BlockSpec((1, D), lambda i: (0, 0))
    return pl.pallas_call(
        kern, name=name, out_shape=jax.ShapeDtypeStruct((T, 2 * D), BF16), grid=(T // tm,),
        in_specs=[row, row, row, row, vec, vec, vec],
        out_specs=pl.BlockSpec((tm, 2 * D), lambda i: (i, 0)),
        compiler_params=_cparams(("parallel",)),
    )(y, xbc, proj, yatt, dskip_e, g_ssd.reshape(1, D), g_att.reshape(1, D))


def _mix_bwd(dmix, y, xbc, proj, yatt, dskip_e, g_ssd, g_att, D, *, name):
    T = y.shape[0]
    tm = _blk(T, 256)

    def kern(d_ref, y_ref, xs_ref, z_ref, ya_ref, dsk_ref, gs_ref, ga_ref,
             dz_ref, dy_ref, dya_ref, dgs_ref, dga_ref):
        @pl.when(pl.program_id(0) == 0)
        def _():
            dgs_ref[...] = jnp.zeros_like(dgs_ref)
            dga_ref[...] = jnp.zeros_like(dga_ref)

        z = z_ref[...]
        sig = _sigmoid(z)
        sz = z * sig
        ytot = y_ref[...] + xs_ref[...] * dsk_ref[...]
        u = ytot * sz
        rs = lax.rsqrt(jnp.mean(u * u, axis=-1, keepdims=True) + EPS)
        un = u * rs
        do = d_ref[:, :D]
        dog = do * gs_ref[...]
        du = rs * (dog - un * jnp.mean(dog * un, axis=-1, keepdims=True))
        dgs_ref[...] += jnp.sum(do * un, axis=0, keepdims=True)
        dy_ref[...] = du * sz
        dz_ref[...] = (du * ytot * (sig * (1.0 + z * (1.0 - sig)))).astype(dz_ref.dtype)
        ya = ya_ref[...]
        ra = lax.rsqrt(jnp.mean(ya * ya, axis=-1, keepdims=True) + EPS)
        yan = ya * ra
        da = d_ref[:, D:]
        dag = da * ga_ref[...]
        dya_ref[...] = ra * (dag - yan * jnp.mean(dag * yan, axis=-1, keepdims=True))
        dga_ref[...] += jnp.sum(da * yan, axis=0, keepdims=True)

    row = pl.BlockSpec((tm, D), lambda i: (i, 0))
    vec = pl.BlockSpec((1, D), lambda i: (0, 0))
    return pl.pallas_call(
        kern, name=name,
        out_shape=[jax.ShapeDtypeStruct((T, D), BF16), jax.ShapeDtypeStruct((T, D), F32),
                   jax.ShapeDtypeStruct((T, D), F32), jax.ShapeDtypeStruct((1, D), F32),
                   jax.ShapeDtypeStruct((1, D), F32)],
        grid=(T // tm,),
        in_specs=[pl.BlockSpec((tm, 2 * D), lambda i: (i, 0)), row, row, row, row, vec, vec, vec],
        out_specs=[row, row, row, vec, vec],
        compiler_params=_cparams(("arbitrary",)),
    )(dmix, y, xbc, proj, yatt, dskip_e, g_ssd.reshape(1, D), g_att.reshape(1, D))


def _head_sum_mat(D):
    i = lax.broadcasted_iota(jnp.int32, (D, LANES), 0)
    c = lax.broadcasted_iota(jnp.int32, (D, LANES), 1)
    return _mask((i >= c * HEAD_DIM) & (i < (c + 1) * HEAD_DIM))


def _collapse_heads(x, *, name):
    R, D = x.shape
    tm = _blk(R, 256)

    def kern(x_ref, o_ref):
        o_ref[...] = _dot_exact_rhs(x_ref[...], _head_sum_mat(D))

    return pl.pallas_call(
        kern, name=name, out_shape=jax.ShapeDtypeStruct((R, LANES), F32), grid=(R // tm,),
        in_specs=[pl.BlockSpec((tm, D), lambda i: (i, 0))],
        out_specs=pl.BlockSpec((tm, LANES), lambda i: (i, 0)),
        compiler_params=_cparams(("parallel",)),
    )(x)


def _adam_math(w, g, m, v):
    m = ADAM_B1 * m + (1.0 - ADAM_B1) * g
    v = ADAM_B2 * v + (1.0 - ADAM_B2) * (g * g)
    m_hat = m / (1.0 - ADAM_B1 ** ADAM_STEP)
    v_hat = v / (1.0 - ADAM_B2 ** ADAM_STEP)
    delta = -ADAM_LR * (m_hat / (jnp.sqrt(v_hat) + ADAM_EPS) + ADAM_WD * w)
    return delta, m, v


def _adamw_sharded(pieces, w, m, v, *, name):
    R, C = w.shape
    tr = _blk(R, 256) if R % 8 == 0 else R

    def kern(p_ref, w_ref, m_ref, v_ref, g_ref, d_ref, nm_ref, nv_ref):
        g = p_ref[0].astype(F32)
        for j in range(1, N_DEV):
            g = g + p_ref[j].astype(F32)
        d, nm, nv = _adam_math(w_ref[...], g, m_ref[...], v_ref[...])
        g_ref[...] = g
        d_ref[...] = d
        nm_ref[...] = nm
        nv_ref[...] = nv

    row = pl.BlockSpec((tr, C), lambda i: (i, 0))
    return pl.pallas_call(
        kern, name=name, out_shape=[jax.ShapeDtypeStruct((R, C), F32)] * 4, grid=(R // tr,),
        in_specs=[pl.BlockSpec((N_DEV, tr, C), lambda i: (0, i, 0)), row, row, row],
        out_specs=[row] * 4,
        compiler_params=_cparams(("parallel",)),
    )(pieces, w, m, v)


def _small_update(parts, wd, md, vd, we, me, ve, nd, ne, D, *, name):
    def kern(p_ref, wd_ref, md_ref, vd_ref, we_ref, me_ref, ve_ref,
             gd_ref, dd_ref, nmd_ref, nvd_ref, ge_ref, de_ref, nme_ref, nve_ref, loss_ref):
        s = p_ref[0]
        for j in range(1, N_DEV):
            s = s + p_ref[j]
        gd = s[:nd]
        d, nm, nv = _adam_math(wd_ref[...], gd, md_ref[...], vd_ref[...])
        gd_ref[...] = gd
        dd_ref[...] = d
        nmd_ref[...] = nm
        nvd_ref[...] = nv
        ge = _dot_exact_rhs(s[nd:nd + ne], _head_sum_mat(D))
        d, nm, nv = _adam_math(we_ref[...], ge, me_ref[...], ve_ref[...])
        ge_ref[...] = ge
        de_ref[...] = d
        nme_ref[...] = nm
        nve_ref[...] = nv
        tot = jnp.sum(jnp.sum(s[nd + ne:], axis=1, keepdims=True), axis=0, keepdims=True)
        loss_ref[...] = jnp.broadcast_to(tot, loss_ref.shape)

    vm = pl.BlockSpec(memory_space=pltpu.VMEM)
    return pl.pallas_call(
        kern, name=name,
        out_shape=[jax.ShapeDtypeStruct((nd, D), F32)] * 4 + [jax.ShapeDtypeStruct((ne, LANES), F32)] * 4
        + [jax.ShapeDtypeStruct((8, LANES), F32)],
        in_specs=[vm] * 7, out_specs=[vm] * 9,
        compiler_params=pltpu.CompilerParams(vmem_limit_bytes=VMEM_LIMIT),
    )(parts, wd, md, vd, we, me, ve)


def _dev_index(p):
    return 4 * p[0] + 2 * p[1] + p[2]


def _all_gather(arrs, *, name):
    n = len(arrs)

    def body(*refs):
        xs, outs = refs[:n], refs[n:2 * n]
        send, recv, loc = refs[2 * n:]
        x, y, c = lax.axis_index("x"), lax.axis_index("y"), lax.axis_index("c")
        me, sib = (x, y, c), (x, y, 1 - c)
        chips = [(1 - x, y), (x, 1 - y), (1 - x, 1 - y)]

        def cp(k, s, block, to, src=None):
            dst = outs[k].at[_dev_index(block)]
            return pltpu.make_async_remote_copy(
                src_ref=dst if src is None else src, dst_ref=dst,
                send_sem=send.at[7 * k + s], recv_sem=recv.at[7 * k + s],
                device_id=to, device_id_type=pl.DeviceIdType.MESH)

        mine = [pltpu.make_async_copy(xs[k], outs[k].at[_dev_index(me)], loc.at[k]) for k in range(n)]
        for m in mine:
            m.start()
        first = []
        for k in range(n):
            first.append(cp(k, 0, me, sib, src=xs[k]))
            for j, ch in enumerate(chips):
                first.append(cp(k, 1 + j, me, (*ch, c), src=xs[k]))
        for f in first:
            f.start()
        passed = []
        for j, ch in enumerate(chips):
            for k in range(n):
                cp(k, 1 + j, (*ch, c), me).wait_recv()
                p = cp(k, 4 + j, (*ch, c), sib)
                p.start()
                passed.append(p)
        for k in range(n):
            cp(k, 0, sib, me).wait_recv()
            for j, ch in enumerate(chips):
                cp(k, 4 + j, (*ch, 1 - c), me).wait_recv()
        for f in first + passed:
            f.wait_send()
        for m in mine:
            m.wait()

    hbm = pl.BlockSpec(memory_space=pl.ANY)
    return pl.pallas_call(
        body, name=name,
        out_shape=[jax.ShapeDtypeStruct((N_DEV,) + a.shape, a.dtype) for a in arrs],
        in_specs=[hbm] * n, out_specs=[hbm] * n,
        scratch_shapes=[pltpu.SemaphoreType.DMA((7 * n,)), pltpu.SemaphoreType.DMA((7 * n,)),
                        pltpu.SemaphoreType.DMA((n,))],
    )(*arrs)


def _all_to_all(arrs, *, name):
    n = len(arrs)
    rels = [(fx, fy, fc) for fx in (0, 1) for fy in (0, 1) for fc in (0, 1) if fx or fy or fc]

    def body(*refs):
        xs, outs = refs[:n], refs[n:2 * n]
        send, recv, loc = refs[2 * n:]
        x, y, c = lax.axis_index("x"), lax.axis_index("y"), lax.axis_index("c")
        me = _dev_index((x, y, c))

        def peer(rel):
            return (1 - x if rel[0] else x, 1 - y if rel[1] else y, 1 - c if rel[2] else c)

        mine = [pltpu.make_async_copy(xs[k].at[me], outs[k].at[me], loc.at[k]) for k in range(n)]
        for m in mine:
            m.start()
        copies = []
        for k in range(n):
            for s, rel in enumerate(rels):
                p = peer(rel)
                copies.append(pltpu.make_async_remote_copy(
                    src_ref=xs[k].at[_dev_index(p)], dst_ref=outs[k].at[me],
                    send_sem=send.at[7 * k + s], recv_sem=recv.at[7 * k + s],
                    device_id=p, device_id_type=pl.DeviceIdType.MESH))
        for cpy in copies:
            cpy.start()
        for k in range(n):
            for s, rel in enumerate(rels):
                p = peer(rel)
                slot = outs[k].at[_dev_index(p)]
                pltpu.make_async_remote_copy(
                    src_ref=slot, dst_ref=slot, send_sem=send.at[7 * k + s], recv_sem=recv.at[7 * k + s],
                    device_id=p, device_id_type=pl.DeviceIdType.MESH).wait_recv()
        for cpy in copies:
            cpy.wait_send()
        for m in mine:
            m.wait()

    hbm = pl.BlockSpec(memory_space=pl.ANY)
    return pl.pallas_call(
        body, name=name,
        out_shape=[jax.ShapeDtypeStruct(a.shape, a.dtype) for a in arrs],
        in_specs=[hbm] * n, out_specs=[hbm] * n,
        scratch_shapes=[pltpu.SemaphoreType.DMA((7 * n,)), pltpu.SemaphoreType.DMA((7 * n,)),
                        pltpu.SemaphoreType.DMA((n,))],
    )(*arrs)


def _layer_fwd(x, P, dims, l):
    Bl, L, D = dims
    Dc = D + 2 * SSM_GROUPS * SSM_STATE
    qc, kc, vc, dtc = D + Dc, 2 * D + Dc, 3 * D + Dc, 4 * D + Dc
    h = _rmsnorm_fwd(x, P["norm_mix_g"], name=f"norm_mix_fwd_{l}")
    (proj,) = _mm(h, P["w_in"], out_dtypes=[F32], name=f"in_proj_{l}")
    xbc = _conv_fwd(proj, P["conv_w"], P["conv_b"], Bl, L, D, Dc, name=f"conv_fwd_{l}")
    y, states = _ssd_fwd(xbc, proj, P["dt_bias_e"], P["a_log_e"], Bl, L, D, dtc, name=f"ssd_fwd_{l}")
    yatt, tot = _attn_fwd(proj, Bl, L, D, qc, kc, vc, name=f"attn_fwd_{l}")
    ymix = _mix_fwd(y, xbc, proj, yatt, P["d_skip_e"], P["ssd_norm_g"], P["att_norm_g"], D, name=f"mix_fwd_{l}")
    (x1,) = _mm(ymix, P["w_out"], out_dtypes=[F32], epilogue=lambda r, res: (r + res,), extras=(x,),
                name=f"out_proj_{l}")
    h2 = _rmsnorm_fwd(x1, P["norm_mlp_g"], name=f"norm_mlp_fwd_{l}")

    def relu2(r):
        a = jnp.maximum(r, 0.0)
        return r, a * a

    u, act = _mm(h2, P["w_up"], out_dtypes=[F32, BF16], epilogue=relu2, name=f"mlp_up_{l}")
    (x2,) = _mm(act, P["w_down"], out_dtypes=[F32], epilogue=lambda r, res: (r + res,), extras=(x1,),
                name=f"mlp_down_{l}")
    saved = dict(x=x, h=h, proj=proj, xbc=xbc, y=y, states=states, yatt=yatt, tot=tot, ymix=ymix,
                 x1=x1, h2=h2, u=u, act=act)
    return x2, saved


def _layer_bwd(dx2, S, P, dims, l):
    Bl, L, D = dims
    Dc = D + 2 * SSM_GROUPS * SSM_STATE
    qc, kc, vc, dtc = D + Dc, 2 * D + Dc, 3 * D + Dc, 4 * D + Dc
    G = {}
    (du,) = _mm(dx2, P["w_down_t"], out_dtypes=[BF16], extras=(S["u"],),
                epilogue=lambda r, u: (r * (2.0 * jnp.maximum(u, 0.0)),), name=f"mlp_down_bwd_{l}")
    (G["w_down"],) = _mm(S["act"], dx2, ta=True, out_dtypes=[F32], name=f"mlp_down_dw_{l}")
    (dh2,) = _mm(du, P["w_up_t"], out_dtypes=[F32], name=f"mlp_up_bwd_{l}")
    (G["w_up"],) = _mm(S["h2"], du, ta=True, out_dtypes=[F32], name=f"mlp_up_dw_{l}")
    dx1, G["norm_mlp_g"] = _rmsnorm_bwd(dh2, S["x1"], P["norm_mlp_g"], dx2, name=f"norm_mlp_bwd_{l}")
    (dmix,) = _mm(dx1, P["w_out_t"], out_dtypes=[F32], name=f"out_proj_bwd_{l}")
    (G["w_out"],) = _mm(S["ymix"], dx1, ta=True, out_dtypes=[F32], name=f"out_proj_dw_{l}")
    dz, dy, dyatt, G["ssd_norm_g"], G["att_norm_g"] = _mix_bwd(
        dmix, S["y"], S["xbc"], S["proj"], S["yatt"], P["d_skip_e"], P["ssd_norm_g"], P["att_norm_g"], D,
        name=f"mix_bwd_{l}")
    dq, dk, dv = _attn_bwd(dyatt, S["tot"], S["proj"], Bl, L, D, qc, kc, vc, name=f"attn_bwd_{l}")
    dxs, dB, dC, ddt, G["dt_bias_e"], G["a_log_e"], G["d_skip_e"] = _ssd_bwd(
        dy, S["states"], S["xbc"], S["proj"], P["dt_bias_e"], P["a_log_e"], P["d_skip_e"],
        Bl, L, D, dtc, name=f"ssd_bwd_{l}")
    dact = jnp.concatenate([dxs, dB, dC], axis=1)
    dxbc, G["conv_w"], G["conv_b"] = _conv_bwd(dact, S["proj"], P["conv_w"], P["conv_b"], Bl, L, D, Dc,
                                               name=f"conv_bwd_{l}")
    dproj = jnp.concatenate([dz, dxbc, dq, dk, dv, ddt], axis=1)
    (dh,) = _mm(dproj, P["w_in_t"], out_dtypes=[F32], name=f"in_proj_bwd_{l}")
    (G["w_in"],) = _mm(S["h"], dproj, ta=True, out_dtypes=[F32], name=f"in_proj_dw_{l}")
    dx, G["norm_mix_g"] = _rmsnorm_bwd(dh, S["x"], P["norm_mix_g"], dx1, name=f"norm_mix_bwd_{l}")
    return dx, G


def _pad_rows(a, rows):
    return jnp.concatenate([a, jnp.zeros((rows - a.shape[0],) + a.shape[1:], a.dtype)], axis=0)


def kernel(x, norm_mix_g, w_in, conv_w, conv_b, dt_bias, a_log, d_skip, ssd_norm_g, att_norm_g, w_out, norm_mlp_g, w_up, w_down, final_norm_g, loss_target, m_norm_mix_g, m_w_in, m_conv_w, m_conv_b, m_dt_bias, m_a_log, m_d_skip, m_ssd_norm_g, m_att_norm_g, m_w_out, m_norm_mlp_g, m_w_up, m_w_down, m_final_norm_g, v_norm_mix_g, v_w_in, v_conv_w, v_conv_b, v_dt_bias, v_a_log, v_d_skip, v_ssd_norm_g, v_att_norm_g, v_w_out, v_norm_mlp_g, v_w_up, v_w_down, v_final_norm_g):
    Bl, L, D = x.shape
    T = Bl * L
    depth = w_in.shape[0]
    H = D // HEAD_DIM
    Dc = D + 2 * SSM_GROUPS * SSM_STATE
    Dff = w_up.shape[2] * N_DEV
    dims = (Bl, L, D)

    g_in, g_out, g_up, g_down, g_conv = _all_gather(
        [w_in.astype(BF16), w_out.astype(BF16), w_up.astype(BF16), w_down.astype(BF16), conv_w],
        name="gather_weights")
    W_in = g_in.transpose(1, 2, 0, 3).reshape(depth, D, -1)
    o = [0, D, D + Dc, D + Dc + H, 2 * D + Dc + H, 3 * D + Dc + H, 4 * D + Dc + H]
    wz, wxbc, wdt, wq, wk, wv = [W_in[:, :, o[i]:o[i + 1]] for i in range(6)]
    W_pack = jnp.concatenate([wz, wxbc, wq, wk, wv, jnp.repeat(wdt, HEAD_DIM, axis=2)], axis=2)
    W_out = g_out.transpose(1, 0, 2, 3).reshape(depth, 2 * D, D)
    W_up = g_up.transpose(1, 2, 0, 3).reshape(depth, D, Dff)
    W_down = g_down.transpose(1, 0, 2, 3).reshape(depth, Dff, D)
    Cw = g_conv.transpose(1, 2, 0, 3).reshape(depth, CONV_WIDTH, Dc)
    rep = lambda a: jnp.repeat(a, HEAD_DIM, axis=1)
    dt_bias_e, a_log_e, d_skip_e = rep(dt_bias), rep(a_log), rep(d_skip)

    params = []
    for l in range(depth):
        params.append(dict(
            norm_mix_g=norm_mix_g[l], w_in=W_pack[l], w_in_t=W_pack[l].T, conv_w=Cw[l], conv_b=conv_b[l],
            dt_bias_e=dt_bias_e[l:l + 1], a_log_e=a_log_e[l:l + 1], d_skip_e=d_skip_e[l:l + 1],
            ssd_norm_g=ssd_norm_g[l], att_norm_g=att_norm_g[l], w_out=W_out[l], w_out_t=W_out[l].T,
            norm_mlp_g=norm_mlp_g[l], w_up=W_up[l], w_up_t=W_up[l].T, w_down=W_down[l], w_down_t=W_down[l].T))

    xa = x.reshape(T, D)
    saved = []
    for l in range(depth):
        xa, s = _layer_fwd(xa, params[l], dims, l)
        saved.append(s)
    dx, g_final, loss_part = _loss_head(xa, final_norm_g, loss_target.reshape(T, D), name="loss_head")

    grads = [None] * depth
    for l in reversed(range(depth)):
        dx, grads[l] = _layer_bwd(dx, saved[l], params[l], dims, l)
    grad_x = dx.reshape(Bl, L, D)

    st = lambda k: jnp.stack([grads[l][k] for l in range(depth)])
    gW = st("w_in")
    gdt = _collapse_heads(gW[:, :, 4 * D + Dc:].reshape(depth * D, D), name="collapse_w_dt")
    gdt = gdt[:, :H].reshape(depth, D, H)
    gW_in = jnp.concatenate([gW[:, :, :D + Dc], gdt, gW[:, :, D + Dc:4 * D + Dc]], axis=2)
    n_in = w_in.shape[2]
    p_in = gW_in.reshape(depth, D, N_DEV, n_in).transpose(2, 0, 1, 3).astype(BF16)
    p_out = st("w_out").reshape(depth, N_DEV, 2 * D // N_DEV, D).transpose(1, 0, 2, 3).astype(BF16)
    p_up = st("w_up").reshape(depth, D, N_DEV, Dff // N_DEV).transpose(2, 0, 1, 3).astype(BF16)
    p_down = st("w_down").reshape(depth, N_DEV, Dff // N_DEV, D).transpose(1, 0, 2, 3).astype(BF16)
    p_conv = st("conv_w").reshape(depth, CONV_WIDTH, N_DEV, Dc // N_DEV).transpose(2, 0, 1, 3).astype(BF16)
    r_in, r_out, r_up, r_down, r_conv = _all_to_all([p_in, p_out, p_up, p_down, p_conv], name="exchange_grads")

    def sharded(pieces, w, m, v, name):
        shp = w.shape
        C = shp[-1]
        res = _adamw_sharded(pieces.reshape(N_DEV, -1, C), w.reshape(-1, C), m.reshape(-1, C),
                             v.reshape(-1, C), name=name)
        return [a.reshape(shp) for a in res]

    u_in = sharded(r_in, w_in, m_w_in, v_w_in, "adamw_w_in")
    u_out = sharded(r_out, w_out, m_w_out, v_w_out, "adamw_w_out")
    u_up = sharded(r_up, w_up, m_w_up, v_w_up, "adamw_w_up")
    u_down = sharded(r_down, w_down, m_w_down, v_w_down, "adamw_w_down")
    u_conv = sharded(r_conv, conv_w, m_conv_w, v_conv_w, "adamw_conv_w")

    rows = lambda a: a.reshape(-1, D)
    direct_names = ["norm_mix_g", "ssd_norm_g", "att_norm_g", "norm_mlp_g", "conv_b"]
    direct_g = [rows(st(k)) for k in direct_names] + [g_final]
    exp_names = ["dt_bias_e", "a_log_e", "d_skip_e"]
    exp_g = [rows(st(k)) for k in exp_names]
    n_direct = sum(a.shape[0] for a in direct_g)
    nd = -(-n_direct // 8) * 8
    ne = -(-3 * depth // 8) * 8
    part = jnp.concatenate([_pad_rows(jnp.concatenate(direct_g, axis=0), nd),
                            _pad_rows(jnp.concatenate(exp_g, axis=0), ne),
                            _pad_rows(loss_part, 8)], axis=0)
    (parts,) = _all_gather([part], name="gather_small_grads")

    def direct_pack(ws):
        return _pad_rows(jnp.concatenate([rows(a) for a in ws], axis=0), nd)

    def exp_pack(ws):
        a = jnp.concatenate(ws, axis=0)
        a = jnp.concatenate([a, jnp.zeros((a.shape[0], LANES - H), F32)], axis=1)
        return _pad_rows(a, ne)

    res = _small_update(
        parts,
        direct_pack([norm_mix_g, ssd_norm_g, att_norm_g, norm_mlp_g, conv_b, final_norm_g]),
        direct_pack([m_norm_mix_g, m_ssd_norm_g, m_att_norm_g, m_norm_mlp_g, m_conv_b, m_final_norm_g]),
        direct_pack([v_norm_mix_g, v_ssd_norm_g, v_att_norm_g, v_norm_mlp_g, v_conv_b, v_final_norm_g]),
        exp_pack([dt_bias, a_log, d_skip]), exp_pack([m_dt_bias, m_a_log, m_d_skip]),
        exp_pack([v_dt_bias, v_a_log, v_d_skip]), nd, ne, D, name="small_update")
    loss = res[8][0, 0]

    def unpack(kind):
        d, e = res[kind], res[4 + kind]
        out = {}
        r0 = 0
        for nm, shp in [("norm_mix_g", (depth, D)), ("ssd_norm_g", (depth, D)), ("att_norm_g", (depth, D)),
                        ("norm_mlp_g", (depth, D)), ("conv_b", (depth, Dc)), ("final_norm_g", (D,))]:
            n = 1
            for s_ in shp:
                n *= s_
            out[nm] = d[r0:r0 + n // D].reshape(shp)
            r0 += n // D
        for i, nm in enumerate(["dt_bias", "a_log", "d_skip"]):
            out[nm] = e[i * depth:(i + 1) * depth, :H]
        return out

    names = ["norm_mix_g", "w_in", "conv_w", "conv_b", "dt_bias", "a_log", "d_skip", "ssd_norm_g",
             "att_norm_g", "w_out", "norm_mlp_g", "w_up", "w_down", "final_norm_g"]
    big = {"w_in": u_in, "w_out": u_out, "w_up": u_up, "w_down": u_down, "conv_w": u_conv}
    outs = [loss, grad_x]
    for kind in range(4):
        small = unpack(kind)
        for nm in names:
            outs.append(big[nm][kind] if nm in big else small[nm])
    return tuple(outs)
```

```python
import functools

import jax
import jax.numpy as jnp
from jax import lax
from jax.experimental import pallas as pl
from jax.experimental.pallas import tpu as pltpu

F32 = jnp.float32
BF16 = jnp.bfloat16

HEAD_DIM = 64
SSM_STATE = 128
SSM_GROUPS = 4
CONV_WIDTH = 4
CHUNK = 128
ATT_BLOCK = 256
EPS = 1e-5
LANES = 128
N_DEV = 8
VMEM_LIMIT = 56 * 1024 * 1024

ADAM_LR = 0.001
ADAM_B1 = 0.9
ADAM_B2 = 0.999
ADAM_EPS = 1e-08
ADAM_WD = 0.01
ADAM_STEP = 10

_NT =(((1,), (1,)), ((), ()))
_TN = (((0,), (0,)), ((), ()))


def _blk(n, pref):
    if n <= pref:
        return n
    b = (pref // LANES) * LANES
    while b >= LANES:
        if n % b == 0:
            return b
        b -= LANES
    return n


def _cparams(sem):
    return pltpu.CompilerParams(dimension_semantics=sem, vmem_limit_bytes=VMEM_LIMIT)


def _dot(a, b):
    return jnp.dot(a, b, preferred_element_type=F32)


def _dg(a, b, dims):
    return lax.dot_general(a, b, dims, preferred_element_type=F32)


def _split3(x):
    h = x.astype(BF16)
    r = x - h.astype(F32)
    m = r.astype(BF16)
    l = (r - m.astype(F32)).astype(BF16)
    return h, m, l


def _dot_exact_rhs(x, c):
    h, m, l = _split3(x)
    return _dot(h, c) + _dot(m, c) + _dot(l, c)


def _dot_exact_lhs(c, x, dims):
    h, m, l = _split3(x)
    return _dg(c, h, dims) + _dg(c, m, dims) + _dg(c, l, dims)


def _mask(cond):
    return jnp.where(cond, 1.0, 0.0).astype(BF16)


def _sigmoid(x):
    return 1.0 / (1.0 + jnp.exp(-x))


def _softplus(x):
    return jnp.maximum(x, 0.0) + jnp.log(1.0 + jnp.exp(-jnp.abs(x)))


def _mm(a, b, *, name, out_dtypes, ta=False, epilogue=None, extras=(), tm=1024, tn=512, tk=2048):
    if ta:
        K, M = a.shape
    else:
        M, K = a.shape
    N = b.shape[1]
    assert b.shape[0] == K
    tm, tn, tk = _blk(M, tm), _blk(N, tn), _blk(K, tk)
    nk = K // tk
    n_ex, n_out = len(extras), len(out_dtypes)

    def kern(*refs):
        a_ref, b_ref = refs[0], refs[1]
        ex = refs[2:2 + n_ex]
        outs = refs[2 + n_ex:2 + n_ex + n_out]
        acc = refs[-1]
        k = pl.program_id(2)

        @pl.when(k == 0)
        def _():
            acc[...] = jnp.zeros_like(acc)

        av = a_ref[...].astype(BF16)
        bv = b_ref[...].astype(BF16)
        acc[...] += _dg(av, bv, _TN) if ta else _dot(av, bv)

        @pl.when(k == nk - 1)
        def _():
            r = acc[...]
            vals = epilogue(r, *[e[...] for e in ex]) if epilogue is not None else (r,)
            for o, v in zip(outs, vals):
                o[...] = v.astype(o.dtype)

    if ta:
        a_spec = pl.BlockSpec((tk, tm), lambda j, i, k: (k, i))
    else:
        a_spec = pl.BlockSpec((tm, tk), lambda j, i, k: (i, k))
    b_spec = pl.BlockSpec((tk, tn), lambda j, i, k: (k, j))
    o_spec = pl.BlockSpec((tm, tn), lambda j, i, k: (i, j))
    res = pl.pallas_call(
        kern, name=name,
        out_shape=[jax.ShapeDtypeStruct((M, N), d) for d in out_dtypes],
        grid=(N // tn, M // tm, nk),
        in_specs=[a_spec, b_spec] + [o_spec] * n_ex,
        out_specs=[o_spec] * n_out,
        scratch_shapes=[pltpu.VMEM((tm, tn), F32)],
        compiler_params=_cparams(("parallel", "parallel", "arbitrary")),
    )(a, b, *extras)
    return res


def _rmsnorm_fwd(x, g, *, name):
    T, D = x.shape
    tm = _blk(T, 512)

    def kern(x_ref, g_ref, h_ref):
        xv = x_ref[...]
        r = lax.rsqrt(jnp.mean(xv * xv, axis=-1, keepdims=True) + EPS)
        h_ref[...] = (xv * r * g_ref[...]).astype(h_ref.dtype)

    return pl.pallas_call(
        kern, name=name, out_shape=jax.ShapeDtypeStruct((T, D), BF16), grid=(T // tm,),
        in_specs=[pl.BlockSpec((tm, D), lambda i: (i, 0)), pl.BlockSpec((1, D), lambda i: (0, 0))],
        out_specs=pl.BlockSpec((tm, D), lambda i: (i, 0)),
        compiler_params=_cparams(("parallel",)),
    )(x, g.reshape(1, D))


def _rmsnorm_bwd(dh, x, g, dres, *, name):
    T, D = x.shape
    tm = _blk(T, 512)

    def kern(dh_ref, x_ref, g_ref, dres_ref, dx_ref, dg_ref):
        @pl.when(pl.program_id(0) == 0)
        def _():
            dg_ref[...] = jnp.zeros_like(dg_ref)

        xv = x_ref[...]
        r = lax.rsqrt(jnp.mean(xv * xv, axis=-1, keepdims=True) + EPS)
        xn = xv * r
        dy = dh_ref[...].astype(F32)
        dyg = dy * g_ref[...]
        dx = r * (dyg - xn * jnp.mean(dyg * xn, axis=-1, keepdims=True))
        dx_ref[...] = dres_ref[...] + dx
        dg_ref[...] += jnp.sum(dy * xn, axis=0, keepdims=True)

    row = pl.BlockSpec((tm, D), lambda i: (i, 0))
    vec = pl.BlockSpec((1, D), lambda i: (0, 0))
    return pl.pallas_call(
        kern, name=name,
        out_shape=[jax.ShapeDtypeStruct((T, D), F32), jax.ShapeDtypeStruct((1, D), F32)],
        grid=(T // tm,), in_specs=[row, row, vec, row], out_specs=[row, vec],
        compiler_params=_cparams(("arbitrary",)),
    )(dh, x, g.reshape(1, D), dres)


def _loss_head(x, g, target, *, name):
    T, D = x.shape
    tm = _blk(T, 512)

    def kern(x_ref, g_ref, t_ref, dx_ref, dg_ref, loss_ref):
        @pl.when(pl.program_id(0) == 0)
        def _():
            dg_ref[...] = jnp.zeros_like(dg_ref)
            loss_ref[...] = jnp.zeros_like(loss_ref)

        xv = x_ref[...]
        gv = g_ref[...]
        r = lax.rsqrt(jnp.mean(xv * xv, axis=-1, keepdims=True) + EPS)
        xn = xv * r
        err = xn * gv - t_ref[...]
        loss_ref[...] += jnp.sum(err * err, axis=0, keepdims=True) * (0.5 / D)
        dy = err * (1.0 / D)
        dyg = dy * gv
        dx_ref[...] = r * (dyg - xn * jnp.mean(dyg * xn, axis=-1, keepdims=True))
        dg_ref[...] += jnp.sum(dy * xn, axis=0, keepdims=True)

    row = pl.BlockSpec((tm, D), lambda i: (i, 0))
    vec = pl.BlockSpec((1, D), lambda i: (0, 0))
    return pl.pallas_call(
        kern, name=name,
        out_shape=[jax.ShapeDtypeStruct((T, D), F32), jax.ShapeDtypeStruct((1, D), F32),
                   jax.ShapeDtypeStruct((1, D), F32)],
        grid=(T // tm,), in_specs=[row, vec, row], out_specs=[row, vec, vec],
        compiler_params=_cparams(("arbitrary",)),
    )(x, g.reshape(1, D), target)


def _shift_down(u, s, L):
    t = lax.broadcasted_iota(jnp.int32, u.shape, 0)
    return jnp.where(t >= s, pltpu.roll(u, s, 0), 0.0)


def _shift_up(u, s, L):
    t = lax.broadcasted_iota(jnp.int32, u.shape, 0)
    return jnp.where(t < L - s, pltpu.roll(u, L - s, 0), 0.0)


def _conv_pre(u, w_ref, b_ref, L):
    pre = u * w_ref[CONV_WIDTH - 1:CONV_WIDTH, :] + b_ref[...]
    for s in range(1, CONV_WIDTH):
        pre = pre + _shift_down(u, s, L) * w_ref[CONV_WIDTH - 1 - s:CONV_WIDTH - s, :]
    return pre


def _conv_fwd(proj, w, b, Bl, L, col0, Dc, *, name):
    cb = LANES
    c0 = col0 // cb

    def kern(u_ref, w_ref, b_ref, o_ref):
        pre = _conv_pre(u_ref[...], w_ref, b_ref, L)
        o_ref[...] = pre * _sigmoid(pre)

    return pl.pallas_call(
        kern, name=name, out_shape=jax.ShapeDtypeStruct((Bl * L, Dc), F32), grid=(Bl, Dc // cb),
        in_specs=[pl.BlockSpec((L, cb), lambda bi, j: (bi, c0 + j)),
                  pl.BlockSpec((CONV_WIDTH, cb), lambda bi, j: (0, j)),
                  pl.BlockSpec((1, cb), lambda bi, j: (0, j))],
        out_specs=pl.BlockSpec((L, cb), lambda bi, j: (bi, j)),
        compiler_params=_cparams(("parallel", "parallel")),
    )(proj, w, b.reshape(1, Dc))


def _conv_bwd(dact, proj, w, b, Bl, L, col0, Dc, *, name):
    cb = LANES
    c0 = col0 // cb

    def kern(d_ref, u_ref, w_ref, b_ref, du_ref, dw_ref, db_ref):
        @pl.when(pl.program_id(1) == 0)
        def _():
            dw_ref[...] = jnp.zeros_like(dw_ref)
            db_ref[...] = jnp.zeros_like(db_ref)

        u = u_ref[...]
        pre = _conv_pre(u, w_ref, b_ref, L)
        sig = _sigmoid(pre)
        dpre = d_ref[...] * (sig * (1.0 + pre * (1.0 - sig)))
        du = dpre * w_ref[CONV_WIDTH - 1:CONV_WIDTH, :]
        dw_ref[CONV_WIDTH - 1:CONV_WIDTH, :] += jnp.sum(dpre * u, axis=0, keepdims=True)
        for s in range(1, CONV_WIDTH):
            k = CONV_WIDTH - 1 - s
            du = du + _shift_up(dpre, s, L) * w_ref[k:k + 1, :]
            dw_ref[k:k + 1, :] += jnp.sum(dpre * _shift_down(u, s, L), axis=0, keepdims=True)
        db_ref[...] += jnp.sum(dpre, axis=0, keepdims=True)
        du_ref[...] = du.astype(du_ref.dtype)

    return pl.pallas_call(
        kern, name=name,
        out_shape=[jax.ShapeDtypeStruct((Bl * L, Dc), BF16),
                   jax.ShapeDtypeStruct((CONV_WIDTH, Dc), F32), jax.ShapeDtypeStruct((1, Dc), F32)],
        grid=(Dc // cb, Bl),
        in_specs=[pl.BlockSpec((L, cb), lambda j, bi: (bi, j)),
                  pl.BlockSpec((L, cb), lambda j, bi: (bi, c0 + j)),
                  pl.BlockSpec((CONV_WIDTH, cb), lambda j, bi: (0, j)),
                  pl.BlockSpec((1, cb), lambda j, bi: (0, j))],
        out_specs=[pl.BlockSpec((L, cb), lambda j, bi: (bi, j)),
                   pl.BlockSpec((CONV_WIDTH, cb), lambda j, bi: (0, j)),
                   pl.BlockSpec((1, cb), lambda j, bi: (0, j))],
        compiler_params=_cparams(("parallel", "arbitrary")),
    )(dact, proj, w, b.reshape(1, Dc))


def _tri_consts():
    r = lax.broadcasted_iota(jnp.int32, (CHUNK, CHUNK), 0)
    c = lax.broadcasted_iota(jnp.int32, (CHUNK, CHUNK), 1)
    lane0 = c < HEAD_DIM
    return r, c, lane0


def _ssd_chunk_common(x, dtr, bias, alog):
    r, c, _ = _tri_consts()
    dt = _softplus(dtr + bias)
    a = dt * (-jnp.exp(alog))
    tril = _mask(c <= r)
    E = _dot_exact_lhs(tril, a, (((1,), (0,)), ((), ())))
    return dt, E, E.T


def _ssd_head_decay(E, ET, h):
    r, c, _ = _tri_consts()
    cm = jnp.broadcast_to(E[:, h * HEAD_DIM:h * HEAD_DIM + 1], (CHUNK, CHUNK))
    rm = jnp.broadcast_to(ET[h * HEAD_DIM:h * HEAD_DIM + 1, :], (CHUNK, CHUNK))
    return jnp.where(c <= r, jnp.exp(jnp.minimum(cm - rm, 0.0)), 0.0)


def _ssd_fwd(xbc, proj, bias_e, alog_e, Bl, L, D, dt_col0, *, name):
    nc = L // CHUNK
    npairs = D // LANES
    ppg = npairs // SSM_GROUPS
    gw = ppg * LANES
    b0 = D // LANES
    c0 = (D + SSM_GROUPS * SSM_STATE) // LANES
    d0 = dt_col0 // gw

    def kern(x_ref, b_ref, c_ref, dtr_ref, bias_ref, alog_ref, y_ref, st_ref, h_scr):
        _, _, lane0 = _tri_consts()
        h_scr[...] = jnp.zeros_like(h_scr)

        def chunk(ci, carry):
            rows = pl.ds(pl.multiple_of(ci * CHUNK, CHUNK), CHUNK)
            Bb = b_ref[rows, :].astype(BF16)
            Cb = c_ref[rows, :].astype(BF16)
            CB = _dg(Cb, Bb, _NT)
            for p in range(ppg):
                lanes = pl.ds(p * LANES, LANES)
                x = x_ref[rows, lanes]
                dt, E, ET = _ssd_chunk_common(x, dtr_ref[rows, lanes], bias_ref[:, lanes], alog_ref[:, lanes])
                xdt = x * dt
                xdtb = xdt.astype(BF16)
                ys = []
                for h in range(2):
                    Wh = (CB * _ssd_head_decay(E, ET, h)).astype(BF16)
                    ys.append(_dot(Wh, xdtb))
                hprev = h_scr[p]
                y = jnp.where(lane0, ys[0], ys[1]) + _dg(Cb, hprev.astype(BF16), _NT) * jnp.exp(E)
                ds = jnp.exp(jnp.broadcast_to(E[CHUNK - 1:CHUNK, :], (CHUNK, LANES)) - E)
                states = _dg((xdt * ds).astype(BF16), Bb, _TN)
                cd = jnp.exp(jnp.broadcast_to(ET[:, CHUNK - 1:CHUNK], (LANES, SSM_STATE)))
                st_ref[ci, p] = hprev
                h_scr[p] = hprev * cd + states
                y_ref[rows, lanes] = y
            return carry

        lax.fori_loop(0, nc, chunk, 0)

    return pl.pallas_call(
        kern, name=name,
        out_shape=[jax.ShapeDtypeStruct((Bl * L, D), F32),
                   jax.ShapeDtypeStruct((Bl, SSM_GROUPS, nc, ppg, LANES, SSM_STATE), F32)],
        grid=(Bl, SSM_GROUPS),
        in_specs=[pl.BlockSpec((L, gw), lambda bi, g: (bi, g)),
                  pl.BlockSpec((L, SSM_STATE), lambda bi, g: (bi, b0 + g)),
                  pl.BlockSpec((L, SSM_STATE), lambda bi, g: (bi, c0 + g)),
                  pl.BlockSpec((L, gw), lambda bi, g: (bi, d0 + g)),
                  pl.BlockSpec((1, gw), lambda bi, g: (0, g)),
                  pl.BlockSpec((1, gw), lambda bi, g: (0, g))],
        out_specs=[pl.BlockSpec((L, gw), lambda bi, g: (bi, g)),
                   pl.BlockSpec((None, None, nc, ppg, LANES, SSM_STATE), lambda bi, g: (bi, g, 0, 0, 0, 0))],
        scratch_shapes=[pltpu.VMEM((ppg, LANES, SSM_STATE), F32)],
        compiler_params=_cparams(("parallel", "parallel")),
    )(xbc, xbc, xbc, proj, bias_e, alog_e)


def _ssd_bwd(dy, states, xbc, proj, bias_e, alog_e, dskip_e, Bl, L, D, dt_col0, *, name):
    nc = L // CHUNK
    npairs = D // LANES
    ppg = npairs // SSM_GROUPS
    gw = ppg * LANES
    b0 = D // LANES
    c0 = (D + SSM_GROUPS * SSM_STATE) // LANES
    d0 = dt_col0 // gw

    def kern(dy_ref, st_ref, x_ref, b_ref, c_ref, dtr_ref, bias_ref, alog_ref, dsk_ref,
             dx_ref, db_ref, dc_ref, ddt_ref, gbias_ref, galog_ref, gdsk_ref, g_scr):
        r, c, lane0 = _tri_consts()
        m0 = lane0.astype(F32)
        masks = (m0, 1.0 - m0)
        triu = _mask(c >= r)
        trils = _mask(c < r)
        ones = jnp.ones((CHUNK, LANES), BF16)
        g_scr[...] = jnp.zeros_like(g_scr)

        @pl.when(pl.program_id(1) == 0)
        def _():
            gbias_ref[...] = jnp.zeros_like(gbias_ref)
            galog_ref[...] = jnp.zeros_like(galog_ref)
            gdsk_ref[...] = jnp.zeros_like(gdsk_ref)

        def chunk(i, carry):
            ci = nc - 1 - i
            rows = pl.ds(pl.multiple_of(ci * CHUNK, CHUNK), CHUNK)
            Bb = b_ref[rows, :].astype(BF16)
            Cb = c_ref[rows, :].astype(BF16)
            CB = _dg(Cb, Bb, _NT)
            gsum = jnp.zeros((CHUNK, CHUNK), F32)
            dB = jnp.zeros((CHUNK, SSM_STATE), F32)
            dC = jnp.zeros((CHUNK, SSM_STATE), F32)
            for p in range(ppg):
                lanes = pl.ds(p * LANES, LANES)
                x = x_ref[rows, lanes]
                dtr = dtr_ref[rows, lanes]
                bias = bias_ref[:, lanes]
                alog = alog_ref[:, lanes]
                dt, E, ET = _ssd_chunk_common(x, dtr, bias, alog)
                xdt = x * dt
                xdtb = xdt.astype(BF16)
                dyv = dy_ref[rows, lanes]
                dyb = dyv.astype(BF16)
                hprev = st_ref[ci, p]
                hpb = hprev.astype(BF16)
                g = g_scr[p]
                gb = g.astype(BF16)
                sd = jnp.exp(E)
                ds = jnp.exp(jnp.broadcast_to(E[CHUNK - 1:CHUNK, :], (CHUNK, LANES)) - E)
                t1 = []
                dE = dyv * (_dg(Cb, hpb, _NT) * sd)
                for h in range(2):
                    Lh = _ssd_head_decay(E, ET, h)
                    Wh = (CB * Lh).astype(BF16)
                    t1.append(_dg(Wh, dyb, _TN))
                    Gh = Lh * _dg((dyv * masks[h]).astype(BF16), xdtb, _NT)
                    gsum = gsum + Gh
                    Qh = Gh * CB
                    dE = dE + _dot_exact_rhs(Qh - Qh.T, _mask(c == h * HEAD_DIM))
                dxst = _dg(Bb, gb, _NT) * ds
                dxdt = jnp.where(lane0, t1[0], t1[1]) + dxst
                dysd = (dyv * sd).astype(BF16)
                dC = dC + _dot(dysd, hpb)
                dB = dB + _dot((xdt * ds).astype(BF16), gb)
                cd = jnp.exp(jnp.broadcast_to(ET[:, CHUNK - 1:CHUNK], (LANES, SSM_STATE)))
                g_scr[p] = g * cd + _dg(dysd, Cb, _TN)
                nn = (((1,), (0,)), ((), ()))
                da = (_dot_exact_lhs(triu, dE, nn) + _dot_exact_lhs(trils, xdt * dxst, nn)
                      + _dot_exact_lhs(ones, g * cd * hprev, _NT))
                aneg = -jnp.exp(alog)
                ddt = da * aneg + dxdt * x
                ddtr = ddt * _sigmoid(dtr + bias)
                dx_ref[rows, lanes] = dxdt * dt + dyv * dsk_ref[:, lanes]
                ddt_ref[rows, lanes] = ddtr.astype(ddt_ref.dtype)
                gbias_ref[:, lanes] += jnp.sum(ddtr, axis=0, keepdims=True)
                galog_ref[:, lanes] += jnp.sum(da * dt, axis=0, keepdims=True) * aneg
                gdsk_ref[:, lanes] += jnp.sum(dyv * x, axis=0, keepdims=True)
            gsb = gsum.astype(BF16)
            db_ref[rows, :] = dB + _dg(gsb, Cb, _TN)
            dc_ref[rows, :] = dC + _dot(gsb, Bb)
            return carry

        lax.fori_loop(0, nc, chunk, 0)

    T = Bl * L
    xspec = pl.BlockSpec((L, gw), lambda g, bi: (bi, g))
    nspec = pl.BlockSpec((L, SSM_STATE), lambda g, bi: (bi, g))
    vspec = pl.BlockSpec((1, gw), lambda g, bi: (0, g))
    return pl.pallas_call(
        kern, name=name,
        out_shape=[jax.ShapeDtypeStruct((T, D), F32),
                   jax.ShapeDtypeStruct((T, SSM_GROUPS * SSM_STATE), F32),
                   jax.ShapeDtypeStruct((T, SSM_GROUPS * SSM_STATE), F32),
                   jax.ShapeDtypeStruct((T, D), BF16),
                   jax.ShapeDtypeStruct((1, D), F32), jax.ShapeDtypeStruct((1, D), F32),
                   jax.ShapeDtypeStruct((1, D), F32)],
        grid=(SSM_GROUPS, Bl),
        in_specs=[xspec,
                  pl.BlockSpec((None, None, nc, ppg, LANES, SSM_STATE), lambda g, bi: (bi, g, 0, 0, 0, 0)),
                  xspec,
                  pl.BlockSpec((L, SSM_STATE), lambda g, bi: (bi, b0 + g)),
                  pl.BlockSpec((L, SSM_STATE), lambda g, bi: (bi, c0 + g)),
                  pl.BlockSpec((L, gw), lambda g, bi: (bi, d0 + g)),
                  vspec, vspec, vspec],
        out_specs=[xspec, nspec, nspec, xspec, vspec, vspec, vspec],
        scratch_shapes=[pltpu.VMEM((ppg, LANES, SSM_STATE), F32)],
        compiler_params=_cparams(("parallel", "arbitrary")),
    )(dy, states, xbc, xbc, xbc, proj, bias_e, alog_e, dskip_e)


def _att_consts():
    r = lax.broadcasted_iota(jnp.int32, (ATT_BLOCK, ATT_BLOCK), 0)
    c = lax.broadcasted_iota(jnp.int32, (ATT_BLOCK, ATT_BLOCK), 1)
    return r, c


def _wide(x):
    return jnp.concatenate([x] * (ATT_BLOCK // LANES), axis=1)


def _att_logs(z):
    lp = jnp.log(1.0 + jnp.exp(-jnp.abs(z)))
    lb = jnp.minimum(z, 0.0) - lp
    l1m = -(jnp.maximum(z, 0.0) + lp)
    return lb, l1m


def _cum2(x, mat2):
    hi = x.astype(BF16)
    lo = (x - hi.astype(F32)).astype(BF16)
    r = _dot(hi, mat2) + _dot(lo, mat2)
    return r[:, :ATT_BLOCK], r[:, ATT_BLOCK:]


def _attn_fwd(proj, Bl, L, D, qc, kc, vc, *, name):
    npairs = D // LANES
    nq = L // ATT_BLOCK
    scale = HEAD_DIM ** -0.5

    def kern(q_ref, k_ref, v_ref, o_ref, tot_ref, qm, kb, vb):
        r, c = _att_consts()
        tri = r > c
        lrow = lax.broadcasted_iota(jnp.int32, (1, LANES), 1) < HEAD_DIM
        q = q_ref[...] * scale
        qm[0] = jnp.where(lrow, q, 0.0).astype(BF16)
        qm[1] = jnp.where(lrow, 0.0, q).astype(BF16)
        kb[...] = k_ref[...].astype(BF16)
        vb[...] = v_ref[...].astype(BF16)
        upper = _mask(r > c)
        mat2 = jnp.concatenate([upper, jnp.ones((ATT_BLOCK, LANES), BF16)], axis=1)

        def tile(qs, ks, st, diag):
            new = []
            for h in range(2):
                carry, acc = st[2 * h], st[2 * h + 1]
                z = _dg(qm[h, qs, :], kb[ks, :], _NT)
                lb, l1m = _att_logs(z)
                if diag:
                    l1m = jnp.where(tri, l1m, 0.0)
                suf, rs = _cum2(l1m, mat2)
                att = jnp.exp(lb + suf + _wide(carry))
                if diag:
                    att = jnp.where(tri, att, 0.0)
                acc = acc + _dot(att.astype(BF16), vb[ks, :])
                new += [carry + rs, acc]
            return tuple(new)

        def qblock(qi, carry):
            qs = pl.ds(pl.multiple_of(qi * ATT_BLOCK, ATT_BLOCK), ATT_BLOCK)
            zero = jnp.zeros((ATT_BLOCK, LANES), F32)
            st = tile(qs, qs, (zero, zero, zero, zero), True)

            def kblock(j, st):
                kj = qi - 1 - j
                ks = pl.ds(pl.multiple_of(kj * ATT_BLOCK, ATT_BLOCK), ATT_BLOCK)
                return tile(qs, ks, st, False)

            st = lax.fori_loop(0, qi, kblock, st)
            o_ref[qs, :] = jnp.where(lrow, st[1], st[3])
            tot_ref[0, qs, :] = st[0]
            tot_ref[1, qs, :] = st[2]
            return carry

        lax.fori_loop(0, nq, qblock, 0)

    spec = lambda col: pl.BlockSpec((L, LANES), lambda bi, p: (bi, col // LANES + p))
    return pl.pallas_call(
        kern, name=name,
        out_shape=[jax.ShapeDtypeStruct((Bl * L, D), F32),
                   jax.ShapeDtypeStruct((Bl, npairs, 2, L, LANES), F32)],
        grid=(Bl, npairs),
        in_specs=[spec(qc), spec(kc), spec(vc)],
        out_specs=[pl.BlockSpec((L, LANES), lambda bi, p: (bi, p)),
                   pl.BlockSpec((None, None, 2, L, LANES), lambda bi, p: (bi, p, 0, 0, 0))],
        scratch_shapes=[pltpu.VMEM((2, L, LANES), BF16), pltpu.VMEM((L, LANES), BF16),
                        pltpu.VMEM((L, LANES), BF16)],
        compiler_params=_cparams(("parallel", "parallel")),
    )(proj, proj, proj)


def _attn_bwd(dout, tot, proj, Bl, L, D, qc, kc, vc, *, name):
    npairs = D // LANES
    nq = L // ATT_BLOCK
    scale = HEAD_DIM ** -0.5

    def kern(do_ref, tot_ref, q_ref, k_ref, v_ref, dq_ref, dk_ref, dv_ref, qm, km, kb, vb, dom, dkacc, dvacc):
        r, c = _att_consts()
        tri = r > c
        lrow = lax.broadcasted_iota(jnp.int32, (1, LANES), 1) < HEAD_DIM
        q = q_ref[...] * scale
        qm[0] = jnp.where(lrow, q, 0.0).astype(BF16)
        qm[1] = jnp.where(lrow, 0.0, q).astype(BF16)
        k = k_ref[...]
        kb[...] = k.astype(BF16)
        km[0] = jnp.where(lrow, k, 0.0).astype(BF16)
        km[1] = jnp.where(lrow, 0.0, k).astype(BF16)
        vb[...] = v_ref[...].astype(BF16)
        do = do_ref[...]
        dom[0] = jnp.where(lrow, do, 0.0).astype(BF16)
        dom[1] = jnp.where(lrow, 0.0, do).astype(BF16)
        dkacc[...] = jnp.zeros_like(dkacc)
        dvacc[...] = jnp.zeros_like(dvacc)
        ones = jnp.ones((ATT_BLOCK, LANES), BF16)
        incl = jnp.concatenate([_mask(r <= c), ones], axis=1)
        strict = jnp.concatenate([_mask(r < c), ones], axis=1)

        def tile(qs, ks, st, diag):
            dq = st[4]
            new = []
            for h in range(2):
                pl_, pe = st[2 * h], st[2 * h + 1]
                z = _dg(qm[h, qs, :], kb[ks, :], _NT)
                lb, l1m = _att_logs(z)
                if diag:
                    l1m = jnp.where(tri, l1m, 0.0)
                pin, rs = _cum2(l1m, incl)
                att = jnp.exp(lb + (_wide(tot_ref[h, qs, :] - pl_) - pin))
                if diag:
                    att = jnp.where(tri, att, 0.0)
                attb = att.astype(BF16)
                dE = att * _dg(dom[h, qs, :], vb[ks, :], _NT)
                r2 = _dot(dE.astype(BF16), strict)
                P = _wide(pe) + r2[:, :ATT_BLOCK]
                dz = dE - jnp.exp(lb) * (dE + P)
                if diag:
                    dz = jnp.where(tri, dz, 0.0)
                dzb = dz.astype(BF16)
                dq = dq + _dot(dzb, km[h, ks, :])
                dkacc[ks, :] += _dg(dzb, qm[h, qs, :], _TN)
                dvacc[ks, :] += _dg(attb, dom[h, qs, :], _TN)
                new += [pl_ + rs, pe + r2[:, ATT_BLOCK:]]
            return tuple(new) + (dq,)

        def qblock(qi, carry):
            qs = pl.ds(pl.multiple_of(qi * ATT_BLOCK, ATT_BLOCK), ATT_BLOCK)
            zero = jnp.zeros((ATT_BLOCK, LANES), F32)

            def kblock(kj, st):
                ks = pl.ds(pl.multiple_of(kj * ATT_BLOCK, ATT_BLOCK), ATT_BLOCK)
                return tile(qs, ks, st, False)

            st = lax.fori_loop(0, qi, kblock, (zero, zero, zero, zero, zero))
            st = tile(qs, qs, st, True)
            dq_ref[qs, :] = (st[4] * scale).astype(dq_ref.dtype)
            return carry

        lax.fori_loop(0, nq, qblock, 0)
        dk_ref[...] = dkacc[...].astype(dk_ref.dtype)
        dv_ref[...] = dvacc[...].astype(dv_ref.dtype)

    spec = lambda col: pl.BlockSpec((L, LANES), lambda bi, p: (bi, col // LANES + p))
    ospec = pl.BlockSpec((L, LANES), lambda bi, p: (bi, p))
    T = Bl * L
    return pl.pallas_call(
        kern, name=name,
        out_shape=[jax.ShapeDtypeStruct((T, D), BF16)] * 3,
        grid=(Bl, npairs),
        in_specs=[ospec, pl.BlockSpec((None, None, 2, L, LANES), lambda bi, p: (bi, p, 0, 0, 0)),
                  spec(qc), spec(kc), spec(vc)],
        out_specs=[ospec, ospec, ospec],
        scratch_shapes=[pltpu.VMEM((2, L, LANES), BF16), pltpu.VMEM((2, L, LANES), BF16),
                        pltpu.VMEM((L, LANES), BF16), pltpu.VMEM((L, LANES), BF16),
                        pltpu.VMEM((2, L, LANES), BF16), pltpu.VMEM((L, LANES), F32),
                        pltpu.VMEM((L, LANES), F32)],
        compiler_params=_cparams(("parallel", "parallel")),
    )(dout, tot, proj, proj, proj)


def _mix_fwd(y, xbc, proj, yatt, dskip_e, g_ssd, g_att, D, *, name):
    T = y.shape[0]
    tm = _blk(T, 256)

    def kern(y_ref, xs_ref, z_ref, ya_ref, dsk_ref, gs_ref, ga_ref, o_ref):
        z = z_ref[...]
        u = (y_ref[...] + xs_ref[...] * dsk_ref[...]) * (z * _sigmoid(z))
        rs = lax.rsqrt(jnp.mean(u * u, axis=-1, keepdims=True) + EPS)
        o_ref[:, :D] = (u * rs * gs_ref[...]).astype(o_ref.dtype)
        ya = ya_ref[...]
        ra = lax.rsqrt(jnp.mean(ya * ya, axis=-1, keepdims=True) + EPS)
        o_ref[:, D:] = (ya * ra * ga_ref[...]).astype(o_ref.dtype)

    row = pl.BlockSpec((tm, D), lambda i: (i, 0))
    vec = pl.BlockSpec((1, D), lambda i: (0, 0))
    return pl.pallas_call(
        kern, name=name, out_shape=jax.ShapeDtypeStruct((T, 2 * D), BF16), grid=(T // tm,),
        in_specs=[row, row, row, row, vec, vec, vec],
        out_specs=pl.BlockSpec((tm, 2 * D), lambda i: (i, 0)),
        compiler_params=_cparams(("parallel",)),
    )(y, xbc, proj, yatt, dskip_e, g_ssd.reshape(1, D), g_att.reshape(1, D))


def _mix_bwd(dmix, y, xbc, proj, yatt, dskip_e, g_ssd, g_att, D, *, name):
    T = y.shape[0]
    tm = _blk(T, 256)

    def kern(d_ref, y_ref, xs_ref, z_ref, ya_ref, dsk_ref, gs_ref, ga_ref,
             dz_ref, dy_ref, dya_ref, dgs_ref, dga_ref):
        @pl.when(pl.program_id(0) == 0)
        def _():
            dgs_ref[...] = jnp.zeros_like(dgs_ref)
            dga_ref[...] = jnp.zeros_like(dga_ref)

        z = z_ref[...]
        sig = _sigmoid(z)
        sz = z * sig
        ytot = y_ref[...] + xs_ref[...] * dsk_ref[...]
        u = ytot * sz
        rs = lax.rsqrt(jnp.mean(u * u, axis=-1, keepdims=True) + EPS)
        un = u * rs
        do = d_ref[:, :D]
        dog = do * gs_ref[...]
        du = rs * (dog - un * jnp.mean(dog * un, axis=-1, keepdims=True))
        dgs_ref[...] += jnp.sum(do * un, axis=0, keepdims=True)
        dy_ref[...] = du * sz
        dz_ref[...] = (du * ytot * (sig * (1.0 + z * (1.0 - sig)))).astype(dz_ref.dtype)
        ya = ya_ref[...]
        ra = lax.rsqrt(jnp.mean(ya * ya, axis=-1, keepdims=True) + EPS)
        yan = ya * ra
        da = d_ref[:, D:]
        dag = da * ga_ref[...]
        dya_ref[...] = ra * (dag - yan * jnp.mean(dag * yan, axis=-1, keepdims=True))
        dga_ref[...] += jnp.sum(da * yan, axis=0, keepdims=True)

    row = pl.BlockSpec((tm, D), lambda i: (i, 0))
    vec = pl.BlockSpec((1, D), lambda i: (0, 0))
    return pl.pallas_call(
        kern, name=name,
        out_shape=[jax.ShapeDtypeStruct((T, D), BF16), jax.ShapeDtypeStruct((T, D), F32),
                   jax.ShapeDtypeStruct((T, D), F32), jax.ShapeDtypeStruct((1, D), F32),
                   jax.ShapeDtypeStruct((1, D), F32)],
        grid=(T // tm,),
        in_specs=[pl.BlockSpec((tm, 2 * D), lambda i: (i, 0)), row, row, row, row, vec, vec, vec],
        out_specs=[row, row, row, vec, vec],
        compiler_params=_cparams(("arbitrary",)),
    )(dmix, y, xbc, proj, yatt, dskip_e, g_ssd.reshape(1, D), g_att.reshape(1, D))


def _head_sum_mat(D):
    i = lax.broadcasted_iota(jnp.int32, (D, LANES), 0)
    c = lax.broadcasted_iota(jnp.int32, (D, LANES), 1)
    return _mask((i >= c * HEAD_DIM) & (i < (c + 1) * HEAD_DIM))


def _collapse_heads(x, *, name):
    R, D = x.shape
    tm = _blk(R, 256)

    def kern(x_ref, o_ref):
        o_ref[...] = _dot_exact_rhs(x_ref[...], _head_sum_mat(D))

    return pl.pallas_call(
        kern, name=name, out_shape=jax.ShapeDtypeStruct((R, LANES), F32), grid=(R // tm,),
        in_specs=[pl.BlockSpec((tm, D), lambda i: (i, 0))],
        out_specs=pl.BlockSpec((tm, LANES), lambda i: (i, 0)),
        compiler_params=_cparams(("parallel",)),
    )(x)


def _adam_math(w, g, m, v):
    m = ADAM_B1 * m + (1.0 - ADAM_B1) * g
    v = ADAM_B2 * v + (1.0 - ADAM_B2) * (g * g)
    m_hat = m / (1.0 - ADAM_B1 ** ADAM_STEP)
    v_hat = v / (1.0 - ADAM_B2 ** ADAM_STEP)
    delta = -ADAM_LR * (m_hat / (jnp.sqrt(v_hat) + ADAM_EPS) + ADAM_WD * w)
    return delta, m, v


def _adamw_sharded(pieces, w, m, v, *, name):
    R, C = w.shape
    tr = _blk(R, 256) if R % 8 == 0 else R

    def kern(p_ref, w_ref, m_ref, v_ref, g_ref, d_ref, nm_ref, nv_ref):
        g = p_ref[0].astype(F32)
        for j in range(1, N_DEV):
            g = g + p_ref[j].astype(F32)
        d, nm, nv = _adam_math(w_ref[...], g, m_ref[...], v_ref[...])
        g_ref[...] = g
        d_ref[...] = d
        nm_ref[...] = nm
        nv_ref[...] = nv

    row = pl.BlockSpec((tr, C), lambda i: (i, 0))
    return pl.pallas_call(
        kern, name=name, out_shape=[jax.ShapeDtypeStruct((R, C), F32)] * 4, grid=(R // tr,),
        in_specs=[pl.BlockSpec((N_DEV, tr, C), lambda i: (0, i, 0)), row, row, row],
        out_specs=[row] * 4,
        compiler_params=_cparams(("parallel",)),
    )(pieces, w, m, v)


def _small_update(parts, wd, md, vd, we, me, ve, nd, ne, D, *, name):
    def kern(p_ref, wd_ref, md_ref, vd_ref, we_ref, me_ref, ve_ref,
             gd_ref, dd_ref, nmd_ref, nvd_ref, ge_ref, de_ref, nme_ref, nve_ref, loss_ref):
        s = p_ref[0]
        for j in range(1, N_DEV):
            s = s + p_ref[j]
        gd = s[:nd]
        d, nm, nv = _adam_math(wd_ref[...], gd, md_ref[...], vd_ref[...])
        gd_ref[...] = gd
        dd_ref[...] = d
        nmd_ref[...] = nm
        nvd_ref[...] = nv
        ge = _dot_exact_rhs(s[nd:nd + ne], _head_sum_mat(D))
        d, nm, nv = _adam_math(we_ref[...], ge, me_ref[...], ve_ref[...])
        ge_ref[...] = ge
        de_ref[...] = d
        nme_ref[...] = nm
        nve_ref[...] = nv
        tot = jnp.sum(jnp.sum(s[nd + ne:], axis=1, keepdims=True), axis=0, keepdims=True)
        loss_ref[...] = jnp.broadcast_to(tot, loss_ref.shape)

    vm = pl.BlockSpec(memory_space=pltpu.VMEM)
    return pl.pallas_call(
        kern, name=name,
        out_shape=[jax.ShapeDtypeStruct((nd, D), F32)] * 4 + [jax.ShapeDtypeStruct((ne, LANES), F32)] * 4
        + [jax.ShapeDtypeStruct((8, LANES), F32)],
        in_specs=[vm] * 7, out_specs=[vm] * 9,
        compiler_params=pltpu.CompilerParams(vmem_limit_bytes=VMEM_LIMIT),
    )(parts, wd, md, vd, we, me, ve)


def _dev_index(p):
    return 4 * p[0] + 2 * p[1] + p[2]


def _all_gather(arrs, *, name):
    n = len(arrs)

    def body(*refs):
        xs, outs = refs[:n], refs[n:2 * n]
        send, recv, loc = refs[2 * n:]
        x, y, c = lax.axis_index("x"), lax.axis_index("y"), lax.axis_index("c")
        me, sib = (x, y, c), (x, y, 1 - c)
        chips = [(1 - x, y), (x, 1 - y), (1 - x, 1 - y)]

        def cp(k, s, block, to, src=None):
            dst = outs[k].at[_dev_index(block)]
            return pltpu.make_async_remote_copy(
                src_ref=dst if src is None else src, dst_ref=dst,
                send_sem=send.at[7 * k + s], recv_sem=recv.at[7 * k + s],
                device_id=to, device_id_type=pl.DeviceIdType.MESH)

        mine = [pltpu.make_async_copy(xs[k], outs[k].at[_dev_index(me)], loc.at[k]) for k in range(n)]
        for m in mine:
            m.start()
        first = []
        for k in range(n):
            first.append(cp(k, 0, me, sib, src=xs[k]))
            for j, ch in enumerate(chips):
                first.append(cp(k, 1 + j, me, (*ch, c), src=xs[k]))
        for f in first:
            f.start()
        passed = []
        for j, ch in enumerate(chips):
            for k in range(n):
                cp(k, 1 + j, (*ch, c), me).wait_recv()
                p = cp(k, 4 + j, (*ch, c), sib)
                p.start()
                passed.append(p)
        for k in range(n):
            cp(k, 0, sib, me).wait_recv()
            for j, ch in enumerate(chips):
                cp(k, 4 + j, (*ch, 1 - c), me).wait_recv()
        for f in first + passed:
            f.wait_send()
        for m in mine:
            m.wait()

    hbm = pl.BlockSpec(memory_space=pl.ANY)
    return pl.pallas_call(
        body, name=name,
        out_shape=[jax.ShapeDtypeStruct((N_DEV,) + a.shape, a.dtype) for a in arrs],
        in_specs=[hbm] * n, out_specs=[hbm] * n,
        scratch_shapes=[pltpu.SemaphoreType.DMA((7 * n,)), pltpu.SemaphoreType.DMA((7 * n,)),
                        pltpu.SemaphoreType.DMA((n,))],
    )(*arrs)


def _all_to_all(arrs, *, name):
    n = len(arrs)
    rels = [(fx, fy, fc) for fx in (0, 1) for fy in (0, 1) for fc in (0, 1) if fx or fy or fc]

    def body(*refs):
        xs, outs = refs[:n], refs[n:2 * n]
        send, recv, loc = refs[2 * n:]
        x, y, c = lax.axis_index("x"), lax.axis_index("y"), lax.axis_index("c")
        me = _dev_index((x, y, c))

        def peer(rel):
            return (1 - x if rel[0] else x, 1 - y if rel[1] else y, 1 - c if rel[2] else c)

        mine = [pltpu.make_async_copy(xs[k].at[me], outs[k].at[me], loc.at[k]) for k in range(n)]
        for m in mine:
            m.start()
        copies = []
        for k in range(n):
            for s, rel in enumerate(rels):
                p = peer(rel)
                copies.append(pltpu.make_async_remote_copy(
                    src_ref=xs[k].at[_dev_index(p)], dst_ref=outs[k].at[me],
                    send_sem=send.at[7 * k + s], recv_sem=recv.at[7 * k + s],
                    device_id=p, device_id_type=pl.DeviceIdType.MESH))
        for cpy in copies:
            cpy.start()
        for k in range(n):
            for s, rel in enumerate(rels):
                p = peer(rel)
                slot = outs[k].at[_dev_index(p)]
                pltpu.make_async_remote_copy(
                    src_ref=slot, dst_ref=slot, send_sem=send.at[7 * k + s], recv_sem=recv.at[7 * k + s],
                    device_id=p, device_id_type=pl.DeviceIdType.MESH).wait_recv()
        for cpy in copies:
            cpy.wait_send()
        for m in mine:
            m.wait()

    hbm = pl.BlockSpec(memory_space=pl.ANY)
    return pl.pallas_call(
        body, name=name,
        out_shape=[jax.ShapeDtypeStruct(a.shape, a.dtype) for a in arrs],
        in_specs=[hbm] * n, out_specs=[hbm] * n,
        scratch_shapes=[pltpu.SemaphoreType.DMA((7 * n,)), pltpu.SemaphoreType.DMA((7 * n,)),
                        pltpu.SemaphoreType.DMA((n,))],
    )(*arrs)


def _layer_fwd(x, P, dims, l):
    Bl, L, D = dims
    Dc = D + 2 * SSM_GROUPS * SSM_STATE
    qc, kc, vc, dtc = D + Dc, 2 * D + Dc, 3 * D + Dc, 4 * D + Dc
    h = _rmsnorm_fwd(x, P["norm_mix_g"], name=f"norm_mix_fwd_{l}")
    (proj,) = _mm(h, P["w_in"], out_dtypes=[F32], name=f"in_proj_{l}")
    xbc = _conv_fwd(proj, P["conv_w"], P["conv_b"], Bl, L, D, Dc, name=f"conv_fwd_{l}")
    y, states = _ssd_fwd(xbc, proj, P["dt_bias_e"], P["a_log_e"], Bl, L, D, dtc, name=f"ssd_fwd_{l}")
    yatt, tot = _attn_fwd(proj, Bl, L, D, qc, kc, vc, name=f"attn_fwd_{l}")
    ymix = _mix_fwd(y, xbc, proj, yatt, P["d_skip_e"], P["ssd_norm_g"], P["att_norm_g"], D, name=f"mix_fwd_{l}")
    (x1,) = _mm(ymix, P["w_out"], out_dtypes=[F32], epilogue=lambda r, res: (r + res,), extras=(x,),
                name=f"out_proj_{l}")
    h2 = _rmsnorm_fwd(x1, P["norm_mlp_g"], name=f"norm_mlp_fwd_{l}")

    def relu2(r):
        a = jnp.maximum(r, 0.0)
        return r, a * a

    u, act = _mm(h2, P["w_up"], out_dtypes=[F32, BF16], epilogue=relu2, name=f"mlp_up_{l}")
    (x2,) = _mm(act, P["w_down"], out_dtypes=[F32], epilogue=lambda r, res: (r + res,), extras=(x1,),
                name=f"mlp_down_{l}")
    saved = dict(x=x, h=h, proj=proj, xbc=xbc, y=y, states=states, yatt=yatt, tot=tot, ymix=ymix,
                 x1=x1, h2=h2, u=u, act=act)
    return x2, saved


def _layer_bwd(dx2, S, P, dims, l):
    Bl, L, D = dims
    Dc = D + 2 * SSM_GROUPS * SSM_STATE
    qc, kc, vc, dtc = D + Dc, 2 * D + Dc, 3 * D + Dc, 4 * D + Dc
    G = {}
    (du,) = _mm(dx2, P["w_down_t"], out_dtypes=[BF16], extras=(S["u"],),
                epilogue=lambda r, u: (r * (2.0 * jnp.maximum(u, 0.0)),), name=f"mlp_down_bwd_{l}")
    (G["w_down"],) = _mm(S["act"], dx2, ta=True, out_dtypes=[F32], name=f"mlp_down_dw_{l}")
    (dh2,) = _mm(du, P["w_up_t"], out_dtypes=[F32], name=f"mlp_up_bwd_{l}")
    (G["w_up"],) = _mm(S["h2"], du, ta=True, out_dtypes=[F32], name=f"mlp_up_dw_{l}")
    dx1, G["norm_mlp_g"] = _rmsnorm_bwd(dh2, S["x1"], P["norm_mlp_g"], dx2, name=f"norm_mlp_bwd_{l}")
    (dmix,) = _mm(dx1, P["w_out_t"], out_dtypes=[F32], name=f"out_proj_bwd_{l}")
    (G["w_out"],) = _mm(S["ymix"], dx1, ta=True, out_dtypes=[F32], name=f"out_proj_dw_{l}")
    dz, dy, dyatt, G["ssd_norm_g"], G["att_norm_g"] = _mix_bwd(
        dmix, S["y"], S["xbc"], S["proj"], S["yatt"], P["d_skip_e"], P["ssd_norm_g"], P["att_norm_g"], D,
        name=f"mix_bwd_{l}")
    dq, dk, dv = _attn_bwd(dyatt, S["tot"], S["proj"], Bl, L, D, qc, kc, vc, name=f"attn_bwd_{l}")
    dxs, dB, dC, ddt, G["dt_bias_e"], G["a_log_e"], G["d_skip_e"] = _ssd_bwd(
        dy, S["states"], S["xbc"], S["proj"], P["dt_bias_e"], P["a_log_e"], P["d_skip_e"],
        Bl, L, D, dtc, name=f"ssd_bwd_{l}")
    dact = jnp.concatenate([dxs, dB, dC], axis=1)
    dxbc, G["conv_w"], G["conv_b"] = _conv_bwd(dact, S["proj"], P["conv_w"], P["conv_b"], Bl, L, D, Dc,
                                               name=f"conv_bwd_{l}")
    dproj = jnp.concatenate([dz, dxbc, dq, dk, dv, ddt], axis=1)
    (dh,) = _mm(dproj, P["w_in_t"], out_dtypes=[F32], name=f"in_proj_bwd_{l}")
    (G["w_in"],) = _mm(S["h"], dproj, ta=True, out_dtypes=[F32], name=f"in_proj_dw_{l}")
    dx, G["norm_mix_g"] = _rmsnorm_bwd(dh, S["x"], P["norm_mix_g"], dx1, name=f"norm_mix_bwd_{l}")
    return dx, G


def _pad_rows(a, rows):
    return jnp.concatenate([a, jnp.zeros((rows - a.shape[0],) + a.shape[1:], a.dtype)], axis=0)


def kernel(x, norm_mix_g, w_in, conv_w, conv_b, dt_bias, a_log, d_skip, ssd_norm_g, att_norm_g, w_out, norm_mlp_g, w_up, w_down, final_norm_g, loss_target, m_norm_mix_g, m_w_in, m_conv_w, m_conv_b, m_dt_bias, m_a_log, m_d_skip, m_ssd_norm_g, m_att_norm_g, m_w_out, m_norm_mlp_g, m_w_up, m_w_down, m_final_norm_g, v_norm_mix_g, v_w_in, v_conv_w, v_conv_b, v_dt_bias, v_a_log, v_d_skip, v_ssd_norm_g, v_att_norm_g, v_w_out, v_norm_mlp_g, v_w_up, v_w_down, v_final_norm_g):
    Bl, L, D = x.shape
    T = Bl * L
    depth = w_in.shape[0]
    H = D // HEAD_DIM
    Dc = D + 2 * SSM_GROUPS * SSM_STATE
    Dff = w_up.shape[2] * N_DEV
    dims = (Bl, L, D)

    g_in, g_out, g_up, g_down, g_conv = _all_gather(
        [w_in.astype(BF16), w_out.astype(BF16), w_up.astype(BF16), w_down.astype(BF16), conv_w],
        name="gather_weights")
    W_in = g_in.transpose(1, 2, 0, 3).reshape(depth, D, -1)
    o = [0, D, D + Dc, D + Dc + H, 2 * D + Dc + H, 3 * D + Dc + H, 4 * D + Dc + H]
    wz, wxbc, wdt, wq, wk, wv = [W_in[:, :, o[i]:o[i + 1]] for i in range(6)]
    W_pack = jnp.concatenate([wz, wxbc, wq, wk, wv, jnp.repeat(wdt, HEAD_DIM, axis=2)], axis=2)
    W_out = g_out.transpose(1, 0, 2, 3).reshape(depth, 2 * D, D)
    W_up = g_up.transpose(1, 2, 0, 3).reshape(depth, D, Dff)
    W_down = g_down.transpose(1, 0, 2, 3).reshape(depth, Dff, D)
    Cw = g_conv.transpose(1, 2, 0, 3).reshape(depth, CONV_WIDTH, Dc)
    rep = lambda a: jnp.repeat(a, HEAD_DIM, axis=1)
    dt_bias_e, a_log_e, d_skip_e = rep(dt_bias), rep(a_log), rep(d_skip)

    params = []
    for l in range(depth):
        params.append(dict(
            norm_mix_g=norm_mix_g[l], w_in=W_pack[l], w_in_t=W_pack[l].T, conv_w=Cw[l], conv_b=conv_b[l],
            dt_bias_e=dt_bias_e[l:l + 1], a_log_e=a_log_e[l:l + 1], d_skip_e=d_skip_e[l:l + 1],
            ssd_norm_g=ssd_norm_g[l], att_norm_g=att_norm_g[l], w_out=W_out[l], w_out_t=W_out[l].T,
            norm_mlp_g=norm_mlp_g[l], w_up=W_up[l], w_up_t=W_up[l].T, w_down=W_down[l], w_down_t=W_down[l].T))

    xa = x.reshape(T, D)
    saved = []
    for l in range(depth):
        xa, s = _layer_fwd(xa, params[l], dims, l)
        saved.append(s)
    dx, g_final, loss_part = _loss_head(xa, final_norm_g, loss_target.reshape(T, D), name="loss_head")

    grads = [None] * depth
    for l in reversed(range(depth)):
        dx, grads[l] = _layer_bwd(dx, saved[l], params[l], dims, l)
    grad_x = dx.reshape(Bl, L, D)

    st = lambda k: jnp.stack([grads[l][k] for l in range(depth)])
    gW = st("w_in")
    gdt = _collapse_heads(gW[:, :, 4 * D + Dc:].reshape(depth * D, D), name="collapse_w_dt")
    gdt = gdt[:, :H].reshape(depth, D, H)
    gW_in = jnp.concatenate([gW[:, :, :D + Dc], gdt, gW[:, :, D + Dc:4 * D + Dc]], axis=2)
    n_in = w_in.shape[2]
    p_in = gW_in.reshape(depth, D, N_DEV, n_in).transpose(2, 0, 1, 3).astype(BF16)
    p_out = st("w_out").reshape(depth, N_DEV, 2 * D // N_DEV, D).transpose(1, 0, 2, 3).astype(BF16)
    p_up = st("w_up").reshape(depth, D, N_DEV, Dff // N_DEV).transpose(2, 0, 1, 3).astype(BF16)
    p_down = st("w_down").reshape(depth, N_DEV, Dff // N_DEV, D).transpose(1, 0, 2, 3).astype(BF16)
    p_conv = st("conv_w").reshape(depth, CONV_WIDTH, N_DEV, Dc // N_DEV).transpose(2, 0, 1, 3).astype(BF16)
    r_in, r_out, r_up, r_down, r_conv = _all_to_all([p_in, p_out, p_up, p_down, p_conv], name="exchange_grads")

    def sharded(pieces, w, m, v, name):
        shp = w.shape
        C = shp[-1]
        res = _adamw_sharded(pieces.reshape(N_DEV, -1, C), w.reshape(-1, C), m.reshape(-1, C),
                             v.reshape(-1, C), name=name)
        return [a.reshape(shp) for a in res]

    u_in = sharded(r_in, w_in, m_w_in, v_w_in, "adamw_w_in")
    u_out = sharded(r_out, w_out, m_w_out, v_w_out, "adamw_w_out")
    u_up = sharded(r_up, w_up, m_w_up, v_w_up, "adamw_w_up")
    u_down = sharded(r_down, w_down, m_w_down, v_w_down, "adamw_w_down")
    u_conv = sharded(r_conv, conv_w, m_conv_w, v_conv_w, "adamw_conv_w")

    rows = lambda a: a.reshape(-1, D)
    direct_names = ["norm_mix_g", "ssd_norm_g", "att_norm_g", "norm_mlp_g", "conv_b"]
    direct_g = [rows(st(k)) for k in direct_names] + [g_final]
    exp_names = ["dt_bias_e", "a_log_e", "d_skip_e"]
    exp_g = [rows(st(k)) for k in exp_names]
    n_direct = sum(a.shape[0] for a in direct_g)
    nd = -(-n_direct // 8) * 8
    ne = -(-3 * depth // 8) * 8
    part = jnp.concatenate([_pad_rows(jnp.concatenate(direct_g, axis=0), nd),
                            _pad_rows(jnp.concatenate(exp_g, axis=0), ne),
                            _pad_rows(loss_part, 8)], axis=0)
    (parts,) = _all_gather([part], name="gather_small_grads")

    def direct_pack(ws):
        return _pad_rows(jnp.concatenate([rows(a) for a in ws], axis=0), nd)

    def exp_pack(ws):
        a = jnp.concatenate(ws, axis=0)
        a = jnp.concatenate([a, jnp.zeros((a.shape[0], LANES - H), F32)], axis=1)
        return _pad_rows(a, ne)

    res = _small_update(
        parts,
        direct_pack([norm_mix_g, ssd_norm_g, att_norm_g, norm_mlp_g, conv_b, final_norm_g]),
        direct_pack([m_norm_mix_g, m_ssd_norm_g, m_att_norm_g, m_norm_mlp_g, m_conv_b, m_final_norm_g]),
        direct_pack([v_norm_mix_g, v_ssd_norm_g, v_att_norm_g, v_norm_mlp_g, v_conv_b, v_final_norm_g]),
        exp_pack([dt_bias, a_log, d_skip]), exp_pack([m_dt_bias, m_a_log, m_d_skip]),
        exp_pack([v_dt_bias, v_a_log, v_d_skip]), nd, ne, D, name="small_update")
    loss = res[8][0, 0]

    def unpack(kind):
        d, e = res[kind], res[4 + kind]
        out = {}
        r0 = 0
        for nm, shp in [("norm_mix_g", (depth, D)), ("ssd_norm_g", (depth, D)), ("att_norm_g", (depth, D)),
                        ("norm_mlp_g", (depth, D)), ("conv_b", (depth, Dc)), ("final_norm_g", (D,))]:
            n = 1
            for s_ in shp:
                n *= s_
            out[nm] = d[r0:r0 + n // D].reshape(shp)
            r0 += n // D
        for i, nm in enumerate(["dt_bias", "a_log", "d_skip"]):
            out[nm] = e[i * depth:(i + 1) * depth, :H]
        return out

    names = ["norm_mix_g", "w_in", "conv_w", "conv_b", "dt_bias", "a_log", "d_skip", "ssd_norm_g",
             "att_norm_g", "w_out", "norm_mlp_g", "w_up", "w_down", "final_norm_g"]
    big = {"w_in": u_in, "w_out": u_out, "w_up": u_up, "w_down": u_down, "conv_w": u_conv}
    outs = [loss, grad_x]
    for kind in range(4):
        small = unpack(kind)
        for nm in names:
            outs.append(big[nm][kind] if nm in big else small[nm])
    return tuple(outs)
```

```python
import functools

import jax
import jax.numpy as jnp
from jax import lax
from jax.experimental import pallas as pl
from jax.experimental.pallas import tpu as pltpu

F32 = jnp.float32
BF16 = jnp.bfloat16

HEAD_DIM = 64
SSM_STATE = 128
SSM_GROUPS = 4
CONV_WIDTH = 4
CHUNK = 128
ATT_BLOCK = 512
ATT_SUB = 256
EPS = 1e-5
LANES = 128
N_DEV = 8
VMEM_LIMIT = 56 * 1024 * 1024

ADAM_LR = 0.001
ADAM_B1 = 0.9
ADAM_B2 = 0.999
ADAM_EPS = 1e-08
ADAM_WD = 0.01
ADAM_STEP = 10

_NT =(((1,), (1,)), ((), ()))
_TN = (((0,), (0,)), ((), ()))


def _blk(n, pref):
    if n <= pref:
        return n
    b = (pref // LANES) * LANES
    while b >= LANES:
        if n % b == 0:
            return b
        b -= LANES
    return n


def _cparams(sem):
    return pltpu.CompilerParams(dimension_semantics=sem, vmem_limit_bytes=VMEM_LIMIT)


def _dot(a, b):
    return jnp.dot(a, b, preferred_element_type=F32)


def _dg(a, b, dims):
    return lax.dot_general(a, b, dims, preferred_element_type=F32)


def _split3(x):
    h = x.astype(BF16)
    r = x - h.astype(F32)
    m = r.astype(BF16)
    l = (r - m.astype(F32)).astype(BF16)
    return h, m, l


def _dot_exact_rhs(x, c):
    h, m, l = _split3(x)
    return _dot(h, c) + _dot(m, c) + _dot(l, c)


def _dot_exact_lhs(c, x, dims):
    h, m, l = _split3(x)
    return _dg(c, h, dims) + _dg(c, m, dims) + _dg(c, l, dims)


def _mask(cond):
    return jnp.where(cond, 1.0, 0.0).astype(BF16)


def _sigmoid(x):
    return 1.0 / (1.0 + jnp.exp(-x))


def _softplus(x):
    return jnp.maximum(x, 0.0) + jnp.log(1.0 + jnp.exp(-jnp.abs(x)))


def _mm(a, b, *, name, out_dtypes, ta=False, epilogue=None, extras=(), tm=1024, tn=512, tk=2048):
    if ta:
        K, M = a.shape
    else:
        M, K = a.shape
    N = b.shape[1]
    assert b.shape[0] == K
    tm, tn, tk = _blk(M, tm), _blk(N, tn), _blk(K, tk)
    nk = K // tk
    n_ex, n_out = len(extras), len(out_dtypes)

    def kern(*refs):
        a_ref, b_ref = refs[0], refs[1]
        ex = refs[2:2 + n_ex]
        outs = refs[2 + n_ex:2 + n_ex + n_out]
        acc = refs[-1]
        k = pl.program_id(2)

        @pl.when(k == 0)
        def _():
            acc[...] = jnp.zeros_like(acc)

        av = a_ref[...].astype(BF16)
        bv = b_ref[...].astype(BF16)
        acc[...] += _dg(av, bv, _TN) if ta else _dot(av, bv)

        @pl.when(k == nk - 1)
        def _():
            r = acc[...]
            vals = epilogue(r, *[e[...] for e in ex]) if epilogue is not None else (r,)
            for o, v in zip(outs, vals):
                o[...] = v.astype(o.dtype)

    if ta:
        a_spec = pl.BlockSpec((tk, tm), lambda j, i, k: (k, i))
    else:
        a_spec = pl.BlockSpec((tm, tk), lambda j, i, k: (i, k))
    b_spec = pl.BlockSpec((tk, tn), lambda j, i, k: (k, j))
    o_spec = pl.BlockSpec((tm, tn), lambda j, i, k: (i, j))
    res = pl.pallas_call(
        kern, name=name,
        out_shape=[jax.ShapeDtypeStruct((M, N), d) for d in out_dtypes],
        grid=(N // tn, M // tm, nk),
        in_specs=[a_spec, b_spec] + [o_spec] * n_ex,
        out_specs=[o_spec] * n_out,
        scratch_shapes=[pltpu.VMEM((tm, tn), F32)],
        compiler_params=_cparams(("parallel", "parallel", "arbitrary")),
    )(a, b, *extras)
    return res


def _rmsnorm_fwd(x, g, *, name):
    T, D = x.shape
    tm = _blk(T, 512)

    def kern(x_ref, g_ref, h_ref):
        xv = x_ref[...]
        r = lax.rsqrt(jnp.mean(xv * xv, axis=-1, keepdims=True) + EPS)
        h_ref[...] = (xv * r * g_ref[...]).astype(h_ref.dtype)

    return pl.pallas_call(
        kern, name=name, out_shape=jax.ShapeDtypeStruct((T, D), BF16), grid=(T // tm,),
        in_specs=[pl.BlockSpec((tm, D), lambda i: (i, 0)), pl.BlockSpec((1, D), lambda i: (0, 0))],
        out_specs=pl.BlockSpec((tm, D), lambda i: (i, 0)),
        compiler_params=_cparams(("parallel",)),
    )(x, g.reshape(1, D))


def _rmsnorm_bwd(dh, x, g, dres, *, name):
    T, D = x.shape
    tm = _blk(T, 512)

    def kern(dh_ref, x_ref, g_ref, dres_ref, dx_ref, dg_ref):
        @pl.when(pl.program_id(0) == 0)
        def _():
            dg_ref[...] = jnp.zeros_like(dg_ref)

        xv = x_ref[...]
        r = lax.rsqrt(jnp.mean(xv * xv, axis=-1, keepdims=True) + EPS)
        xn = xv * r
        dy = dh_ref[...].astype(F32)
        dyg = dy * g_ref[...]
        dx = r * (dyg - xn * jnp.mean(dyg * xn, axis=-1, keepdims=True))
        dx_ref[...] = dres_ref[...] + dx
        dg_ref[...] += jnp.sum(dy * xn, axis=0, keepdims=True)

    row = pl.BlockSpec((tm, D), lambda i: (i, 0))
    vec = pl.BlockSpec((1, D), lambda i: (0, 0))
    return pl.pallas_call(
        kern, name=name,
        out_shape=[jax.ShapeDtypeStruct((T, D), F32), jax.ShapeDtypeStruct((1, D), F32)],
        grid=(T // tm,), in_specs=[row, row, vec, row], out_specs=[row, vec],
        compiler_params=_cparams(("arbitrary",)),
    )(dh, x, g.reshape(1, D), dres)


def _loss_head(x, g, target, *, name):
    T, D = x.shape
    tm = _blk(T, 512)

    def kern(x_ref, g_ref, t_ref, dx_ref, dg_ref, loss_ref):
        @pl.when(pl.program_id(0) == 0)
        def _():
            dg_ref[...] = jnp.zeros_like(dg_ref)
            loss_ref[...] = jnp.zeros_like(loss_ref)

        xv = x_ref[...]
        gv = g_ref[...]
        r = lax.rsqrt(jnp.mean(xv * xv, axis=-1, keepdims=True) + EPS)
        xn = xv * r
        err = xn * gv - t_ref[...]
        loss_ref[...] += jnp.sum(err * err, axis=0, keepdims=True) * (0.5 / D)
        dy = err * (1.0 / D)
        dyg = dy * gv
        dx_ref[...] = r * (dyg - xn * jnp.mean(dyg * xn, axis=-1, keepdims=True))
        dg_ref[...] += jnp.sum(dy * xn, axis=0, keepdims=True)

    row = pl.BlockSpec((tm, D), lambda i: (i, 0))
    vec = pl.BlockSpec((1, D), lambda i: (0, 0))
    return pl.pallas_call(
        kern, name=name,
        out_shape=[jax.ShapeDtypeStruct((T, D), F32), jax.ShapeDtypeStruct((1, D), F32),
                   jax.ShapeDtypeStruct((1, D), F32)],
        grid=(T // tm,), in_specs=[row, vec, row], out_specs=[row, vec, vec],
        compiler_params=_cparams(("arbitrary",)),
    )(x, g.reshape(1, D), target)


def _shift_down(u, s, L):
    t = lax.broadcasted_iota(jnp.int32, u.shape, 0)
    return jnp.where(t >= s, pltpu.roll(u, s, 0), 0.0)


def _shift_up(u, s, L):
    t = lax.broadcasted_iota(jnp.int32, u.shape, 0)
    return jnp.where(t < L - s, pltpu.roll(u, L - s, 0), 0.0)


def _conv_pre(u, w_ref, b_ref, L):
    pre = u * w_ref[CONV_WIDTH - 1:CONV_WIDTH, :] + b_ref[...]
    for s in range(1, CONV_WIDTH):
        pre = pre + _shift_down(u, s, L) * w_ref[CONV_WIDTH - 1 - s:CONV_WIDTH - s, :]
    return pre


def _conv_fwd(proj, w, b, Bl, L, col0, Dc, *, name):
    cb = LANES
    c0 = col0 // cb

    def kern(u_ref, w_ref, b_ref, o_ref):
        pre = _conv_pre(u_ref[...], w_ref, b_ref, L)
        o_ref[...] = pre * _sigmoid(pre)

    return pl.pallas_call(
        kern, name=name, out_shape=jax.ShapeDtypeStruct((Bl * L, Dc), F32), grid=(Bl, Dc // cb),
        in_specs=[pl.BlockSpec((L, cb), lambda bi, j: (bi, c0 + j)),
                  pl.BlockSpec((CONV_WIDTH, cb), lambda bi, j: (0, j)),
                  pl.BlockSpec((1, cb), lambda bi, j: (0, j))],
        out_specs=pl.BlockSpec((L, cb), lambda bi, j: (bi, j)),
        compiler_params=_cparams(("parallel", "parallel")),
    )(proj, w, b.reshape(1, Dc))


def _conv_bwd(dact, proj, w, b, Bl, L, col0, Dc, *, name):
    cb = LANES
    c0 = col0 // cb

    def kern(d_ref, u_ref, w_ref, b_ref, du_ref, dw_ref, db_ref):
        @pl.when(pl.program_id(1) == 0)
        def _():
            dw_ref[...] = jnp.zeros_like(dw_ref)
            db_ref[...] = jnp.zeros_like(db_ref)

        u = u_ref[...]
        pre = _conv_pre(u, w_ref, b_ref, L)
        sig = _sigmoid(pre)
        dpre = d_ref[...] * (sig * (1.0 + pre * (1.0 - sig)))
        du = dpre * w_ref[CONV_WIDTH - 1:CONV_WIDTH, :]
        dw_ref[CONV_WIDTH - 1:CONV_WIDTH, :] += jnp.sum(dpre * u, axis=0, keepdims=True)
        for s in range(1, CONV_WIDTH):
            k = CONV_WIDTH - 1 - s
            du = du + _shift_up(dpre, s, L) * w_ref[k:k + 1, :]
            dw_ref[k:k + 1, :] += jnp.sum(dpre * _shift_down(u, s, L), axis=0, keepdims=True)
        db_ref[...] += jnp.sum(dpre, axis=0, keepdims=True)
        du_ref[...] = du.astype(du_ref.dtype)

    return pl.pallas_call(
        kern, name=name,
        out_shape=[jax.ShapeDtypeStruct((Bl * L, Dc), BF16),
                   jax.ShapeDtypeStruct((CONV_WIDTH, Dc), F32), jax.ShapeDtypeStruct((1, Dc), F32)],
        grid=(Dc // cb, Bl),
        in_specs=[pl.BlockSpec((L, cb), lambda j, bi: (bi, j)),
                  pl.BlockSpec((L, cb), lambda j, bi: (bi, c0 + j)),
                  pl.BlockSpec((CONV_WIDTH, cb), lambda j, bi: (0, j)),
                  pl.BlockSpec((1, cb), lambda j, bi: (0, j))],
        out_specs=[pl.BlockSpec((L, cb), lambda j, bi: (bi, j)),
                   pl.BlockSpec((CONV_WIDTH, cb), lambda j, bi: (0, j)),
                   pl.BlockSpec((1, cb), lambda j, bi: (0, j))],
        compiler_params=_cparams(("parallel", "arbitrary")),
    )(dact, proj, w, b.reshape(1, Dc))


def _tri_consts():
    r = lax.broadcasted_iota(jnp.int32, (CHUNK, CHUNK), 0)
    c = lax.broadcasted_iota(jnp.int32, (CHUNK, CHUNK), 1)
    lane0 = c < HEAD_DIM
    return r, c, lane0


def _ssd_chunk_common(x, dtr, bias, alog):
    r, c, _ = _tri_consts()
    dt = _softplus(dtr + bias)
    a = dt * (-jnp.exp(alog))
    tril = _mask(c <= r)
    E = _dot_exact_lhs(tril, a, (((1,), (0,)), ((), ())))
    return dt, E, E.T


def _ssd_head_decay(E, ET, h):
    r, c, _ = _tri_consts()
    cm = jnp.broadcast_to(E[:, h * HEAD_DIM:h * HEAD_DIM + 1], (CHUNK, CHUNK))
    rm = jnp.broadcast_to(ET[h * HEAD_DIM:h * HEAD_DIM + 1, :], (CHUNK, CHUNK))
    return jnp.where(c <= r, jnp.exp(jnp.minimum(cm - rm, 0.0)), 0.0)


def _ssd_fwd(xbc, proj, bias_e, alog_e, Bl, L, D, dt_col0, *, name):
    nc = L // CHUNK
    npairs = D // LANES
    ppg = npairs // SSM_GROUPS
    gw = ppg * LANES
    b0 = D // LANES
    c0 = (D + SSM_GROUPS * SSM_STATE) // LANES
    d0 = dt_col0 // gw

    def kern(x_ref, b_ref, c_ref, dtr_ref, bias_ref, alog_ref, y_ref, st_ref, h_scr):
        _, _, lane0 = _tri_consts()
        h_scr[...] = jnp.zeros_like(h_scr)

        def chunk(ci, carry):
            rows = pl.ds(pl.multiple_of(ci * CHUNK, CHUNK), CHUNK)
            Bb = b_ref[rows, :].astype(BF16)
            Cb = c_ref[rows, :].astype(BF16)
            CB = _dg(Cb, Bb, _NT)
            for p in range(ppg):
                lanes = pl.ds(p * LANES, LANES)
                x = x_ref[rows, lanes]
                dt, E, ET = _ssd_chunk_common(x, dtr_ref[rows, lanes], bias_ref[:, lanes], alog_ref[:, lanes])
                xdt = x * dt
                xdtb = xdt.astype(BF16)
                ys = []
                for h in range(2):
                    Wh = (CB * _ssd_head_decay(E, ET, h)).astype(BF16)
                    ys.append(_dot(Wh, xdtb))
                hprev = h_scr[p]
                y = jnp.where(lane0, ys[0], ys[1]) + _dg(Cb, hprev.astype(BF16), _NT) * jnp.exp(E)
                ds = jnp.exp(jnp.broadcast_to(E[CHUNK - 1:CHUNK, :], (CHUNK, LANES)) - E)
                states = _dg((xdt * ds).astype(BF16), Bb, _TN)
                cd = jnp.exp(jnp.broadcast_to(ET[:, CHUNK - 1:CHUNK], (LANES, SSM_STATE)))
                st_ref[ci, p] = hprev
                h_scr[p] = hprev * cd + states
                y_ref[rows, lanes] = y
            return carry

        lax.fori_loop(0, nc, chunk, 0)

    return pl.pallas_call(
        kern, name=name,
        out_shape=[jax.ShapeDtypeStruct((Bl * L, D), F32),
                   jax.ShapeDtypeStruct((Bl, SSM_GROUPS, nc, ppg, LANES, SSM_STATE), F32)],
        grid=(Bl, SSM_GROUPS),
        in_specs=[pl.BlockSpec((L, gw), lambda bi, g: (bi, g)),
                  pl.BlockSpec((L, SSM_STATE), lambda bi, g: (bi, b0 + g)),
                  pl.BlockSpec((L, SSM_STATE), lambda bi, g: (bi, c0 + g)),
                  pl.BlockSpec((L, gw), lambda bi, g: (bi, d0 + g)),
                  pl.BlockSpec((1, gw), lambda bi, g: (0, g)),
                  pl.BlockSpec((1, gw), lambda bi, g: (0, g))],
        out_specs=[pl.BlockSpec((L, gw), lambda bi, g: (bi, g)),
                   pl.BlockSpec((None, None, nc, ppg, LANES, SSM_STATE), lambda bi, g: (bi, g, 0, 0, 0, 0))],
        scratch_shapes=[pltpu.VMEM((ppg, LANES, SSM_STATE), F32)],
        compiler_params=_cparams(("parallel", "parallel")),
    )(xbc, xbc, xbc, proj, bias_e, alog_e)


def _ssd_bwd(dy, states, xbc, proj, bias_e, alog_e, dskip_e, Bl, L, D, dt_col0, *, name):
    nc = L // CHUNK
    npairs = D // LANES
    ppg = npairs // SSM_GROUPS
    gw = ppg * LANES
    b0 = D // LANES
    c0 = (D + SSM_GROUPS * SSM_STATE) // LANES
    d0 = dt_col0 // gw

    def kern(dy_ref, st_ref, x_ref, b_ref, c_ref, dtr_ref, bias_ref, alog_ref, dsk_ref,
             dx_ref, db_ref, dc_ref, ddt_ref, gbias_ref, galog_ref, gdsk_ref, g_scr):
        r, c, lane0 = _tri_consts()
        m0 = lane0.astype(F32)
        masks = (m0, 1.0 - m0)
        triu = _mask(c >= r)
        trils = _mask(c < r)
        ones = jnp.ones((CHUNK, LANES), BF16)
        g_scr[...] = jnp.zeros_like(g_scr)

        @pl.when(pl.program_id(1) == 0)
        def _():
            gbias_ref[...] = jnp.zeros_like(gbias_ref)
            galog_ref[...] = jnp.zeros_like(galog_ref)
            gdsk_ref[...] = jnp.zeros_like(gdsk_ref)

        def chunk(i, carry):
            ci = nc - 1 - i
            rows = pl.ds(pl.multiple_of(ci * CHUNK, CHUNK), CHUNK)
            Bb = b_ref[rows, :].astype(BF16)
            Cb = c_ref[rows, :].astype(BF16)
            CB = _dg(Cb, Bb, _NT)
            gsum = jnp.zeros((CHUNK, CHUNK), F32)
            dB = jnp.zeros((CHUNK, SSM_STATE), F32)
            dC = jnp.zeros((CHUNK, SSM_STATE), F32)
            for p in range(ppg):
                lanes = pl.ds(p * LANES, LANES)
                x = x_ref[rows, lanes]
                dtr = dtr_ref[rows, lanes]
                bias = bias_ref[:, lanes]
                alog = alog_ref[:, lanes]
                dt, E, ET = _ssd_chunk_common(x, dtr, bias, alog)
                xdt = x * dt
                xdtb = xdt.astype(BF16)
                dyv = dy_ref[rows, lanes]
                dyb = dyv.astype(BF16)
                hprev = st_ref[ci, p]
                hpb = hprev.astype(BF16)
                g = g_scr[p]
                gb = g.astype(BF16)
                sd = jnp.exp(E)
                ds = jnp.exp(jnp.broadcast_to(E[CHUNK - 1:CHUNK, :], (CHUNK, LANES)) - E)
                t1 = []
                dE = dyv * (_dg(Cb, hpb, _NT) * sd)
                for h in range(2):
                    Lh = _ssd_head_decay(E, ET, h)
                    Wh = (CB * Lh).astype(BF16)
                    t1.append(_dg(Wh, dyb, _TN))
                    Gh = Lh * _dg((dyv * masks[h]).astype(BF16), xdtb, _NT)
                    gsum = gsum + Gh
                    Qh = Gh * CB
                    dE = dE + _dot_exact_rhs(Qh - Qh.T, _mask(c == h * HEAD_DIM))
                dxst = _dg(Bb, gb, _NT) * ds
                dxdt = jnp.where(lane0, t1[0], t1[1]) + dxst
                dysd = (dyv * sd).astype(BF16)
                dC = dC + _dot(dysd, hpb)
                dB = dB + _dot((xdt * ds).astype(BF16), gb)
                cd = jnp.exp(jnp.broadcast_to(ET[:, CHUNK - 1:CHUNK], (LANES, SSM_STATE)))
                g_scr[p] = g * cd + _dg(dysd, Cb, _TN)
                nn = (((1,), (0,)), ((), ()))
                da = (_dot_exact_lhs(triu, dE, nn) + _dot_exact_lhs(trils, xdt * dxst, nn)
                      + _dot_exact_lhs(ones, g * cd * hprev, _NT))
                aneg = -jnp.exp(alog)
                ddt = da * aneg + dxdt * x
                ddtr = ddt * _sigmoid(dtr + bias)
                dx_ref[rows, lanes] = dxdt * dt + dyv * dsk_ref[:, lanes]
                ddt_ref[rows, lanes] = ddtr.astype(ddt_ref.dtype)
                gbias_ref[:, lanes] += jnp.sum(ddtr, axis=0, keepdims=True)
                galog_ref[:, lanes] += jnp.sum(da * dt, axis=0, keepdims=True) * aneg
                gdsk_ref[:, lanes] += jnp.sum(dyv * x, axis=0, keepdims=True)
            gsb = gsum.astype(BF16)
            db_ref[rows, :] = dB + _dg(gsb, Cb, _TN)
            dc_ref[rows, :] = dC + _dot(gsb, Bb)
            return carry

        lax.fori_loop(0, nc, chunk, 0)

    T = Bl * L
    xspec = pl.BlockSpec((L, gw), lambda g, bi: (bi, g))
    nspec = pl.BlockSpec((L, SSM_STATE), lambda g, bi: (bi, g))
    vspec = pl.BlockSpec((1, gw), lambda g, bi: (0, g))
    return pl.pallas_call(
        kern, name=name,
        out_shape=[jax.ShapeDtypeStruct((T, D), F32),
                   jax.ShapeDtypeStruct((T, SSM_GROUPS * SSM_STATE), F32),
                   jax.ShapeDtypeStruct((T, SSM_GROUPS * SSM_STATE), F32),
                   jax.ShapeDtypeStruct((T, D), BF16),
                   jax.ShapeDtypeStruct((1, D), F32), jax.ShapeDtypeStruct((1, D), F32),
                   jax.ShapeDtypeStruct((1, D), F32)],
        grid=(SSM_GROUPS, Bl),
        in_specs=[xspec,
                  pl.BlockSpec((None, None, nc, ppg, LANES, SSM_STATE), lambda g, bi: (bi, g, 0, 0, 0, 0)),
                  xspec,
                  pl.BlockSpec((L, SSM_STATE), lambda g, bi: (bi, b0 + g)),
                  pl.BlockSpec((L, SSM_STATE), lambda g, bi: (bi, c0 + g)),
                  pl.BlockSpec((L, gw), lambda g, bi: (bi, d0 + g)),
                  vspec, vspec, vspec],
        out_specs=[xspec, nspec, nspec, xspec, vspec, vspec, vspec],
        scratch_shapes=[pltpu.VMEM((ppg, LANES, SSM_STATE), F32)],
        compiler_params=_cparams(("parallel", "arbitrary")),
    )(dy, states, xbc, xbc, xbc, proj, bias_e, alog_e, dskip_e)


def _att_consts():
    r = lax.broadcasted_iota(jnp.int32, (ATT_BLOCK, ATT_BLOCK), 0)
    c = lax.broadcasted_iota(jnp.int32, (ATT_BLOCK, ATT_BLOCK), 1)
    return r, c


def _wide(x):
    return jnp.concatenate([x] * (ATT_BLOCK // LANES), axis=1)


def _att_nl(z):
    return jnp.maximum(z, 0.0) + jnp.log(1.0 + jnp.exp(-jnp.abs(z)))


def _tri2(pred):
    j = lax.broadcasted_iota(jnp.int32, (ATT_SUB, ATT_SUB), 0)
    s = lax.broadcasted_iota(jnp.int32, (ATT_SUB, ATT_SUB), 1)
    t = _mask(pred(j, s))
    return jnp.concatenate([t, t], axis=0)


def _cumsum_mm(x, mat):
    hi = x.astype(BF16)
    lo = (x - hi.astype(F32)).astype(BF16)
    return _dot(jnp.concatenate([hi, lo], axis=1), mat)


def _attn_fwd(proj, Bl, L, D, qc, kc, vc, *, name):
    npairs = D // LANES
    nq = L // ATT_BLOCK
    scale = HEAD_DIM ** -0.5

    def kern(q_ref, k_ref, v_ref, o_ref, tot_ref, qm, kb, vb, carry, acc):
        r, c = _att_consts()
        tri = r > c
        lrow = lax.broadcasted_iota(jnp.int32, (1, LANES), 1) < HEAD_DIM
        q = q_ref[...] * scale
        qm[0] = jnp.where(lrow, q, 0.0).astype(BF16)
        qm[1] = jnp.where(lrow, 0.0, q).astype(BF16)
        kb[...] = k_ref[...].astype(BF16)
        vb[...] = v_ref[...].astype(BF16)
        mat = _tri2(lambda j, s: j >= s)

        def tile(qs, ks, diag):
            for h in range(2):
                z = _dg(qm[h, qs, :], kb[ks, :], _NT)
                nl = _att_nl(z)
                if diag:
                    nl = jnp.where(tri, nl, 0.0)
                cy = carry[h]
                atts = []
                for b in reversed(range(ATT_BLOCK // ATT_SUB)):
                    cols = slice(b * ATT_SUB, (b + 1) * ATT_SUB)
                    inc = _cumsum_mm(nl[:, cols], mat)
                    atts.append(jnp.exp(z[:, cols] - inc - cy))
                    cy = cy + inc[:, 0:1]
                att = jnp.concatenate(atts[::-1], axis=1)
                if diag:
                    att = jnp.where(tri, att, 0.0)
                acc[h] += _dot(att.astype(BF16), vb[ks, :])
                carry[h] = cy

        def qblock(qi, _):
            qs = pl.ds(pl.multiple_of(qi * ATT_BLOCK, ATT_BLOCK), ATT_BLOCK)
            carry[...] = jnp.zeros_like(carry)
            acc[...] = jnp.zeros_like(acc)
            tile(qs, qs, True)

            def kblock(j, _):
                kj = qi - 1 - j
                tile(qs, pl.ds(pl.multiple_of(kj * ATT_BLOCK, ATT_BLOCK), ATT_BLOCK), False)
                return 0

            lax.fori_loop(0, qi, kblock, 0)
            o_ref[qs, :] = jnp.where(lrow, acc[0], acc[1])
            for h in range(2):
                tot_ref[h, qs, :] = jnp.broadcast_to(carry[h], (ATT_BLOCK, LANES))
            return 0

        lax.fori_loop(0, nq, qblock, 0)

    spec = lambda col: pl.BlockSpec((L, LANES), lambda bi, p: (bi, col // LANES + p))
    return pl.pallas_call(
        kern, name=name,
        out_shape=[jax.ShapeDtypeStruct((Bl * L, D), F32),
                   jax.ShapeDtypeStruct((Bl, npairs, 2, L, LANES), F32)],
        grid=(Bl, npairs),
        in_specs=[spec(qc), spec(kc), spec(vc)],
        out_specs=[pl.BlockSpec((L, LANES), lambda bi, p: (bi, p)),
                   pl.BlockSpec((None, None, 2, L, LANES), lambda bi, p: (bi, p, 0, 0, 0))],
        scratch_shapes=[pltpu.VMEM((2, L, LANES), BF16), pltpu.VMEM((L, LANES), BF16),
                        pltpu.VMEM((L, LANES), BF16), pltpu.VMEM((2, ATT_BLOCK, 1), F32),
                        pltpu.VMEM((2, ATT_BLOCK, LANES), F32)],
        compiler_params=_cparams(("parallel", "parallel")),
    )(proj, proj, proj)


def _attn_bwd(dout, tot, proj, Bl, L, D, qc, kc, vc, *, name):
    npairs = D // LANES
    nq = L // ATT_BLOCK
    scale = HEAD_DIM ** -0.5

    def kern(do_ref, tot_ref, q_ref, k_ref, v_ref, dq_ref, dk_ref, dv_ref, qm, km, kb, vb, dom, dkacc, dvacc,
             pre, pde, dqacc):
        r, c = _att_consts()
        tri = r > c
        lrow = lax.broadcasted_iota(jnp.int32, (1, LANES), 1) < HEAD_DIM
        q = q_ref[...] * scale
        qm[0] = jnp.where(lrow, q, 0.0).astype(BF16)
        qm[1] = jnp.where(lrow, 0.0, q).astype(BF16)
        k = k_ref[...]
        kb[...] = k.astype(BF16)
        km[0] = jnp.where(lrow, k, 0.0).astype(BF16)
        km[1] = jnp.where(lrow, 0.0, k).astype(BF16)
        vb[...] = v_ref[...].astype(BF16)
        do = do_ref[...]
        dom[0] = jnp.where(lrow, do, 0.0).astype(BF16)
        dom[1] = jnp.where(lrow, 0.0, do).astype(BF16)
        dkacc[...] = jnp.zeros_like(dkacc)
        dvacc[...] = jnp.zeros_like(dvacc)
        mat = _tri2(lambda j, s: j <= s)
        mat1 = mat[:ATT_SUB]

        def tile(qs, ks, diag):
            for h in range(2):
                z = _dg(qm[h, qs, :], kb[ks, :], _NT)
                nl = _att_nl(z)
                lb = z - nl
                if diag:
                    nl = jnp.where(tri, nl, 0.0)
                datt = _dg(dom[h, qs, :], vb[ks, :], _NT)
                base = pre[h] - tot_ref[h, qs, 0:1]
                pe = pde[h]
                atts, dzs = [], []
                for b in range(ATT_BLOCK // ATT_SUB):
                    cols = slice(b * ATT_SUB, (b + 1) * ATT_SUB)
                    pin = _cumsum_mm(nl[:, cols], mat)
                    att = jnp.exp(lb[:, cols] + base + pin)
                    if diag:
                        att = jnp.where(tri[:, cols], att, 0.0)
                    dE = att * datt[:, cols]
                    pde_in = _dot(dE.astype(BF16), mat1)
                    dz = dE - jnp.exp(lb[:, cols]) * (pe + pde_in)
                    if diag:
                        dz = jnp.where(tri[:, cols], dz, 0.0)
                    atts.append(att.astype(BF16))
                    dzs.append(dz.astype(BF16))
                    base = base + pin[:, ATT_SUB - 1:ATT_SUB]
                    pe = pe + pde_in[:, ATT_SUB - 1:ATT_SUB]
                pre[h] = base + tot_ref[h, qs, 0:1]
                pde[h] = pe
                attb = jnp.concatenate(atts, axis=1)
                dzb = jnp.concatenate(dzs, axis=1)
                dqacc[...] += _dot(dzb, km[h, ks, :])
                dkacc[ks, :] += _dg(dzb, qm[h, qs, :], _TN)
                dvacc[ks, :] += _dg(attb, dom[h, qs, :], _TN)

        def qblock(qi, _):
            qs = pl.ds(pl.multiple_of(qi * ATT_BLOCK, ATT_BLOCK), ATT_BLOCK)
            pre[...] = jnp.zeros_like(pre)
            pde[...] = jnp.zeros_like(pde)
            dqacc[...] = jnp.zeros_like(dqacc)

            def kblock(kj, _):
                tile(qs, pl.ds(pl.multiple_of(kj * ATT_BLOCK, ATT_BLOCK), ATT_BLOCK), False)
                return 0

            lax.fori_loop(0, qi, kblock, 0)
            tile(qs, qs, True)
            dq_ref[qs, :] = (dqacc[...] * scale).astype(dq_ref.dtype)
            return 0

        lax.fori_loop(0, nq, qblock, 0)
        dk_ref[...] = dkacc[...].astype(dk_ref.dtype)
        dv_ref[...] = dvacc[...].astype(dv_ref.dtype)

    spec = lambda col: pl.BlockSpec((L, LANES), lambda bi, p: (bi, col // LANES + p))
    ospec = pl.BlockSpec((L, LANES), lambda bi, p: (bi, p))
    T = Bl * L
    return pl.pallas_call(
        kern, name=name,
        out_shape=[jax.ShapeDtypeStruct((T, D), BF16)] * 3,
        grid=(Bl, npairs),
        in_specs=[ospec, pl.BlockSpec((None, None, 2, L, LANES), lambda bi, p: (bi, p, 0, 0, 0)),
                  spec(qc), spec(kc), spec(vc)],
        out_specs=[ospec, ospec, ospec],
        scratch_shapes=[pltpu.VMEM((2, L, LANES), BF16), pltpu.VMEM((2, L, LANES), BF16),
                        pltpu.VMEM((L, LANES), BF16), pltpu.VMEM((L, LANES), BF16),
                        pltpu.VMEM((2, L, LANES), BF16), pltpu.VMEM((L, LANES), F32),
                        pltpu.VMEM((L, LANES), F32), pltpu.VMEM((2, ATT_BLOCK, 1), F32),
                        pltpu.VMEM((2, ATT_BLOCK, 1), F32), pltpu.VMEM((ATT_BLOCK, LANES), F32)],
        compiler_params=_cparams(("parallel", "parallel")),
    )(dout, tot, proj, proj, proj)


def _mix_fwd(y, xbc, proj, yatt, dskip_e, g_ssd, g_att, D, *, name):
    T = y.shape[0]
    tm = _blk(T, 256)

    def kern(y_ref, xs_ref, z_ref, ya_ref, dsk_ref, gs_ref, ga_ref, o_ref):
        z = z_ref[...]
        u = (y_ref[...] + xs_ref[...] * dsk_ref[...]) * (z * _sigmoid(z))
        rs = lax.rsqrt(jnp.mean(u * u, axis=-1, keepdims=True) + EPS)
        o_ref[:, :D] = (u * rs * gs_ref[...]).astype(o_ref.dtype)
        ya = ya_ref[...]
        ra = lax.rsqrt(jnp.mean(ya * ya, axis=-1, keepdims=True) + EPS)
        o_ref[:, D:] = (ya * ra * ga_ref[...]).astype(o_ref.dtype)

    row = pl.BlockSpec((tm, D), lambda i: (i, 0))
    vec = pl.BlockSpec((1, D), lambda i: (0, 0))
    return pl.pallas_call(
        kern, name=name, out_shape=jax.ShapeDtypeStruct((T, 2 * D), BF16), grid=(T // tm,),
        in_specs=[row, row, row, row, vec, vec, vec],
        out_specs=pl.BlockSpec((tm, 2 * D), lambda i: (i, 0)),
        compiler_params=_cparams(("parallel",)),
    )(y, xbc, proj, yatt, dskip_e, g_ssd.reshape(1, D), g_att.reshape(1, D))


def _mix_bwd(dmix, y, xbc, proj, yatt, dskip_e, g_ssd, g_att, D, *, name):
    T = y.shape[0]
    tm = _blk(T, 256)

    def kern(d_ref, y_ref, xs_ref, z_ref, ya_ref, dsk_ref, gs_ref, ga_ref,
             dz_ref, dy_ref, dya_ref, dgs_ref, dga_ref):
        @pl.when(pl.program_id(0) == 0)
        def _():
            dgs_ref[...] = jnp.zeros_like(dgs_ref)
            dga_ref[...] = jnp.zeros_like(dga_ref)

        z = z_ref[...]
        sig = _sigmoid(z)
        sz = z * sig
        ytot = y_ref[...] + xs_ref[...] * dsk_ref[...]
        u = ytot * sz
        rs = lax.rsqrt(jnp.mean(u * u, axis=-1, keepdims=True) + EPS)
        un = u * rs
        do = d_ref[:, :D]
        dog = do * gs_ref[...]
        du = rs * (dog - un * jnp.mean(dog * un, axis=-1, keepdims=True))
        dgs_ref[...] += jnp.sum(do * un, axis=0, keepdims=True)
        dy_ref[...] = du * sz
        dz_ref[...] = (du * ytot * (sig * (1.0 + z * (1.0 - sig)))).astype(dz_ref.dtype)
        ya = ya_ref[...]
        ra = lax.rsqrt(jnp.mean(ya * ya, axis=-1, keepdims=True) + EPS)
        yan = ya * ra
        da = d_ref[:, D:]
        dag = da * ga_ref[...]
        dya_ref[...] = ra * (dag - yan * jnp.mean(dag * yan, axis=-1, keepdims=True))
        dga_ref[...] += jnp.sum(da * yan, axis=0, keepdims=True)

    row = pl.BlockSpec((tm, D), lambda i: (i, 0))
    vec = pl.BlockSpec((1, D), lambda i: (0, 0))
    return pl.pallas_call(
        kern, name=name,
        out_shape=[jax.ShapeDtypeStruct((T, D), BF16), jax.ShapeDtypeStruct((T, D), F32),
                   jax.ShapeDtypeStruct((T, D), F32), jax.ShapeDtypeStruct((1, D), F32),
                   jax.ShapeDtypeStruct((1, D), F32)],
        grid=(T // tm,),
        in_specs=[pl.BlockSpec((tm, 2 * D), lambda i: (i, 0)), row, row, row, row, vec, vec, vec],
        out_specs=[row, row, row, vec, vec],
        compiler_params=_cparams(("arbitrary",)),
    )(dmix, y, xbc, proj, yatt, dskip_e, g_ssd.reshape(1, D), g_att.reshape(1, D))


def _head_sum_mat(D):
    i = lax.broadcasted_iota(jnp.int32, (D, LANES), 0)
    c = lax.broadcasted_iota(jnp.int32, (D, LANES), 1)
    return _mask((i >= c * HEAD_DIM) & (i < (c + 1) * HEAD_DIM))


def _collapse_heads(x, *, name):
    R, D = x.shape
    tm = _blk(R, 256)

    def kern(x_ref, o_ref):
        o_ref[...] = _dot_exact_rhs(x_ref[...], _head_sum_mat(D))

    return pl.pallas_call(
        kern, name=name, out_shape=jax.ShapeDtypeStruct((R, LANES), F32), grid=(R // tm,),
        in_specs=[pl.BlockSpec((tm, D), lambda i: (i, 0))],
        out_specs=pl.BlockSpec((tm, LANES), lambda i: (i, 0)),
        compiler_params=_cparams(("parallel",)),
    )(x)


def _adam_math(w, g, m, v):
    m = ADAM_B1 * m + (1.0 - ADAM_B1) * g
    v = ADAM_B2 * v + (1.0 - ADAM_B2) * (g * g)
    m_hat = m / (1.0 - ADAM_B1 ** ADAM_STEP)
    v_hat = v / (1.0 - ADAM_B2 ** ADAM_STEP)
    delta = -ADAM_LR * (m_hat / (jnp.sqrt(v_hat) + ADAM_EPS) + ADAM_WD * w)
    return delta, m, v


def _adamw_sharded(pieces, w, m, v, *, name):
    R, C = w.shape
    tr = _blk(R, 256) if R % 8 == 0 else R

    def kern(p_ref, w_ref, m_ref, v_ref, g_ref, d_ref, nm_ref, nv_ref):
        g = p_ref[0].astype(F32)
        for j in range(1, N_DEV):
            g = g + p_ref[j].astype(F32)
        d, nm, nv = _adam_math(w_ref[...], g, m_ref[...], v_ref[...])
        g_ref[...] = g
        d_ref[...] = d
        nm_ref[...] = nm
        nv_ref[...] = nv

    row = pl.BlockSpec((tr, C), lambda i: (i, 0))
    return pl.pallas_call(
        kern, name=name, out_shape=[jax.ShapeDtypeStruct((R, C), F32)] * 4, grid=(R // tr,),
        in_specs=[pl.BlockSpec((N_DEV, tr, C), lambda i: (0, i, 0)), row, row, row],
        out_specs=[row] * 4,
        compiler_params=_cparams(("parallel",)),
    )(pieces, w, m, v)


def _small_update(parts, wd, md, vd, we, me, ve, nd, ne, D, *, name):
    def kern(p_ref, wd_ref, md_ref, vd_ref, we_ref, me_ref, ve_ref,
             gd_ref, dd_ref, nmd_ref, nvd_ref, ge_ref, de_ref, nme_ref, nve_ref, loss_ref):
        s = p_ref[0]
        for j in range(1, N_DEV):
            s = s + p_ref[j]
        gd = s[:nd]
        d, nm, nv = _adam_math(wd_ref[...], gd, md_ref[...], vd_ref[...])
        gd_ref[...] = gd
        dd_ref[...] = d
        nmd_ref[...] = nm
        nvd_ref[...] = nv
        ge = _dot_exact_rhs(s[nd:nd + ne], _head_sum_mat(D))
        d, nm, nv = _adam_math(we_ref[...], ge, me_ref[...], ve_ref[...])
        ge_ref[...] = ge
        de_ref[...] = d
        nme_ref[...] = nm
        nve_ref[...] = nv
        tot = jnp.sum(jnp.sum(s[nd + ne:], axis=1, keepdims=True), axis=0, keepdims=True)
        loss_ref[...] = jnp.broadcast_to(tot, loss_ref.shape)

    vm = pl.BlockSpec(memory_space=pltpu.VMEM)
    return pl.pallas_call(
        kern, name=name,
        out_shape=[jax.ShapeDtypeStruct((nd, D), F32)] * 4 + [jax.ShapeDtypeStruct((ne, LANES), F32)] * 4
        + [jax.ShapeDtypeStruct((8, LANES), F32)],
        in_specs=[vm] * 7, out_specs=[vm] * 9,
        compiler_params=pltpu.CompilerParams(vmem_limit_bytes=VMEM_LIMIT),
    )(parts, wd, md, vd, we, me, ve)


def _dev_index(p):
    return 4 * p[0] + 2 * p[1] + p[2]


def _all_gather(arrs, *, name):
    n = len(arrs)

    def body(*refs):
        xs, outs = refs[:n], refs[n:2 * n]
        send, recv, loc = refs[2 * n:]
        x, y, c = lax.axis_index("x"), lax.axis_index("y"), lax.axis_index("c")
        me, sib = (x, y, c), (x, y, 1 - c)
        chips = [(1 - x, y), (x, 1 - y), (1 - x, 1 - y)]

        def cp(k, s, block, to, src=None):
            dst = outs[k].at[_dev_index(block)]
            return pltpu.make_async_remote_copy(
                src_ref=dst if src is None else src, dst_ref=dst,
                send_sem=send.at[7 * k + s], recv_sem=recv.at[7 * k + s],
                device_id=to, device_id_type=pl.DeviceIdType.MESH)

        mine = [pltpu.make_async_copy(xs[k], outs[k].at[_dev_index(me)], loc.at[k]) for k in range(n)]
        for m in mine:
            m.start()
        first = []
        for k in range(n):
            first.append(cp(k, 0, me, sib, src=xs[k]))
            for j, ch in enumerate(chips):
                first.append(cp(k, 1 + j, me, (*ch, c), src=xs[k]))
        for f in first:
            f.start()
        passed = []
        for j, ch in enumerate(chips):
            for k in range(n):
                cp(k, 1 + j, (*ch, c), me).wait_recv()
                p = cp(k, 4 + j, (*ch, c), sib)
                p.start()
                passed.append(p)
        for k in range(n):
            cp(k, 0, sib, me).wait_recv()
            for j, ch in enumerate(chips):
                cp(k, 4 + j, (*ch, 1 - c), me).wait_recv()
        for f in first + passed:
            f.wait_send()
        for m in mine:
            m.wait()

    hbm = pl.BlockSpec(memory_space=pl.ANY)
    return pl.pallas_call(
        body, name=name,
        out_shape=[jax.ShapeDtypeStruct((N_DEV,) + a.shape, a.dtype) for a in arrs],
        in_specs=[hbm] * n, out_specs=[hbm] * n,
        scratch_shapes=[pltpu.SemaphoreType.DMA((7 * n,)), pltpu.SemaphoreType.DMA((7 * n,)),
                        pltpu.SemaphoreType.DMA((n,))],
    )(*arrs)


def _all_to_all(arrs, *, name):
    n = len(arrs)
    rels = [(fx, fy, fc) for fx in (0, 1) for fy in (0, 1) for fc in (0, 1) if fx or fy or fc]

    def body(*refs):
        xs, outs = refs[:n], refs[n:2 * n]
        send, recv, loc = refs[2 * n:]
        x, y, c = lax.axis_index("x"), lax.axis_index("y"), lax.axis_index("c")
        me = _dev_index((x, y, c))

        def peer(rel):
            return (1 - x if rel[0] else x, 1 - y if rel[1] else y, 1 - c if rel[2] else c)

        mine = [pltpu.make_async_copy(xs[k].at[me], outs[k].at[me], loc.at[k]) for k in range(n)]
        for m in mine:
            m.start()
        copies = []
        for k in range(n):
            for s, rel in enumerate(rels):
                p = peer(rel)
                copies.append(pltpu.make_async_remote_copy(
                    src_ref=xs[k].at[_dev_index(p)], dst_ref=outs[k].at[me],
                    send_sem=send.at[7 * k + s], recv_sem=recv.at[7 * k + s],
                    device_id=p, device_id_type=pl.DeviceIdType.MESH))
        for cpy in copies:
            cpy.start()
        for k in range(n):
            for s, rel in enumerate(rels):
                p = peer(rel)
                slot = outs[k].at[_dev_index(p)]
                pltpu.make_async_remote_copy(
                    src_ref=slot, dst_ref=slot, send_sem=send.at[7 * k + s], recv_sem=recv.at[7 * k + s],
                    device_id=p, device_id_type=pl.DeviceIdType.MESH).wait_recv()
        for cpy in copies:
            cpy.wait_send()
        for m in mine:
            m.wait()

    hbm = pl.BlockSpec(memory_space=pl.ANY)
    return pl.pallas_call(
        body, name=name,
        out_shape=[jax.ShapeDtypeStruct(a.shape, a.dtype) for a in arrs],
        in_specs=[hbm] * n, out_specs=[hbm] * n,
        scratch_shapes=[pltpu.SemaphoreType.DMA((7 * n,)), pltpu.SemaphoreType.DMA((7 * n,)),
                        pltpu.SemaphoreType.DMA((n,))],
    )(*arrs)


def _layer_fwd(x, P, dims, l):
    Bl, L, D = dims
    Dc = D + 2 * SSM_GROUPS * SSM_STATE
    qc, kc, vc, dtc = D + Dc, 2 * D + Dc, 3 * D + Dc, 4 * D + Dc
    h = _rmsnorm_fwd(x, P["norm_mix_g"], name=f"norm_mix_fwd_{l}")
    (proj,) = _mm(h, P["w_in"], out_dtypes=[F32], name=f"in_proj_{l}")
    xbc = _conv_fwd(proj, P["conv_w"], P["conv_b"], Bl, L, D, Dc, name=f"conv_fwd_{l}")
    y, states = _ssd_fwd(xbc, proj, P["dt_bias_e"], P["a_log_e"], Bl, L, D, dtc, name=f"ssd_fwd_{l}")
    yatt, tot = _attn_fwd(proj, Bl, L, D, qc, kc, vc, name=f"attn_fwd_{l}")
    ymix = _mix_fwd(y, xbc, proj, yatt, P["d_skip_e"], P["ssd_norm_g"], P["att_norm_g"], D, name=f"mix_fwd_{l}")
    (x1,) = _mm(ymix, P["w_out"], out_dtypes=[F32], epilogue=lambda r, res: (r + res,), extras=(x,),
                name=f"out_proj_{l}")
    h2 = _rmsnorm_fwd(x1, P["norm_mlp_g"], name=f"norm_mlp_fwd_{l}")

    def relu2(r):
        a = jnp.maximum(r, 0.0)
        return r, a * a

    u, act = _mm(h2, P["w_up"], out_dtypes=[F32, BF16], epilogue=relu2, name=f"mlp_up_{l}")
    (x2,) = _mm(act, P["w_down"], out_dtypes=[F32], epilogue=lambda r, res: (r + res,), extras=(x1,),
                name=f"mlp_down_{l}")
    saved = dict(x=x, h=h, proj=proj, xbc=xbc, y=y, states=states, yatt=yatt, tot=tot, ymix=ymix,
                 x1=x1, h2=h2, u=u, act=act)
    return x2, saved


def _layer_bwd(dx2, S, P, dims, l):
    Bl, L, D = dims
    Dc = D + 2 * SSM_GROUPS * SSM_STATE
    qc, kc, vc, dtc = D + Dc, 2 * D + Dc, 3 * D + Dc, 4 * D + Dc
    G = {}
    (du,) = _mm(dx2, P["w_down_t"], out_dtypes=[BF16], extras=(S["u"],),
                epilogue=lambda r, u: (r * (2.0 * jnp.maximum(u, 0.0)),), name=f"mlp_down_bwd_{l}")
    (G["w_down"],) = _mm(S["act"], dx2, ta=True, out_dtypes=[F32], name=f"mlp_down_dw_{l}")
    (dh2,) = _mm(du, P["w_up_t"], out_dtypes=[F32], name=f"mlp_up_bwd_{l}")
    (G["w_up"],) = _mm(S["h2"], du, ta=True, out_dtypes=[F32], name=f"mlp_up_dw_{l}")
    dx1, G["norm_mlp_g"] = _rmsnorm_bwd(dh2, S["x1"], P["norm_mlp_g"], dx2, name=f"norm_mlp_bwd_{l}")
    (dmix,) = _mm(dx1, P["w_out_t"], out_dtypes=[F32], name=f"out_proj_bwd_{l}")
    (G["w_out"],) = _mm(S["ymix"], dx1, ta=True, out_dtypes=[F32], name=f"out_proj_dw_{l}")
    dz, dy, dyatt, G["ssd_norm_g"], G["att_norm_g"] = _mix_bwd(
        dmix, S["y"], S["xbc"], S["proj"], S["yatt"], P["d_skip_e"], P["ssd_norm_g"], P["att_norm_g"], D,
        name=f"mix_bwd_{l}")
    dq, dk, dv = _attn_bwd(dyatt, S["tot"], S["proj"], Bl, L, D, qc, kc, vc, name=f"attn_bwd_{l}")
    dxs, dB, dC, ddt, G["dt_bias_e"], G["a_log_e"], G["d_skip_e"] = _ssd_bwd(
        dy, S["states"], S["xbc"], S["proj"], P["dt_bias_e"], P["a_log_e"], P["d_skip_e"],
        Bl, L, D, dtc, name=f"ssd_bwd_{l}")
    dact = jnp.concatenate([dxs, dB, dC], axis=1)
    dxbc, G["conv_w"], G["conv_b"] = _conv_bwd(dact, S["proj"], P["conv_w"], P["conv_b"], Bl, L, D, Dc,
                                               name=f"conv_bwd_{l}")
    dproj = jnp.concatenate([dz, dxbc, dq, dk, dv, ddt], axis=1)
    (dh,) = _mm(dproj, P["w_in_t"], out_dtypes=[F32], name=f"in_proj_bwd_{l}")
    (G["w_in"],) = _mm(S["h"], dproj, ta=True, out_dtypes=[F32], name=f"in_proj_dw_{l}")
    dx, G["norm_mix_g"] = _rmsnorm_bwd(dh, S["x"], P["norm_mix_g"], dx1, name=f"norm_mix_bwd_{l}")
    return dx, G


def _pad_rows(a, rows):
    return jnp.concatenate([a, jnp.zeros((rows - a.shape[0],) + a.shape[1:], a.dtype)], axis=0)


def kernel(x, norm_mix_g, w_in, conv_w, conv_b, dt_bias, a_log, d_skip, ssd_norm_g, att_norm_g, w_out, norm_mlp_g, w_up, w_down, final_norm_g, loss_target, m_norm_mix_g, m_w_in, m_conv_w, m_conv_b, m_dt_bias, m_a_log, m_d_skip, m_ssd_norm_g, m_att_norm_g, m_w_out, m_norm_mlp_g, m_w_up, m_w_down, m_final_norm_g, v_norm_mix_g, v_w_in, v_conv_w, v_conv_b, v_dt_bias, v_a_log, v_d_skip, v_ssd_norm_g, v_att_norm_g, v_w_out, v_norm_mlp_g, v_w_up, v_w_down, v_final_norm_g):
    Bl, L, D = x.shape
    T = Bl * L
    depth = w_in.shape[0]
    H = D // HEAD_DIM
    Dc = D + 2 * SSM_GROUPS * SSM_STATE
    Dff = w_up.shape[2] * N_DEV
    dims = (Bl, L, D)

    g_in, g_out, g_up, g_down, g_conv = _all_gather(
        [w_in.astype(BF16), w_out.astype(BF16), w_up.astype(BF16), w_down.astype(BF16), conv_w],
        name="gather_weights")
    W_in = g_in.transpose(1, 2, 0, 3).reshape(depth, D, -1)
    o = [0, D, D + Dc, D + Dc + H, 2 * D + Dc + H, 3 * D + Dc + H, 4 * D + Dc + H]
    wz, wxbc, wdt, wq, wk, wv = [W_in[:, :, o[i]:o[i + 1]] for i in range(6)]
    W_pack = jnp.concatenate([wz, wxbc, wq, wk, wv, jnp.repeat(wdt, HEAD_DIM, axis=2)], axis=2)
    W_out = g_out.transpose(1, 0, 2, 3).reshape(depth, 2 * D, D)
    W_up = g_up.transpose(1, 2, 0, 3).reshape(depth, D, Dff)
    W_down = g_down.transpose(1, 0, 2, 3).reshape(depth, Dff, D)
    Cw = g_conv.transpose(1, 2, 0, 3).reshape(depth, CONV_WIDTH, Dc)
    rep = lambda a: jnp.repeat(a, HEAD_DIM, axis=1)
    dt_bias_e, a_log_e, d_skip_e = rep(dt_bias), rep(a_log), rep(d_skip)

    params = []
    for l in range(depth):
        params.append(dict(
            norm_mix_g=norm_mix_g[l], w_in=W_pack[l], w_in_t=W_pack[l].T, conv_w=Cw[l], conv_b=conv_b[l],
            dt_bias_e=dt_bias_e[l:l + 1], a_log_e=a_log_e[l:l + 1], d_skip_e=d_skip_e[l:l + 1],
            ssd_norm_g=ssd_norm_g[l], att_norm_g=att_norm_g[l], w_out=W_out[l], w_out_t=W_out[l].T,
            norm_mlp_g=norm_mlp_g[l], w_up=W_up[l], w_up_t=W_up[l].T, w_down=W_down[l], w_down_t=W_down[l].T))

    xa = x.reshape(T, D)
    saved = []
    for l in range(depth):
        xa, s = _layer_fwd(xa, params[l], dims, l)
        saved.append(s)
    dx, g_final, loss_part = _loss_head(xa, final_norm_g, loss_target.reshape(T, D), name="loss_head")

    grads = [None] * depth
    for l in reversed(range(depth)):
        dx, grads[l] = _layer_bwd(dx, saved[l], params[l], dims, l)
    grad_x = dx.reshape(Bl, L, D)

    st = lambda k: jnp.stack([grads[l][k] for l in range(depth)])
    gW = st("w_in")
    gdt = _collapse_heads(gW[:, :, 4 * D + Dc:].reshape(depth * D, D), name="collapse_w_dt")
    gdt = gdt[:, :H].reshape(depth, D, H)
    gW_in = jnp.concatenate([gW[:, :, :D + Dc], gdt, gW[:, :, D + Dc:4 * D + Dc]], axis=2)
    n_in = w_in.shape[2]
    p_in = gW_in.reshape(depth, D, N_DEV, n_in).transpose(2, 0, 1, 3).astype(BF16)
    p_out = st("w_out").reshape(depth, N_DEV, 2 * D // N_DEV, D).transpose(1, 0, 2, 3).astype(BF16)
    p_up = st("w_up").reshape(depth, D, N_DEV, Dff // N_DEV).transpose(2, 0, 1, 3).astype(BF16)
    p_down = st("w_down").reshape(depth, N_DEV, Dff // N_DEV, D).transpose(1, 0, 2, 3).astype(BF16)
    p_conv = st("conv_w").reshape(depth, CONV_WIDTH, N_DEV, Dc // N_DEV).transpose(2, 0, 1, 3).astype(BF16)
    r_in, r_out, r_up, r_down, r_conv = _all_to_all([p_in, p_out, p_up, p_down, p_conv], name="exchange_grads")

    def sharded(pieces, w, m, v, name):
        shp = w.shape
        C = shp[-1]
        res = _adamw_sharded(pieces.reshape(N_DEV, -1, C), w.reshape(-1, C), m.reshape(-1, C),
                             v.reshape(-1, C), name=name)
        return [a.reshape(shp) for a in res]

    u_in = sharded(r_in, w_in, m_w_in, v_w_in, "adamw_w_in")
    u_out = sharded(r_out, w_out, m_w_out, v_w_out, "adamw_w_out")
    u_up = sharded(r_up, w_up, m_w_up, v_w_up, "adamw_w_up")
    u_down = sharded(r_down, w_down, m_w_down, v_w_down, "adamw_w_down")
    u_conv = sharded(r_conv, conv_w, m_conv_w, v_conv_w, "adamw_conv_w")

    rows = lambda a: a.reshape(-1, D)
    direct_names = ["norm_mix_g", "ssd_norm_g", "att_norm_g", "norm_mlp_g", "conv_b"]
    direct_g = [rows(st(k)) for k in direct_names] + [g_final]
    exp_names = ["dt_bias_e", "a_log_e", "d_skip_e"]
    exp_g = [rows(st(k)) for k in exp_names]
    n_direct = sum(a.shape[0] for a in direct_g)
    nd = -(-n_direct // 8) * 8
    ne = -(-3 * depth // 8) * 8
    part = jnp.concatenate([_pad_rows(jnp.concatenate(direct_g, axis=0), nd),
                            _pad_rows(jnp.concatenate(exp_g, axis=0), ne),
                            _pad_rows(loss_part, 8)], axis=0)
    (parts,) = _all_gather([part], name="gather_small_grads")

    def direct_pack(ws):
        return _pad_rows(jnp.concatenate([rows(a) for a in ws], axis=0), nd)

    def exp_pack(ws):
        a = jnp.concatenate(ws, axis=0)
        a = jnp.concatenate([a, jnp.zeros((a.shape[0], LANES - H), F32)], axis=1)
        return _pad_rows(a, ne)

    res = _small_update(
        parts,
        direct_pack([norm_mix_g, ssd_norm_g, att_norm_g, norm_mlp_g, conv_b, final_norm_g]),
        direct_pack([m_norm_mix_g, m_ssd_norm_g, m_att_norm_g, m_norm_mlp_g, m_conv_b, m_final_norm_g]),
        direct_pack([v_norm_mix_g, v_ssd_norm_g, v_att_norm_g, v_norm_mlp_g, v_conv_b, v_final_norm_g]),
        exp_pack([dt_bias, a_log, d_skip]), exp_pack([m_dt_bias, m_a_log, m_d_skip]),
        exp_pack([v_dt_bias, v_a_log, v_d_skip]), nd, ne, D, name="small_update")
    loss = res[8][0, 0]

    def unpack(kind):
        d, e = res[kind], res[4 + kind]
        out = {}
        r0 = 0
        for nm, shp in [("norm_mix_g", (depth, D)), ("ssd_norm_g", (depth, D)), ("att_norm_g", (depth, D)),
                        ("norm_mlp_g", (depth, D)), ("conv_b", (depth, Dc)), ("final_norm_g", (D,))]:
            n = 1
            for s_ in shp:
                n *= s_
            out[nm] = d[r0:r0 + n // D].reshape(shp)
            r0 += n // D
        for i, nm in enumerate(["dt_bias", "a_log", "d_skip"]):
            out[nm] = e[i * depth:(i + 1) * depth, :H]
        return out

    names = ["norm_mix_g", "w_in", "conv_w", "conv_b", "dt_bias", "a_log", "d_skip", "ssd_norm_g",
             "att_norm_g", "w_out", "norm_mlp_g", "w_up", "w_down", "final_norm_g"]
    big = {"w_in": u_in, "w_out": u_out, "w_up": u_up, "w_down": u_down, "conv_w": u_conv}
    outs = [loss, grad_x]
    for kind in range(4):
        small = unpack(kind)
        for nm in names:
            outs.append(big[nm][kind] if nm in big else small[nm])
    return tuple(outs)
```

```python
import functools

import jax
import jax.numpy as jnp
from jax import lax
from jax.experimental import pallas as pl
from jax.experimental.pallas import tpu as pltpu

F32 = jnp.float32
BF16 = jnp.bfloat16

HEAD_DIM = 64
SSM_STATE = 128
SSM_GROUPS = 4
CONV_WIDTH = 4
CHUNK = 128
ATT_BLOCK = 512
ATT_SUB = 256
EPS = 1e-5
LANES = 128
N_DEV = 8
VMEM_LIMIT = 56 * 1024 * 1024

ADAM_LR = 0.001
ADAM_B1 = 0.9
ADAM_B2 = 0.999
ADAM_EPS = 1e-08
ADAM_WD = 0.01
ADAM_STEP = 10

_NT =(((1,), (1,)), ((), ()))
_TN = (((0,), (0,)), ((), ()))


def _blk(n, pref):
    if n <= pref:
        return n
    b = (pref // LANES) * LANES
    while b >= LANES:
        if n % b == 0:
            return b
        b -= LANES
    return n


def _cparams(sem):
    return pltpu.CompilerParams(dimension_semantics=sem, vmem_limit_bytes=VMEM_LIMIT)


def _dot(a, b):
    return jnp.dot(a, b, preferred_element_type=F32)


def _dg(a, b, dims):
    return lax.dot_general(a, b, dims, preferred_element_type=F32)


def _split3(x):
    h = x.astype(BF16)
    r = x - h.astype(F32)
    m = r.astype(BF16)
    l = (r - m.astype(F32)).astype(BF16)
    return h, m, l


def _dot_exact_rhs(x, c):
    h, m, l = _split3(x)
    return _dot(h, c) + _dot(m, c) + _dot(l, c)


def _dot_exact_lhs(c, x, dims):
    h, m, l = _split3(x)
    return _dg(c, h, dims) + _dg(c, m, dims) + _dg(c, l, dims)


def _mask(cond):
    return jnp.where(cond, 1.0, 0.0).astype(BF16)


def _sigmoid(x):
    return 1.0 / (1.0 + jnp.exp(-x))


def _softplus(x):
    return jnp.maximum(x, 0.0) + jnp.log(1.0 + jnp.exp(-jnp.abs(x)))


def _mm(a, b, *, name, out_dtypes, ta=False, epilogue=None, extras=(), tm=1024, tn=512, tk=2048):
    if ta:
        K, M = a.shape
    else:
        M, K = a.shape
    N = b.shape[1]
    assert b.shape[0] == K
    tm, tn, tk = _blk(M, tm), _blk(N, tn), _blk(K, tk)
    nk = K // tk
    n_ex, n_out = len(extras), len(out_dtypes)

    def kern(*refs):
        a_ref, b_ref = refs[0], refs[1]
        ex = refs[2:2 + n_ex]
        outs = refs[2 + n_ex:2 + n_ex + n_out]
        acc = refs[-1]
        k = pl.program_id(2)

        @pl.when(k == 0)
        def _():
            acc[...] = jnp.zeros_like(acc)

        av = a_ref[...].astype(BF16)
        bv = b_ref[...].astype(BF16)
        acc[...] += _dg(av, bv, _TN) if ta else _dot(av, bv)

        @pl.when(k == nk - 1)
        def _():
            r = acc[...]
            vals = epilogue(r, *[e[...] for e in ex]) if epilogue is not None else (r,)
            for o, v in zip(outs, vals):
                o[...] = v.astype(o.dtype)

    if ta:
        a_spec = pl.BlockSpec((tk, tm), lambda j, i, k: (k, i))
    else:
        a_spec = pl.BlockSpec((tm, tk), lambda j, i, k: (i, k))
    b_spec = pl.BlockSpec((tk, tn), lambda j, i, k: (k, j))
    o_spec = pl.BlockSpec((tm, tn), lambda j, i, k: (i, j))
    res = pl.pallas_call(
        kern, name=name,
        out_shape=[jax.ShapeDtypeStruct((M, N), d) for d in out_dtypes],
        grid=(N // tn, M // tm, nk),
        in_specs=[a_spec, b_spec] + [o_spec] * n_ex,
        out_specs=[o_spec] * n_out,
        scratch_shapes=[pltpu.VMEM((tm, tn), F32)],
        compiler_params=_cparams(("parallel", "parallel", "arbitrary")),
    )(a, b, *extras)
    return res


def _rmsnorm_fwd(x, g, *, name):
    T, D = x.shape
    tm = _blk(T, 512)

    def kern(x_ref, g_ref, h_ref):
        xv = x_ref[...]
        r = lax.rsqrt(jnp.mean(xv * xv, axis=-1, keepdims=True) + EPS)
        h_ref[...] = (xv * r * g_ref[...]).astype(h_ref.dtype)

    return pl.pallas_call(
        kern, name=name, out_shape=jax.ShapeDtypeStruct((T, D), BF16), grid=(T // tm,),
        in_specs=[pl.BlockSpec((tm, D), lambda i: (i, 0)), pl.BlockSpec((1, D), lambda i: (0, 0))],
        out_specs=pl.BlockSpec((tm, D), lambda i: (i, 0)),
        compiler_params=_cparams(("parallel",)),
    )(x, g.reshape(1, D))


def _rmsnorm_bwd(dh, x, g, dres, *, name):
    T, D = x.shape
    tm = _blk(T, 512)

    def kern(dh_ref, x_ref, g_ref, dres_ref, dx_ref, dg_ref):
        @pl.when(pl.program_id(0) == 0)
        def _():
            dg_ref[...] = jnp.zeros_like(dg_ref)

        xv = x_ref[...]
        r = lax.rsqrt(jnp.mean(xv * xv, axis=-1, keepdims=True) + EPS)
        xn = xv * r
        dy = dh_ref[...].astype(F32)
        dyg = dy * g_ref[...]
        dx = r * (dyg - xn * jnp.mean(dyg * xn, axis=-1, keepdims=True))
        dx_ref[...] = dres_ref[...] + dx
        dg_ref[...] += jnp.sum(dy * xn, axis=0, keepdims=True)

    row = pl.BlockSpec((tm, D), lambda i: (i, 0))
    vec = pl.BlockSpec((1, D), lambda i: (0, 0))
    return pl.pallas_call(
        kern, name=name,
        out_shape=[jax.ShapeDtypeStruct((T, D), F32), jax.ShapeDtypeStruct((1, D), F32)],
        grid=(T // tm,), in_specs=[row, row, vec, row], out_specs=[row, vec],
        compiler_params=_cparams(("arbitrary",)),
    )(dh, x, g.reshape(1, D), dres)


def _loss_head(x, g, target, *, name):
    T, D = x.shape
    tm = _blk(T, 512)

    def kern(x_ref, g_ref, t_ref, dx_ref, dg_ref, loss_ref):
        @pl.when(pl.program_id(0) == 0)
        def _():
            dg_ref[...] = jnp.zeros_like(dg_ref)
            loss_ref[...] = jnp.zeros_like(loss_ref)

        xv = x_ref[...]
        gv = g_ref[...]
        r = lax.rsqrt(jnp.mean(xv * xv, axis=-1, keepdims=True) + EPS)
        xn = xv * r
        err = xn * gv - t_ref[...]
        loss_ref[...] += jnp.sum(err * err, axis=0, keepdims=True) * (0.5 / D)
        dy = err * (1.0 / D)
        dyg = dy * gv
        dx_ref[...] = r * (dyg - xn * jnp.mean(dyg * xn, axis=-1, keepdims=True))
        dg_ref[...] += jnp.sum(dy * xn, axis=0, keepdims=True)

    row = pl.BlockSpec((tm, D), lambda i: (i, 0))
    vec = pl.BlockSpec((1, D), lambda i: (0, 0))
    return pl.pallas_call(
        kern, name=name,
        out_shape=[jax.ShapeDtypeStruct((T, D), F32), jax.ShapeDtypeStruct((1, D), F32),
                   jax.ShapeDtypeStruct((1, D), F32)],
        grid=(T // tm,), in_specs=[row, vec, row], out_specs=[row, vec, vec],
        compiler_params=_cparams(("arbitrary",)),
    )(x, g.reshape(1, D), target)


def _shift_down(u, s, L):
    t = lax.broadcasted_iota(jnp.int32, u.shape, 0)
    return jnp.where(t >= s, pltpu.roll(u, s, 0), 0.0)


def _shift_up(u, s, L):
    t = lax.broadcasted_iota(jnp.int32, u.shape, 0)
    return jnp.where(t < L - s, pltpu.roll(u, L - s, 0), 0.0)


def _conv_pre(u, w_ref, b_ref, L):
    pre = u * w_ref[CONV_WIDTH - 1:CONV_WIDTH, :] + b_ref[...]
    for s in range(1, CONV_WIDTH):
        pre = pre + _shift_down(u, s, L) * w_ref[CONV_WIDTH - 1 - s:CONV_WIDTH - s, :]
    return pre


def _conv_fwd(proj, w, b, Bl, L, col0, Dc, *, name):
    cb = LANES
    c0 = col0 // cb

    def kern(u_ref, w_ref, b_ref, o_ref):
        pre = _conv_pre(u_ref[...], w_ref, b_ref, L)
        o_ref[...] = pre * _sigmoid(pre)

    return pl.pallas_call(
        kern, name=name, out_shape=jax.ShapeDtypeStruct((Bl * L, Dc), F32), grid=(Bl, Dc // cb),
        in_specs=[pl.BlockSpec((L, cb), lambda bi, j: (bi, c0 + j)),
                  pl.BlockSpec((CONV_WIDTH, cb), lambda bi, j: (0, j)),
                  pl.BlockSpec((1, cb), lambda bi, j: (0, j))],
        out_specs=pl.BlockSpec((L, cb), lambda bi, j: (bi, j)),
        compiler_params=_cparams(("parallel", "parallel")),
    )(proj, w, b.reshape(1, Dc))


def _conv_bwd(dact, proj, w, b, Bl, L, col0, Dc, *, name):
    cb = LANES
    c0 = col0 // cb

    def kern(d_ref, u_ref, w_ref, b_ref, du_ref, dw_ref, db_ref):
        @pl.when(pl.program_id(1) == 0)
        def _():
            dw_ref[...] = jnp.zeros_like(dw_ref)
            db_ref[...] = jnp.zeros_like(db_ref)

        u = u_ref[...]
        pre = _conv_pre(u, w_ref, b_ref, L)
        sig = _sigmoid(pre)
        dpre = d_ref[...] * (sig * (1.0 + pre * (1.0 - sig)))
        du = dpre * w_ref[CONV_WIDTH - 1:CONV_WIDTH, :]
        dw_ref[CONV_WIDTH - 1:CONV_WIDTH, :] += jnp.sum(dpre * u, axis=0, keepdims=True)
        for s in range(1, CONV_WIDTH):
            k = CONV_WIDTH - 1 - s
            du = du + _shift_up(dpre, s, L) * w_ref[k:k + 1, :]
            dw_ref[k:k + 1, :] += jnp.sum(dpre * _shift_down(u, s, L), axis=0, keepdims=True)
        db_ref[...] += jnp.sum(dpre, axis=0, keepdims=True)
        du_ref[...] = du.astype(du_ref.dtype)

    return pl.pallas_call(
        kern, name=name,
        out_shape=[jax.ShapeDtypeStruct((Bl * L, Dc), BF16),
                   jax.ShapeDtypeStruct((CONV_WIDTH, Dc), F32), jax.ShapeDtypeStruct((1, Dc), F32)],
        grid=(Dc // cb, Bl),
        in_specs=[pl.BlockSpec((L, cb), lambda j, bi: (bi, j)),
                  pl.BlockSpec((L, cb), lambda j, bi: (bi, c0 + j)),
                  pl.BlockSpec((CONV_WIDTH, cb), lambda j, bi: (0, j)),
                  pl.BlockSpec((1, cb), lambda j, bi: (0, j))],
        out_specs=[pl.BlockSpec((L, cb), lambda j, bi: (bi, j)),
                   pl.BlockSpec((CONV_WIDTH, cb), lambda j, bi: (0, j)),
                   pl.BlockSpec((1, cb), lambda j, bi: (0, j))],
        compiler_params=_cparams(("parallel", "arbitrary")),
    )(dact, proj, w, b.reshape(1, Dc))


def _tri_consts():
    r = lax.broadcasted_iota(jnp.int32, (CHUNK, CHUNK), 0)
    c = lax.broadcasted_iota(jnp.int32, (CHUNK, CHUNK), 1)
    lane0 = c < HEAD_DIM
    return r, c, lane0


def _ssd_chunk_common(x, dtr, bias, alog):
    r, c, _ = _tri_consts()
    dt = _softplus(dtr + bias)
    a = dt * (-jnp.exp(alog))
    tril = _mask(c <= r)
    E = _dot_exact_lhs(tril, a, (((1,), (0,)), ((), ())))
    return dt, E, E.T


def _ssd_head_decay(E, ET, h):
    r, c, _ = _tri_consts()
    cm = jnp.broadcast_to(E[:, h * HEAD_DIM:h * HEAD_DIM + 1], (CHUNK, CHUNK))
    rm = jnp.broadcast_to(ET[h * HEAD_DIM:h * HEAD_DIM + 1, :], (CHUNK, CHUNK))
    return jnp.where(c <= r, jnp.exp(jnp.minimum(cm - rm, 0.0)), 0.0)


def _ssd_fwd(xbc, proj, bias_e, alog_e, Bl, L, D, dt_col0, *, name):
    nc = L // CHUNK
    npairs = D // LANES
    ppg = npairs // SSM_GROUPS
    gw = ppg * LANES
    b0 = D // LANES
    c0 = (D + SSM_GROUPS * SSM_STATE) // LANES
    d0 = dt_col0 // gw

    def kern(x_ref, b_ref, c_ref, dtr_ref, bias_ref, alog_ref, y_ref, st_ref, h_scr):
        _, _, lane0 = _tri_consts()
        h_scr[...] = jnp.zeros_like(h_scr)

        def chunk(ci, carry):
            rows = pl.ds(pl.multiple_of(ci * CHUNK, CHUNK), CHUNK)
            Bb = b_ref[rows, :].astype(BF16)
            Cb = c_ref[rows, :].astype(BF16)
            CB = _dg(Cb, Bb, _NT)
            lanes = [pl.ds(p * LANES, LANES) for p in range(ppg)]
            ins = [(x_ref[rows, ln], dtr_ref[rows, ln], bias_ref[:, ln], alog_ref[:, ln], h_scr[p])
                   for p, ln in enumerate(lanes)]
            outs = []
            for x, dtr, bias, alog, hprev in ins:
                dt, E, ET = _ssd_chunk_common(x, dtr, bias, alog)
                xdt = x * dt
                xdtb = xdt.astype(BF16)
                ys = []
                for h in range(2):
                    Wh = (CB * _ssd_head_decay(E, ET, h)).astype(BF16)
                    ys.append(_dot(Wh, xdtb))
                y = jnp.where(lane0, ys[0], ys[1]) + _dg(Cb, hprev.astype(BF16), _NT) * jnp.exp(E)
                ds = jnp.exp(jnp.broadcast_to(E[CHUNK - 1:CHUNK, :], (CHUNK, LANES)) - E)
                states = _dg((xdt * ds).astype(BF16), Bb, _TN)
                cd = jnp.exp(jnp.broadcast_to(ET[:, CHUNK - 1:CHUNK], (LANES, SSM_STATE)))
                outs.append((hprev, hprev * cd + states, y))
            for p, (hprev, hnew, y) in enumerate(outs):
                st_ref[ci, p] = hprev
                h_scr[p] = hnew
                y_ref[rows, lanes[p]] = y
            return carry

        lax.fori_loop(0, nc, chunk, 0)

    return pl.pallas_call(
        kern, name=name,
        out_shape=[jax.ShapeDtypeStruct((Bl * L, D), F32),
                   jax.ShapeDtypeStruct((Bl, SSM_GROUPS, nc, ppg, LANES, SSM_STATE), F32)],
        grid=(Bl, SSM_GROUPS),
        in_specs=[pl.BlockSpec((L, gw), lambda bi, g: (bi, g)),
                  pl.BlockSpec((L, SSM_STATE), lambda bi, g: (bi, b0 + g)),
                  pl.BlockSpec((L, SSM_STATE), lambda bi, g: (bi, c0 + g)),
                  pl.BlockSpec((L, gw), lambda bi, g: (bi, d0 + g)),
                  pl.BlockSpec((1, gw), lambda bi, g: (0, g)),
                  pl.BlockSpec((1, gw), lambda bi, g: (0, g))],
        out_specs=[pl.BlockSpec((L, gw), lambda bi, g: (bi, g)),
                   pl.BlockSpec((None, None, nc, ppg, LANES, SSM_STATE), lambda bi, g: (bi, g, 0, 0, 0, 0))],
        scratch_shapes=[pltpu.VMEM((ppg, LANES, SSM_STATE), F32)],
        compiler_params=_cparams(("parallel", "parallel")),
    )(xbc, xbc, xbc, proj, bias_e, alog_e)


def _ssd_bwd(dy, states, xbc, proj, bias_e, alog_e, dskip_e, Bl, L, D, dt_col0, *, name):
    nc = L // CHUNK
    npairs = D // LANES
    ppg = npairs // SSM_GROUPS
    gw = ppg * LANES
    b0 = D // LANES
    c0 = (D + SSM_GROUPS * SSM_STATE) // LANES
    d0 = dt_col0 // gw

    def kern(dy_ref, st_ref, x_ref, b_ref, c_ref, dtr_ref, bias_ref, alog_ref, dsk_ref,
             dx_ref, db_ref, dc_ref, ddt_ref, gbias_ref, galog_ref, gdsk_ref, g_scr):
        r, c, lane0 = _tri_consts()
        m0 = lane0.astype(F32)
        masks = (m0, 1.0 - m0)
        triu = _mask(c >= r)
        trils = _mask(c < r)
        ones = jnp.ones((CHUNK, LANES), BF16)
        g_scr[...] = jnp.zeros_like(g_scr)

        @pl.when(pl.program_id(1) == 0)
        def _():
            gbias_ref[...] = jnp.zeros_like(gbias_ref)
            galog_ref[...] = jnp.zeros_like(galog_ref)
            gdsk_ref[...] = jnp.zeros_like(gdsk_ref)

        def chunk(i, carry):
            ci = nc - 1 - i
            rows = pl.ds(pl.multiple_of(ci * CHUNK, CHUNK), CHUNK)
            Bb = b_ref[rows, :].astype(BF16)
            Cb = c_ref[rows, :].astype(BF16)
            CB = _dg(Cb, Bb, _NT)
            gsum = jnp.zeros((CHUNK, CHUNK), F32)
            dB = jnp.zeros((CHUNK, SSM_STATE), F32)
            dC = jnp.zeros((CHUNK, SSM_STATE), F32)
            lanes = [pl.ds(p * LANES, LANES) for p in range(ppg)]
            ins = [(x_ref[rows, ln], dtr_ref[rows, ln], bias_ref[:, ln], alog_ref[:, ln], dy_ref[rows, ln],
                    st_ref[ci, p], g_scr[p], dsk_ref[:, ln]) for p, ln in enumerate(lanes)]
            outs = []
            for x, dtr, bias, alog, dyv, hprev, g, dsk in ins:
                dt, E, ET = _ssd_chunk_common(x, dtr, bias, alog)
                xdt = x * dt
                xdtb = xdt.astype(BF16)
                dyb = dyv.astype(BF16)
                hpb = hprev.astype(BF16)
                gb = g.astype(BF16)
                sd = jnp.exp(E)
                ds = jnp.exp(jnp.broadcast_to(E[CHUNK - 1:CHUNK, :], (CHUNK, LANES)) - E)
                t1 = []
                dE = dyv * (_dg(Cb, hpb, _NT) * sd)
                for h in range(2):
                    Lh = _ssd_head_decay(E, ET, h)
                    Wh = (CB * Lh).astype(BF16)
                    t1.append(_dg(Wh, dyb, _TN))
                    Gh = Lh * _dg((dyv * masks[h]).astype(BF16), xdtb, _NT)
                    gsum = gsum + Gh
                    Qh = Gh * CB
                    dE = dE + _dot_exact_rhs(Qh - Qh.T, _mask(c == h * HEAD_DIM))
                dxst = _dg(Bb, gb, _NT) * ds
                dxdt = jnp.where(lane0, t1[0], t1[1]) + dxst
                dysd = (dyv * sd).astype(BF16)
                dC = dC + _dot(dysd, hpb)
                dB = dB + _dot((xdt * ds).astype(BF16), gb)
                cd = jnp.exp(jnp.broadcast_to(ET[:, CHUNK - 1:CHUNK], (LANES, SSM_STATE)))
                gnew = g * cd + _dg(dysd, Cb, _TN)
                nn = (((1,), (0,)), ((), ()))
                da = (_dot_exact_lhs(triu, dE, nn) + _dot_exact_lhs(trils, xdt * dxst, nn)
                      + _dot_exact_lhs(ones, g * cd * hprev, _NT))
                aneg = -jnp.exp(alog)
                ddt = da * aneg + dxdt * x
                ddtr = ddt * _sigmoid(dtr + bias)
                outs.append((gnew, dxdt * dt + dyv * dsk, ddtr, jnp.sum(ddtr, axis=0, keepdims=True),
                             jnp.sum(da * dt, axis=0, keepdims=True) * aneg,
                             jnp.sum(dyv * x, axis=0, keepdims=True)))
            for p, (gnew, dx, ddtr, sb, sa, sk) in enumerate(outs):
                g_scr[p] = gnew
                dx_ref[rows, lanes[p]] = dx
                ddt_ref[rows, lanes[p]] = ddtr.astype(ddt_ref.dtype)
                gbias_ref[:, lanes[p]] += sb
                galog_ref[:, lanes[p]] += sa
                gdsk_ref[:, lanes[p]] += sk
            gsb = gsum.astype(BF16)
            db_ref[rows, :] = dB + _dg(gsb, Cb, _TN)
            dc_ref[rows, :] = dC + _dot(gsb, Bb)
            return carry

        lax.fori_loop(0, nc, chunk, 0)

    T = Bl * L
    xspec = pl.BlockSpec((L, gw), lambda g, bi: (bi, g))
    nspec = pl.BlockSpec((L, SSM_STATE), lambda g, bi: (bi, g))
    vspec = pl.BlockSpec((1, gw), lambda g, bi: (0, g))
    return pl.pallas_call(
        kern, name=name,
        out_shape=[jax.ShapeDtypeStruct((T, D), F32),
                   jax.ShapeDtypeStruct((T, SSM_GROUPS * SSM_STATE), F32),
                   jax.ShapeDtypeStruct((T, SSM_GROUPS * SSM_STATE), F32),
                   jax.ShapeDtypeStruct((T, D), BF16),
                   jax.ShapeDtypeStruct((1, D), F32), jax.ShapeDtypeStruct((1, D), F32),
                   jax.ShapeDtypeStruct((1, D), F32)],
        grid=(SSM_GROUPS, Bl),
        in_specs=[xspec,
                  pl.BlockSpec((None, None, nc, ppg, LANES, SSM_STATE), lambda g, bi: (bi, g, 0, 0, 0, 0)),
                  xspec,
                  pl.BlockSpec((L, SSM_STATE), lambda g, bi: (bi, b0 + g)),
                  pl.BlockSpec((L, SSM_STATE), lambda g, bi: (bi, c0 + g)),
                  pl.BlockSpec((L, gw), lambda g, bi: (bi, d0 + g)),
                  vspec, vspec, vspec],
        out_specs=[xspec, nspec, nspec, xspec, vspec, vspec, vspec],
        scratch_shapes=[pltpu.VMEM((ppg, LANES, SSM_STATE), F32)],
        compiler_params=_cparams(("parallel", "arbitrary")),
    )(dy, states, xbc, xbc, xbc, proj, bias_e, alog_e, dskip_e)


def _att_nl(z):
    return jnp.maximum(z, 0.0) + jnp.log(1.0 + jnp.exp(-jnp.abs(z)))


def _sub_pred(pred):
    r = lax.broadcasted_iota(jnp.int32, (ATT_SUB, ATT_SUB), 0)
    c = lax.broadcasted_iota(jnp.int32, (ATT_SUB, ATT_SUB), 1)
    return pred(r, c)


def _cumsum_mm(x, tri):
    return _dot(x.astype(BF16), tri)


def _att_slices(q0, roff, nrows, k0, nsub):
    qs = pl.ds(pl.multiple_of(q0 + roff, ATT_SUB), nrows)
    ks = pl.ds(pl.multiple_of(k0, ATT_SUB), nsub * ATT_SUB)
    return qs, ks, pl.ds(roff, nrows)


def _attn_fwd(proj, Bl, L, D, qc, kc, vc, *, name):
    npairs = D // LANES
    nq = L // ATT_BLOCK
    nsb = ATT_BLOCK // ATT_SUB
    scale = HEAD_DIM ** -0.5

    def kern(q_ref, k_ref, v_ref, o_ref, tot_ref, qm, kb, vb, carry, acc):
        tri = _sub_pred(lambda t, s: t > s)
        lrow = lax.broadcasted_iota(jnp.int32, (1, LANES), 1) < HEAD_DIM
        q = q_ref[...] * scale
        qm[0] = jnp.where(lrow, q, 0.0).astype(BF16)
        qm[1] = jnp.where(lrow, 0.0, q).astype(BF16)
        kb[...] = k_ref[...].astype(BF16)
        vb[...] = v_ref[...].astype(BF16)
        mat = _mask(_sub_pred(lambda j, s: j >= s))

        def tile(q0, roff, nrows, k0, nsub, diag):
            qs, ks, rows = _att_slices(q0, roff, nrows, k0, nsub)
            for h in range(2):
                z = _dg(qm[h, qs, :], kb[ks, :], _NT)
                nl = _att_nl(z)
                cy = carry[h, rows, :]
                atts = []
                for b in reversed(range(nsub)):
                    cols = slice(b * ATT_SUB, (b + 1) * ATT_SUB)
                    masked = diag and b == nsub - 1
                    x = jnp.where(tri, nl[:, cols], 0.0) if masked else nl[:, cols]
                    inc = _cumsum_mm(x, mat)
                    a = jnp.exp(z[:, cols] - inc - cy)
                    atts.append((jnp.where(tri, a, 0.0) if masked else a).astype(BF16))
                    cy = cy + inc[:, 0:1]
                att = jnp.concatenate(atts[::-1], axis=1) if nsub > 1 else atts[0]
                acc[h, rows, :] += _dot(att, vb[ks, :])
                carry[h, rows, :] = cy

        def qblock(qi, _):
            q0 = qi * ATT_BLOCK
            qs = pl.ds(pl.multiple_of(q0, ATT_BLOCK), ATT_BLOCK)
            carry[...] = jnp.zeros_like(carry)
            acc[...] = jnp.zeros_like(acc)
            for rb in range(nsb):
                tile(q0, rb * ATT_SUB, ATT_SUB, q0, rb + 1, True)

            def kblock(j, _):
                tile(q0, 0, ATT_BLOCK, (qi - 1 - j) * ATT_BLOCK, nsb, False)
                return 0

            lax.fori_loop(0, qi, kblock, 0)
            o_ref[qs, :] = jnp.where(lrow, acc[0], acc[1])
            for h in range(2):
                tot_ref[h, qs, :] = jnp.broadcast_to(carry[h], (ATT_BLOCK, LANES))
            return 0

        lax.fori_loop(0, nq, qblock, 0)

    spec = lambda col: pl.BlockSpec((L, LANES), lambda bi, p: (bi, col // LANES + p))
    return pl.pallas_call(
        kern, name=name,
        out_shape=[jax.ShapeDtypeStruct((Bl * L, D), F32),
                   jax.ShapeDtypeStruct((Bl, npairs, 2, L, LANES), F32)],
        grid=(Bl, npairs),
        in_specs=[spec(qc), spec(kc), spec(vc)],
        out_specs=[pl.BlockSpec((L, LANES), lambda bi, p: (bi, p)),
                   pl.BlockSpec((None, None, 2, L, LANES), lambda bi, p: (bi, p, 0, 0, 0))],
        scratch_shapes=[pltpu.VMEM((2, L, LANES), BF16), pltpu.VMEM((L, LANES), BF16),
                        pltpu.VMEM((L, LANES), BF16), pltpu.VMEM((2, ATT_BLOCK, 1), F32),
                        pltpu.VMEM((2, ATT_BLOCK, LANES), F32)],
        compiler_params=_cparams(("parallel", "parallel")),
    )(proj, proj, proj)


def _attn_bwd(dout, tot, proj, Bl, L, D, qc, kc, vc, *, name):
    npairs = D // LANES
    nq = L // ATT_BLOCK
    nsb = ATT_BLOCK // ATT_SUB
    scale = HEAD_DIM ** -0.5

    def kern(do_ref, tot_ref, q_ref, k_ref, v_ref, dq_ref, dk_ref, dv_ref, qm, km, kb, vb, dom, dkacc, dvacc,
             pre, pde, dqacc):
        tri = _sub_pred(lambda t, s: t > s)
        lrow = lax.broadcasted_iota(jnp.int32, (1, LANES), 1) < HEAD_DIM
        q = q_ref[...] * scale
        qm[0] = jnp.where(lrow, q, 0.0).astype(BF16)
        qm[1] = jnp.where(lrow, 0.0, q).astype(BF16)
        k = k_ref[...]
        kb[...] = k.astype(BF16)
        km[0] = jnp.where(lrow, k, 0.0).astype(BF16)
        km[1] = jnp.where(lrow, 0.0, k).astype(BF16)
        vb[...] = v_ref[...].astype(BF16)
        do = do_ref[...]
        dom[0] = jnp.where(lrow, do, 0.0).astype(BF16)
        dom[1] = jnp.where(lrow, 0.0, do).astype(BF16)
        dkacc[...] = jnp.zeros_like(dkacc)
        dvacc[...] = jnp.zeros_like(dvacc)
        mat = _mask(_sub_pred(lambda j, s: j <= s))

        def tile(q0, roff, nrows, k0, nsub, diag):
            qs, ks, rows = _att_slices(q0, roff, nrows, k0, nsub)
            for h in range(2):
                z = _dg(qm[h, qs, :], kb[ks, :], _NT)
                nl = _att_nl(z)
                lb = z - nl
                datt = _dg(dom[h, qs, :], vb[ks, :], _NT)
                tn = tot_ref[h, qs, 0:1]
                base = pre[h, rows, :] - tn
                pe = pde[h, rows, :]
                atts, dzs = [], []
                for b in range(nsub):
                    cols = slice(b * ATT_SUB, (b + 1) * ATT_SUB)
                    masked = diag and b == nsub - 1
                    x = jnp.where(tri, nl[:, cols], 0.0) if masked else nl[:, cols]
                    pin = _cumsum_mm(x, mat)
                    att = jnp.exp(lb[:, cols] + base + pin)
                    if masked:
                        att = jnp.where(tri, att, 0.0)
                    dE = att * datt[:, cols]
                    pde_in = _cumsum_mm(dE, mat)
                    dz = dE - jnp.exp(lb[:, cols]) * (pe + pde_in)
                    if masked:
                        dz = jnp.where(tri, dz, 0.0)
                    atts.append(att.astype(BF16))
                    dzs.append(dz.astype(BF16))
                    base = base + pin[:, ATT_SUB - 1:ATT_SUB]
                    pe = pe + pde_in[:, ATT_SUB - 1:ATT_SUB]
                pre[h, rows, :] = base + tn
                pde[h, rows, :] = pe
                attb = jnp.concatenate(atts, axis=1) if nsub > 1 else atts[0]
                dzb = jnp.concatenate(dzs, axis=1) if nsub > 1 else dzs[0]
                dqacc[rows, :] += _dot(dzb, km[h, ks, :])
                dkacc[ks, :] += _dg(dzb, qm[h, qs, :], _TN)
                dvacc[ks, :] += _dg(attb, dom[h, qs, :], _TN)

        def qblock(qi, _):
            q0 = qi * ATT_BLOCK
            qs = pl.ds(pl.multiple_of(q0, ATT_BLOCK), ATT_BLOCK)
            pre[...] = jnp.zeros_like(pre)
            pde[...] = jnp.zeros_like(pde)
            dqacc[...] = jnp.zeros_like(dqacc)

            def kblock(kj, _):
                tile(q0, 0, ATT_BLOCK, kj * ATT_BLOCK, nsb, False)
                return 0

            lax.fori_loop(0, qi, kblock, 0)
            for rb in range(nsb):
                tile(q0, rb * ATT_SUB, ATT_SUB, q0, rb + 1, True)
            dq_ref[qs, :] = (dqacc[...] * scale).astype(dq_ref.dtype)
            return 0

        lax.fori_loop(0, nq, qblock, 0)
        dk_ref[...] = dkacc[...].astype(dk_ref.dtype)
        dv_ref[...] = dvacc[...].astype(dv_ref.dtype)

    spec = lambda col: pl.BlockSpec((L, LANES), lambda bi, p: (bi, col // LANES + p))
    ospec = pl.BlockSpec((L, LANES), lambda bi, p: (bi, p))
    T = Bl * L
    return pl.pallas_call(
        kern, name=name,
        out_shape=[jax.ShapeDtypeStruct((T, D), BF16)] * 3,
        grid=(Bl, npairs),
        in_specs=[ospec, pl.BlockSpec((None, None, 2, L, LANES), lambda bi, p: (bi, p, 0, 0, 0)),
                  spec(qc), spec(kc), spec(vc)],
        out_specs=[ospec, ospec, ospec],
        scratch_shapes=[pltpu.VMEM((2, L, LANES), BF16), pltpu.VMEM((2, L, LANES), BF16),
                        pltpu.VMEM((L, LANES), BF16), pltpu.VMEM((L, LANES), BF16),
                        pltpu.VMEM((2, L, LANES), BF16), pltpu.VMEM((L, LANES), F32),
                        pltpu.VMEM((L, LANES), F32), pltpu.VMEM((2, ATT_BLOCK, 1), F32),
                        pltpu.VMEM((2, ATT_BLOCK, 1), F32), pltpu.VMEM((ATT_BLOCK, LANES), F32)],
        compiler_params=_cparams(("parallel", "parallel")),
    )(dout, tot, proj, proj, proj)


def _mix_fwd(y, xbc, proj, yatt, dskip_e, g_ssd, g_att, D, *, name):
    T = y.shape[0]
    tm = _blk(T, 256)

    def kern(y_ref, xs_ref, z_ref, ya_ref, dsk_ref, gs_ref, ga_ref, o_ref):
        z = z_ref[...]
        u = (y_ref[...] + xs_ref[...] * dsk_ref[...]) * (z * _sigmoid(z))
        rs = lax.rsqrt(jnp.mean(u * u, axis=-1, keepdims=True) + EPS)
        o_ref[:, :D] = (u * rs * gs_ref[...]).astype(o_ref.dtype)
        ya = ya_ref[...]
        ra = lax.rsqrt(jnp.mean(ya * ya, axis=-1, keepdims=True) + EPS)
        o_ref[:, D:] = (ya * ra * ga_ref[...]).astype(o_ref.dtype)

    row = pl.BlockSpec((tm, D), lambda i: (i, 0))
    vec = pl.BlockSpec((1, D), lambda i: (0, 0))
    return pl.pallas_call(
        kern, name=name, out_shape=jax.ShapeDtypeStruct((T, 2 * D), BF16), grid=(T // tm,),
        in_specs=[row, row, row, row, vec, vec, vec],
        out_specs=pl.BlockSpec((tm, 2 * D), lambda i: (i, 0)),
        compiler_params=_cparams(("parallel",)),
    )(y, xbc, proj, yatt, dskip_e, g_ssd.reshape(1, D), g_att.reshape(1, D))


def _mix_bwd(dmix, y, xbc, proj, yatt, dskip_e, g_ssd, g_att, D, *, name):
    T = y.shape[0]
    tm = _blk(T, 256)

    def kern(d_ref, y_ref, xs_ref, z_ref, ya_ref, dsk_ref, gs_ref, ga_ref,
             dz_ref, dy_ref, dya_ref, dgs_ref, dga_ref):
        @pl.when(pl.program_id(0) == 0)
        def _():
            dgs_ref[...] = jnp.zeros_like(dgs_ref)
            dga_ref[...] = jnp.zeros_like(dga_ref)

        z = z_ref[...]
        sig = _sigmoid(z)
        sz = z * sig
        ytot = y_ref[...] + xs_ref[...] * dsk_ref[...]
        u = ytot * sz
        rs = lax.rsqrt(jnp.mean(u * u, axis=-1, keepdims=True) + EPS)
        un = u * rs
        do = d_ref[:, :D]
        dog = do * gs_ref[...]
        du = rs * (dog - un * jnp.mean(dog * un, axis=-1, keepdims=True))
        dgs_ref[...] += jnp.sum(do * un, axis=0, keepdims=True)
        dy_ref[...] = du * sz
        dz_ref[...] = (du * ytot * (sig * (1.0 + z * (1.0 - sig)))).astype(dz_ref.dtype)
        ya = ya_ref[...]
        ra = lax.rsqrt(jnp.mean(ya * ya, axis=-1, keepdims=True) + EPS)
        yan = ya * ra
        da = d_ref[:, D:]
        dag = da * ga_ref[...]
        dya_ref[...] = ra * (dag - yan * jnp.mean(dag * yan, axis=-1, keepdims=True))
        dga_ref[...] += jnp.sum(da * yan, axis=0, keepdims=True)

    row = pl.BlockSpec((tm, D), lambda i: (i, 0))
    vec = pl.BlockSpec((1, D), lambda i: (0, 0))
    return pl.pallas_call(
        kern, name=name,
        out_shape=[jax.ShapeDtypeStruct((T, D), BF16), jax.ShapeDtypeStruct((T, D), F32),
                   jax.ShapeDtypeStruct((T, D), F32), jax.ShapeDtypeStruct((1, D), F32),
                   jax.ShapeDtypeStruct((1, D), F32)],
        grid=(T // tm,),
        in_specs=[pl.BlockSpec((tm, 2 * D), lambda i: (i, 0)), row, row, row, row, vec, vec, vec],
        out_specs=[row, row, row, vec, vec],
        compiler_params=_cparams(("arbitrary",)),
    )(dmix, y, xbc, proj, yatt, dskip_e, g_ssd.reshape(1, D), g_att.reshape(1, D))


def _head_sum_mat(D):
    i = lax.broadcasted_iota(jnp.int32, (D, LANES), 0)
    c = lax.broadcasted_iota(jnp.int32, (D, LANES), 1)
    return _mask((i >= c * HEAD_DIM) & (i < (c + 1) * HEAD_DIM))


def _collapse_heads(x, *, name):
    R, D = x.shape
    tm = _blk(R, 256)

    def kern(x_ref, o_ref):
        o_ref[...] = _dot_exact_rhs(x_ref[...], _head_sum_mat(D))

    return pl.pallas_call(
        kern, name=name, out_shape=jax.ShapeDtypeStruct((R, LANES), F32), grid=(R // tm,),
        in_specs=[pl.BlockSpec((tm, D), lambda i: (i, 0))],
        out_specs=pl.BlockSpec((tm, LANES), lambda i: (i, 0)),
        compiler_params=_cparams(("parallel",)),
    )(x)


def _adam_math(w, g, m, v):
    m = ADAM_B1 * m + (1.0 - ADAM_B1) * g
    v = ADAM_B2 * v + (1.0 - ADAM_B2) * (g * g)
    m_hat = m / (1.0 - ADAM_B1 ** ADAM_STEP)
    v_hat = v / (1.0 - ADAM_B2 ** ADAM_STEP)
    delta = -ADAM_LR * (m_hat / (jnp.sqrt(v_hat) + ADAM_EPS) + ADAM_WD * w)
    return delta, m, v


def _adamw_sharded(pieces, w, m, v, *, name):
    R, C = w.shape
    tr = _blk(R, 256) if R % 8 == 0 else R

    def kern(p_ref, w_ref, m_ref, v_ref, g_ref, d_ref, nm_ref, nv_ref):
        g = p_ref[0].astype(F32)
        for j in range(1, N_DEV):
            g = g + p_ref[j].astype(F32)
        d, nm, nv = _adam_math(w_ref[...], g, m_ref[...], v_ref[...])
        g_ref[...] = g
        d_ref[...] = d
        nm_ref[...] = nm
        nv_ref[...] = nv

    row = pl.BlockSpec((tr, C), lambda i: (i, 0))
    return pl.pallas_call(
        kern, name=name, out_shape=[jax.ShapeDtypeStruct((R, C), F32)] * 4, grid=(R // tr,),
        in_specs=[pl.BlockSpec((N_DEV, tr, C), lambda i: (0, i, 0)), row, row, row],
        out_specs=[row] * 4,
        compiler_params=_cparams(("parallel",)),
    )(pieces, w, m, v)


def _small_update(parts, wd, md, vd, we, me, ve, nd, ne, D, *, name):
    def kern(p_ref, wd_ref, md_ref, vd_ref, we_ref, me_ref, ve_ref,
             gd_ref, dd_ref, nmd_ref, nvd_ref, ge_ref, de_ref, nme_ref, nve_ref, loss_ref):
        s = p_ref[0]
        for j in range(1, N_DEV):
            s = s + p_ref[j]
        gd = s[:nd]
        d, nm, nv = _adam_math(wd_ref[...], gd, md_ref[...], vd_ref[...])
        gd_ref[...] = gd
        dd_ref[...] = d
        nmd_ref[...] = nm
        nvd_ref[...] = nv
        ge = _dot_exact_rhs(s[nd:nd + ne], _head_sum_mat(D))
        d, nm, nv = _adam_math(we_ref[...], ge, me_ref[...], ve_ref[...])
        ge_ref[...] = ge
        de_ref[...] = d
        nme_ref[...] = nm
        nve_ref[...] = nv
        tot = jnp.sum(jnp.sum(s[nd + ne:], axis=1, keepdims=True), axis=0, keepdims=True)
        loss_ref[...] = jnp.broadcast_to(tot, loss_ref.shape)

    vm = pl.BlockSpec(memory_space=pltpu.VMEM)
    return pl.pallas_call(
        kern, name=name,
        out_shape=[jax.ShapeDtypeStruct((nd, D), F32)] * 4 + [jax.ShapeDtypeStruct((ne, LANES), F32)] * 4
        + [jax.ShapeDtypeStruct((8, LANES), F32)],
        in_specs=[vm] * 7, out_specs=[vm] * 9,
        compiler_params=pltpu.CompilerParams(vmem_limit_bytes=VMEM_LIMIT),
    )(parts, wd, md, vd, we, me, ve)


def _dev_index(p):
    return 4 * p[0] + 2 * p[1] + p[2]


def _all_gather(arrs, *, name):
    n = len(arrs)

    def body(*refs):
        xs, outs = refs[:n], refs[n:2 * n]
        send, recv, loc = refs[2 * n:]
        x, y, c = lax.axis_index("x"), lax.axis_index("y"), lax.axis_index("c")
        me, sib = (x, y, c), (x, y, 1 - c)
        chips = [(1 - x, y), (x, 1 - y), (1 - x, 1 - y)]

        def cp(k, s, block, to, src=None):
            dst = outs[k].at[_dev_index(block)]
            return pltpu.make_async_remote_copy(
                src_ref=dst if src is None else src, dst_ref=dst,
                send_sem=send.at[7 * k + s], recv_sem=recv.at[7 * k + s],
                device_id=to, device_id_type=pl.DeviceIdType.MESH)

        mine = [pltpu.make_async_copy(xs[k], outs[k].at[_dev_index(me)], loc.at[k]) for k in range(n)]
        for m in mine:
            m.start()
        first = []
        for k in range(n):
            first.append(cp(k, 0, me, sib, src=xs[k]))
            for j, ch in enumerate(chips):
                first.append(cp(k, 1 + j, me, (*ch, c), src=xs[k]))
        for f in first:
            f.start()
        passed = []
        for j, ch in enumerate(chips):
            for k in range(n):
                cp(k, 1 + j, (*ch, c), me).wait_recv()
                p = cp(k, 4 + j, (*ch, c), sib)
                p.start()
                passed.append(p)
        for k in range(n):
            cp(k, 0, sib, me).wait_recv()
            for j, ch in enumerate(chips):
                cp(k, 4 + j, (*ch, 1 - c), me).wait_recv()
        for f in first + passed:
            f.wait_send()
        for m in mine:
            m.wait()

    hbm = pl.BlockSpec(memory_space=pl.ANY)
    return pl.pallas_call(
        body, name=name,
        out_shape=[jax.ShapeDtypeStruct((N_DEV,) + a.shape, a.dtype) for a in arrs],
        in_specs=[hbm] * n, out_specs=[hbm] * n,
        scratch_shapes=[pltpu.SemaphoreType.DMA((7 * n,)), pltpu.SemaphoreType.DMA((7 * n,)),
                        pltpu.SemaphoreType.DMA((n,))],
    )(*arrs)


def _all_to_all(arrs, *, name):
    n = len(arrs)
    rels = [(fx, fy, fc) for fx in (0, 1) for fy in (0, 1) for fc in (0, 1) if fx or fy or fc]

    def body(*refs):
        xs, outs = refs[:n], refs[n:2 * n]
        send, recv, loc = refs[2 * n:]
        x, y, c = lax.axis_index("x"), lax.axis_index("y"), lax.axis_index("c")
        me = _dev_index((x, y, c))

        def peer(rel):
            return (1 - x if rel[0] else x, 1 - y if rel[1] else y, 1 - c if rel[2] else c)

        mine = [pltpu.make_async_copy(xs[k].at[me], outs[k].at[me], loc.at[k]) for k in range(n)]
        for m in mine:
            m.start()
        copies = []
        for k in range(n):
            for s, rel in enumerate(rels):
                p = peer(rel)
                copies.append(pltpu.make_async_remote_copy(
                    src_ref=xs[k].at[_dev_index(p)], dst_ref=outs[k].at[me],
                    send_sem=send.at[7 * k + s], recv_sem=recv.at[7 * k + s],
                    device_id=p, device_id_type=pl.DeviceIdType.MESH))
        for cpy in copies:
            cpy.start()
        for k in range(n):
            for s, rel in enumerate(rels):
                p = peer(rel)
                slot = outs[k].at[_dev_index(p)]
                pltpu.make_async_remote_copy(
                    src_ref=slot, dst_ref=slot, send_sem=send.at[7 * k + s], recv_sem=recv.at[7 * k + s],
                    device_id=p, device_id_type=pl.DeviceIdType.MESH).wait_recv()
        for cpy in copies:
            cpy.wait_send()
        for m in mine:
            m.wait()

    hbm = pl.BlockSpec(memory_space=pl.ANY)
    return pl.pallas_call(
        body, name=name,
        out_shape=[jax.ShapeDtypeStruct(a.shape, a.dtype) for a in arrs],
        in_specs=[hbm] * n, out_specs=[hbm] * n,
        scratch_shapes=[pltpu.SemaphoreType.DMA((7 * n,)), pltpu.SemaphoreType.DMA((7 * n,)),
                        pltpu.SemaphoreType.DMA((n,))],
    )(*arrs)


def _layer_fwd(x, P, dims, l):
    Bl, L, D = dims
    Dc = D + 2 * SSM_GROUPS * SSM_STATE
    qc, kc, vc, dtc = D + Dc, 2 * D + Dc, 3 * D + Dc, 4 * D + Dc
    h = _rmsnorm_fwd(x, P["norm_mix_g"], name=f"norm_mix_fwd_{l}")
    (proj,) = _mm(h, P["w_in"], out_dtypes=[F32], name=f"in_proj_{l}")
    xbc = _conv_fwd(proj, P["conv_w"], P["conv_b"], Bl, L, D, Dc, name=f"conv_fwd_{l}")
    y, states = _ssd_fwd(xbc, proj, P["dt_bias_e"], P["a_log_e"], Bl, L, D, dtc, name=f"ssd_fwd_{l}")
    yatt, tot = _attn_fwd(proj, Bl, L, D, qc, kc, vc, name=f"attn_fwd_{l}")
    ymix = _mix_fwd(y, xbc, proj, yatt, P["d_skip_e"], P["ssd_norm_g"], P["att_norm_g"], D, name=f"mix_fwd_{l}")
    (x1,) = _mm(ymix, P["w_out"], out_dtypes=[F32], epilogue=lambda r, res: (r + res,), extras=(x,),
                name=f"out_proj_{l}")
    h2 = _rmsnorm_fwd(x1, P["norm_mlp_g"], name=f"norm_mlp_fwd_{l}")

    def relu2(r):
        a = jnp.maximum(r, 0.0)
        return r, a * a

    u, act = _mm(h2, P["w_up"], out_dtypes=[F32, BF16], epilogue=relu2, name=f"mlp_up_{l}")
    (x2,) = _mm(act, P["w_down"], out_dtypes=[F32], epilogue=lambda r, res: (r + res,), extras=(x1,),
                name=f"mlp_down_{l}")
    saved = dict(x=x, h=h, proj=proj, xbc=xbc, y=y, states=states, yatt=yatt, tot=tot, ymix=ymix,
                 x1=x1, h2=h2, u=u, act=act)
    return x2, saved


def _layer_bwd(dx2, S, P, dims, l):
    Bl, L, D = dims
    Dc = D + 2 * SSM_GROUPS * SSM_STATE
    qc, kc, vc, dtc = D + Dc, 2 * D + Dc, 3 * D + Dc, 4 * D + Dc
    G = {}
    (du,) = _mm(dx2, P["w_down_t"], out_dtypes=[BF16], extras=(S["u"],),
                epilogue=lambda r, u: (r * (2.0 * jnp.maximum(u, 0.0)),), name=f"mlp_down_bwd_{l}")
    (G["w_down"],) = _mm(S["act"], dx2, ta=True, out_dtypes=[F32], name=f"mlp_down_dw_{l}")
    (dh2,) = _mm(du, P["w_up_t"], out_dtypes=[F32], name=f"mlp_up_bwd_{l}")
    (G["w_up"],) = _mm(S["h2"], du, ta=True, out_dtypes=[F32], name=f"mlp_up_dw_{l}")
    dx1, G["norm_mlp_g"] = _rmsnorm_bwd(dh2, S["x1"], P["norm_mlp_g"], dx2, name=f"norm_mlp_bwd_{l}")
    (dmix,) = _mm(dx1, P["w_out_t"], out_dtypes=[F32], name=f"out_proj_bwd_{l}")
    (G["w_out"],) = _mm(S["ymix"], dx1, ta=True, out_dtypes=[F32], name=f"out_proj_dw_{l}")
    dz, dy, dyatt, G["ssd_norm_g"], G["att_norm_g"] = _mix_bwd(
        dmix, S["y"], S["xbc"], S["proj"], S["yatt"], P["d_skip_e"], P["ssd_norm_g"], P["att_norm_g"], D,
        name=f"mix_bwd_{l}")
    dq, dk, dv = _attn_bwd(dyatt, S["tot"], S["proj"], Bl, L, D, qc, kc, vc, name=f"attn_bwd_{l}")
    dxs, dB, dC, ddt, G["dt_bias_e"], G["a_log_e"], G["d_skip_e"] = _ssd_bwd(
        dy, S["states"], S["xbc"], S["proj"], P["dt_bias_e"], P["a_log_e"], P["d_skip_e"],
        Bl, L, D, dtc, name=f"ssd_bwd_{l}")
    dact = jnp.concatenate([dxs, dB, dC], axis=1)
    dxbc, G["conv_w"], G["conv_b"] = _conv_bwd(dact, S["proj"], P["conv_w"], P["conv_b"], Bl, L, D, Dc,
                                               name=f"conv_bwd_{l}")
    dproj = jnp.concatenate([dz, dxbc, dq, dk, dv, ddt], axis=1)
    (dh,) = _mm(dproj, P["w_in_t"], out_dtypes=[F32], name=f"in_proj_bwd_{l}")
    (G["w_in"],) = _mm(S["h"], dproj, ta=True, out_dtypes=[F32], name=f"in_proj_dw_{l}")
    dx, G["norm_mix_g"] = _rmsnorm_bwd(dh, S["x"], P["norm_mix_g"], dx1, name=f"norm_mix_bwd_{l}")
    return dx, G


def _pad_rows(a, rows):
    return jnp.concatenate([a, jnp.zeros((rows - a.shape[0],) + a.shape[1:], a.dtype)], axis=0)


def kernel(x, norm_mix_g, w_in, conv_w, conv_b, dt_bias, a_log, d_skip, ssd_norm_g, att_norm_g, w_out, norm_mlp_g, w_up, w_down, final_norm_g, loss_target, m_norm_mix_g, m_w_in, m_conv_w, m_conv_b, m_dt_bias, m_a_log, m_d_skip, m_ssd_norm_g, m_att_norm_g, m_w_out, m_norm_mlp_g, m_w_up, m_w_down, m_final_norm_g, v_norm_mix_g, v_w_in, v_conv_w, v_conv_b, v_dt_bias, v_a_log, v_d_skip, v_ssd_norm_g, v_att_norm_g, v_w_out, v_norm_mlp_g, v_w_up, v_w_down, v_final_norm_g):
    Bl, L, D = x.shape
    T = Bl * L
    depth = w_in.shape[0]
    H = D // HEAD_DIM
    Dc = D + 2 * SSM_GROUPS * SSM_STATE
    Dff = w_up.shape[2] * N_DEV
    dims = (Bl, L, D)

    g_in, g_out, g_up, g_down, g_conv = _all_gather(
        [w_in.astype(BF16), w_out.astype(BF16), w_up.astype(BF16), w_down.astype(BF16), conv_w],
        name="gather_weights")
    W_in = g_in.transpose(1, 2, 0, 3).reshape(depth, D, -1)
    o = [0, D, D + Dc, D + Dc + H, 2 * D + Dc + H, 3 * D + Dc + H, 4 * D + Dc + H]
    wz, wxbc, wdt, wq, wk, wv = [W_in[:, :, o[i]:o[i + 1]] for i in range(6)]
    W_pack = jnp.concatenate([wz, wxbc, wq, wk, wv, jnp.repeat(wdt, HEAD_DIM, axis=2)], axis=2)
    W_out = g_out.transpose(1, 0, 2, 3).reshape(depth, 2 * D, D)
    W_up = g_up.transpose(1, 2, 0, 3).reshape(depth, D, Dff)
    W_down = g_down.transpose(1, 0, 2, 3).reshape(depth, Dff, D)
    Cw = g_conv.transpose(1, 2, 0, 3).reshape(depth, CONV_WIDTH, Dc)
    rep = lambda a: jnp.repeat(a, HEAD_DIM, axis=1)
    dt_bias_e, a_log_e, d_skip_e = rep(dt_bias), rep(a_log), rep(d_skip)

    params = []
    for l in range(depth):
        params.append(dict(
            norm_mix_g=norm_mix_g[l], w_in=W_pack[l], w_in_t=W_pack[l].T, conv_w=Cw[l], conv_b=conv_b[l],
            dt_bias_e=dt_bias_e[l:l + 1], a_log_e=a_log_e[l:l + 1], d_skip_e=d_skip_e[l:l + 1],
            ssd_norm_g=ssd_norm_g[l], att_norm_g=att_norm_g[l], w_out=W_out[l], w_out_t=W_out[l].T,
            norm_mlp_g=norm_mlp_g[l], w_up=W_up[l], w_up_t=W_up[l].T, w_down=W_down[l], w_down_t=W_down[l].T))

    xa = x.reshape(T, D)
    saved = []
    for l in range(depth):
        xa, s = _layer_fwd(xa, params[l], dims, l)
        saved.append(s)
    dx, g_final, loss_part = _loss_head(xa, final_norm_g, loss_target.reshape(T, D), name="loss_head")

    grads = [None] * depth
    for l in reversed(range(depth)):
        dx, grads[l] = _layer_bwd(dx, saved[l], params[l], dims, l)
    grad_x = dx.reshape(Bl, L, D)

    st = lambda k: jnp.stack([grads[l][k] for l in range(depth)])
    gW = st("w_in")
    gdt = _collapse_heads(gW[:, :, 4 * D + Dc:].reshape(depth * D, D), name="collapse_w_dt")
    gdt = gdt[:, :H].reshape(depth, D, H)
    gW_in = jnp.concatenate([gW[:, :, :D + Dc], gdt, gW[:, :, D + Dc:4 * D + Dc]], axis=2)
    n_in = w_in.shape[2]
    p_in = gW_in.reshape(depth, D, N_DEV, n_in).transpose(2, 0, 1, 3).astype(BF16)
    p_out = st("w_out").reshape(depth, N_DEV, 2 * D // N_DEV, D).transpose(1, 0, 2, 3).astype(BF16)
    p_up = st("w_up").reshape(depth, D, N_DEV, Dff // N_DEV).transpose(2, 0, 1, 3).astype(BF16)
    p_down = st("w_down").reshape(depth, N_DEV, Dff // N_DEV, D).transpose(1, 0, 2, 3).astype(BF16)
    p_conv = st("conv_w").reshape(depth, CONV_WIDTH, N_DEV, Dc // N_DEV).transpose(2, 0, 1, 3).astype(BF16)
    r_in, r_out, r_up, r_down, r_conv = _all_to_all([p_in, p_out, p_up, p_down, p_conv], name="exchange_grads")

    def sharded(pieces, w, m, v, name):
        shp = w.shape
        C = shp[-1]
        res = _adamw_sharded(pieces.reshape(N_DEV, -1, C), w.reshape(-1, C), m.reshape(-1, C),
                             v.reshape(-1, C), name=name)
        return [a.reshape(shp) for a in res]

    u_in = sharded(r_in, w_in, m_w_in, v_w_in, "adamw_w_in")
    u_out = sharded(r_out, w_out, m_w_out, v_w_out, "adamw_w_out")
    u_up = sharded(r_up, w_up, m_w_up, v_w_up, "adamw_w_up")
    u_down = sharded(r_down, w_down, m_w_down, v_w_down, "adamw_w_down")
    u_conv = sharded(r_conv, conv_w, m_conv_w, v_conv_w, "adamw_conv_w")

    rows = lambda a: a.reshape(-1, D)
    direct_names = ["norm_mix_g", "ssd_norm_g", "att_norm_g", "norm_mlp_g", "conv_b"]
    direct_g = [rows(st(k)) for k in direct_names] + [g_final]
    exp_names = ["dt_bias_e", "a_log_e", "d_skip_e"]
    exp_g = [rows(st(k)) for k in exp_names]
    n_direct = sum(a.shape[0] for a in direct_g)
    nd = -(-n_direct // 8) * 8
    ne = -(-3 * depth // 8) * 8
    part = jnp.concatenate([_pad_rows(jnp.concatenate(direct_g, axis=0), nd),
                            _pad_rows(jnp.concatenate(exp_g, axis=0), ne),
                            _pad_rows(loss_part, 8)], axis=0)
    (parts,) = _all_gather([part], name="gather_small_grads")

    def direct_pack(ws):
        return _pad_rows(jnp.concatenate([rows(a) for a in ws], axis=0), nd)

    def exp_pack(ws):
        a = jnp.concatenate(ws, axis=0)
        a = jnp.concatenate([a, jnp.zeros((a.shape[0], LANES - H), F32)], axis=1)
        return _pad_rows(a, ne)

    res = _small_update(
        parts,
        direct_pack([norm_mix_g, ssd_norm_g, att_norm_g, norm_mlp_g, conv_b, final_norm_g]),
        direct_pack([m_norm_mix_g, m_ssd_norm_g, m_att_norm_g, m_norm_mlp_g, m_conv_b, m_final_norm_g]),
        direct_pack([v_norm_mix_g, v_ssd_norm_g, v_att_norm_g, v_norm_mlp_g, v_conv_b, v_final_norm_g]),
        exp_pack([dt_bias, a_log, d_skip]), exp_pack([m_dt_bias, m_a_log, m_d_skip]),
        exp_pack([v_dt_bias, v_a_log, v_d_skip]), nd, ne, D, name="small_update")
    loss = res[8][0, 0]

    def unpack(kind):
        d, e = res[kind], res[4 + kind]
        out = {}
        r0 = 0
        for nm, shp in [("norm_mix_g", (depth, D)), ("ssd_norm_g", (depth, D)), ("att_norm_g", (depth, D)),
                        ("norm_mlp_g", (depth, D)), ("conv_b", (depth, Dc)), ("final_norm_g", (D,))]:
            n = 1
            for s_ in shp:
                n *= s_
            out[nm] = d[r0:r0 + n // D].reshape(shp)
            r0 += n // D
        for i, nm in enumerate(["dt_bias", "a_log", "d_skip"]):
            out[nm] = e[i * depth:(i + 1) * depth, :H]
        return out

    names = ["norm_mix_g", "w_in", "conv_w", "conv_b", "dt_bias", "a_log", "d_skip", "ssd_norm_g",
             "att_norm_g", "w_out", "norm_mlp_g", "w_up", "w_down", "final_norm_g"]
    big = {"w_in": u_in, "w_out": u_out, "w_up": u_up, "w_down": u_down, "conv_w": u_conv}
    outs = [loss, grad_x]
    for kind in range(4):
        small = unpack(kind)
        for nm in names:
            outs.append(big[nm][kind] if nm in big else small[nm])
    return tuple(outs)
```

```python
import functools

import jax
import jax.numpy as jnp
from jax import lax
from jax.experimental import pallas as pl
from jax.experimental.pallas import tpu as pltpu

F32 = jnp.float32
BF16 = jnp.bfloat16

HEAD_DIM = 64
SSM_STATE = 128
SSM_GROUPS = 4
CONV_WIDTH = 4
CHUNK = 128
ATT_BLOCK = 512
ATT_SUB = 256
EPS = 1e-5
LANES = 128
N_DEV = 8
VMEM_LIMIT = 56 * 1024 * 1024

ADAM_LR = 0.001
ADAM_B1 = 0.9
ADAM_B2 = 0.999
ADAM_EPS = 1e-08
ADAM_WD = 0.01
ADAM_STEP = 10

_NT =(((1,), (1,)), ((), ()))
_TN = (((0,), (0,)), ((), ()))


def _blk(n, pref):
    if n <= pref:
        return n
    b = (pref // LANES) * LANES
    while b >= LANES:
        if n % b == 0:
            return b
        b -= LANES
    return n


def _cparams(sem):
    return pltpu.CompilerParams(dimension_semantics=sem, vmem_limit_bytes=VMEM_LIMIT)


def _dot(a, b):
    return jnp.dot(a, b, preferred_element_type=F32)


def _dg(a, b, dims):
    return lax.dot_general(a, b, dims, preferred_element_type=F32)


def _split3(x):
    h = x.astype(BF16)
    r = x - h.astype(F32)
    m = r.astype(BF16)
    l = (r - m.astype(F32)).astype(BF16)
    return h, m, l


def _dot_exact_rhs(x, c):
    h, m, l = _split3(x)
    return _dot(h, c) + _dot(m, c) + _dot(l, c)


def _dot_exact_lhs(c, x, dims):
    h, m, l = _split3(x)
    return _dg(c, h, dims) + _dg(c, m, dims) + _dg(c, l, dims)


def _mask(cond):
    return jnp.where(cond, 1.0, 0.0).astype(BF16)


def _sigmoid(x):
    return 1.0 / (1.0 + jnp.exp(-x))


def _softplus(x):
    return jnp.maximum(x, 0.0) + jnp.log(1.0 + jnp.exp(-jnp.abs(x)))


def _mm(a, b, *, name, out_dtypes, ta=False, epilogue=None, extras=(), tm=1024, tn=512, tk=2048, comm=None):
    if ta:
        K, M = a.shape
    else:
        M, K = a.shape
    N = b.shape[1]
    assert b.shape[0] == K
    tm, tn, tk = _blk(M, tm), _blk(N, tn), _blk(K, tk)
    nj, ni, nk = N // tn, M // tm, K // tk
    n_ex, n_out = len(extras), len(out_dtypes)
    c_make, c_ins, c_outs, c_alias = comm if comm is not None else (None, [], [], {})
    n_ci, n_co = len(c_ins), len(c_outs)

    def kern(*refs):
        a_ref, b_ref = refs[0], refs[1]
        ex = refs[2:2 + n_ex]
        ci = refs[2 + n_ex:2 + n_ex + n_ci]
        o0 = 2 + n_ex + n_ci
        outs = refs[o0:o0 + n_out]
        co = refs[o0 + n_out:o0 + n_out + n_co]
        acc = refs[o0 + n_out + n_co]
        j, i, k = pl.program_id(0), pl.program_id(1), pl.program_id(2)
        if comm is not None:
            start, finish = c_make(ci, co, *refs[o0 + n_out + n_co + 1:])

            @pl.when((j == 0) & (i == 0) & (k == 0))
            def _():
                start()

        @pl.when(k == 0)
        def _():
            acc[...] = jnp.zeros_like(acc)

        av = a_ref[...].astype(BF16)
        bv = b_ref[...].astype(BF16)
        acc[...] += _dg(av, bv, _TN) if ta else _dot(av, bv)

        @pl.when(k == nk - 1)
        def _():
            r = acc[...]
            vals = epilogue(r, *[e[...] for e in ex]) if epilogue is not None else (r,)
            for o, v in zip(outs, vals):
                o[...] = v.astype(o.dtype)

        if comm is not None:
            @pl.when((j == nj - 1) & (i == ni - 1) & (k == nk - 1))
            def _():
                finish()

    if ta:
        a_spec = pl.BlockSpec((tk, tm), lambda j, i, k: (k, i))
    else:
        a_spec = pl.BlockSpec((tm, tk), lambda j, i, k: (i, k))
    b_spec = pl.BlockSpec((tk, tn), lambda j, i, k: (k, j))
    o_spec = pl.BlockSpec((tm, tn), lambda j, i, k: (i, j))
    hbm = pl.BlockSpec(memory_space=pl.ANY)
    scratch = [pltpu.VMEM((tm, tn), F32)]
    if comm is not None:
        scratch += [pltpu.SemaphoreType.DMA((7 * n_co,)), pltpu.SemaphoreType.DMA((7 * n_co,)),
                    pltpu.SemaphoreType.DMA((n_co,))]
    res = pl.pallas_call(
        kern, name=name,
        out_shape=[jax.ShapeDtypeStruct((M, N), d) for d in out_dtypes] + list(c_outs),
        grid=(nj, ni, nk),
        in_specs=[a_spec, b_spec] + [o_spec] * n_ex + [hbm] * n_ci,
        out_specs=[o_spec] * n_out + [hbm] * n_co,
        scratch_shapes=scratch,
        input_output_aliases={2 + n_ex + ki: n_out + ko for ki, ko in c_alias.items()},
        compiler_params=_cparams(("parallel", "parallel", "arbitrary") if comm is None
                                 else ("arbitrary", "arbitrary", "arbitrary")),
    )(a, b, *extras, *c_ins)
    return res[:n_out], res[n_out:]


def _rmsnorm_fwd(x, g, *, name):
    T, D = x.shape
    tm = _blk(T, 512)

    def kern(x_ref, g_ref, h_ref):
        xv = x_ref[...]
        r = lax.rsqrt(jnp.mean(xv * xv, axis=-1, keepdims=True) + EPS)
        h_ref[...] = (xv * r * g_ref[...]).astype(h_ref.dtype)

    return pl.pallas_call(
        kern, name=name, out_shape=jax.ShapeDtypeStruct((T, D), BF16), grid=(T // tm,),
        in_specs=[pl.BlockSpec((tm, D), lambda i: (i, 0)), pl.BlockSpec((1, D), lambda i: (0, 0))],
        out_specs=pl.BlockSpec((tm, D), lambda i: (i, 0)),
        compiler_params=_cparams(("parallel",)),
    )(x, g.reshape(1, D))


def _rmsnorm_bwd(dh, x, g, dres, *, name):
    T, D = x.shape
    tm = _blk(T, 512)

    def kern(dh_ref, x_ref, g_ref, dres_ref, dx_ref, dg_ref):
        @pl.when(pl.program_id(0) == 0)
        def _():
            dg_ref[...] = jnp.zeros_like(dg_ref)

        xv = x_ref[...]
        r = lax.rsqrt(jnp.mean(xv * xv, axis=-1, keepdims=True) + EPS)
        xn = xv * r
        dy = dh_ref[...].astype(F32)
        dyg = dy * g_ref[...]
        dx = r * (dyg - xn * jnp.mean(dyg * xn, axis=-1, keepdims=True))
        dx_ref[...] = dres_ref[...] + dx
        dg_ref[...] += jnp.sum(dy * xn, axis=0, keepdims=True)

    row = pl.BlockSpec((tm, D), lambda i: (i, 0))
    vec = pl.BlockSpec((1, D), lambda i: (0, 0))
    return pl.pallas_call(
        kern, name=name,
        out_shape=[jax.ShapeDtypeStruct((T, D), F32), jax.ShapeDtypeStruct((1, D), F32)],
        grid=(T // tm,), in_specs=[row, row, vec, row], out_specs=[row, vec],
        compiler_params=_cparams(("arbitrary",)),
    )(dh, x, g.reshape(1, D), dres)


def _loss_head(x, g, target, *, name):
    T, D = x.shape
    tm = _blk(T, 512)

    def kern(x_ref, g_ref, t_ref, dx_ref, dg_ref, loss_ref):
        @pl.when(pl.program_id(0) == 0)
        def _():
            dg_ref[...] = jnp.zeros_like(dg_ref)
            loss_ref[...] = jnp.zeros_like(loss_ref)

        xv = x_ref[...]
        gv = g_ref[...]
        r = lax.rsqrt(jnp.mean(xv * xv, axis=-1, keepdims=True) + EPS)
        xn = xv * r
        err = xn * gv - t_ref[...]
        loss_ref[...] += jnp.sum(err * err, axis=0, keepdims=True) * (0.5 / D)
        dy = err * (1.0 / D)
        dyg = dy * gv
        dx_ref[...] = r * (dyg - xn * jnp.mean(dyg * xn, axis=-1, keepdims=True))
        dg_ref[...] += jnp.sum(dy * xn, axis=0, keepdims=True)

    row = pl.BlockSpec((tm, D), lambda i: (i, 0))
    vec = pl.BlockSpec((1, D), lambda i: (0, 0))
    return pl.pallas_call(
        kern, name=name,
        out_shape=[jax.ShapeDtypeStruct((T, D), F32), jax.ShapeDtypeStruct((1, D), F32),
                   jax.ShapeDtypeStruct((1, D), F32)],
        grid=(T // tm,), in_specs=[row, vec, row], out_specs=[row, vec, vec],
        compiler_params=_cparams(("arbitrary",)),
    )(x, g.reshape(1, D), target)


def _shift_down(u, s, L):
    t = lax.broadcasted_iota(jnp.int32, u.shape, 0)
    return jnp.where(t >= s, pltpu.roll(u, s, 0), 0.0)


def _shift_up(u, s, L):
    t = lax.broadcasted_iota(jnp.int32, u.shape, 0)
    return jnp.where(t < L - s, pltpu.roll(u, L - s, 0), 0.0)


def _conv_pre(u, w_ref, b_ref, L):
    pre = u * w_ref[CONV_WIDTH - 1:CONV_WIDTH, :] + b_ref[...]
    for s in range(1, CONV_WIDTH):
        pre = pre + _shift_down(u, s, L) * w_ref[CONV_WIDTH - 1 - s:CONV_WIDTH - s, :]
    return pre


def _conv_fwd(proj, w, b, Bl, L, col0, Dc, *, name):
    cb = LANES
    c0 = col0 // cb

    def kern(u_ref, w_ref, b_ref, o_ref):
        pre = _conv_pre(u_ref[...], w_ref, b_ref, L)
        o_ref[...] = pre * _sigmoid(pre)

    return pl.pallas_call(
        kern, name=name, out_shape=jax.ShapeDtypeStruct((Bl * L, Dc), F32), grid=(Bl, Dc // cb),
        in_specs=[pl.BlockSpec((L, cb), lambda bi, j: (bi, c0 + j)),
                  pl.BlockSpec((CONV_WIDTH, cb), lambda bi, j: (0, j)),
                  pl.BlockSpec((1, cb), lambda bi, j: (0, j))],
        out_specs=pl.BlockSpec((L, cb), lambda bi, j: (bi, j)),
        compiler_params=_cparams(("parallel", "parallel")),
    )(proj, w, b.reshape(1, Dc))


def _conv_bwd(dact, proj, w, b, Bl, L, col0, Dc, *, name):
    cb = LANES
    c0 = col0 // cb

    def kern(d_ref, u_ref, w_ref, b_ref, du_ref, dw_ref, db_ref):
        @pl.when(pl.program_id(1) == 0)
        def _():
            dw_ref[...] = jnp.zeros_like(dw_ref)
            db_ref[...] = jnp.zeros_like(db_ref)

        u = u_ref[...]
        pre = _conv_pre(u, w_ref, b_ref, L)
        sig = _sigmoid(pre)
        dpre = d_ref[...] * (sig * (1.0 + pre * (1.0 - sig)))
        du = dpre * w_ref[CONV_WIDTH - 1:CONV_WIDTH, :]
        dw_ref[CONV_WIDTH - 1:CONV_WIDTH, :] += jnp.sum(dpre * u, axis=0, keepdims=True)
        for s in range(1, CONV_WIDTH):
            k = CONV_WIDTH - 1 - s
            du = du + _shift_up(dpre, s, L) * w_ref[k:k + 1, :]
            dw_ref[k:k + 1, :] += jnp.sum(dpre * _shift_down(u, s, L), axis=0, keepdims=True)
        db_ref[...] += jnp.sum(dpre, axis=0, keepdims=True)
        du_ref[...] = du.astype(du_ref.dtype)

    return pl.pallas_call(
        kern, name=name,
        out_shape=[jax.ShapeDtypeStruct((Bl * L, Dc), BF16),
                   jax.ShapeDtypeStruct((CONV_WIDTH, Dc), F32), jax.ShapeDtypeStruct((1, Dc), F32)],
        grid=(Dc // cb, Bl),
        in_specs=[pl.BlockSpec((L, cb), lambda j, bi: (bi, j)),
                  pl.BlockSpec((L, cb), lambda j, bi: (bi, c0 + j)),
                  pl.BlockSpec((CONV_WIDTH, cb), lambda j, bi: (0, j)),
                  pl.BlockSpec((1, cb), lambda j, bi: (0, j))],
        out_specs=[pl.BlockSpec((L, cb), lambda j, bi: (bi, j)),
                   pl.BlockSpec((CONV_WIDTH, cb), lambda j, bi: (0, j)),
                   pl.BlockSpec((1, cb), lambda j, bi: (0, j))],
        compiler_params=_cparams(("parallel", "arbitrary")),
    )(dact, proj, w, b.reshape(1, Dc))


def _tri_consts():
    r = lax.broadcasted_iota(jnp.int32, (CHUNK, CHUNK), 0)
    c = lax.broadcasted_iota(jnp.int32, (CHUNK, CHUNK), 1)
    lane0 = c < HEAD_DIM
    return r, c, lane0


def _ssd_chunk_common(x, dtr, bias, alog):
    r, c, _ = _tri_consts()
    dt = _softplus(dtr + bias)
    a = dt * (-jnp.exp(alog))
    tril = _mask(c <= r)
    E = _dot_exact_lhs(tril, a, (((1,), (0,)), ((), ())))
    return dt, E, E.T


def _ssd_head_decay(E, ET, h):
    r, c, _ = _tri_consts()
    cm = jnp.broadcast_to(E[:, h * HEAD_DIM:h * HEAD_DIM + 1], (CHUNK, CHUNK))
    rm = jnp.broadcast_to(ET[h * HEAD_DIM:h * HEAD_DIM + 1, :], (CHUNK, CHUNK))
    return jnp.where(c <= r, jnp.exp(jnp.minimum(cm - rm, 0.0)), 0.0)


def _ssd_fwd(xbc, proj, bias_e, alog_e, Bl, L, D, dt_col0, *, name):
    nc = L // CHUNK
    npairs = D // LANES
    ppg = npairs // SSM_GROUPS
    gw = ppg * LANES
    b0 = D // LANES
    c0 = (D + SSM_GROUPS * SSM_STATE) // LANES
    d0 = dt_col0 // gw

    def kern(x_ref, b_ref, c_ref, dtr_ref, bias_ref, alog_ref, y_ref, st_ref, h_scr):
        _, _, lane0 = _tri_consts()
        h_scr[...] = jnp.zeros_like(h_scr)

        def chunk(ci, carry):
            rows = pl.ds(pl.multiple_of(ci * CHUNK, CHUNK), CHUNK)
            Bb = b_ref[rows, :].astype(BF16)
            Cb = c_ref[rows, :].astype(BF16)
            CB = _dg(Cb, Bb, _NT)
            lanes = [pl.ds(p * LANES, LANES) for p in range(ppg)]
            ins = [(x_ref[rows, ln], dtr_ref[rows, ln], bias_ref[:, ln], alog_ref[:, ln], h_scr[p])
                   for p, ln in enumerate(lanes)]
            outs = []
            for x, dtr, bias, alog, hprev in ins:
                dt, E, ET = _ssd_chunk_common(x, dtr, bias, alog)
                xdt = x * dt
                xdtb = xdt.astype(BF16)
                ys = []
                for h in range(2):
                    Wh = (CB * _ssd_head_decay(E, ET, h)).astype(BF16)
                    ys.append(_dot(Wh, xdtb))
                y = jnp.where(lane0, ys[0], ys[1]) + _dg(Cb, hprev.astype(BF16), _NT) * jnp.exp(E)
                ds = jnp.exp(jnp.broadcast_to(E[CHUNK - 1:CHUNK, :], (CHUNK, LANES)) - E)
                states = _dg((xdt * ds).astype(BF16), Bb, _TN)
                cd = jnp.exp(jnp.broadcast_to(ET[:, CHUNK - 1:CHUNK], (LANES, SSM_STATE)))
                outs.append((hprev, hprev * cd + states, y))
            for p, (hprev, hnew, y) in enumerate(outs):
                st_ref[ci, p] = hprev
                h_scr[p] = hnew
                y_ref[rows, lanes[p]] = y
            return carry

        lax.fori_loop(0, nc, chunk, 0)

    return pl.pallas_call(
        kern, name=name,
        out_shape=[jax.ShapeDtypeStruct((Bl * L, D), F32),
                   jax.ShapeDtypeStruct((Bl, SSM_GROUPS, nc, ppg, LANES, SSM_STATE), F32)],
        grid=(Bl, SSM_GROUPS),
        in_specs=[pl.BlockSpec((L, gw), lambda bi, g: (bi, g)),
                  pl.BlockSpec((L, SSM_STATE), lambda bi, g: (bi, b0 + g)),
                  pl.BlockSpec((L, SSM_STATE), lambda bi, g: (bi, c0 + g)),
                  pl.BlockSpec((L, gw), lambda bi, g: (bi, d0 + g)),
                  pl.BlockSpec((1, gw), lambda bi, g: (0, g)),
                  pl.BlockSpec((1, gw), lambda bi, g: (0, g))],
        out_specs=[pl.BlockSpec((L, gw), lambda bi, g: (bi, g)),
                   pl.BlockSpec((None, None, nc, ppg, LANES, SSM_STATE), lambda bi, g: (bi, g, 0, 0, 0, 0))],
        scratch_shapes=[pltpu.VMEM((ppg, LANES, SSM_STATE), F32)],
        compiler_params=_cparams(("parallel", "parallel")),
    )(xbc, xbc, xbc, proj, bias_e, alog_e)


def _ssd_bwd(dy, states, xbc, proj, bias_e, alog_e, dskip_e, Bl, L, D, dt_col0, *, name):
    nc = L // CHUNK
    npairs = D // LANES
    ppg = npairs // SSM_GROUPS
    gw = ppg * LANES
    b0 = D // LANES
    c0 = (D + SSM_GROUPS * SSM_STATE) // LANES
    d0 = dt_col0 // gw

    def kern(dy_ref, st_ref, x_ref, b_ref, c_ref, dtr_ref, bias_ref, alog_ref, dsk_ref,
             dx_ref, db_ref, dc_ref, ddt_ref, gbias_ref, galog_ref, gdsk_ref, g_scr):
        r, c, lane0 = _tri_consts()
        m0 = lane0.astype(F32)
        masks = (m0, 1.0 - m0)
        triu = _mask(c >= r)
        trils = _mask(c < r)
        ones = jnp.ones((CHUNK, LANES), BF16)
        g_scr[...] = jnp.zeros_like(g_scr)

        @pl.when(pl.program_id(1) == 0)
        def _():
            gbias_ref[...] = jnp.zeros_like(gbias_ref)
            galog_ref[...] = jnp.zeros_like(galog_ref)
            gdsk_ref[...] = jnp.zeros_like(gdsk_ref)

        def chunk(i, carry):
            ci = nc - 1 - i
            rows = pl.ds(pl.multiple_of(ci * CHUNK, CHUNK), CHUNK)
            Bb = b_ref[rows, :].astype(BF16)
            Cb = c_ref[rows, :].astype(BF16)
            CB = _dg(Cb, Bb, _NT)
            gsum = jnp.zeros((CHUNK, CHUNK), F32)
            dB = jnp.zeros((CHUNK, SSM_STATE), F32)
            dC = jnp.zeros((CHUNK, SSM_STATE), F32)
            lanes = [pl.ds(p * LANES, LANES) for p in range(ppg)]
            ins = [(x_ref[rows, ln], dtr_ref[rows, ln], bias_ref[:, ln], alog_ref[:, ln], dy_ref[rows, ln],
                    st_ref[ci, p], g_scr[p], dsk_ref[:, ln]) for p, ln in enumerate(lanes)]
            outs = []
            for x, dtr, bias, alog, dyv, hprev, g, dsk in ins:
                dt, E, ET = _ssd_chunk_common(x, dtr, bias, alog)
                xdt = x * dt
                xdtb = xdt.astype(BF16)
                dyb = dyv.astype(BF16)
                hpb = hprev.astype(BF16)
                gb = g.astype(BF16)
                sd = jnp.exp(E)
                ds = jnp.exp(jnp.broadcast_to(E[CHUNK - 1:CHUNK, :], (CHUNK, LANES)) - E)
                t1 = []
                dE = dyv * (_dg(Cb, hpb, _NT) * sd)
                for h in range(2):
                    Lh = _ssd_head_decay(E, ET, h)
                    Wh = (CB * Lh).astype(BF16)
                    t1.append(_dg(Wh, dyb, _TN))
                    Gh = Lh * _dg((dyv * masks[h]).astype(BF16), xdtb, _NT)
                    gsum = gsum + Gh
                    Qh = Gh * CB
                    dE = dE + _dot_exact_rhs(Qh - Qh.T, _mask(c == h * HEAD_DIM))
                dxst = _dg(Bb, gb, _NT) * ds
                dxdt = jnp.where(lane0, t1[0], t1[1]) + dxst
                dysd = (dyv * sd).astype(BF16)
                dC = dC + _dot(dysd, hpb)
                dB = dB + _dot((xdt * ds).astype(BF16), gb)
                cd = jnp.exp(jnp.broadcast_to(ET[:, CHUNK - 1:CHUNK], (LANES, SSM_STATE)))
                gnew = g * cd + _dg(dysd, Cb, _TN)
                nn = (((1,), (0,)), ((), ()))
                da = (_dot_exact_lhs(triu, dE, nn) + _dot_exact_lhs(trils, xdt * dxst, nn)
                      + _dot_exact_lhs(ones, g * cd * hprev, _NT))
                aneg = -jnp.exp(alog)
                ddt = da * aneg + dxdt * x
                ddtr = ddt * _sigmoid(dtr + bias)
                outs.append((gnew, dxdt * dt + dyv * dsk, ddtr, jnp.sum(ddtr, axis=0, keepdims=True),
                             jnp.sum(da * dt, axis=0, keepdims=True) * aneg,
                             jnp.sum(dyv * x, axis=0, keepdims=True)))
            for p, (gnew, dx, ddtr, sb, sa, sk) in enumerate(outs):
                g_scr[p] = gnew
                dx_ref[rows, lanes[p]] = dx
                ddt_ref[rows, lanes[p]] = ddtr.astype(ddt_ref.dtype)
                gbias_ref[:, lanes[p]] += sb
                galog_ref[:, lanes[p]] += sa
                gdsk_ref[:, lanes[p]] += sk
            gsb = gsum.astype(BF16)
            db_ref[rows, :] = dB + _dg(gsb, Cb, _TN)
            dc_ref[rows, :] = dC + _dot(gsb, Bb)
            return carry

        lax.fori_loop(0, nc, chunk, 0)

    T = Bl * L
    xspec = pl.BlockSpec((L, gw), lambda g, bi: (bi, g))
    nspec = pl.BlockSpec((L, SSM_STATE), lambda g, bi: (bi, g))
    vspec = pl.BlockSpec((1, gw), lambda g, bi: (0, g))
    return pl.pallas_call(
        kern, name=name,
        out_shape=[jax.ShapeDtypeStruct((T, D), F32),
                   jax.ShapeDtypeStruct((T, SSM_GROUPS * SSM_STATE), F32),
                   jax.ShapeDtypeStruct((T, SSM_GROUPS * SSM_STATE), F32),
                   jax.ShapeDtypeStruct((T, D), BF16),
                   jax.ShapeDtypeStruct((1, D), F32), jax.ShapeDtypeStruct((1, D), F32),
                   jax.ShapeDtypeStruct((1, D), F32)],
        grid=(SSM_GROUPS, Bl),
        in_specs=[xspec,
                  pl.BlockSpec((None, None, nc, ppg, LANES, SSM_STATE), lambda g, bi: (bi, g, 0, 0, 0, 0)),
                  xspec,
                  pl.BlockSpec((L, SSM_STATE), lambda g, bi: (bi, b0 + g)),
                  pl.BlockSpec((L, SSM_STATE), lambda g, bi: (bi, c0 + g)),
                  pl.BlockSpec((L, gw), lambda g, bi: (bi, d0 + g)),
                  vspec, vspec, vspec],
        out_specs=[xspec, nspec, nspec, xspec, vspec, vspec, vspec],
        scratch_shapes=[pltpu.VMEM((ppg, LANES, SSM_STATE), F32)],
        compiler_params=_cparams(("parallel", "arbitrary")),
    )(dy, states, xbc, xbc, xbc, proj, bias_e, alog_e, dskip_e)


def _att_nl(z):
    return jnp.maximum(z, 0.0) + jnp.log(1.0 + jnp.exp(-jnp.abs(z)))


def _sub_pred(pred):
    r = lax.broadcasted_iota(jnp.int32, (ATT_SUB, ATT_SUB), 0)
    c = lax.broadcasted_iota(jnp.int32, (ATT_SUB, ATT_SUB), 1)
    return pred(r, c)


def _cumsum_mm(x, tri):
    return _dot(x.astype(BF16), tri)


def _att_slices(q0, roff, nrows, k0, nsub):
    qs = pl.ds(pl.multiple_of(q0 + roff, ATT_SUB), nrows)
    ks = pl.ds(pl.multiple_of(k0, ATT_SUB), nsub * ATT_SUB)
    return qs, ks, pl.ds(roff, nrows)


def _attn_fwd(proj, Bl, L, D, qc, kc, vc, *, name):
    npairs = D // LANES
    nq = L // ATT_BLOCK
    nsb = ATT_BLOCK // ATT_SUB
    scale = HEAD_DIM ** -0.5

    def kern(q_ref, k_ref, v_ref, o_ref, tot_ref, qm, kb, vb, carry, acc):
        tri = _sub_pred(lambda t, s: t > s)
        lrow = lax.broadcasted_iota(jnp.int32, (1, LANES), 1) < HEAD_DIM
        q = q_ref[...] * scale
        qm[0] = jnp.where(lrow, q, 0.0).astype(BF16)
        qm[1] = jnp.where(lrow, 0.0, q).astype(BF16)
        kb[...] = k_ref[...].astype(BF16)
        vb[...] = v_ref[...].astype(BF16)
        mat = _mask(_sub_pred(lambda j, s: j >= s))

        def tile(q0, roff, nrows, k0, nsub, diag):
            qs, ks, rows = _att_slices(q0, roff, nrows, k0, nsub)
            for h in range(2):
                z = _dg(qm[h, qs, :], kb[ks, :], _NT)
                nl = _att_nl(z)
                cy = carry[h, rows, :]
                atts = []
                for b in reversed(range(nsub)):
                    cols = slice(b * ATT_SUB, (b + 1) * ATT_SUB)
                    masked = diag and b == nsub - 1
                    x = jnp.where(tri, nl[:, cols], 0.0) if masked else nl[:, cols]
                    inc = _cumsum_mm(x, mat)
                    a = jnp.exp(z[:, cols] - inc - cy)
                    atts.append((jnp.where(tri, a, 0.0) if masked else a).astype(BF16))
                    cy = cy + inc[:, 0:1]
                att = jnp.concatenate(atts[::-1], axis=1) if nsub > 1 else atts[0]
                acc[h, rows, :] += _dot(att, vb[ks, :])
                carry[h, rows, :] = cy

        def qblock(qi, _):
            q0 = qi * ATT_BLOCK
            qs = pl.ds(pl.multiple_of(q0, ATT_BLOCK), ATT_BLOCK)
            carry[...] = jnp.zeros_like(carry)
            acc[...] = jnp.zeros_like(acc)
            for rb in range(nsb):
                tile(q0, rb * ATT_SUB, ATT_SUB, q0, rb + 1, True)

            def kblock(j, _):
                tile(q0, 0, ATT_BLOCK, (qi - 1 - j) * ATT_BLOCK, nsb, False)
                return 0

            lax.fori_loop(0, qi, kblock, 0)
            o_ref[qs, :] = jnp.where(lrow, acc[0], acc[1])
            for h in range(2):
                tot_ref[h, qs, :] = jnp.broadcast_to(carry[h], (ATT_BLOCK, LANES))
            return 0

        lax.fori_loop(0, nq, qblock, 0)

    spec = lambda col: pl.BlockSpec((L, LANES), lambda bi, p: (bi, col // LANES + p))
    return pl.pallas_call(
        kern, name=name,
        out_shape=[jax.ShapeDtypeStruct((Bl * L, D), F32),
                   jax.ShapeDtypeStruct((Bl, npairs, 2, L, LANES), F32)],
        grid=(Bl, npairs),
        in_specs=[spec(qc), spec(kc), spec(vc)],
        out_specs=[pl.BlockSpec((L, LANES), lambda bi, p: (bi, p)),
                   pl.BlockSpec((None, None, 2, L, LANES), lambda bi, p: (bi, p, 0, 0, 0))],
        scratch_shapes=[pltpu.VMEM((2, L, LANES), BF16), pltpu.VMEM((L, LANES), BF16),
                        pltpu.VMEM((L, LANES), BF16), pltpu.VMEM((2, ATT_BLOCK, 1), F32),
                        pltpu.VMEM((2, ATT_BLOCK, LANES), F32)],
        compiler_params=_cparams(("parallel", "parallel")),
    )(proj, proj, proj)


def _attn_bwd(dout, tot, proj, Bl, L, D, qc, kc, vc, *, name):
    npairs = D // LANES
    nq = L // ATT_BLOCK
    nsb = ATT_BLOCK // ATT_SUB
    scale = HEAD_DIM ** -0.5

    def kern(do_ref, tot_ref, q_ref, k_ref, v_ref, dq_ref, dk_ref, dv_ref, qm, km, kb, vb, dom, dkacc, dvacc,
             pre, pde, dqacc):
        tri = _sub_pred(lambda t, s: t > s)
        lrow = lax.broadcasted_iota(jnp.int32, (1, LANES), 1) < HEAD_DIM
        q = q_ref[...] * scale
        qm[0] = jnp.where(lrow, q, 0.0).astype(BF16)
        qm[1] = jnp.where(lrow, 0.0, q).astype(BF16)
        k = k_ref[...]
        kb[...] = k.astype(BF16)
        km[0] = jnp.where(lrow, k, 0.0).astype(BF16)
        km[1] = jnp.where(lrow, 0.0, k).astype(BF16)
        vb[...] = v_ref[...].astype(BF16)
        do = do_ref[...]
        dom[0] = jnp.where(lrow, do, 0.0).astype(BF16)
        dom[1] = jnp.where(lrow, 0.0, do).astype(BF16)
        dkacc[...] = jnp.zeros_like(dkacc)
        dvacc[...] = jnp.zeros_like(dvacc)
        mat = _mask(_sub_pred(lambda j, s: j <= s))

        def tile(q0, roff, nrows, k0, nsub, diag):
            qs, ks, rows = _att_slices(q0, roff, nrows, k0, nsub)
            for h in range(2):
                z = _dg(qm[h, qs, :], kb[ks, :], _NT)
                nl = _att_nl(z)
                lb = z - nl
                datt = _dg(dom[h, qs, :], vb[ks, :], _NT)
                tn = tot_ref[h, qs, 0:1]
                base = pre[h, rows, :] - tn
                pe = pde[h, rows, :]
                atts, dzs = [], []
                for b in range(nsub):
                    cols = slice(b * ATT_SUB, (b + 1) * ATT_SUB)
                    masked = diag and b == nsub - 1
                    x = jnp.where(tri, nl[:, cols], 0.0) if masked else nl[:, cols]
                    pin = _cumsum_mm(x, mat)
                    att = jnp.exp(lb[:, cols] + base + pin)
                    if masked:
                        att = jnp.where(tri, att, 0.0)
                    dE = att * datt[:, cols]
                    pde_in = _cumsum_mm(dE, mat)
                    dz = dE - jnp.exp(lb[:, cols]) * (pe + pde_in)
                    if masked:
                        dz = jnp.where(tri, dz, 0.0)
                    atts.append(att.astype(BF16))
                    dzs.append(dz.astype(BF16))
                    base = base + pin[:, ATT_SUB - 1:ATT_SUB]
                    pe = pe + pde_in[:, ATT_SUB - 1:ATT_SUB]
                pre[h, rows, :] = base + tn
                pde[h, rows, :] = pe
                attb = jnp.concatenate(atts, axis=1) if nsub > 1 else atts[0]
                dzb = jnp.concatenate(dzs, axis=1) if nsub > 1 else dzs[0]
                dqacc[rows, :] += _dot(dzb, km[h, ks, :])
                dkacc[ks, :] += _dg(dzb, qm[h, qs, :], _TN)
                dvacc[ks, :] += _dg(attb, dom[h, qs, :], _TN)

        def qblock(qi, _):
            q0 = qi * ATT_BLOCK
            qs = pl.ds(pl.multiple_of(q0, ATT_BLOCK), ATT_BLOCK)
            pre[...] = jnp.zeros_like(pre)
            pde[...] = jnp.zeros_like(pde)
            dqacc[...] = jnp.zeros_like(dqacc)

            def kblock(kj, _):
                tile(q0, 0, ATT_BLOCK, kj * ATT_BLOCK, nsb, False)
                return 0

            lax.fori_loop(0, qi, kblock, 0)
            for rb in range(nsb):
                tile(q0, rb * ATT_SUB, ATT_SUB, q0, rb + 1, True)
            dq_ref[qs, :] = (dqacc[...] * scale).astype(dq_ref.dtype)
            return 0

        lax.fori_loop(0, nq, qblock, 0)
        dk_ref[...] = dkacc[...].astype(dk_ref.dtype)
        dv_ref[...] = dvacc[...].astype(dv_ref.dtype)

    spec = lambda col: pl.BlockSpec((L, LANES), lambda bi, p: (bi, col // LANES + p))
    ospec = pl.BlockSpec((L, LANES), lambda bi, p: (bi, p))
    T = Bl * L
    return pl.pallas_call(
        kern, name=name,
        out_shape=[jax.ShapeDtypeStruct((T, D), BF16)] * 3,
        grid=(Bl, npairs),
        in_specs=[ospec, pl.BlockSpec((None, None, 2, L, LANES), lambda bi, p: (bi, p, 0, 0, 0)),
                  spec(qc), spec(kc), spec(vc)],
        out_specs=[ospec, ospec, ospec],
        scratch_shapes=[pltpu.VMEM((2, L, LANES), BF16), pltpu.VMEM((2, L, LANES), BF16),
                        pltpu.VMEM((L, LANES), BF16), pltpu.VMEM((L, LANES), BF16),
                        pltpu.VMEM((2, L, LANES), BF16), pltpu.VMEM((L, LANES), F32),
                        pltpu.VMEM((L, LANES), F32), pltpu.VMEM((2, ATT_BLOCK, 1), F32),
                        pltpu.VMEM((2, ATT_BLOCK, 1), F32), pltpu.VMEM((ATT_BLOCK, LANES), F32)],
        compiler_params=_cparams(("parallel", "parallel")),
    )(dout, tot, proj, proj, proj)


def _mix_fwd(y, xbc, proj, yatt, dskip_e, g_ssd, g_att, D, *, name):
    T = y.shape[0]
    tm = _blk(T, 256)

    def kern(y_ref, xs_ref, z_ref, ya_ref, dsk_ref, gs_ref, ga_ref, o_ref):
        z = z_ref[...]
        u = (y_ref[...] + xs_ref[...] * dsk_ref[...]) * (z * _sigmoid(z))
        rs = lax.rsqrt(jnp.mean(u * u, axis=-1, keepdims=True) + EPS)
        o_ref[:, :D] = (u * rs * gs_ref[...]).astype(o_ref.dtype)
        ya = ya_ref[...]
        ra = lax.rsqrt(jnp.mean(ya * ya, axis=-1, keepdims=True) + EPS)
        o_ref[:, D:] = (ya * ra * ga_ref[...]).astype(o_ref.dtype)

    row = pl.BlockSpec((tm, D), lambda i: (i, 0))
    vec = pl.BlockSpec((1, D), lambda i: (0, 0))
    return pl.pallas_call(
        kern, name=name, out_shape=jax.ShapeDtypeStruct((T, 2 * D), BF16), grid=(T // tm,),
        in_specs=[row, row, row, row, vec, vec, vec],
        out_specs=pl.BlockSpec((tm, 2 * D), lambda i: (i, 0)),
        compiler_params=_cparams(("parallel",)),
    )(y, xbc, proj, yatt, dskip_e, g_ssd.reshape(1, D), g_att.reshape(1, D))


def _mix_bwd(dmix, y, xbc, proj, yatt, dskip_e, g_ssd, g_att, D, *, name):
    T = y.shape[0]
    tm = _blk(T, 256)

    def kern(d_ref, y_ref, xs_ref, z_ref, ya_ref, dsk_ref, gs_ref, ga_ref,
             dz_ref, dy_ref, dya_ref, dgs_ref, dga_ref):
        @pl.when(pl.program_id(0) == 0)
        def _():
            dgs_ref[...] = jnp.zeros_like(dgs_ref)
            dga_ref[...] = jnp.zeros_like(dga_ref)

        z = z_ref[...]
        sig = _sigmoid(z)
        sz = z * sig
        ytot = y_ref[...] + xs_ref[...] * dsk_ref[...]
        u = ytot * sz
        rs = lax.rsqrt(jnp.mean(u * u, axis=-1, keepdims=True) + EPS)
        un = u * rs
        do = d_ref[:, :D]
        dog = do * gs_ref[...]
        du = rs * (dog - un * jnp.mean(dog * un, axis=-1, keepdims=True))
        dgs_ref[...] += jnp.sum(do * un, axis=0, keepdims=True)
        dy_ref[...] = du * sz
        dz_ref[...] = (du * ytot * (sig * (1.0 + z * (1.0 - sig)))).astype(dz_ref.dtype)
        ya = ya_ref[...]
        ra = lax.rsqrt(jnp.mean(ya * ya, axis=-1, keepdims=True) + EPS)
        yan = ya * ra
        da = d_ref[:, D:]
        dag = da * ga_ref[...]
        dya_ref[...] = ra * (dag - yan * jnp.mean(dag * yan, axis=-1, keepdims=True))
        dga_ref[...] += jnp.sum(da * yan, axis=0, keepdims=True)

    row = pl.BlockSpec((tm, D), lambda i: (i, 0))
    vec = pl.BlockSpec((1, D), lambda i: (0, 0))
    return pl.pallas_call(
        kern, name=name,
        out_shape=[jax.ShapeDtypeStruct((T, D), BF16), jax.ShapeDtypeStruct((T, D), F32),
                   jax.ShapeDtypeStruct((T, D), F32), jax.ShapeDtypeStruct((1, D), F32),
                   jax.ShapeDtypeStruct((1, D), F32)],
        grid=(T // tm,),
        in_specs=[pl.BlockSpec((tm, 2 * D), lambda i: (i, 0)), row, row, row, row, vec, vec, vec],
        out_specs=[row, row, row, vec, vec],
        compiler_params=_cparams(("arbitrary",)),
    )(dmix, y, xbc, proj, yatt, dskip_e, g_ssd.reshape(1, D), g_att.reshape(1, D))


def _head_sum_mat(D):
    i = lax.broadcasted_iota(jnp.int32, (D, LANES), 0)
    c = lax.broadcasted_iota(jnp.int32, (D, LANES), 1)
    return _mask((i >= c * HEAD_DIM) & (i < (c + 1) * HEAD_DIM))


def _collapse_heads(x, *, name):
    R, D = x.shape
    tm = _blk(R, 256)

    def kern(x_ref, o_ref):
        o_ref[...] = _dot_exact_rhs(x_ref[...], _head_sum_mat(D))

    return pl.pallas_call(
        kern, name=name, out_shape=jax.ShapeDtypeStruct((R, LANES), F32), grid=(R // tm,),
        in_specs=[pl.BlockSpec((tm, D), lambda i: (i, 0))],
        out_specs=pl.BlockSpec((tm, LANES), lambda i: (i, 0)),
        compiler_params=_cparams(("parallel",)),
    )(x)


def _adam_math(w, g, m, v):
    m = ADAM_B1 * m + (1.0 - ADAM_B1) * g
    v = ADAM_B2 * v + (1.0 - ADAM_B2) * (g * g)
    m_hat = m / (1.0 - ADAM_B1 ** ADAM_STEP)
    v_hat = v / (1.0 - ADAM_B2 ** ADAM_STEP)
    delta = -ADAM_LR * (m_hat / (jnp.sqrt(v_hat) + ADAM_EPS) + ADAM_WD * w)
    return delta, m, v


def _adamw_sharded(pieces, w, m, v, *, name):
    R, C = w.shape
    tr = _blk(R, 256) if R % 8 == 0 else R

    def kern(p_ref, w_ref, m_ref, v_ref, g_ref, d_ref, nm_ref, nv_ref):
        g = p_ref[0].astype(F32)
        for j in range(1, N_DEV):
            g = g + p_ref[j].astype(F32)
        d, nm, nv = _adam_math(w_ref[...], g, m_ref[...], v_ref[...])
        g_ref[...] = g
        d_ref[...] = d
        nm_ref[...] = nm
        nv_ref[...] = nv

    row = pl.BlockSpec((tr, C), lambda i: (i, 0))
    return pl.pallas_call(
        kern, name=name, out_shape=[jax.ShapeDtypeStruct((R, C), F32)] * 4, grid=(R // tr,),
        in_specs=[pl.BlockSpec((N_DEV, tr, C), lambda i: (0, i, 0)), row, row, row],
        out_specs=[row] * 4,
        compiler_params=_cparams(("parallel",)),
    )(pieces, w, m, v)


def _small_update(parts, wd, md, vd, we, me, ve, nd, ne, D, *, name):
    def kern(p_ref, wd_ref, md_ref, vd_ref, we_ref, me_ref, ve_ref,
             gd_ref, dd_ref, nmd_ref, nvd_ref, ge_ref, de_ref, nme_ref, nve_ref, loss_ref):
        s = p_ref[0]
        for j in range(1, N_DEV):
            s = s + p_ref[j]
        gd = s[:nd]
        d, nm, nv = _adam_math(wd_ref[...], gd, md_ref[...], vd_ref[...])
        gd_ref[...] = gd
        dd_ref[...] = d
        nmd_ref[...] = nm
        nvd_ref[...] = nv
        ge = _dot_exact_rhs(s[nd:nd + ne], _head_sum_mat(D))
        d, nm, nv = _adam_math(we_ref[...], ge, me_ref[...], ve_ref[...])
        ge_ref[...] = ge
        de_ref[...] = d
        nme_ref[...] = nm
        nve_ref[...] = nv
        tot = jnp.sum(jnp.sum(s[nd + ne:], axis=1, keepdims=True), axis=0, keepdims=True)
        loss_ref[...] = jnp.broadcast_to(tot, loss_ref.shape)

    vm = pl.BlockSpec(memory_space=pltpu.VMEM)
    return pl.pallas_call(
        kern, name=name,
        out_shape=[jax.ShapeDtypeStruct((nd, D), F32)] * 4 + [jax.ShapeDtypeStruct((ne, LANES), F32)] * 4
        + [jax.ShapeDtypeStruct((8, LANES), F32)],
        in_specs=[vm] * 7, out_specs=[vm] * 9,
        compiler_params=pltpu.CompilerParams(vmem_limit_bytes=VMEM_LIMIT),
    )(parts, wd, md, vd, we, me, ve)


def _dev_index(p):
    return 4 * p[0] + 2 * p[1] + p[2]


def _all_gather(arrs, *, name):
    n = len(arrs)

    def body(*refs):
        xs, outs = refs[:n], refs[n:2 * n]
        send, recv, loc = refs[2 * n:]
        x, y, c = lax.axis_index("x"), lax.axis_index("y"), lax.axis_index("c")
        me, sib = (x, y, c), (x, y, 1 - c)
        chips = [(1 - x, y), (x, 1 - y), (1 - x, 1 - y)]

        def cp(k, s, block, to, src=None):
            dst = outs[k].at[_dev_index(block)]
            return pltpu.make_async_remote_copy(
                src_ref=dst if src is None else src, dst_ref=dst,
                send_sem=send.at[7 * k + s], recv_sem=recv.at[7 * k + s],
                device_id=to, device_id_type=pl.DeviceIdType.MESH)

        mine = [pltpu.make_async_copy(xs[k], outs[k].at[_dev_index(me)], loc.at[k]) for k in range(n)]
        for m in mine:
            m.start()
        first = []
        for k in range(n):
            first.append(cp(k, 0, me, sib, src=xs[k]))
            for j, ch in enumerate(chips):
                first.append(cp(k, 1 + j, me, (*ch, c), src=xs[k]))
        for f in first:
            f.start()
        passed = []
        for j, ch in enumerate(chips):
            for k in range(n):
                cp(k, 1 + j, (*ch, c), me).wait_recv()
                p = cp(k, 4 + j, (*ch, c), sib)
                p.start()
                passed.append(p)
        for k in range(n):
            cp(k, 0, sib, me).wait_recv()
            for j, ch in enumerate(chips):
                cp(k, 4 + j, (*ch, 1 - c), me).wait_recv()
        for f in first + passed:
            f.wait_send()
        for m in mine:
            m.wait()

    hbm = pl.BlockSpec(memory_space=pl.ANY)
    return pl.pallas_call(
        body, name=name,
        out_shape=[jax.ShapeDtypeStruct((N_DEV,) + a.shape, a.dtype) for a in arrs],
        in_specs=[hbm] * n, out_specs=[hbm] * n,
        scratch_shapes=[pltpu.SemaphoreType.DMA((7 * n,)), pltpu.SemaphoreType.DMA((7 * n,)),
                        pltpu.SemaphoreType.DMA((n,))],
    )(*arrs)


_RELS = [(fx, fy, fc) for fx in (0, 1) for fy in (0, 1) for fc in (0, 1) if fx or fy or fc]


def _direct_ops(src_of, dst_of, n_in):
    def make(ins, outs, send, recv, loc):
        xs, n = ins[:n_in], len(outs)
        x, y, c = lax.axis_index("x"), lax.axis_index("y"), lax.axis_index("c")
        me = _dev_index((x, y, c))
        peers = [(1 - x if r[0] else x, 1 - y if r[1] else y, 1 - c if r[2] else c) for r in _RELS]

        def remote(k, s, src, dst):
            return pltpu.make_async_remote_copy(
                src_ref=src, dst_ref=dst, send_sem=send.at[7 * k + s], recv_sem=recv.at[7 * k + s],
                device_id=peers[s], device_id_type=pl.DeviceIdType.MESH)

        def local(k):
            return pltpu.make_async_copy(src_of(xs, k, me, me), dst_of(outs, k, me), loc.at[k])

        def sends():
            return [remote(k, s, src_of(xs, k, me, _dev_index(p)), dst_of(outs, k, me))
                    for k in range(n) for s, p in enumerate(peers)]

        def start():
            for k in range(n):
                local(k).start()
            for cp in sends():
                cp.start()

        def finish():
            for k in range(n):
                for s, p in enumerate(peers):
                    slot = dst_of(outs, k, _dev_index(p))
                    remote(k, s, slot, slot).wait_recv()
            for cp in sends():
                cp.wait_send()
            for k in range(n):
                local(k).wait()

        return start, finish

    return make


def _gather_comm(arrs):
    make = _direct_ops(lambda xs, k, me, p: xs[k], lambda outs, k, j: outs[k].at[j], len(arrs))
    return make, list(arrs), [jax.ShapeDtypeStruct((N_DEV,) + a.shape, a.dtype) for a in arrs], {}


def _exchange_comm(pieces, bufs, layer):
    n = len(pieces)
    make = _direct_ops(lambda xs, k, me, p: xs[k].at[p], lambda outs, k, j: outs[k].at[j, layer], n)
    return (make, list(pieces) + list(bufs), [jax.ShapeDtypeStruct(b.shape, b.dtype) for b in bufs],
            {n + k: k for k in range(n)})


def _comm_call(comm, *, name):
    make, ins, out_shapes, alias = comm
    n_ci, n_co = len(ins), len(out_shapes)

    def body(*refs):
        start, finish = make(refs[:n_ci], refs[n_ci:n_ci + n_co], *refs[n_ci + n_co:])
        start()
        finish()

    hbm = pl.BlockSpec(memory_space=pl.ANY)
    return pl.pallas_call(
        body, name=name, out_shape=list(out_shapes), in_specs=[hbm] * n_ci, out_specs=[hbm] * n_co,
        scratch_shapes=[pltpu.SemaphoreType.DMA((7 * n_co,)), pltpu.SemaphoreType.DMA((7 * n_co,)),
                        pltpu.SemaphoreType.DMA((n_co,))],
        input_output_aliases=alias,
    )(*ins)


def _layer_fwd(x, P, dims, l, nxt):
    Bl, L, D = dims
    Dc = D + 2 * SSM_GROUPS * SSM_STATE
    qc, kc, vc, dtc = D + Dc, 2 * D + Dc, 3 * D + Dc, 4 * D + Dc
    gat = (lambda key: _gather_comm([nxt[key]])) if nxt is not None else (lambda key: None)
    h = _rmsnorm_fwd(x, P["norm_mix_g"], name=f"norm_mix_fwd_{l}")
    (proj,), g_in = _mm(h, P["w_in"], out_dtypes=[F32], name=f"in_proj_{l}", comm=gat("w_in"))
    xbc = _conv_fwd(proj, P["conv_w"], P["conv_b"], Bl, L, D, Dc, name=f"conv_fwd_{l}")
    y, states = _ssd_fwd(xbc, proj, P["dt_bias_e"], P["a_log_e"], Bl, L, D, dtc, name=f"ssd_fwd_{l}")
    yatt, tot = _attn_fwd(proj, Bl, L, D, qc, kc, vc, name=f"attn_fwd_{l}")
    ymix = _mix_fwd(y, xbc, proj, yatt, P["d_skip_e"], P["ssd_norm_g"], P["att_norm_g"], D, name=f"mix_fwd_{l}")
    (x1,), g_out = _mm(ymix, P["w_out"], out_dtypes=[F32], epilogue=lambda r, res: (r + res,), extras=(x,),
                       name=f"out_proj_{l}", comm=gat("w_out"))
    h2 = _rmsnorm_fwd(x1, P["norm_mlp_g"], name=f"norm_mlp_fwd_{l}")

    def relu2(r):
        a = jnp.maximum(r, 0.0)
        return r, a * a

    (u, act), g_up = _mm(h2, P["w_up"], out_dtypes=[F32, BF16], epilogue=relu2, name=f"mlp_up_{l}",
                         comm=gat("w_up"))
    (x2,), g_down = _mm(act, P["w_down"], out_dtypes=[F32], epilogue=lambda r, res: (r + res,), extras=(x1,),
                        name=f"mlp_down_{l}", comm=gat("w_down"))
    saved = dict(x=x, h=h, proj=proj, xbc=xbc, y=y, states=states, yatt=yatt, tot=tot, ymix=ymix,
                 x1=x1, h2=h2, u=u, act=act)
    gathered = dict(w_in=g_in[0], w_out=g_out[0], w_up=g_up[0], w_down=g_down[0]) if nxt is not None else None
    return x2, saved, gathered


def _layer_bwd(dx2, S, P, dims, l, pend, bufs):
    Bl, L, D = dims
    Dc = D + 2 * SSM_GROUPS * SSM_STATE
    qc, kc, vc, dtc = D + Dc, 2 * D + Dc, 3 * D + Dc, 4 * D + Dc
    G = {}

    def exch(keys):
        if pend is None:
            return None
        return _exchange_comm([pend[k] for k in keys], [bufs[k] for k in keys], l + 1)

    def took(keys, outs):
        if pend is not None:
            bufs.update(zip(keys, outs))

    (du,), o = _mm(dx2, P["w_down_t"], out_dtypes=[BF16], extras=(S["u"],),
                   epilogue=lambda r, u: (r * (2.0 * jnp.maximum(u, 0.0)),), name=f"mlp_down_bwd_{l}",
                   comm=exch(["w_in"]))
    took(["w_in"], o)
    (G["w_down"],), o = _mm(S["act"], dx2, ta=True, out_dtypes=[F32], name=f"mlp_down_dw_{l}",
                            comm=exch(["w_up"]))
    took(["w_up"], o)
    (dh2,), o = _mm(du, P["w_up_t"], out_dtypes=[F32], name=f"mlp_up_bwd_{l}", comm=exch(["w_down"]))
    took(["w_down"], o)
    (G["w_up"],), o = _mm(S["h2"], du, ta=True, out_dtypes=[F32], name=f"mlp_up_dw_{l}",
                          comm=exch(["w_out", "conv_w"]))
    took(["w_out", "conv_w"], o)
    dx1, G["norm_mlp_g"] = _rmsnorm_bwd(dh2, S["x1"], P["norm_mlp_g"], dx2, name=f"norm_mlp_bwd_{l}")
    (dmix,), _ = _mm(dx1, P["w_out_t"], out_dtypes=[F32], name=f"out_proj_bwd_{l}")
    (G["w_out"],), _ = _mm(S["ymix"], dx1, ta=True, out_dtypes=[F32], name=f"out_proj_dw_{l}")
    dz, dy, dyatt, G["ssd_norm_g"], G["att_norm_g"] = _mix_bwd(
        dmix, S["y"], S["xbc"], S["proj"], S["yatt"], P["d_skip_e"], P["ssd_norm_g"], P["att_norm_g"], D,
        name=f"mix_bwd_{l}")
    dq, dk, dv = _attn_bwd(dyatt, S["tot"], S["proj"], Bl, L, D, qc, kc, vc, name=f"attn_bwd_{l}")
    dxs, dB, dC, ddt, G["dt_bias_e"], G["a_log_e"], G["d_skip_e"] = _ssd_bwd(
        dy, S["states"], S["xbc"], S["proj"], P["dt_bias_e"], P["a_log_e"], P["d_skip_e"],
        Bl, L, D, dtc, name=f"ssd_bwd_{l}")
    dact = jnp.concatenate([dxs, dB, dC], axis=1)
    dxbc, G["conv_w"], G["conv_b"] = _conv_bwd(dact, S["proj"], P["conv_w"], P["conv_b"], Bl, L, D, Dc,
                                               name=f"conv_bwd_{l}")
    dproj = jnp.concatenate([dz, dxbc, dq, dk, dv, ddt], axis=1)
    (dh,), _ = _mm(dproj, P["w_in_t"], out_dtypes=[F32], name=f"in_proj_bwd_{l}")
    (G["w_in"],), _ = _mm(S["h"], dproj, ta=True, out_dtypes=[F32], name=f"in_proj_dw_{l}")
    dx, G["norm_mix_g"] = _rmsnorm_bwd(dh, S["x"], P["norm_mix_g"], dx1, name=f"norm_mix_bwd_{l}")
    return dx, G, bufs


def _pad_rows(a, rows):
    return jnp.concatenate([a, jnp.zeros((rows - a.shape[0],) + a.shape[1:], a.dtype)], axis=0)


def kernel(x, norm_mix_g, w_in, conv_w, conv_b, dt_bias, a_log, d_skip, ssd_norm_g, att_norm_g, w_out, norm_mlp_g, w_up, w_down, final_norm_g, loss_target, m_norm_mix_g, m_w_in, m_conv_w, m_conv_b, m_dt_bias, m_a_log, m_d_skip, m_ssd_norm_g, m_att_norm_g, m_w_out, m_norm_mlp_g, m_w_up, m_w_down, m_final_norm_g, v_norm_mix_g, v_w_in, v_conv_w, v_conv_b, v_dt_bias, v_a_log, v_d_skip, v_ssd_norm_g, v_att_norm_g, v_w_out, v_norm_mlp_g, v_w_up, v_w_down, v_final_norm_g):
    Bl, L, D = x.shape
    T = Bl * L
    depth = w_in.shape[0]
    H = D // HEAD_DIM
    Dc = D + 2 * SSM_GROUPS * SSM_STATE
    Dff = w_up.shape[2] * N_DEV
    dims = (Bl, L, D)

    bf = lambda a: a.astype(BF16)
    shards = [dict(w_in=bf(w_in[l]), w_out=bf(w_out[l]), w_up=bf(w_up[l]), w_down=bf(w_down[l]))
              for l in range(depth)]
    g0 = _all_gather([shards[0][k] for k in ("w_in", "w_out", "w_up", "w_down")] + [conv_w],
                     name="gather_weights_0")
    gathered = dict(zip(("w_in", "w_out", "w_up", "w_down"), g0))
    Cw = g0[4].transpose(1, 2, 0, 3).reshape(depth, CONV_WIDTH, Dc)
    rep = lambda a: jnp.repeat(a, HEAD_DIM, axis=1)
    dt_bias_e, a_log_e, d_skip_e = rep(dt_bias), rep(a_log), rep(d_skip)
    o = [0, D, D + Dc, D + Dc + H, 2 * D + Dc + H, 3 * D + Dc + H, 4 * D + Dc + H]

    def layer_params(g, l):
        W_in = g["w_in"].transpose(1, 0, 2).reshape(D, -1)
        wz, wxbc, wdt, wq, wk, wv = [W_in[:, o[i]:o[i + 1]] for i in range(6)]
        W_pack = jnp.concatenate([wz, wxbc, wq, wk, wv, jnp.repeat(wdt, HEAD_DIM, axis=1)], axis=1)
        W_out = g["w_out"].reshape(2 * D, D)
        W_up = g["w_up"].transpose(1, 0, 2).reshape(D, Dff)
        W_down = g["w_down"].reshape(Dff, D)
        return dict(
            norm_mix_g=norm_mix_g[l], w_in=W_pack, w_in_t=W_pack.T, conv_w=Cw[l], conv_b=conv_b[l],
            dt_bias_e=dt_bias_e[l:l + 1], a_log_e=a_log_e[l:l + 1], d_skip_e=d_skip_e[l:l + 1],
            ssd_norm_g=ssd_norm_g[l], att_norm_g=att_norm_g[l], w_out=W_out, w_out_t=W_out.T,
            norm_mlp_g=norm_mlp_g[l], w_up=W_up, w_up_t=W_up.T, w_down=W_down, w_down_t=W_down.T)

    xa = x.reshape(T, D)
    saved, params = [], []
    for l in range(depth):
        params.append(layer_params(gathered, l))
        xa, s, gathered = _layer_fwd(xa, params[l], dims, l, shards[l + 1] if l + 1 < depth else None)
        saved.append(s)
    dx, g_final, loss_part = _loss_head(xa, final_norm_g, loss_target.reshape(T, D), name="loss_head")

    n_in = w_in.shape[2]

    def pieces_of(G, l):
        gW = G["w_in"]
        gdt = _collapse_heads(gW[:, 4 * D + Dc:], name=f"collapse_w_dt_{l}")[:, :H]
        gW_in = jnp.concatenate([gW[:, :D + Dc], gdt, gW[:, D + Dc:4 * D + Dc]], axis=1)
        return dict(
            w_in=bf(gW_in.reshape(D, N_DEV, n_in).transpose(1, 0, 2)),
            w_out=bf(G["w_out"].reshape(N_DEV, 2 * D // N_DEV, D)),
            w_up=bf(G["w_up"].reshape(D, N_DEV, Dff // N_DEV).transpose(1, 0, 2)),
            w_down=bf(G["w_down"].reshape(N_DEV, Dff // N_DEV, D)),
            conv_w=bf(G["conv_w"].reshape(CONV_WIDTH, N_DEV, Dc // N_DEV).transpose(1, 0, 2)))

    keys = ("w_in", "w_out", "w_up", "w_down", "conv_w")
    bufs = {k: lax.empty((N_DEV, depth) + w.shape[1:], BF16)
            for k, w in zip(keys, (w_in, w_out, w_up, w_down, conv_w))}
    grads = [None] * depth
    pend = None
    for l in reversed(range(depth)):
        dx, grads[l], bufs = _layer_bwd(dx, saved[l], params[l], dims, l, pend, bufs)
        pend = pieces_of(grads[l], l)
    grad_x = dx.reshape(Bl, L, D)
    last = _comm_call(_exchange_comm([pend[k] for k in keys], [bufs[k] for k in keys], 0), name="exchange_grads_0")
    r_in, r_out, r_up, r_down, r_conv = last
    st = lambda k: jnp.stack([grads[l][k] for l in range(depth)])

    def sharded(pieces, w, m, v, name):
        shp = w.shape
        C = shp[-1]
        res = _adamw_sharded(pieces.reshape(N_DEV, -1, C), w.reshape(-1, C), m.reshape(-1, C),
                             v.reshape(-1, C), name=name)
        return [a.reshape(shp) for a in res]

    u_in = sharded(r_in, w_in, m_w_in, v_w_in, "adamw_w_in")
    u_out = sharded(r_out, w_out, m_w_out, v_w_out, "adamw_w_out")
    u_up = sharded(r_up, w_up, m_w_up, v_w_up, "adamw_w_up")
    u_down = sharded(r_down, w_down, m_w_down, v_w_down, "adamw_w_down")
    u_conv = sharded(r_conv, conv_w, m_conv_w, v_conv_w, "adamw_conv_w")

    rows = lambda a: a.reshape(-1, D)
    direct_names = ["norm_mix_g", "ssd_norm_g", "att_norm_g", "norm_mlp_g", "conv_b"]
    direct_g = [rows(st(k)) for k in direct_names] + [g_final]
    exp_names = ["dt_bias_e", "a_log_e", "d_skip_e"]
    exp_g = [rows(st(k)) for k in exp_names]
    n_direct = sum(a.shape[0] for a in direct_g)
    nd = -(-n_direct // 8) * 8
    ne = -(-3 * depth // 8) * 8
    part = jnp.concatenate([_pad_rows(jnp.concatenate(direct_g, axis=0), nd),
                            _pad_rows(jnp.concatenate(exp_g, axis=0), ne),
                            _pad_rows(loss_part, 8)], axis=0)
    (parts,) = _all_gather([part], name="gather_small_grads")

    def direct_pack(ws):
        return _pad_rows(jnp.concatenate([rows(a) for a in ws], axis=0), nd)

    def exp_pack(ws):
        a = jnp.concatenate(ws, axis=0)
        a = jnp.concatenate([a, jnp.zeros((a.shape[0], LANES - H), F32)], axis=1)
        return _pad_rows(a, ne)

    res = _small_update(
        parts,
        direct_pack([norm_mix_g, ssd_norm_g, att_norm_g, norm_mlp_g, conv_b, final_norm_g]),
        direct_pack([m_norm_mix_g, m_ssd_norm_g, m_att_norm_g, m_norm_mlp_g, m_conv_b, m_final_norm_g]),
        direct_pack([v_norm_mix_g, v_ssd_norm_g, v_att_norm_g, v_norm_mlp_g, v_conv_b, v_final_norm_g]),
        exp_pack([dt_bias, a_log, d_skip]), exp_pack([m_dt_bias, m_a_log, m_d_skip]),
        exp_pack([v_dt_bias, v_a_log, v_d_skip]), nd, ne, D, name="small_update")
    loss = res[8][0, 0]

    def unpack(kind):
        d, e = res[kind], res[4 + kind]
        out = {}
        r0 = 0
        for nm, shp in [("norm_mix_g", (depth, D)), ("ssd_norm_g", (depth, D)), ("att_norm_g", (depth, D)),
                        ("norm_mlp_g", (depth, D)), ("conv_b", (depth, Dc)), ("final_norm_g", (D,))]:
            n = 1
            for s_ in shp:
                n *= s_
            out[nm] = d[r0:r0 + n // D].reshape(shp)
            r0 += n // D
        for i, nm in enumerate(["dt_bias", "a_log", "d_skip"]):
            out[nm] = e[i * depth:(i + 1) * depth, :H]
        return out

    names = ["norm_mix_g", "w_in", "conv_w", "conv_b", "dt_bias", "a_log", "d_skip", "ssd_norm_g",
             "att_norm_g", "w_out", "norm_mlp_g", "w_up", "w_down", "final_norm_g"]
    big = {"w_in": u_in, "w_out": u_out, "w_up": u_up, "w_down": u_down, "conv_w": u_conv}
    outs = [loss, grad_x]
    for kind in range(4):
        small = unpack(kind)
        for nm in names:
            outs.append(big[nm][kind] if nm in big else small[nm])
    return tuple(outs)
```

```python
import functools

import jax
import jax.numpy as jnp
from jax import lax
from jax.experimental import pallas as pl
from jax.experimental.pallas import tpu as pltpu

F32 = jnp.float32
BF16 = jnp.bfloat16

HEAD_DIM = 64
SSM_STATE = 128
SSM_GROUPS = 4
CONV_WIDTH = 4
CHUNK = 128
ATT_BLOCK = 512
ATT_SUB = 256
SSD_UNROLL = 4
EPS = 1e-5
LANES = 128
N_DEV = 8
VMEM_LIMIT = 56 * 1024 * 1024

ADAM_LR = 0.001
ADAM_B1 = 0.9
ADAM_B2 = 0.999
ADAM_EPS = 1e-08
ADAM_WD = 0.01
ADAM_STEP = 10

_NT =(((1,), (1,)), ((), ()))
_TN = (((0,), (0,)), ((), ()))


def _blk(n, pref):
    if n <= pref:
        return n
    b = (pref // LANES) * LANES
    while b >= LANES:
        if n % b == 0:
            return b
        b -= LANES
    return n


def _cparams(sem):
    return pltpu.CompilerParams(dimension_semantics=sem, vmem_limit_bytes=VMEM_LIMIT)


def _dot(a, b):
    return jnp.dot(a, b, preferred_element_type=F32)


def _dg(a, b, dims):
    return lax.dot_general(a, b, dims, preferred_element_type=F32)


def _split3(x):
    h = x.astype(BF16)
    r = x - h.astype(F32)
    m = r.astype(BF16)
    l = (r - m.astype(F32)).astype(BF16)
    return h, m, l


def _dot_exact_rhs(x, c):
    h, m, l = _split3(x)
    return _dot(h, c) + _dot(m, c) + _dot(l, c)


def _dot_exact_lhs(c, x, dims):
    h, m, l = _split3(x)
    return _dg(c, h, dims) + _dg(c, m, dims) + _dg(c, l, dims)


def _mask(cond):
    return jnp.where(cond, 1.0, 0.0).astype(BF16)


def _sigmoid(x):
    return 1.0 / (1.0 + jnp.exp(-x))


def _softplus(x):
    return jnp.maximum(x, 0.0) + jnp.log(1.0 + jnp.exp(-jnp.abs(x)))


def _mm(a, b, *, name, out_dtypes, ta=False, epilogue=None, extras=(), tm=1024, tn=1024, tk=2048, comm=None):
    if ta:
        K, M = a.shape
    else:
        M, K = a.shape
    N = b.shape[1]
    assert b.shape[0] == K
    tm, tn, tk = _blk(M, tm), _blk(N, tn), _blk(K, tk)
    nj, ni, nk = N // tn, M // tm, K // tk
    n_ex, n_out = len(extras), len(out_dtypes)
    c_make, c_ins, c_outs, c_alias = comm if comm is not None else (None, [], [], {})
    n_ci, n_co = len(c_ins), len(c_outs)

    def kern(*refs):
        a_ref, b_ref = refs[0], refs[1]
        ex = refs[2:2 + n_ex]
        ci = refs[2 + n_ex:2 + n_ex + n_ci]
        o0 = 2 + n_ex + n_ci
        outs = refs[o0:o0 + n_out]
        co = refs[o0 + n_out:o0 + n_out + n_co]
        acc = refs[o0 + n_out + n_co]
        j, i, k = pl.program_id(0), pl.program_id(1), pl.program_id(2)
        if comm is not None:
            start, finish = c_make(ci, co, *refs[o0 + n_out + n_co + 1:])

            @pl.when((j == 0) & (i == 0) & (k == 0))
            def _():
                start()

        av = a_ref[...].astype(BF16)
        bv = b_ref[...].astype(BF16)
        prod = _dg(av, bv, _TN) if ta else _dot(av, bv)

        def write(r):
            vals = epilogue(r, *[e[...] for e in ex]) if epilogue is not None else (r,)
            for o, v in zip(outs, vals):
                o[...] = v.astype(o.dtype)

        if nk == 1:
            write(prod)
        else:
            @pl.when(k == 0)
            def _():
                acc[...] = prod

            @pl.when(k > 0)
            def _():
                acc[...] += prod

            @pl.when(k == nk - 1)
            def _():
                write(acc[...])

        if comm is not None:
            @pl.when((j == nj - 1) & (i == ni - 1) & (k == nk - 1))
            def _():
                finish()

    if ta:
        a_spec = pl.BlockSpec((tk, tm), lambda j, i, k: (k, i))
    else:
        a_spec = pl.BlockSpec((tm, tk), lambda j, i, k: (i, k))
    b_spec = pl.BlockSpec((tk, tn), lambda j, i, k: (k, j))
    o_spec = pl.BlockSpec((tm, tn), lambda j, i, k: (i, j))
    hbm = pl.BlockSpec(memory_space=pl.ANY)
    scratch = [pltpu.VMEM((tm, tn), F32)]
    if comm is not None:
        scratch += [pltpu.SemaphoreType.DMA((7 * n_co,)), pltpu.SemaphoreType.DMA((7 * n_co,)),
                    pltpu.SemaphoreType.DMA((n_co,))]
    res = pl.pallas_call(
        kern, name=name,
        out_shape=[jax.ShapeDtypeStruct((M, N), d) for d in out_dtypes] + list(c_outs),
        grid=(nj, ni, nk),
        in_specs=[a_spec, b_spec] + [o_spec] * n_ex + [hbm] * n_ci,
        out_specs=[o_spec] * n_out + [hbm] * n_co,
        scratch_shapes=scratch,
        input_output_aliases={2 + n_ex + ki: n_out + ko for ki, ko in c_alias.items()},
        compiler_params=_cparams(("parallel", "parallel", "arbitrary") if comm is None
                                 else ("arbitrary", "arbitrary", "arbitrary")),
    )(a, b, *extras, *c_ins)
    return res[:n_out], res[n_out:]


def _rmsnorm_fwd(x, g, *, name):
    T, D = x.shape
    tm = _blk(T, 512)

    def kern(x_ref, g_ref, h_ref):
        xv = x_ref[...]
        r = lax.rsqrt(jnp.mean(xv * xv, axis=-1, keepdims=True) + EPS)
        h_ref[...] = (xv * r * g_ref[...]).astype(h_ref.dtype)

    return pl.pallas_call(
        kern, name=name, out_shape=jax.ShapeDtypeStruct((T, D), BF16), grid=(T // tm,),
        in_specs=[pl.BlockSpec((tm, D), lambda i: (i, 0)), pl.BlockSpec((1, D), lambda i: (0, 0))],
        out_specs=pl.BlockSpec((tm, D), lambda i: (i, 0)),
        compiler_params=_cparams(("parallel",)),
    )(x, g.reshape(1, D))


def _rmsnorm_bwd(dh, x, g, dres, *, name):
    T, D = x.shape
    tm = _blk(T, 512)

    def kern(dh_ref, x_ref, g_ref, dres_ref, dx_ref, dg_ref):
        @pl.when(pl.program_id(0) == 0)
        def _():
            dg_ref[...] = jnp.zeros_like(dg_ref)

        xv = x_ref[...]
        r = lax.rsqrt(jnp.mean(xv * xv, axis=-1, keepdims=True) + EPS)
        xn = xv * r
        dy = dh_ref[...].astype(F32)
        dyg = dy * g_ref[...]
        dx = r * (dyg - xn * jnp.mean(dyg * xn, axis=-1, keepdims=True))
        dx_ref[...] = dres_ref[...] + dx
        dg_ref[...] += jnp.sum(dy * xn, axis=0, keepdims=True)

    row = pl.BlockSpec((tm, D), lambda i: (i, 0))
    vec = pl.BlockSpec((1, D), lambda i: (0, 0))
    return pl.pallas_call(
        kern, name=name,
        out_shape=[jax.ShapeDtypeStruct((T, D), F32), jax.ShapeDtypeStruct((1, D), F32)],
        grid=(T // tm,), in_specs=[row, row, vec, row], out_specs=[row, vec],
        compiler_params=_cparams(("arbitrary",)),
    )(dh, x, g.reshape(1, D), dres)


def _loss_head(x, g, target, *, name):
    T, D = x.shape
    tm = _blk(T, 512)

    def kern(x_ref, g_ref, t_ref, dx_ref, dg_ref, loss_ref):
        @pl.when(pl.program_id(0) == 0)
        def _():
            dg_ref[...] = jnp.zeros_like(dg_ref)
            loss_ref[...] = jnp.zeros_like(loss_ref)

        xv = x_ref[...]
        gv = g_ref[...]
        r = lax.rsqrt(jnp.mean(xv * xv, axis=-1, keepdims=True) + EPS)
        xn = xv * r
        err = xn * gv - t_ref[...]
        loss_ref[...] += jnp.sum(err * err, axis=0, keepdims=True) * (0.5 / D)
        dy = err * (1.0 / D)
        dyg = dy * gv
        dx_ref[...] = r * (dyg - xn * jnp.mean(dyg * xn, axis=-1, keepdims=True))
        dg_ref[...] += jnp.sum(dy * xn, axis=0, keepdims=True)

    row = pl.BlockSpec((tm, D), lambda i: (i, 0))
    vec = pl.BlockSpec((1, D), lambda i: (0, 0))
    return pl.pallas_call(
        kern, name=name,
        out_shape=[jax.ShapeDtypeStruct((T, D), F32), jax.ShapeDtypeStruct((1, D), F32),
                   jax.ShapeDtypeStruct((1, D), F32)],
        grid=(T // tm,), in_specs=[row, vec, row], out_specs=[row, vec, vec],
        compiler_params=_cparams(("arbitrary",)),
    )(x, g.reshape(1, D), target)


def _shift_down(u, s, L):
    t = lax.broadcasted_iota(jnp.int32, u.shape, 0)
    return jnp.where(t >= s, pltpu.roll(u, s, 0), 0.0)


def _shift_up(u, s, L):
    t = lax.broadcasted_iota(jnp.int32, u.shape, 0)
    return jnp.where(t < L - s, pltpu.roll(u, L - s, 0), 0.0)


def _conv_pre(u, w_ref, b_ref, L):
    pre = u * w_ref[CONV_WIDTH - 1:CONV_WIDTH, :] + b_ref[...]
    for s in range(1, CONV_WIDTH):
        pre = pre + _shift_down(u, s, L) * w_ref[CONV_WIDTH - 1 - s:CONV_WIDTH - s, :]
    return pre


def _conv_fwd(proj, w, b, Bl, L, col0, Dc, *, name):
    cb = LANES
    c0 = col0 // cb

    def kern(u_ref, w_ref, b_ref, o_ref):
        pre = _conv_pre(u_ref[...], w_ref, b_ref, L)
        o_ref[...] = pre * _sigmoid(pre)

    return pl.pallas_call(
        kern, name=name, out_shape=jax.ShapeDtypeStruct((Bl * L, Dc), F32), grid=(Bl, Dc // cb),
        in_specs=[pl.BlockSpec((L, cb), lambda bi, j: (bi, c0 + j)),
                  pl.BlockSpec((CONV_WIDTH, cb), lambda bi, j: (0, j)),
                  pl.BlockSpec((1, cb), lambda bi, j: (0, j))],
        out_specs=pl.BlockSpec((L, cb), lambda bi, j: (bi, j)),
        compiler_params=_cparams(("parallel", "parallel")),
    )(proj, w, b.reshape(1, Dc))


def _conv_bwd(dact, proj, w, b, Bl, L, col0, Dc, *, name):
    cb = LANES
    c0 = col0 // cb

    def kern(d_ref, u_ref, w_ref, b_ref, du_ref, dw_ref, db_ref):
        @pl.when(pl.program_id(1) == 0)
        def _():
            dw_ref[...] = jnp.zeros_like(dw_ref)
            db_ref[...] = jnp.zeros_like(db_ref)

        u = u_ref[...]
        pre = _conv_pre(u, w_ref, b_ref, L)
        sig = _sigmoid(pre)
        dpre = d_ref[...] * (sig * (1.0 + pre * (1.0 - sig)))
        du = dpre * w_ref[CONV_WIDTH - 1:CONV_WIDTH, :]
        dw_ref[CONV_WIDTH - 1:CONV_WIDTH, :] += jnp.sum(dpre * u, axis=0, keepdims=True)
        for s in range(1, CONV_WIDTH):
            k = CONV_WIDTH - 1 - s
            du = du + _shift_up(dpre, s, L) * w_ref[k:k + 1, :]
            dw_ref[k:k + 1, :] += jnp.sum(dpre * _shift_down(u, s, L), axis=0, keepdims=True)
        db_ref[...] += jnp.sum(dpre, axis=0, keepdims=True)
        du_ref[...] = du.astype(du_ref.dtype)

    return pl.pallas_call(
        kern, name=name,
        out_shape=[jax.ShapeDtypeStruct((Bl * L, Dc), BF16),
                   jax.ShapeDtypeStruct((CONV_WIDTH, Dc), F32), jax.ShapeDtypeStruct((1, Dc), F32)],
        grid=(Dc // cb, Bl),
        in_specs=[pl.BlockSpec((L, cb), lambda j, bi: (bi, j)),
                  pl.BlockSpec((L, cb), lambda j, bi: (bi, c0 + j)),
                  pl.BlockSpec((CONV_WIDTH, cb), lambda j, bi: (0, j)),
                  pl.BlockSpec((1, cb), lambda j, bi: (0, j))],
        out_specs=[pl.BlockSpec((L, cb), lambda j, bi: (bi, j)),
                   pl.BlockSpec((CONV_WIDTH, cb), lambda j, bi: (0, j)),
                   pl.BlockSpec((1, cb), lambda j, bi: (0, j))],
        compiler_params=_cparams(("parallel", "arbitrary")),
    )(dact, proj, w, b.reshape(1, Dc))


def _tri_consts():
    r = lax.broadcasted_iota(jnp.int32, (CHUNK, CHUNK), 0)
    c = lax.broadcasted_iota(jnp.int32, (CHUNK, CHUNK), 1)
    lane0 = c < HEAD_DIM
    return r, c, lane0


def _ssd_chunk_common(x, dtr, bias, alog):
    r, c, _ = _tri_consts()
    dt = _softplus(dtr + bias)
    a = dt * (-jnp.exp(alog))
    tril = _mask(c <= r)
    E = _dot_exact_lhs(tril, a, (((1,), (0,)), ((), ())))
    return dt, E, E.T


def _ssd_head_decay(E, ET, h):
    r, c, _ = _tri_consts()
    cm = jnp.broadcast_to(E[:, h * HEAD_DIM:h * HEAD_DIM + 1], (CHUNK, CHUNK))
    rm = jnp.broadcast_to(ET[h * HEAD_DIM:h * HEAD_DIM + 1, :], (CHUNK, CHUNK))
    return jnp.where(c <= r, jnp.exp(jnp.minimum(cm - rm, 0.0)), 0.0)


def _ssd_fwd(xbc, proj, bias_e, alog_e, Bl, L, D, dt_col0, *, name):
    nc = L // CHUNK
    npairs = D // LANES
    ppg = npairs // SSM_GROUPS
    gw = ppg * LANES
    b0 = D // LANES
    c0 = (D + SSM_GROUPS * SSM_STATE) // LANES
    d0 = dt_col0 // gw
    unroll = SSD_UNROLL if nc % SSD_UNROLL == 0 else 1

    def kern(x_ref, b_ref, c_ref, dtr_ref, bias_ref, alog_ref, y_ref, st_ref, h_scr):
        _, _, lane0 = _tri_consts()
        h_scr[...] = jnp.zeros_like(h_scr)

        lanes = [pl.ds(p * LANES, LANES) for p in range(ppg)]

        def one_chunk(ci, hs):
            rows = pl.ds(pl.multiple_of(ci * CHUNK, CHUNK), CHUNK)
            Bb = b_ref[rows, :].astype(BF16)
            Cb = c_ref[rows, :].astype(BF16)
            CB = _dg(Cb, Bb, _NT)
            new = []
            for p, ln in enumerate(lanes):
                x, hprev = x_ref[rows, ln], hs[p]
                dt, E, ET = _ssd_chunk_common(x, dtr_ref[rows, ln], bias_ref[:, ln], alog_ref[:, ln])
                xdt = x * dt
                xdtb = xdt.astype(BF16)
                ys = []
                for h in range(2):
                    Wh = (CB * _ssd_head_decay(E, ET, h)).astype(BF16)
                    ys.append(_dot(Wh, xdtb))
                y = jnp.where(lane0, ys[0], ys[1]) + _dg(Cb, hprev.astype(BF16), _NT) * jnp.exp(E)
                ds = jnp.exp(jnp.broadcast_to(E[CHUNK - 1:CHUNK, :], (CHUNK, LANES)) - E)
                states = _dg((xdt * ds).astype(BF16), Bb, _TN)
                cd = jnp.exp(jnp.broadcast_to(ET[:, CHUNK - 1:CHUNK], (LANES, SSM_STATE)))
                st_ref[ci, p] = hprev
                y_ref[rows, ln] = y
                new.append(hprev * cd + states)
            return new

        def chunks(i, carry):
            hs = [h_scr[p] for p in range(ppg)]
            for u in range(unroll):
                hs = one_chunk(i * unroll + u, hs)
            for p in range(ppg):
                h_scr[p] = hs[p]
            return carry

        lax.fori_loop(0, nc // unroll, chunks, 0)

    return pl.pallas_call(
        kern, name=name,
        out_shape=[jax.ShapeDtypeStruct((Bl * L, D), F32),
                   jax.ShapeDtypeStruct((Bl, SSM_GROUPS, nc, ppg, LANES, SSM_STATE), F32)],
        grid=(Bl, SSM_GROUPS),
        in_specs=[pl.BlockSpec((L, gw), lambda bi, g: (bi, g)),
                  pl.BlockSpec((L, SSM_STATE), lambda bi, g: (bi, b0 + g)),
                  pl.BlockSpec((L, SSM_STATE), lambda bi, g: (bi, c0 + g)),
                  pl.BlockSpec((L, gw), lambda bi, g: (bi, d0 + g)),
                  pl.BlockSpec((1, gw), lambda bi, g: (0, g)),
                  pl.BlockSpec((1, gw), lambda bi, g: (0, g))],
        out_specs=[pl.BlockSpec((L, gw), lambda bi, g: (bi, g)),
                   pl.BlockSpec((None, None, nc, ppg, LANES, SSM_STATE), lambda bi, g: (bi, g, 0, 0, 0, 0))],
        scratch_shapes=[pltpu.VMEM((ppg, LANES, SSM_STATE), F32)],
        compiler_params=_cparams(("parallel", "parallel")),
    )(xbc, xbc, xbc, proj, bias_e, alog_e)


def _ssd_bwd(dy, states, xbc, proj, bias_e, alog_e, dskip_e, Bl, L, D, dt_col0, *, name):
    nc = L // CHUNK
    npairs = D // LANES
    ppg = npairs // SSM_GROUPS
    gw = ppg * LANES
    b0 = D // LANES
    c0 = (D + SSM_GROUPS * SSM_STATE) // LANES
    d0 = dt_col0 // gw
    unroll = SSD_UNROLL if nc % SSD_UNROLL == 0 else 1

    def kern(dy_ref, st_ref, x_ref, b_ref, c_ref, dtr_ref, bias_ref, alog_ref, dsk_ref,
             dx_ref, db_ref, dc_ref, ddt_ref, gbias_ref, galog_ref, gdsk_ref, g_scr):
        r, c, lane0 = _tri_consts()
        m0 = lane0.astype(F32)
        masks = (m0, 1.0 - m0)
        triu = _mask(c >= r)
        trils = _mask(c < r)
        ones = jnp.ones((CHUNK, LANES), BF16)
        g_scr[...] = jnp.zeros_like(g_scr)

        @pl.when(pl.program_id(1) == 0)
        def _():
            gbias_ref[...] = jnp.zeros_like(gbias_ref)
            galog_ref[...] = jnp.zeros_like(galog_ref)
            gdsk_ref[...] = jnp.zeros_like(gdsk_ref)

        lanes = [pl.ds(p * LANES, LANES) for p in range(ppg)]

        def one_chunk(ci, gs):
            rows = pl.ds(pl.multiple_of(ci * CHUNK, CHUNK), CHUNK)
            Bb = b_ref[rows, :].astype(BF16)
            Cb = c_ref[rows, :].astype(BF16)
            CB = _dg(Cb, Bb, _NT)
            gsum = jnp.zeros((CHUNK, CHUNK), F32)
            dB = jnp.zeros((CHUNK, SSM_STATE), F32)
            dC = jnp.zeros((CHUNK, SSM_STATE), F32)
            ins = [(x_ref[rows, ln], dtr_ref[rows, ln], bias_ref[:, ln], alog_ref[:, ln], dy_ref[rows, ln],
                    st_ref[ci, p], gs[p], dsk_ref[:, ln]) for p, ln in enumerate(lanes)]
            outs = []
            for x, dtr, bias, alog, dyv, hprev, g, dsk in ins:
                dt, E, ET = _ssd_chunk_common(x, dtr, bias, alog)
                xdt = x * dt
                xdtb = xdt.astype(BF16)
                dyb = dyv.astype(BF16)
                hpb = hprev.astype(BF16)
                gb = g.astype(BF16)
                sd = jnp.exp(E)
                ds = jnp.exp(jnp.broadcast_to(E[CHUNK - 1:CHUNK, :], (CHUNK, LANES)) - E)
                t1 = []
                dE = dyv * (_dg(Cb, hpb, _NT) * sd)
                for h in range(2):
                    Lh = _ssd_head_decay(E, ET, h)
                    Wh = (CB * Lh).astype(BF16)
                    t1.append(_dg(Wh, dyb, _TN))
                    Gh = Lh * _dg((dyv * masks[h]).astype(BF16), xdtb, _NT)
                    gsum = gsum + Gh
                    Qh = Gh * CB
                    dE = dE + _dot_exact_rhs(Qh - Qh.T, _mask(c == h * HEAD_DIM))
                dxst = _dg(Bb, gb, _NT) * ds
                dxdt = jnp.where(lane0, t1[0], t1[1]) + dxst
                dysd = (dyv * sd).astype(BF16)
                dC = dC + _dot(dysd, hpb)
                dB = dB + _dot((xdt * ds).astype(BF16), gb)
                cd = jnp.exp(jnp.broadcast_to(ET[:, CHUNK - 1:CHUNK], (LANES, SSM_STATE)))
                gnew = g * cd + _dg(dysd, Cb, _TN)
                nn = (((1,), (0,)), ((), ()))
                da = (_dot_exact_lhs(triu, dE, nn) + _dot_exact_lhs(trils, xdt * dxst, nn)
                      + _dot_exact_lhs(ones, g * cd * hprev, _NT))
                aneg = -jnp.exp(alog)
                ddt = da * aneg + dxdt * x
                ddtr = ddt * _sigmoid(dtr + bias)
                outs.append((gnew, dxdt * dt + dyv * dsk, ddtr, jnp.sum(ddtr, axis=0, keepdims=True),
                             jnp.sum(da * dt, axis=0, keepdims=True) * aneg,
                             jnp.sum(dyv * x, axis=0, keepdims=True)))
            for p, (gnew, dx, ddtr, sb, sa, sk) in enumerate(outs):
                dx_ref[rows, lanes[p]] = dx
                ddt_ref[rows, lanes[p]] = ddtr.astype(ddt_ref.dtype)
                gbias_ref[:, lanes[p]] += sb
                galog_ref[:, lanes[p]] += sa
                gdsk_ref[:, lanes[p]] += sk
            gsb = gsum.astype(BF16)
            db_ref[rows, :] = dB + _dg(gsb, Cb, _TN)
            dc_ref[rows, :] = dC + _dot(gsb, Bb)
            return [o[0] for o in outs]

        def chunks(i, carry):
            gs = [g_scr[p] for p in range(ppg)]
            for u in range(unroll):
                gs = one_chunk(nc - 1 - (i * unroll + u), gs)
            for p in range(ppg):
                g_scr[p] = gs[p]
            return carry

        lax.fori_loop(0, nc // unroll, chunks, 0)

    T = Bl * L
    xspec = pl.BlockSpec((L, gw), lambda g, bi: (bi, g))
    nspec = pl.BlockSpec((L, SSM_STATE), lambda g, bi: (bi, g))
    vspec = pl.BlockSpec((1, gw), lambda g, bi: (0, g))
    return pl.pallas_call(
        kern, name=name,
        out_shape=[jax.ShapeDtypeStruct((T, D), F32),
                   jax.ShapeDtypeStruct((T, SSM_GROUPS * SSM_STATE), F32),
                   jax.ShapeDtypeStruct((T, SSM_GROUPS * SSM_STATE), F32),
                   jax.ShapeDtypeStruct((T, D), BF16),
                   jax.ShapeDtypeStruct((1, D), F32), jax.ShapeDtypeStruct((1, D), F32),
                   jax.ShapeDtypeStruct((1, D), F32)],
        grid=(SSM_GROUPS, Bl),
        in_specs=[xspec,
                  pl.BlockSpec((None, None, nc, ppg, LANES, SSM_STATE), lambda g, bi: (bi, g, 0, 0, 0, 0)),
                  xspec,
                  pl.BlockSpec((L, SSM_STATE), lambda g, bi: (bi, b0 + g)),
                  pl.BlockSpec((L, SSM_STATE), lambda g, bi: (bi, c0 + g)),
                  pl.BlockSpec((L, gw), lambda g, bi: (bi, d0 + g)),
                  vspec, vspec, vspec],
        out_specs=[xspec, nspec, nspec, xspec, vspec, vspec, vspec],
        scratch_shapes=[pltpu.VMEM((ppg, LANES, SSM_STATE), F32)],
        compiler_params=_cparams(("parallel", "arbitrary")),
    )(dy, states, xbc, xbc, xbc, proj, bias_e, alog_e, dskip_e)


def _att_nl(z):
    return jnp.maximum(z, 0.0) + jnp.log(1.0 + jnp.exp(-jnp.abs(z)))


def _sub_pred(pred):
    r = lax.broadcasted_iota(jnp.int32, (ATT_SUB, ATT_SUB), 0)
    c = lax.broadcasted_iota(jnp.int32, (ATT_SUB, ATT_SUB), 1)
    return pred(r, c)


def _cumsum_mm(x, tri):
    return _dot(x.astype(BF16), tri)


def _att_slices(q0, roff, nrows, k0, nsub):
    qs = pl.ds(pl.multiple_of(q0 + roff, ATT_SUB), nrows)
    ks = pl.ds(pl.multiple_of(k0, ATT_SUB), nsub * ATT_SUB)
    return qs, ks, pl.ds(roff, nrows)


def _attn_fwd(proj, Bl, L, D, qc, kc, vc, *, name):
    npairs = D // LANES
    nq = L // ATT_BLOCK
    nsb = ATT_BLOCK // ATT_SUB
    scale = HEAD_DIM ** -0.5

    def kern(q_ref, k_ref, v_ref, o_ref, tot_ref, qm, kb, vb, carry, acc):
        tri = _sub_pred(lambda t, s: t > s)
        lrow = lax.broadcasted_iota(jnp.int32, (1, LANES), 1) < HEAD_DIM
        q = q_ref[...] * scale
        qm[0] = jnp.where(lrow, q, 0.0).astype(BF16)
        qm[1] = jnp.where(lrow, 0.0, q).astype(BF16)
        kb[...] = k_ref[...].astype(BF16)
        vb[...] = v_ref[...].astype(BF16)
        mat = _mask(_sub_pred(lambda j, s: j >= s))

        def tile(q0, roff, nrows, k0, nsub, diag):
            qs, ks, rows = _att_slices(q0, roff, nrows, k0, nsub)
            for h in range(2):
                z = _dg(qm[h, qs, :], kb[ks, :], _NT)
                nl = _att_nl(z)
                cy = carry[h, rows, :]
                atts = []
                for b in reversed(range(nsub)):
                    cols = slice(b * ATT_SUB, (b + 1) * ATT_SUB)
                    masked = diag and b == nsub - 1
                    x = jnp.where(tri, nl[:, cols], 0.0) if masked else nl[:, cols]
                    inc = _cumsum_mm(x, mat)
                    a = jnp.exp(z[:, cols] - inc - cy)
                    atts.append((jnp.where(tri, a, 0.0) if masked else a).astype(BF16))
                    cy = cy + inc[:, 0:1]
                att = jnp.concatenate(atts[::-1], axis=1) if nsub > 1 else atts[0]
                acc[h, rows, :] += _dot(att, vb[ks, :])
                carry[h, rows, :] = cy

        def qblock(qi, _):
            q0 = qi * ATT_BLOCK
            qs = pl.ds(pl.multiple_of(q0, ATT_BLOCK), ATT_BLOCK)
            carry[...] = jnp.zeros_like(carry)
            acc[...] = jnp.zeros_like(acc)
            for rb in range(nsb):
                tile(q0, rb * ATT_SUB, ATT_SUB, q0, rb + 1, True)

            def kblock(j, _):
                tile(q0, 0, ATT_BLOCK, (qi - 1 - j) * ATT_BLOCK, nsb, False)
                return 0

            lax.fori_loop(0, qi, kblock, 0)
            o_ref[qs, :] = jnp.where(lrow, acc[0], acc[1])
            for h in range(2):
                tot_ref[h, qs, :] = jnp.broadcast_to(carry[h], (ATT_BLOCK, LANES))
            return 0

        lax.fori_loop(0, nq, qblock, 0)

    spec = lambda col: pl.BlockSpec((L, LANES), lambda bi, p: (bi, col // LANES + p))
    return pl.pallas_call(
        kern, name=name,
        out_shape=[jax.ShapeDtypeStruct((Bl * L, D), F32),
                   jax.ShapeDtypeStruct((Bl, npairs, 2, L, LANES), F32)],
        grid=(Bl, npairs),
        in_specs=[spec(qc), spec(kc), spec(vc)],
        out_specs=[pl.BlockSpec((L, LANES), lambda bi, p: (bi, p)),
                   pl.BlockSpec((None, None, 2, L, LANES), lambda bi, p: (bi, p, 0, 0, 0))],
        scratch_shapes=[pltpu.VMEM((2, L, LANES), BF16), pltpu.VMEM((L, LANES), BF16),
                        pltpu.VMEM((L, LANES), BF16), pltpu.VMEM((2, ATT_BLOCK, 1), F32),
                        pltpu.VMEM((2, ATT_BLOCK, LANES), F32)],
        compiler_params=_cparams(("parallel", "parallel")),
    )(proj, proj, proj)


def _attn_bwd(dout, tot, proj, Bl, L, D, qc, kc, vc, *, name):
    npairs = D // LANES
    nq = L // ATT_BLOCK
    nsb = ATT_BLOCK // ATT_SUB
    scale = HEAD_DIM ** -0.5

    def kern(do_ref, tot_ref, q_ref, k_ref, v_ref, dq_ref, dk_ref, dv_ref, qm, km, kb, vb, dom, dkacc, dvacc,
             pre, pde, dqacc):
        tri = _sub_pred(lambda t, s: t > s)
        lrow = lax.broadcasted_iota(jnp.int32, (1, LANES), 1) < HEAD_DIM
        q = q_ref[...] * scale
        qm[0] = jnp.where(lrow, q, 0.0).astype(BF16)
        qm[1] = jnp.where(lrow, 0.0, q).astype(BF16)
        k = k_ref[...]
        kb[...] = k.astype(BF16)
        km[0] = jnp.where(lrow, k, 0.0).astype(BF16)
        km[1] = jnp.where(lrow, 0.0, k).astype(BF16)
        vb[...] = v_ref[...].astype(BF16)
        do = do_ref[...]
        dom[0] = jnp.where(lrow, do, 0.0).astype(BF16)
        dom[1] = jnp.where(lrow, 0.0, do).astype(BF16)
        dkacc[...] = jnp.zeros_like(dkacc)
        dvacc[...] = jnp.zeros_like(dvacc)
        mat = _mask(_sub_pred(lambda j, s: j <= s))

        def tile(q0, roff, nrows, k0, nsub, diag):
            qs, ks, rows = _att_slices(q0, roff, nrows, k0, nsub)
            for h in range(2):
                z = _dg(qm[h, qs, :], kb[ks, :], _NT)
                nl = _att_nl(z)
                lb = z - nl
                datt = _dg(dom[h, qs, :], vb[ks, :], _NT)
                tn = tot_ref[h, qs, 0:1]
                base = pre[h, rows, :] - tn
                pe = pde[h, rows, :]
                atts, dzs = [], []
                for b in range(nsub):
                    cols = slice(b * ATT_SUB, (b + 1) * ATT_SUB)
                    masked = diag and b == nsub - 1
                    x = jnp.where(tri, nl[:, cols], 0.0) if masked else nl[:, cols]
                    pin = _cumsum_mm(x, mat)
                    att = jnp.exp(lb[:, cols] + base + pin)
                    if masked:
                        att = jnp.where(tri, att, 0.0)
                    dE = att * datt[:, cols]
                    pde_in = _cumsum_mm(dE, mat)
                    dz = dE - jnp.exp(lb[:, cols]) * (pe + pde_in)
                    if masked:
                        dz = jnp.where(tri, dz, 0.0)
                    atts.append(att.astype(BF16))
                    dzs.append(dz.astype(BF16))
                    base = base + pin[:, ATT_SUB - 1:ATT_SUB]
                    pe = pe + pde_in[:, ATT_SUB - 1:ATT_SUB]
                pre[h, rows, :] = base + tn
                pde[h, rows, :] = pe
                attb = jnp.concatenate(atts, axis=1) if nsub > 1 else atts[0]
                dzb = jnp.concatenate(dzs, axis=1) if nsub > 1 else dzs[0]
                dqacc[rows, :] += _dot(dzb, km[h, ks, :])
                dkacc[ks, :] += _dg(dzb, qm[h, qs, :], _TN)
                dvacc[ks, :] += _dg(attb, dom[h, qs, :], _TN)

        def qblock(qi, _):
            q0 = qi * ATT_BLOCK
            qs = pl.ds(pl.multiple_of(q0, ATT_BLOCK), ATT_BLOCK)
            pre[...] = jnp.zeros_like(pre)
            pde[...] = jnp.zeros_like(pde)
            dqacc[...] = jnp.zeros_like(dqacc)

            def kblock(kj, _):
                tile(q0, 0, ATT_BLOCK, kj * ATT_BLOCK, nsb, False)
                return 0

            lax.fori_loop(0, qi, kblock, 0)
            for rb in range(nsb):
                tile(q0, rb * ATT_SUB, ATT_SUB, q0, rb + 1, True)
            dq_ref[qs, :] = (dqacc[...] * scale).astype(dq_ref.dtype)
            return 0

        lax.fori_loop(0, nq, qblock, 0)
        dk_ref[...] = dkacc[...].astype(dk_ref.dtype)
        dv_ref[...] = dvacc[...].astype(dv_ref.dtype)

    spec = lambda col: pl.BlockSpec((L, LANES), lambda bi, p: (bi, col // LANES + p))
    ospec = pl.BlockSpec((L, LANES), lambda bi, p: (bi, p))
    T = Bl * L
    return pl.pallas_call(
        kern, name=name,
        out_shape=[jax.ShapeDtypeStruct((T, D), BF16)] * 3,
        grid=(Bl, npairs),
        in_specs=[ospec, pl.BlockSpec((None, None, 2, L, LANES), lambda bi, p: (bi, p, 0, 0, 0)),
                  spec(qc), spec(kc), spec(vc)],
        out_specs=[ospec, ospec, ospec],
        scratch_shapes=[pltpu.VMEM((2, L, LANES), BF16), pltpu.VMEM((2, L, LANES), BF16),
                        pltpu.VMEM((L, LANES), BF16), pltpu.VMEM((L, LANES), BF16),
                        pltpu.VMEM((2, L, LANES), BF16), pltpu.VMEM((L, LANES), F32),
                        pltpu.VMEM((L, LANES), F32), pltpu.VMEM((2, ATT_BLOCK, 1), F32),
                        pltpu.VMEM((2, ATT_BLOCK, 1), F32), pltpu.VMEM((ATT_BLOCK, LANES), F32)],
        compiler_params=_cparams(("parallel", "parallel")),
    )(dout, tot, proj, proj, proj)


def _mix_fwd(y, xbc, proj, yatt, dskip_e, g_ssd, g_att, D, *, name):
    T = y.shape[0]
    tm = _blk(T, 256)

    def kern(y_ref, xs_ref, z_ref, ya_ref, dsk_ref, gs_ref, ga_ref, o_ref):
        z = z_ref[...]
        u = (y_ref[...] + xs_ref[...] * dsk_ref[...]) * (z * _sigmoid(z))
        rs = lax.rsqrt(jnp.mean(u * u, axis=-1, keepdims=True) + EPS)
        o_ref[:, :D] = (u * rs * gs_ref[...]).astype(o_ref.dtype)
        ya = ya_ref[...]
        ra = lax.rsqrt(jnp.mean(ya * ya, axis=-1, keepdims=True) + EPS)
        o_ref[:, D:] = (ya * ra * ga_ref[...]).astype(o_ref.dtype)

    row = pl.BlockSpec((tm, D), lambda i: (i, 0))
    vec = pl.BlockSpec((1, D), lambda i: (0, 0))
    return pl.pallas_call(
        kern, name=name, out_shape=jax.ShapeDtypeStruct((T, 2 * D), BF16), grid=(T // tm,),
        in_specs=[row, row, row, row, vec, vec, vec],
        out_specs=pl.BlockSpec((tm, 2 * D), lambda i: (i, 0)),
        compiler_params=_cparams(("parallel",)),
    )(y, xbc, proj, yatt, dskip_e, g_ssd.reshape(1, D), g_att.reshape(1, D))


def _mix_bwd(dmix, y, xbc, proj, yatt, dskip_e, g_ssd, g_att, D, *, name):
    T = y.shape[0]
    tm = _blk(T, 256)

    def kern(d_ref, y_ref, xs_ref, z_ref, ya_ref, dsk_ref, gs_ref, ga_ref,
             dz_ref, dy_ref, dya_ref, dgs_ref, dga_ref):
        @pl.when(pl.program_id(0) == 0)
        def _():
            dgs_ref[...] = jnp.zeros_like(dgs_ref)
            dga_ref[...] = jnp.zeros_like(dga_ref)

        z = z_ref[...]
        sig = _sigmoid(z)
        sz = z * sig
        ytot = y_ref[...] + xs_ref[...] * dsk_ref[...]
        u = ytot * sz
        rs = lax.rsqrt(jnp.mean(u * u, axis=-1, keepdims=True) + EPS)
        un = u * rs
        do = d_ref[:, :D]
        dog = do * gs_ref[...]
        du = rs * (dog - un * jnp.mean(dog * un, axis=-1, keepdims=True))
        dgs_ref[...] += jnp.sum(do * un, axis=0, keepdims=True)
        dy_ref[...] = du * sz
        dz_ref[...] = (du * ytot * (sig * (1.0 + z * (1.0 - sig)))).astype(dz_ref.dtype)
        ya = ya_ref[...]
        ra = lax.rsqrt(jnp.mean(ya * ya, axis=-1, keepdims=True) + EPS)
        yan = ya * ra
        da = d_ref[:, D:]
        dag = da * ga_ref[...]
        dya_ref[...] = ra * (dag - yan * jnp.mean(dag * yan, axis=-1, keepdims=True))
        dga_ref[...] += jnp.sum(da * yan, axis=0, keepdims=True)

    row = pl.BlockSpec((tm, D), lambda i: (i, 0))
    vec = pl.BlockSpec((1, D), lambda i: (0, 0))
    return pl.pallas_call(
        kern, name=name,
        out_shape=[jax.ShapeDtypeStruct((T, D), BF16), jax.ShapeDtypeStruct((T, D), F32),
                   jax.ShapeDtypeStruct((T, D), F32), jax.ShapeDtypeStruct((1, D), F32),
                   jax.ShapeDtypeStruct((1, D), F32)],
        grid=(T // tm,),
        in_specs=[pl.BlockSpec((tm, 2 * D), lambda i: (i, 0)), row, row, row, row, vec, vec, vec],
        out_specs=[row, row, row, vec, vec],
        compiler_params=_cparams(("arbitrary",)),
    )(dmix, y, xbc, proj, yatt, dskip_e, g_ssd.reshape(1, D), g_att.reshape(1, D))


def _head_sum_mat(D):
    i = lax.broadcasted_iota(jnp.int32, (D, LANES), 0)
    c = lax.broadcasted_iota(jnp.int32, (D, LANES), 1)
    return _mask((i >= c * HEAD_DIM) & (i < (c + 1) * HEAD_DIM))


def _collapse_heads(x, *, name):
    R, D = x.shape
    tm = _blk(R, 256)

    def kern(x_ref, o_ref):
        o_ref[...] = _dot_exact_rhs(x_ref[...], _head_sum_mat(D))

    return pl.pallas_call(
        kern, name=name, out_shape=jax.ShapeDtypeStruct((R, LANES), F32), grid=(R // tm,),
        in_specs=[pl.BlockSpec((tm, D), lambda i: (i, 0))],
        out_specs=pl.BlockSpec((tm, LANES), lambda i: (i, 0)),
        compiler_params=_cparams(("parallel",)),
    )(x)


def _adam_math(w, g, m, v):
    m = ADAM_B1 * m + (1.0 - ADAM_B1) * g
    v = ADAM_B2 * v + (1.0 - ADAM_B2) * (g * g)
    m_hat = m / (1.0 - ADAM_B1 ** ADAM_STEP)
    v_hat = v / (1.0 - ADAM_B2 ** ADAM_STEP)
    delta = -ADAM_LR * (m_hat / (jnp.sqrt(v_hat) + ADAM_EPS) + ADAM_WD * w)
    return delta, m, v


def _adamw_sharded(pieces, w, m, v, *, name):
    R, C = w.shape
    tr = _blk(R, 256) if R % 8 == 0 else R

    def kern(p_ref, w_ref, m_ref, v_ref, g_ref, d_ref, nm_ref, nv_ref):
        g = p_ref[0].astype(F32)
        for j in range(1, N_DEV):
            g = g + p_ref[j].astype(F32)
        d, nm, nv = _adam_math(w_ref[...], g, m_ref[...], v_ref[...])
        g_ref[...] = g
        d_ref[...] = d
        nm_ref[...] = nm
        nv_ref[...] = nv

    row = pl.BlockSpec((tr, C), lambda i: (i, 0))
    return pl.pallas_call(
        kern, name=name, out_shape=[jax.ShapeDtypeStruct((R, C), F32)] * 4, grid=(R // tr,),
        in_specs=[pl.BlockSpec((N_DEV, tr, C), lambda i: (0, i, 0)), row, row, row],
        out_specs=[row] * 4,
        compiler_params=_cparams(("parallel",)),
    )(pieces, w, m, v)


def _small_update(parts, wd, md, vd, we, me, ve, nd, ne, D, *, name):
    def kern(p_ref, wd_ref, md_ref, vd_ref, we_ref, me_ref, ve_ref,
             gd_ref, dd_ref, nmd_ref, nvd_ref, ge_ref, de_ref, nme_ref, nve_ref, loss_ref):
        s = p_ref[0]
        for j in range(1, N_DEV):
            s = s + p_ref[j]
        gd = s[:nd]
        d, nm, nv = _adam_math(wd_ref[...], gd, md_ref[...], vd_ref[...])
        gd_ref[...] = gd
        dd_ref[...] = d
        nmd_ref[...] = nm
        nvd_ref[...] = nv
        ge = _dot_exact_rhs(s[nd:nd + ne], _head_sum_mat(D))
        d, nm, nv = _adam_math(we_ref[...], ge, me_ref[...], ve_ref[...])
        ge_ref[...] = ge
        de_ref[...] = d
        nme_ref[...] = nm
        nve_ref[...] = nv
        tot = jnp.sum(jnp.sum(s[nd + ne:], axis=1, keepdims=True), axis=0, keepdims=True)
        loss_ref[...] = jnp.broadcast_to(tot, loss_ref.shape)

    vm = pl.BlockSpec(memory_space=pltpu.VMEM)
    return pl.pallas_call(
        kern, name=name,
        out_shape=[jax.ShapeDtypeStruct((nd, D), F32)] * 4 + [jax.ShapeDtypeStruct((ne, LANES), F32)] * 4
        + [jax.ShapeDtypeStruct((8, LANES), F32)],
        in_specs=[vm] * 7, out_specs=[vm] * 9,
        compiler_params=pltpu.CompilerParams(vmem_limit_bytes=VMEM_LIMIT),
    )(parts, wd, md, vd, we, me, ve)


def _dev_index(p):
    return 4 * p[0] + 2 * p[1] + p[2]


def _all_gather(arrs, *, name):
    n = len(arrs)

    def body(*refs):
        xs, outs = refs[:n], refs[n:2 * n]
        send, recv, loc = refs[2 * n:]
        x, y, c = lax.axis_index("x"), lax.axis_index("y"), lax.axis_index("c")
        me, sib = (x, y, c), (x, y, 1 - c)
        chips = [(1 - x, y), (x, 1 - y), (1 - x, 1 - y)]

        def cp(k, s, block, to, src=None):
            dst = outs[k].at[_dev_index(block)]
            return pltpu.make_async_remote_copy(
                src_ref=dst if src is None else src, dst_ref=dst,
                send_sem=send.at[7 * k + s], recv_sem=recv.at[7 * k + s],
                device_id=to, device_id_type=pl.DeviceIdType.MESH)

        mine = [pltpu.make_async_copy(xs[k], outs[k].at[_dev_index(me)], loc.at[k]) for k in range(n)]
        for m in mine:
            m.start()
        first = []
        for k in range(n):
            first.append(cp(k, 0, me, sib, src=xs[k]))
            for j, ch in enumerate(chips):
                first.append(cp(k, 1 + j, me, (*ch, c), src=xs[k]))
        for f in first:
            f.start()
        passed = []
        for j, ch in enumerate(chips):
            for k in range(n):
                cp(k, 1 + j, (*ch, c), me).wait_recv()
                p = cp(k, 4 + j, (*ch, c), sib)
                p.start()
                passed.append(p)
        for k in range(n):
            cp(k, 0, sib, me).wait_recv()
            for j, ch in enumerate(chips):
                cp(k, 4 + j, (*ch, 1 - c), me).wait_recv()
        for f in first + passed:
            f.wait_send()
        for m in mine:
            m.wait()

    hbm = pl.BlockSpec(memory_space=pl.ANY)
    return pl.pallas_call(
        body, name=name,
        out_shape=[jax.ShapeDtypeStruct((N_DEV,) + a.shape, a.dtype) for a in arrs],
        in_specs=[hbm] * n, out_specs=[hbm] * n,
        scratch_shapes=[pltpu.SemaphoreType.DMA((7 * n,)), pltpu.SemaphoreType.DMA((7 * n,)),
                        pltpu.SemaphoreType.DMA((n,))],
    )(*arrs)


_RELS = [(fx, fy, fc) for fx in (0, 1) for fy in (0, 1) for fc in (0, 1) if fx or fy or fc]


def _direct_ops(src_of, dst_of, n_in):
    def make(ins, outs, send, recv, loc):
        xs, n = ins[:n_in], len(outs)
        x, y, c = lax.axis_index("x"), lax.axis_index("y"), lax.axis_index("c")
        me = _dev_index((x, y, c))
        peers = [(1 - x if r[0] else x, 1 - y if r[1] else y, 1 - c if r[2] else c) for r in _RELS]

        def remote(k, s, src, dst):
            return pltpu.make_async_remote_copy(
                src_ref=src, dst_ref=dst, send_sem=send.at[7 * k + s], recv_sem=recv.at[7 * k + s],
                device_id=peers[s], device_id_type=pl.DeviceIdType.MESH)

        def local(k):
            return pltpu.make_async_copy(src_of(xs, k, me, me), dst_of(outs, k, me), loc.at[k])

        def sends():
            return [remote(k, s, src_of(xs, k, me, _dev_index(p)), dst_of(outs, k, me))
                    for k in range(n) for s, p in enumerate(peers)]

        def start():
            for k in range(n):
                local(k).start()
            for cp in sends():
                cp.start()

        def finish():
            for k in range(n):
                for s, p in enumerate(peers):
                    slot = dst_of(outs, k, _dev_index(p))
                    remote(k, s, slot, slot).wait_recv()
            for cp in sends():
                cp.wait_send()
            for k in range(n):
                local(k).wait()

        return start, finish

    return make


def _gather_comm(arrs):
    make = _direct_ops(lambda xs, k, me, p: xs[k], lambda outs, k, j: outs[k].at[j], len(arrs))
    return make, list(arrs), [jax.ShapeDtypeStruct((N_DEV,) + a.shape, a.dtype) for a in arrs], {}


def _exchange_comm(pieces, bufs, layer):
    n = len(pieces)
    make = _direct_ops(lambda xs, k, me, p: xs[k].at[p], lambda outs, k, j: outs[k].at[j, layer], n)
    return (make, list(pieces) + list(bufs), [jax.ShapeDtypeStruct(b.shape, b.dtype) for b in bufs],
            {n + k: k for k in range(n)})


def _comm_call(comm, *, name):
    make, ins, out_shapes, alias = comm
    n_ci, n_co = len(ins), len(out_shapes)

    def body(*refs):
        start, finish = make(refs[:n_ci], refs[n_ci:n_ci + n_co], *refs[n_ci + n_co:])
        start()
        finish()

    hbm = pl.BlockSpec(memory_space=pl.ANY)
    return pl.pallas_call(
        body, name=name, out_shape=list(out_shapes), in_specs=[hbm] * n_ci, out_specs=[hbm] * n_co,
        scratch_shapes=[pltpu.SemaphoreType.DMA((7 * n_co,)), pltpu.SemaphoreType.DMA((7 * n_co,)),
                        pltpu.SemaphoreType.DMA((n_co,))],
        input_output_aliases=alias,
    )(*ins)


def _layer_fwd(x, P, dims, l, nxt):
    Bl, L, D = dims
    Dc = D + 2 * SSM_GROUPS * SSM_STATE
    qc, kc, vc, dtc = D + Dc, 2 * D + Dc, 3 * D + Dc, 4 * D + Dc
    gat = (lambda key: _gather_comm([nxt[key]])) if nxt is not None else (lambda key: None)
    h = _rmsnorm_fwd(x, P["norm_mix_g"], name=f"norm_mix_fwd_{l}")
    (proj,), g_in = _mm(h, P["w_in"], out_dtypes=[F32], name=f"in_proj_{l}", comm=gat("w_in"))
    xbc = _conv_fwd(proj, P["conv_w"], P["conv_b"], Bl, L, D, Dc, name=f"conv_fwd_{l}")
    y, states = _ssd_fwd(xbc, proj, P["dt_bias_e"], P["a_log_e"], Bl, L, D, dtc, name=f"ssd_fwd_{l}")
    yatt, tot = _attn_fwd(proj, Bl, L, D, qc, kc, vc, name=f"attn_fwd_{l}")
    ymix = _mix_fwd(y, xbc, proj, yatt, P["d_skip_e"], P["ssd_norm_g"], P["att_norm_g"], D, name=f"mix_fwd_{l}")
    (x1,), g_out = _mm(ymix, P["w_out"], out_dtypes=[F32], epilogue=lambda r, res: (r + res,), extras=(x,),
                       name=f"out_proj_{l}", comm=gat("w_out"))
    h2 = _rmsnorm_fwd(x1, P["norm_mlp_g"], name=f"norm_mlp_fwd_{l}")

    def relu2(r):
        a = jnp.maximum(r, 0.0)
        return r, a * a

    (u, act), g_up = _mm(h2, P["w_up"], out_dtypes=[F32, BF16], epilogue=relu2, name=f"mlp_up_{l}",
                         comm=gat("w_up"))
    (x2,), g_down = _mm(act, P["w_down"], out_dtypes=[F32], epilogue=lambda r, res: (r + res,), extras=(x1,),
                        name=f"mlp_down_{l}", comm=gat("w_down"))
    saved = dict(x=x, h=h, proj=proj, xbc=xbc, y=y, states=states, yatt=yatt, tot=tot, ymix=ymix,
                 x1=x1, h2=h2, u=u, act=act)
    gathered = dict(w_in=g_in[0], w_out=g_out[0], w_up=g_up[0], w_down=g_down[0]) if nxt is not None else None
    return x2, saved, gathered


def _layer_bwd(dx2, S, P, dims, l, pend, bufs):
    Bl, L, D = dims
    Dc = D + 2 * SSM_GROUPS * SSM_STATE
    qc, kc, vc, dtc = D + Dc, 2 * D + Dc, 3 * D + Dc, 4 * D + Dc
    G = {}

    def exch(keys):
        if pend is None:
            return None
        return _exchange_comm([pend[k] for k in keys], [bufs[k] for k in keys], l + 1)

    def took(keys, outs):
        if pend is not None:
            bufs.update(zip(keys, outs))

    (du,), o = _mm(dx2, P["w_down_t"], out_dtypes=[BF16], extras=(S["u"],),
                   epilogue=lambda r, u: (r * (2.0 * jnp.maximum(u, 0.0)),), name=f"mlp_down_bwd_{l}",
                   comm=exch(["w_in"]))
    took(["w_in"], o)
    (G["w_down"],), o = _mm(S["act"], dx2, ta=True, out_dtypes=[F32], name=f"mlp_down_dw_{l}",
                            comm=exch(["w_up"]))
    took(["w_up"], o)
    (dh2,), o = _mm(du, P["w_up_t"], out_dtypes=[F32], name=f"mlp_up_bwd_{l}", comm=exch(["w_down"]))
    took(["w_down"], o)
    (G["w_up"],), o = _mm(S["h2"], du, ta=True, out_dtypes=[F32], name=f"mlp_up_dw_{l}",
                          comm=exch(["w_out", "conv_w"]))
    took(["w_out", "conv_w"], o)
    dx1, G["norm_mlp_g"] = _rmsnorm_bwd(dh2, S["x1"], P["norm_mlp_g"], dx2, name=f"norm_mlp_bwd_{l}")
    (dmix,), _ = _mm(dx1, P["w_out_t"], out_dtypes=[F32], name=f"out_proj_bwd_{l}")
    (G["w_out"],), _ = _mm(S["ymix"], dx1, ta=True, out_dtypes=[F32], name=f"out_proj_dw_{l}")
    dz, dy, dyatt, G["ssd_norm_g"], G["att_norm_g"] = _mix_bwd(
        dmix, S["y"], S["xbc"], S["proj"], S["yatt"], P["d_skip_e"], P["ssd_norm_g"], P["att_norm_g"], D,
        name=f"mix_bwd_{l}")
    dq, dk, dv = _attn_bwd(dyatt, S["tot"], S["proj"], Bl, L, D, qc, kc, vc, name=f"attn_bwd_{l}")
    dxs, dB, dC, ddt, G["dt_bias_e"], G["a_log_e"], G["d_skip_e"] = _ssd_bwd(
        dy, S["states"], S["xbc"], S["proj"], P["dt_bias_e"], P["a_log_e"], P["d_skip_e"],
        Bl, L, D, dtc, name=f"ssd_bwd_{l}")
    dact = jnp.concatenate([dxs, dB, dC], axis=1)
    dxbc, G["conv_w"], G["conv_b"] = _conv_bwd(dact, S["proj"], P["conv_w"], P["conv_b"], Bl, L, D, Dc,
                                               name=f"conv_bwd_{l}")
    dproj = jnp.concatenate([dz, dxbc, dq, dk, dv, ddt], axis=1)
    (dh,), _ = _mm(dproj, P["w_in_t"], out_dtypes=[F32], name=f"in_proj_bwd_{l}")
    (G["w_in"],), _ = _mm(S["h"], dproj, ta=True, out_dtypes=[F32], name=f"in_proj_dw_{l}")
    dx, G["norm_mix_g"] = _rmsnorm_bwd(dh, S["x"], P["norm_mix_g"], dx1, name=f"norm_mix_bwd_{l}")
    return dx, G, bufs


def _pad_rows(a, rows):
    return jnp.concatenate([a, jnp.zeros((rows - a.shape[0],) + a.shape[1:], a.dtype)], axis=0)


def kernel(x, norm_mix_g, w_in, conv_w, conv_b, dt_bias, a_log, d_skip, ssd_norm_g, att_norm_g, w_out, norm_mlp_g, w_up, w_down, final_norm_g, loss_target, m_norm_mix_g, m_w_in, m_conv_w, m_conv_b, m_dt_bias, m_a_log, m_d_skip, m_ssd_norm_g, m_att_norm_g, m_w_out, m_norm_mlp_g, m_w_up, m_w_down, m_final_norm_g, v_norm_mix_g, v_w_in, v_conv_w, v_conv_b, v_dt_bias, v_a_log, v_d_skip, v_ssd_norm_g, v_att_norm_g, v_w_out, v_norm_mlp_g, v_w_up, v_w_down, v_final_norm_g):
    Bl, L, D = x.shape
    T = Bl * L
    depth = w_in.shape[0]
    H = D // HEAD_DIM
    Dc = D + 2 * SSM_GROUPS * SSM_STATE
    Dff = w_up.shape[2] * N_DEV
    dims = (Bl, L, D)

    bf = lambda a: a.astype(BF16)
    shards = [dict(w_in=bf(w_in[l]), w_out=bf(w_out[l]), w_up=bf(w_up[l]), w_down=bf(w_down[l]))
              for l in range(depth)]
    g0 = _all_gather([shards[0][k] for k in ("w_in", "w_out", "w_up", "w_down")] + [conv_w],
                     name="gather_weights_0")
    gathered = dict(zip(("w_in", "w_out", "w_up", "w_down"), g0))
    Cw = g0[4].transpose(1, 2, 0, 3).reshape(depth, CONV_WIDTH, Dc)
    rep = lambda a: jnp.repeat(a, HEAD_DIM, axis=1)
    dt_bias_e, a_log_e, d_skip_e = rep(dt_bias), rep(a_log), rep(d_skip)
    o = [0, D, D + Dc, D + Dc + H, 2 * D + Dc + H, 3 * D + Dc + H, 4 * D + Dc + H]

    def layer_params(g, l):
        W_in = g["w_in"].transpose(1, 0, 2).reshape(D, -1)
        wz, wxbc, wdt, wq, wk, wv = [W_in[:, o[i]:o[i + 1]] for i in range(6)]
        W_pack = jnp.concatenate([wz, wxbc, wq, wk, wv, jnp.repeat(wdt, HEAD_DIM, axis=1)], axis=1)
        W_out = g["w_out"].reshape(2 * D, D)
        W_up = g["w_up"].transpose(1, 0, 2).reshape(D, Dff)
        W_down = g["w_down"].reshape(Dff, D)
        return dict(
            norm_mix_g=norm_mix_g[l], w_in=W_pack, w_in_t=W_pack.T, conv_w=Cw[l], conv_b=conv_b[l],
            dt_bias_e=dt_bias_e[l:l + 1], a_log_e=a_log_e[l:l + 1], d_skip_e=d_skip_e[l:l + 1],
            ssd_norm_g=ssd_norm_g[l], att_norm_g=att_norm_g[l], w_out=W_out, w_out_t=W_out.T,
            norm_mlp_g=norm_mlp_g[l], w_up=W_up, w_up_t=W_up.T, w_down=W_down, w_down_t=W_down.T)

    xa = x.reshape(T, D)
    saved, params = [], []
    for l in range(depth):
        params.append(layer_params(gathered, l))
        xa, s, gathered = _layer_fwd(xa, params[l], dims, l, shards[l + 1] if l + 1 < depth else None)
        saved.append(s)
    dx, g_final, loss_part = _loss_head(xa, final_norm_g, loss_target.reshape(T, D), name="loss_head")

    n_in = w_in.shape[2]

    def pieces_of(G, l):
        gW = G["w_in"]
        gdt = _collapse_heads(gW[:, 4 * D + Dc:], name=f"collapse_w_dt_{l}")[:, :H]
        gW_in = jnp.concatenate([gW[:, :D + Dc], gdt, gW[:, D + Dc:4 * D + Dc]], axis=1)
        return dict(
            w_in=bf(gW_in.reshape(D, N_DEV, n_in).transpose(1, 0, 2)),
            w_out=bf(G["w_out"].reshape(N_DEV, 2 * D // N_DEV, D)),
            w_up=bf(G["w_up"].reshape(D, N_DEV, Dff // N_DEV).transpose(1, 0, 2)),
            w_down=bf(G["w_down"].reshape(N_DEV, Dff // N_DEV, D)),
            conv_w=bf(G["conv_w"].reshape(CONV_WIDTH, N_DEV, Dc // N_DEV).transpose(1, 0, 2)))

    keys = ("w_in", "w_out", "w_up", "w_down", "conv_w")
    bufs = {k: lax.empty((N_DEV, depth) + w.shape[1:], BF16)
            for k, w in zip(keys, (w_in, w_out, w_up, w_down, conv_w))}
    grads = [None] * depth
    pend = None
    for l in reversed(range(depth)):
        dx, grads[l], bufs = _layer_bwd(dx, saved[l], params[l], dims, l, pend, bufs)
        pend = pieces_of(grads[l], l)
    grad_x = dx.reshape(Bl, L, D)
    last = _comm_call(_exchange_comm([pend[k] for k in keys], [bufs[k] for k in keys], 0), name="exchange_grads_0")
    r_in, r_out, r_up, r_down, r_conv = last
    st = lambda k: jnp.stack([grads[l][k] for l in range(depth)])

    def sharded(pieces, w, m, v, name):
        shp = w.shape
        C = shp[-1]
        res = _adamw_sharded(pieces.reshape(N_DEV, -1, C), w.reshape(-1, C), m.reshape(-1, C),
                             v.reshape(-1, C), name=name)
        return [a.reshape(shp) for a in res]

    u_in = sharded(r_in, w_in, m_w_in, v_w_in, "adamw_w_in")
    u_out = sharded(r_out, w_out, m_w_out, v_w_out, "adamw_w_out")
    u_up = sharded(r_up, w_up, m_w_up, v_w_up, "adamw_w_up")
    u_down = sharded(r_down, w_down, m_w_down, v_w_down, "adamw_w_down")
    u_conv = sharded(r_conv, conv_w, m_conv_w, v_conv_w, "adamw_conv_w")

    rows = lambda a: a.reshape(-1, D)
    direct_names = ["norm_mix_g", "ssd_norm_g", "att_norm_g", "norm_mlp_g", "conv_b"]
    direct_g = [rows(st(k)) for k in direct_names] + [g_final]
    exp_names = ["dt_bias_e", "a_log_e", "d_skip_e"]
    exp_g = [rows(st(k)) for k in exp_names]
    n_direct = sum(a.shape[0] for a in direct_g)
    nd = -(-n_direct // 8) * 8
    ne = -(-3 * depth // 8) * 8
    part = jnp.concatenate([_pad_rows(jnp.concatenate(direct_g, axis=0), nd),
                            _pad_rows(jnp.concatenate(exp_g, axis=0), ne),
                            _pad_rows(loss_part, 8)], axis=0)
    (parts,) = _all_gather([part], name="gather_small_grads")

    def direct_pack(ws):
        return _pad_rows(jnp.concatenate([rows(a) for a in ws], axis=0), nd)

    def exp_pack(ws):
        a = jnp.concatenate(ws, axis=0)
        a = jnp.concatenate([a, jnp.zeros((a.shape[0], LANES - H), F32)], axis=1)
        return _pad_rows(a, ne)

    res = _small_update(
        parts,
        direct_pack([norm_mix_g, ssd_norm_g, att_norm_g, norm_mlp_g, conv_b, final_norm_g]),
        direct_pack([m_norm_mix_g, m_ssd_norm_g, m_att_norm_g, m_norm_mlp_g, m_conv_b, m_final_norm_g]),
        direct_pack([v_norm_mix_g, v_ssd_norm_g, v_att_norm_g, v_norm_mlp_g, v_conv_b, v_final_norm_g]),
        exp_pack([dt_bias, a_log, d_skip]), exp_pack([m_dt_bias, m_a_log, m_d_skip]),
        exp_pack([v_dt_bias, v_a_log, v_d_skip]), nd, ne, D, name="small_update")
    loss = res[8][0, 0]

    def unpack(kind):
        d, e = res[kind], res[4 + kind]
        out = {}
        r0 = 0
        for nm, shp in [("norm_mix_g", (depth, D)), ("ssd_norm_g", (depth, D)), ("att_norm_g", (depth, D)),
                        ("norm_mlp_g", (depth, D)), ("conv_b", (depth, Dc)), ("final_norm_g", (D,))]:
            n = 1
            for s_ in shp:
                n *= s_
            out[nm] = d[r0:r0 + n // D].reshape(shp)
            r0 += n // D
        for i, nm in enumerate(["dt_bias", "a_log", "d_skip"]):
            out[nm] = e[i * depth:(i + 1) * depth, :H]
        return out

    names = ["norm_mix_g", "w_in", "conv_w", "conv_b", "dt_bias", "a_log", "d_skip", "ssd_norm_g",
             "att_norm_g", "w_out", "norm_mlp_g", "w_up", "w_down", "final_norm_g"]
    big = {"w_in": u_in, "w_out": u_out, "w_up": u_up, "w_down": u_down, "conv_w": u_conv}
    outs = [loss, grad_x]
    for kind in range(4):
        small = unpack(kind)
        for nm in names:
            outs.append(big[nm][kind] if nm in big else small[nm])
    return tuple(outs)
```

```python
import functools

import jax
import jax.numpy as jnp
from jax import lax
from jax.experimental import pallas as pl
from jax.experimental.pallas import tpu as pltpu

F32 = jnp.float32
BF16 = jnp.bfloat16

HEAD_DIM = 64
SSM_STATE = 128
SSM_GROUPS = 4
CONV_WIDTH = 4
CHUNK = 128
ATT_BLOCK = 512
ATT_SUB = 256
SSD_UNROLL = 4
EPS = 1e-5
LANES = 128
N_DEV = 8
VMEM_LIMIT = 56 * 1024 * 1024
MM_VMEM_BUDGET = 44 * 1024 * 1024

ADAM_LR = 0.001
ADAM_B1 = 0.9
ADAM_B2 = 0.999
ADAM_EPS = 1e-08
ADAM_WD = 0.01
ADAM_STEP = 10

_NT =(((1,), (1,)), ((), ()))
_TN = (((0,), (0,)), ((), ()))


def _blk(n, pref):
    if n <= pref:
        return n
    b = (pref // LANES) * LANES
    while b >= LANES:
        if n % b == 0:
            return b
        b -= LANES
    return n


def _cparams(sem):
    return pltpu.CompilerParams(dimension_semantics=sem, vmem_limit_bytes=VMEM_LIMIT)


def _dot(a, b):
    return jnp.dot(a, b, preferred_element_type=F32)


def _dg(a, b, dims):
    return lax.dot_general(a, b, dims, preferred_element_type=F32)


def _split3(x):
    h = x.astype(BF16)
    r = x - h.astype(F32)
    m = r.astype(BF16)
    l = (r - m.astype(F32)).astype(BF16)
    return h, m, l


def _dot_exact_rhs(x, c):
    h, m, l = _split3(x)
    return _dot(h, c) + _dot(m, c) + _dot(l, c)


def _dot_exact_lhs(c, x, dims):
    h, m, l = _split3(x)
    return _dg(c, h, dims) + _dg(c, m, dims) + _dg(c, l, dims)


def _dot_exact_lhs_stacked(c, x):
    return _dot(jnp.concatenate([c, c, c], axis=1), jnp.concatenate(_split3(x), axis=0))


def _mask(cond):
    return jnp.where(cond, 1.0, 0.0).astype(BF16)


def _sigmoid(x):
    return 1.0 / (1.0 + jnp.exp(-x))


def _softplus(x):
    return jnp.maximum(x, 0.0) + jnp.log(1.0 + jnp.exp(-jnp.abs(x)))


def _mm_tiles(M, N, K, a_bytes, b_bytes, mn_bytes, tm, tn, tk):
    tn = _blk(N, tn)
    for tm_try, tk_try in ((2 * tm, tk), (2 * tm, tk // 2), (tm, tk), (tm, tk // 2), (tm // 2, tk // 2)):
        bm, bk = _blk(M, tm_try), _blk(K, tk_try)
        need = 2 * (bm * bk * a_bytes + bk * tn * b_bytes + bm * tn * mn_bytes) + bm * tn * 4
        if need <= MM_VMEM_BUDGET:
            return bm, tn, bk
    return _blk(M, tm // 2), tn, _blk(K, tk // 4)


def _mm(a, b, *, name, out_dtypes, ta=False, epilogue=None, extras=(), tm=1024, tn=1024, tk=2048, comm=None):
    if ta:
        K, M = a.shape
    else:
        M, K = a.shape
    N = b.shape[1]
    assert b.shape[0] == K
    tm, tn, tk = _mm_tiles(M, N, K, a.dtype.itemsize, b.dtype.itemsize,
                           sum(e.dtype.itemsize for e in extras) + sum(jnp.dtype(d).itemsize for d in out_dtypes),
                           tm, tn, tk)
    nj, ni, nk = N // tn, M // tm, K // tk
    n_ex, n_out = len(extras), len(out_dtypes)
    c_make, c_ins, c_outs, c_alias = comm if comm is not None else (None, [], [], {})
    n_ci, n_co = len(c_ins), len(c_outs)

    def kern(*refs):
        a_ref, b_ref = refs[0], refs[1]
        ex = refs[2:2 + n_ex]
        ci = refs[2 + n_ex:2 + n_ex + n_ci]
        o0 = 2 + n_ex + n_ci
        outs = refs[o0:o0 + n_out]
        co = refs[o0 + n_out:o0 + n_out + n_co]
        acc = refs[o0 + n_out + n_co]
        j, i, k = pl.program_id(0), pl.program_id(1), pl.program_id(2)
        if comm is not None:
            start, finish = c_make(ci, co, *refs[o0 + n_out + n_co + 1:])

            @pl.when((j == 0) & (i == 0) & (k == 0))
            def _():
                start()

        av = a_ref[...].astype(BF16)
        bv = b_ref[...].astype(BF16)
        prod = _dg(av, bv, _TN) if ta else _dot(av, bv)

        def write(r):
            vals = epilogue(r, *[e[...] for e in ex]) if epilogue is not None else (r,)
            for o, v in zip(outs, vals):
                o[...] = v.astype(o.dtype)

        if nk == 1:
            write(prod)
        else:
            @pl.when(k == 0)
            def _():
                acc[...] = prod

            @pl.when(k > 0)
            def _():
                acc[...] += prod

            @pl.when(k == nk - 1)
            def _():
                write(acc[...])

        if comm is not None:
            @pl.when((j == nj - 1) & (i == ni - 1) & (k == nk - 1))
            def _():
                finish()

    if ta:
        a_spec = pl.BlockSpec((tk, tm), lambda j, i, k: (k, i))
    else:
        a_spec = pl.BlockSpec((tm, tk), lambda j, i, k: (i, k))
    b_spec = pl.BlockSpec((tk, tn), lambda j, i, k: (k, j))
    o_spec = pl.BlockSpec((tm, tn), lambda j, i, k: (i, j))
    hbm = pl.BlockSpec(memory_space=pl.ANY)
    scratch = [pltpu.VMEM((tm, tn), F32)]
    if comm is not None:
        scratch += [pltpu.SemaphoreType.DMA((7 * n_co,)), pltpu.SemaphoreType.DMA((7 * n_co,)),
                    pltpu.SemaphoreType.DMA((n_co,))]
    res = pl.pallas_call(
        kern, name=name,
        out_shape=[jax.ShapeDtypeStruct((M, N), d) for d in out_dtypes] + list(c_outs),
        grid=(nj, ni, nk),
        in_specs=[a_spec, b_spec] + [o_spec] * n_ex + [hbm] * n_ci,
        out_specs=[o_spec] * n_out + [hbm] * n_co,
        scratch_shapes=scratch,
        input_output_aliases={2 + n_ex + ki: n_out + ko for ki, ko in c_alias.items()},
        compiler_params=_cparams(("parallel", "parallel", "arbitrary") if comm is None
                                 else ("arbitrary", "arbitrary", "arbitrary")),
    )(a, b, *extras, *c_ins)
    return res[:n_out], res[n_out:]


def _rmsnorm_fwd(x, g, *, name):
    T, D = x.shape
    tm = _blk(T, 512)

    def kern(x_ref, g_ref, h_ref):
        xv = x_ref[...]
        r = lax.rsqrt(jnp.mean(xv * xv, axis=-1, keepdims=True) + EPS)
        h_ref[...] = (xv * r * g_ref[...]).astype(h_ref.dtype)

    return pl.pallas_call(
        kern, name=name, out_shape=jax.ShapeDtypeStruct((T, D), BF16), grid=(T // tm,),
        in_specs=[pl.BlockSpec((tm, D), lambda i: (i, 0)), pl.BlockSpec((1, D), lambda i: (0, 0))],
        out_specs=pl.BlockSpec((tm, D), lambda i: (i, 0)),
        compiler_params=_cparams(("parallel",)),
    )(x, g.reshape(1, D))


def _rmsnorm_bwd(dh, x, g, dres, *, name):
    T, D = x.shape
    tm = _blk(T, 512)

    def kern(dh_ref, x_ref, g_ref, dres_ref, dx_ref, dg_ref):
        @pl.when(pl.program_id(0) == 0)
        def _():
            dg_ref[...] = jnp.zeros_like(dg_ref)

        xv = x_ref[...]
        r = lax.rsqrt(jnp.mean(xv * xv, axis=-1, keepdims=True) + EPS)
        xn = xv * r
        dy = dh_ref[...].astype(F32)
        dyg = dy * g_ref[...]
        dx = r * (dyg - xn * jnp.mean(dyg * xn, axis=-1, keepdims=True))
        dx_ref[...] = dres_ref[...] + dx
        dg_ref[...] += jnp.sum(dy * xn, axis=0, keepdims=True)

    row = pl.BlockSpec((tm, D), lambda i: (i, 0))
    vec = pl.BlockSpec((1, D), lambda i: (0, 0))
    return pl.pallas_call(
        kern, name=name,
        out_shape=[jax.ShapeDtypeStruct((T, D), F32), jax.ShapeDtypeStruct((1, D), F32)],
        grid=(T // tm,), in_specs=[row, row, vec, row], out_specs=[row, vec],
        compiler_params=_cparams(("arbitrary",)),
    )(dh, x, g.reshape(1, D), dres)


def _loss_head(x, g, target, *, name):
    T, D = x.shape
    tm = _blk(T, 512)

    def kern(x_ref, g_ref, t_ref, dx_ref, dg_ref, loss_ref):
        @pl.when(pl.program_id(0) == 0)
        def _():
            dg_ref[...] = jnp.zeros_like(dg_ref)
            loss_ref[...] = jnp.zeros_like(loss_ref)

        xv = x_ref[...]
        gv = g_ref[...]
        r = lax.rsqrt(jnp.mean(xv * xv, axis=-1, keepdims=True) + EPS)
        xn = xv * r
        err = xn * gv - t_ref[...]
        loss_ref[...] += jnp.sum(err * err, axis=0, keepdims=True) * (0.5 / D)
        dy = err * (1.0 / D)
        dyg = dy * gv
        dx_ref[...] = r * (dyg - xn * jnp.mean(dyg * xn, axis=-1, keepdims=True))
        dg_ref[...] += jnp.sum(dy * xn, axis=0, keepdims=True)

    row = pl.BlockSpec((tm, D), lambda i: (i, 0))
    vec = pl.BlockSpec((1, D), lambda i: (0, 0))
    return pl.pallas_call(
        kern, name=name,
        out_shape=[jax.ShapeDtypeStruct((T, D), F32), jax.ShapeDtypeStruct((1, D), F32),
                   jax.ShapeDtypeStruct((1, D), F32)],
        grid=(T // tm,), in_specs=[row, vec, row], out_specs=[row, vec, vec],
        compiler_params=_cparams(("arbitrary",)),
    )(x, g.reshape(1, D), target)


def _shift_down(u, s, L):
    t = lax.broadcasted_iota(jnp.int32, u.shape, 0)
    return jnp.where(t >= s, pltpu.roll(u, s, 0), 0.0)


def _shift_up(u, s, L):
    t = lax.broadcasted_iota(jnp.int32, u.shape, 0)
    return jnp.where(t < L - s, pltpu.roll(u, L - s, 0), 0.0)


def _conv_pre(u, w_ref, b_ref, L):
    pre = u * w_ref[CONV_WIDTH - 1:CONV_WIDTH, :] + b_ref[...]
    for s in range(1, CONV_WIDTH):
        pre = pre + _shift_down(u, s, L) * w_ref[CONV_WIDTH - 1 - s:CONV_WIDTH - s, :]
    return pre


def _conv_fwd(proj, w, b, Bl, L, col0, Dc, *, name):
    cb = LANES
    c0 = col0 // cb

    def kern(u_ref, w_ref, b_ref, o_ref):
        pre = _conv_pre(u_ref[...], w_ref, b_ref, L)
        o_ref[...] = pre * _sigmoid(pre)

    return pl.pallas_call(
        kern, name=name, out_shape=jax.ShapeDtypeStruct((Bl * L, Dc), F32), grid=(Bl, Dc // cb),
        in_specs=[pl.BlockSpec((L, cb), lambda bi, j: (bi, c0 + j)),
                  pl.BlockSpec((CONV_WIDTH, cb), lambda bi, j: (0, j)),
                  pl.BlockSpec((1, cb), lambda bi, j: (0, j))],
        out_specs=pl.BlockSpec((L, cb), lambda bi, j: (bi, j)),
        compiler_params=_cparams(("parallel", "parallel")),
    )(proj, w, b.reshape(1, Dc))


def _conv_bwd(dact, proj, w, b, Bl, L, col0, Dc, *, name):
    cb = LANES
    c0 = col0 // cb

    def kern(d_ref, u_ref, w_ref, b_ref, du_ref, dw_ref, db_ref):
        @pl.when(pl.program_id(1) == 0)
        def _():
            dw_ref[...] = jnp.zeros_like(dw_ref)
            db_ref[...] = jnp.zeros_like(db_ref)

        u = u_ref[...]
        pre = _conv_pre(u, w_ref, b_ref, L)
        sig = _sigmoid(pre)
        dpre = d_ref[...] * (sig * (1.0 + pre * (1.0 - sig)))
        du = dpre * w_ref[CONV_WIDTH - 1:CONV_WIDTH, :]
        dw_ref[CONV_WIDTH - 1:CONV_WIDTH, :] += jnp.sum(dpre * u, axis=0, keepdims=True)
        for s in range(1, CONV_WIDTH):
            k = CONV_WIDTH - 1 - s
            du = du + _shift_up(dpre, s, L) * w_ref[k:k + 1, :]
            dw_ref[k:k + 1, :] += jnp.sum(dpre * _shift_down(u, s, L), axis=0, keepdims=True)
        db_ref[...] += jnp.sum(dpre, axis=0, keepdims=True)
        du_ref[...] = du.astype(du_ref.dtype)

    return pl.pallas_call(
        kern, name=name,
        out_shape=[jax.ShapeDtypeStruct((Bl * L, Dc), BF16),
                   jax.ShapeDtypeStruct((CONV_WIDTH, Dc), F32), jax.ShapeDtypeStruct((1, Dc), F32)],
        grid=(Dc // cb, Bl),
        in_specs=[pl.BlockSpec((L, cb), lambda j, bi: (bi, j)),
                  pl.BlockSpec((L, cb), lambda j, bi: (bi, c0 + j)),
                  pl.BlockSpec((CONV_WIDTH, cb), lambda j, bi: (0, j)),
                  pl.BlockSpec((1, cb), lambda j, bi: (0, j))],
        out_specs=[pl.BlockSpec((L, cb), lambda j, bi: (bi, j)),
                   pl.BlockSpec((CONV_WIDTH, cb), lambda j, bi: (0, j)),
                   pl.BlockSpec((1, cb), lambda j, bi: (0, j))],
        compiler_params=_cparams(("parallel", "arbitrary")),
    )(dact, proj, w, b.reshape(1, Dc))


def _tri_consts():
    r = lax.broadcasted_iota(jnp.int32, (CHUNK, CHUNK), 0)
    c = lax.broadcasted_iota(jnp.int32, (CHUNK, CHUNK), 1)
    lane0 = c < HEAD_DIM
    return r, c, lane0


def _ssd_chunk_common(x, dtr, bias, alog):
    r, c, _ = _tri_consts()
    dt = _softplus(dtr + bias)
    a = dt * (-jnp.exp(alog))
    tril = _mask(c <= r)
    E = _dot_exact_lhs_stacked(tril, a)
    return dt, E, E.T


def _ssd_head_decay(E, ET, h):
    r, c, _ = _tri_consts()
    cm = jnp.broadcast_to(E[:, h * HEAD_DIM:h * HEAD_DIM + 1], (CHUNK, CHUNK))
    rm = jnp.broadcast_to(ET[h * HEAD_DIM:h * HEAD_DIM + 1, :], (CHUNK, CHUNK))
    return jnp.where(c <= r, jnp.exp(jnp.minimum(cm - rm, 0.0)), 0.0)


def _ssd_fwd(xbc, proj, bias_e, alog_e, Bl, L, D, dt_col0, *, name):
    nc = L // CHUNK
    npairs = D // LANES
    ppg = npairs // SSM_GROUPS
    gw = ppg * LANES
    b0 = D // LANES
    c0 = (D + SSM_GROUPS * SSM_STATE) // LANES
    d0 = dt_col0 // gw
    unroll = SSD_UNROLL if nc % SSD_UNROLL == 0 else 1

    def kern(x_ref, b_ref, c_ref, dtr_ref, bias_ref, alog_ref, y_ref, st_ref, h_scr):
        _, _, lane0 = _tri_consts()
        h_scr[...] = jnp.zeros_like(h_scr)

        lanes = [pl.ds(p * LANES, LANES) for p in range(ppg)]

        def one_chunk(ci, hs):
            rows = pl.ds(pl.multiple_of(ci * CHUNK, CHUNK), CHUNK)
            Bb = b_ref[rows, :].astype(BF16)
            Cb = c_ref[rows, :].astype(BF16)
            CB = _dg(Cb, Bb, _NT)
            new = []
            for p, ln in enumerate(lanes):
                x, hprev = x_ref[rows, ln], hs[p]
                dt, E, ET = _ssd_chunk_common(x, dtr_ref[rows, ln], bias_ref[:, ln], alog_ref[:, ln])
                xdt = x * dt
                xdtb = xdt.astype(BF16)
                ys = []
                for h in range(2):
                    Wh = (CB * _ssd_head_decay(E, ET, h)).astype(BF16)
                    ys.append(_dot(Wh, xdtb))
                y = jnp.where(lane0, ys[0], ys[1]) + _dg(Cb, hprev.astype(BF16), _NT) * jnp.exp(E)
                ds = jnp.exp(jnp.broadcast_to(E[CHUNK - 1:CHUNK, :], (CHUNK, LANES)) - E)
                states = _dg((xdt * ds).astype(BF16), Bb, _TN)
                cd = jnp.exp(jnp.broadcast_to(ET[:, CHUNK - 1:CHUNK], (LANES, SSM_STATE)))
                st_ref[ci, p] = hprev
                y_ref[rows, ln] = y
                new.append(hprev * cd + states)
            return new

        def chunks(i, carry):
            hs = [h_scr[p] for p in range(ppg)]
            for u in range(unroll):
                hs = one_chunk(i * unroll + u, hs)
            for p in range(ppg):
                h_scr[p] = hs[p]
            return carry

        lax.fori_loop(0, nc // unroll, chunks, 0)

    return pl.pallas_call(
        kern, name=name,
        out_shape=[jax.ShapeDtypeStruct((Bl * L, D), F32),
                   jax.ShapeDtypeStruct((Bl, SSM_GROUPS, nc, ppg, LANES, SSM_STATE), F32)],
        grid=(Bl, SSM_GROUPS),
        in_specs=[pl.BlockSpec((L, gw), lambda bi, g: (bi, g)),
                  pl.BlockSpec((L, SSM_STATE), lambda bi, g: (bi, b0 + g)),
                  pl.BlockSpec((L, SSM_STATE), lambda bi, g: (bi, c0 + g)),
                  pl.BlockSpec((L, gw), lambda bi, g: (bi, d0 + g)),
                  pl.BlockSpec((1, gw), lambda bi, g: (0, g)),
                  pl.BlockSpec((1, gw), lambda bi, g: (0, g))],
        out_specs=[pl.BlockSpec((L, gw), lambda bi, g: (bi, g)),
                   pl.BlockSpec((None, None, nc, ppg, LANES, SSM_STATE), lambda bi, g: (bi, g, 0, 0, 0, 0))],
        scratch_shapes=[pltpu.VMEM((ppg, LANES, SSM_STATE), F32)],
        compiler_params=_cparams(("parallel", "parallel")),
    )(xbc, xbc, xbc, proj, bias_e, alog_e)


def _ssd_bwd(dy, states, xbc, proj, bias_e, alog_e, dskip_e, Bl, L, D, dt_col0, *, name):
    nc = L // CHUNK
    npairs = D // LANES
    ppg = npairs // SSM_GROUPS
    gw = ppg * LANES
    b0 = D // LANES
    c0 = (D + SSM_GROUPS * SSM_STATE) // LANES
    d0 = dt_col0 // gw
    unroll = SSD_UNROLL if nc % SSD_UNROLL == 0 else 1

    def kern(dy_ref, st_ref, x_ref, b_ref, c_ref, dtr_ref, bias_ref, alog_ref, dsk_ref,
             dx_ref, db_ref, dc_ref, ddt_ref, gbias_ref, galog_ref, gdsk_ref, g_scr):
        r, c, lane0 = _tri_consts()
        m0 = lane0.astype(F32)
        masks = (m0, 1.0 - m0)
        triu = _mask(c >= r)
        trils = _mask(c < r)
        ones = jnp.ones((CHUNK, LANES), BF16)
        g_scr[...] = jnp.zeros_like(g_scr)

        @pl.when(pl.program_id(1) == 0)
        def _():
            gbias_ref[...] = jnp.zeros_like(gbias_ref)
            galog_ref[...] = jnp.zeros_like(galog_ref)
            gdsk_ref[...] = jnp.zeros_like(gdsk_ref)

        lanes = [pl.ds(p * LANES, LANES) for p in range(ppg)]

        def one_chunk(ci, gs):
            rows = pl.ds(pl.multiple_of(ci * CHUNK, CHUNK), CHUNK)
            Bb = b_ref[rows, :].astype(BF16)
            Cb = c_ref[rows, :].astype(BF16)
            CB = _dg(Cb, Bb, _NT)
            gsum = jnp.zeros((CHUNK, CHUNK), F32)
            dB = jnp.zeros((CHUNK, SSM_STATE), F32)
            dC = jnp.zeros((CHUNK, SSM_STATE), F32)
            ins = [(x_ref[rows, ln], dtr_ref[rows, ln], bias_ref[:, ln], alog_ref[:, ln], dy_ref[rows, ln],
                    st_ref[ci, p], gs[p], dsk_ref[:, ln]) for p, ln in enumerate(lanes)]
            outs = []
            for x, dtr, bias, alog, dyv, hprev, g, dsk in ins:
                dt, E, ET = _ssd_chunk_common(x, dtr, bias, alog)
                xdt = x * dt
                xdtb = xdt.astype(BF16)
                dyb = dyv.astype(BF16)
                hpb = hprev.astype(BF16)
                gb = g.astype(BF16)
                sd = jnp.exp(E)
                ds = jnp.exp(jnp.broadcast_to(E[CHUNK - 1:CHUNK, :], (CHUNK, LANES)) - E)
                t1 = []
                dE = dyv * (_dg(Cb, hpb, _NT) * sd)
                for h in range(2):
                    Lh = _ssd_head_decay(E, ET, h)
                    Wh = (CB * Lh).astype(BF16)
                    t1.append(_dg(Wh, dyb, _TN))
                    Gh = Lh * _dg((dyv * masks[h]).astype(BF16), xdtb, _NT)
                    gsum = gsum + Gh
                    Qh = Gh * CB
                    dE = dE + _dot_exact_rhs(Qh - Qh.T, _mask(c == h * HEAD_DIM))
                dxst = _dg(Bb, gb, _NT) * ds
                dxdt = jnp.where(lane0, t1[0], t1[1]) + dxst
                dysd = (dyv * sd).astype(BF16)
                dC = dC + _dot(dysd, hpb)
                dB = dB + _dot((xdt * ds).astype(BF16), gb)
                cd = jnp.exp(jnp.broadcast_to(ET[:, CHUNK - 1:CHUNK], (LANES, SSM_STATE)))
                gnew = g * cd + _dg(dysd, Cb, _TN)
                nn = (((1,), (0,)), ((), ()))
                da = (_dot_exact_lhs(triu, dE, nn) + _dot_exact_lhs(trils, xdt * dxst, nn)
                      + _dot_exact_lhs(ones, g * cd * hprev, _NT))
                aneg = -jnp.exp(alog)
                ddt = da * aneg + dxdt * x
                ddtr = ddt * _sigmoid(dtr + bias)
                outs.append((gnew, dxdt * dt + dyv * dsk, ddtr, jnp.sum(ddtr, axis=0, keepdims=True),
                             jnp.sum(da * dt, axis=0, keepdims=True) * aneg,
                             jnp.sum(dyv * x, axis=0, keepdims=True)))
            for p, (gnew, dx, ddtr, sb, sa, sk) in enumerate(outs):
                dx_ref[rows, lanes[p]] = dx
                ddt_ref[rows, lanes[p]] = ddtr.astype(ddt_ref.dtype)
                gbias_ref[:, lanes[p]] += sb
                galog_ref[:, lanes[p]] += sa
                gdsk_ref[:, lanes[p]] += sk
            gsb = gsum.astype(BF16)
            db_ref[rows, :] = dB + _dg(gsb, Cb, _TN)
            dc_ref[rows, :] = dC + _dot(gsb, Bb)
            return [o[0] for o in outs]

        def chunks(i, carry):
            gs = [g_scr[p] for p in range(ppg)]
            for u in range(unroll):
                gs = one_chunk(nc - 1 - (i * unroll + u), gs)
            for p in range(ppg):
                g_scr[p] = gs[p]
            return carry

        lax.fori_loop(0, nc // unroll, chunks, 0)

    T = Bl * L
    xspec = pl.BlockSpec((L, gw), lambda g, bi: (bi, g))
    nspec = pl.BlockSpec((L, SSM_STATE), lambda g, bi: (bi, g))
    vspec = pl.BlockSpec((1, gw), lambda g, bi: (0, g))
    return pl.pallas_call(
        kern, name=name,
        out_shape=[jax.ShapeDtypeStruct((T, D), F32),
                   jax.ShapeDtypeStruct((T, SSM_GROUPS * SSM_STATE), F32),
                   jax.ShapeDtypeStruct((T, SSM_GROUPS * SSM_STATE), F32),
                   jax.ShapeDtypeStruct((T, D), BF16),
                   jax.ShapeDtypeStruct((1, D), F32), jax.ShapeDtypeStruct((1, D), F32),
                   jax.ShapeDtypeStruct((1, D), F32)],
        grid=(SSM_GROUPS, Bl),
        in_specs=[xspec,
                  pl.BlockSpec((None, None, nc, ppg, LANES, SSM_STATE), lambda g, bi: (bi, g, 0, 0, 0, 0)),
                  xspec,
                  pl.BlockSpec((L, SSM_STATE), lambda g, bi: (bi, b0 + g)),
                  pl.BlockSpec((L, SSM_STATE), lambda g, bi: (bi, c0 + g)),
                  pl.BlockSpec((L, gw), lambda g, bi: (bi, d0 + g)),
                  vspec, vspec, vspec],
        out_specs=[xspec, nspec, nspec, xspec, vspec, vspec, vspec],
        scratch_shapes=[pltpu.VMEM((ppg, LANES, SSM_STATE), F32)],
        compiler_params=_cparams(("parallel", "arbitrary")),
    )(dy, states, xbc, xbc, xbc, proj, bias_e, alog_e, dskip_e)


def _att_nl(z):
    return jnp.maximum(z, 0.0) + jnp.log(1.0 + jnp.exp(-jnp.abs(z)))


def _sub_pred(pred):
    r = lax.broadcasted_iota(jnp.int32, (ATT_SUB, ATT_SUB), 0)
    c = lax.broadcasted_iota(jnp.int32, (ATT_SUB, ATT_SUB), 1)
    return pred(r, c)


def _cumsum_mm(x, tri):
    return _dot(x.astype(BF16), tri)


def _att_slices(q0, roff, nrows, k0, nsub):
    qs = pl.ds(pl.multiple_of(q0 + roff, ATT_SUB), nrows)
    ks = pl.ds(pl.multiple_of(k0, ATT_SUB), nsub * ATT_SUB)
    return qs, ks, pl.ds(roff, nrows)


def _attn_fwd(proj, Bl, L, D, qc, kc, vc, *, name):
    npairs = D // LANES
    nq = L // ATT_BLOCK
    nsb = ATT_BLOCK // ATT_SUB
    scale = HEAD_DIM ** -0.5

    def kern(q_ref, k_ref, v_ref, o_ref, tot_ref, qm, kb, vb, carry, acc):
        tri = _sub_pred(lambda t, s: t > s)
        lrow = lax.broadcasted_iota(jnp.int32, (1, LANES), 1) < HEAD_DIM
        q = q_ref[...] * scale
        qm[0] = jnp.where(lrow, q, 0.0).astype(BF16)
        qm[1] = jnp.where(lrow, 0.0, q).astype(BF16)
        kb[...] = k_ref[...].astype(BF16)
        vb[...] = v_ref[...].astype(BF16)
        mat = _mask(_sub_pred(lambda j, s: j >= s))

        def tile(q0, roff, nrows, k0, nsub, diag):
            qs, ks, rows = _att_slices(q0, roff, nrows, k0, nsub)
            for h in range(2):
                z = _dg(qm[h, qs, :], kb[ks, :], _NT)
                nl = _att_nl(z)
                cy = carry[h, rows, :]
                atts = []
                for b in reversed(range(nsub)):
                    cols = slice(b * ATT_SUB, (b + 1) * ATT_SUB)
                    masked = diag and b == nsub - 1
                    x = jnp.where(tri, nl[:, cols], 0.0) if masked else nl[:, cols]
                    inc = _cumsum_mm(x, mat)
                    a = jnp.exp(z[:, cols] - inc - cy)
                    atts.append((jnp.where(tri, a, 0.0) if masked else a).astype(BF16))
                    cy = cy + inc[:, 0:1]
                att = jnp.concatenate(atts[::-1], axis=1) if nsub > 1 else atts[0]
                acc[h, rows, :] += _dot(att, vb[ks, :])
                carry[h, rows, :] = cy

        def qblock(qi, _):
            q0 = qi * ATT_BLOCK
            qs = pl.ds(pl.multiple_of(q0, ATT_BLOCK), ATT_BLOCK)
            carry[...] = jnp.zeros_like(carry)
            acc[...] = jnp.zeros_like(acc)
            for rb in range(nsb):
                tile(q0, rb * ATT_SUB, ATT_SUB, q0, rb + 1, True)

            def kblock(j, _):
                tile(q0, 0, ATT_BLOCK, (qi - 1 - j) * ATT_BLOCK, nsb, False)
                return 0

            lax.fori_loop(0, qi, kblock, 0)
            o_ref[qs, :] = jnp.where(lrow, acc[0], acc[1])
            for h in range(2):
                tot_ref[h, qs, :] = jnp.broadcast_to(carry[h], (ATT_BLOCK, LANES))
            return 0

        lax.fori_loop(0, nq, qblock, 0)

    spec = lambda col: pl.BlockSpec((L, LANES), lambda bi, p: (bi, col // LANES + p))
    return pl.pallas_call(
        kern, name=name,
        out_shape=[jax.ShapeDtypeStruct((Bl * L, D), F32),
                   jax.ShapeDtypeStruct((Bl, npairs, 2, L, LANES), F32)],
        grid=(Bl, npairs),
        in_specs=[spec(qc), spec(kc), spec(vc)],
        out_specs=[pl.BlockSpec((L, LANES), lambda bi, p: (bi, p)),
                   pl.BlockSpec((None, None, 2, L, LANES), lambda bi, p: (bi, p, 0, 0, 0))],
        scratch_shapes=[pltpu.VMEM((2, L, LANES), BF16), pltpu.VMEM((L, LANES), BF16),
                        pltpu.VMEM((L, LANES), BF16), pltpu.VMEM((2, ATT_BLOCK, 1), F32),
                        pltpu.VMEM((2, ATT_BLOCK, LANES), F32)],
        compiler_params=_cparams(("parallel", "parallel")),
    )(proj, proj, proj)


def _attn_bwd(dout, tot, proj, Bl, L, D, qc, kc, vc, *, name):
    npairs = D // LANES
    nq = L // ATT_BLOCK
    nsb = ATT_BLOCK // ATT_SUB
    scale = HEAD_DIM ** -0.5

    def kern(do_ref, tot_ref, q_ref, k_ref, v_ref, dq_ref, dk_ref, dv_ref, qm, km, kb, vb, dom, dkacc, dvacc,
             pre, pde, dqacc):
        tri = _sub_pred(lambda t, s: t > s)
        lrow = lax.broadcasted_iota(jnp.int32, (1, LANES), 1) < HEAD_DIM
        q = q_ref[...] * scale
        qm[0] = jnp.where(lrow, q, 0.0).astype(BF16)
        qm[1] = jnp.where(lrow, 0.0, q).astype(BF16)
        k = k_ref[...]
        kb[...] = k.astype(BF16)
        km[0] = jnp.where(lrow, k, 0.0).astype(BF16)
        km[1] = jnp.where(lrow, 0.0, k).astype(BF16)
        vb[...] = v_ref[...].astype(BF16)
        do = do_ref[...]
        dom[0] = jnp.where(lrow, do, 0.0).astype(BF16)
        dom[1] = jnp.where(lrow, 0.0, do).astype(BF16)
        dkacc[...] = jnp.zeros_like(dkacc)
        dvacc[...] = jnp.zeros_like(dvacc)
        mat = _mask(_sub_pred(lambda j, s: j <= s))

        def tile(q0, roff, nrows, k0, nsub, diag):
            qs, ks, rows = _att_slices(q0, roff, nrows, k0, nsub)
            for h in range(2):
                z = _dg(qm[h, qs, :], kb[ks, :], _NT)
                nl = _att_nl(z)
                lb = z - nl
                datt = _dg(dom[h, qs, :], vb[ks, :], _NT)
                tn = tot_ref[h, qs, 0:1]
                base = pre[h, rows, :] - tn
                pe = pde[h, rows, :]
                atts, dzs = [], []
                for b in range(nsub):
                    cols = slice(b * ATT_SUB, (b + 1) * ATT_SUB)
                    masked = diag and b == nsub - 1
                    x = jnp.where(tri, nl[:, cols], 0.0) if masked else nl[:, cols]
                    pin = _cumsum_mm(x, mat)
                    att = jnp.exp(lb[:, cols] + base + pin)
                    if masked:
                        att = jnp.where(tri, att, 0.0)
                    dE = att * datt[:, cols]
                    pde_in = _cumsum_mm(dE, mat)
                    dz = dE - jnp.exp(lb[:, cols]) * (pe + pde_in)
                    if masked:
                        dz = jnp.where(tri, dz, 0.0)
                    atts.append(att.astype(BF16))
                    dzs.append(dz.astype(BF16))
                    base = base + pin[:, ATT_SUB - 1:ATT_SUB]
                    pe = pe + pde_in[:, ATT_SUB - 1:ATT_SUB]
                pre[h, rows, :] = base + tn
                pde[h, rows, :] = pe
                attb = jnp.concatenate(atts, axis=1) if nsub > 1 else atts[0]
                dzb = jnp.concatenate(dzs, axis=1) if nsub > 1 else dzs[0]
                dqacc[rows, :] += _dot(dzb, km[h, ks, :])
                dkacc[ks, :] += _dg(dzb, qm[h, qs, :], _TN)
                dvacc[ks, :] += _dg(attb, dom[h, qs, :], _TN)

        def qblock(qi, _):
            q0 = qi * ATT_BLOCK
            qs = pl.ds(pl.multiple_of(q0, ATT_BLOCK), ATT_BLOCK)
            pre[...] = jnp.zeros_like(pre)
            pde[...] = jnp.zeros_like(pde)
            dqacc[...] = jnp.zeros_like(dqacc)

            def kblock(kj, _):
                tile(q0, 0, ATT_BLOCK, kj * ATT_BLOCK, nsb, False)
                return 0

            lax.fori_loop(0, qi, kblock, 0)
            for rb in range(nsb):
                tile(q0, rb * ATT_SUB, ATT_SUB, q0, rb + 1, True)
            dq_ref[qs, :] = (dqacc[...] * scale).astype(dq_ref.dtype)
            return 0

        lax.fori_loop(0, nq, qblock, 0)
        dk_ref[...] = dkacc[...].astype(dk_ref.dtype)
        dv_ref[...] = dvacc[...].astype(dv_ref.dtype)

    spec = lambda col: pl.BlockSpec((L, LANES), lambda bi, p: (bi, col // LANES + p))
    ospec = pl.BlockSpec((L, LANES), lambda bi, p: (bi, p))
    T = Bl * L
    return pl.pallas_call(
        kern, name=name,
        out_shape=[jax.ShapeDtypeStruct((T, D), BF16)] * 3,
        grid=(Bl, npairs),
        in_specs=[ospec, pl.BlockSpec((None, None, 2, L, LANES), lambda bi, p: (bi, p, 0, 0, 0)),
                  spec(qc), spec(kc), spec(vc)],
        out_specs=[ospec, ospec, ospec],
        scratch_shapes=[pltpu.VMEM((2, L, LANES), BF16), pltpu.VMEM((2, L, LANES), BF16),
                        pltpu.VMEM((L, LANES), BF16), pltpu.VMEM((L, LANES), BF16),
                        pltpu.VMEM((2, L, LANES), BF16), pltpu.VMEM((L, LANES), F32),
                        pltpu.VMEM((L, LANES), F32), pltpu.VMEM((2, ATT_BLOCK, 1), F32),
                        pltpu.VMEM((2, ATT_BLOCK, 1), F32), pltpu.VMEM((ATT_BLOCK, LANES), F32)],
        compiler_params=_cparams(("parallel", "parallel")),
    )(dout, tot, proj, proj, proj)


def _mix_fwd(y, xbc, proj, yatt, dskip_e, g_ssd, g_att, D, *, name):
    T = y.shape[0]
    tm = _blk(T, 256)

    def kern(y_ref, xs_ref, z_ref, ya_ref, dsk_ref, gs_ref, ga_ref, o_ref):
        z = z_ref[...]
        u = (y_ref[...] + xs_ref[...] * dsk_ref[...]) * (z * _sigmoid(z))
        rs = lax.rsqrt(jnp.mean(u * u, axis=-1, keepdims=True) + EPS)
        o_ref[:, :D] = (u * rs * gs_ref[...]).astype(o_ref.dtype)
        ya = ya_ref[...]
        ra = lax.rsqrt(jnp.mean(ya * ya, axis=-1, keepdims=True) + EPS)
        o_ref[:, D:] = (ya * ra * ga_ref[...]).astype(o_ref.dtype)

    row = pl.BlockSpec((tm, D), lambda i: (i, 0))
    vec = pl.BlockSpec((1, D), lambda i: (0, 0))
    return pl.pallas_call(
        kern, name=name, out_shape=jax.ShapeDtypeStruct((T, 2 * D), BF16), grid=(T // tm,),
        in_specs=[row, row, row, row, vec, vec, vec],
        out_specs=pl.BlockSpec((tm, 2 * D), lambda i: (i, 0)),
        compiler_params=_cparams(("parallel",)),
    )(y, xbc, proj, yatt, dskip_e, g_ssd.reshape(1, D), g_att.reshape(1, D))


def _mix_bwd(dmix, y, xbc, proj, yatt, dskip_e, g_ssd, g_att, D, *, name):
    T = y.shape[0]
    tm = _blk(T, 256)

    def kern(d_ref, y_ref, xs_ref, z_ref, ya_ref, dsk_ref, gs_ref, ga_ref,
             dz_ref, dy_ref, dya_ref, dgs_ref, dga_ref):
        @pl.when(pl.program_id(0) == 0)
        def _():
            dgs_ref[...] = jnp.zeros_like(dgs_ref)
            dga_ref[...] = jnp.zeros_like(dga_ref)

        z = z_ref[...]
        sig = _sigmoid(z)
        sz = z * sig
        ytot = y_ref[...] + xs_ref[...] * dsk_ref[...]
        u = ytot * sz
        rs = lax.rsqrt(jnp.mean(u * u, axis=-1, keepdims=True) + EPS)
        un = u * rs
        do = d_ref[:, :D]
        dog = do * gs_ref[...]
        du = rs * (dog - un * jnp.mean(dog * un, axis=-1, keepdims=True))
        dgs_ref[...] += jnp.sum(do * un, axis=0, keepdims=True)
        dy_ref[...] = du * sz
        dz_ref[...] = (du * ytot * (sig * (1.0 + z * (1.0 - sig)))).astype(dz_ref.dtype)
        ya = ya_ref[...]
        ra = lax.rsqrt(jnp.mean(ya * ya, axis=-1, keepdims=True) + EPS)
        yan = ya * ra
        da = d_ref[:, D:]
        dag = da * ga_ref[...]
        dya_ref[...] = ra * (dag - yan * jnp.mean(dag * yan, axis=-1, keepdims=True))
        dga_ref[...] += jnp.sum(da * yan, axis=0, keepdims=True)

    row = pl.BlockSpec((tm, D), lambda i: (i, 0))
    vec = pl.BlockSpec((1, D), lambda i: (0, 0))
    return pl.pallas_call(
        kern, name=name,
        out_shape=[jax.ShapeDtypeStruct((T, D), BF16), jax.ShapeDtypeStruct((T, D), F32),
                   jax.ShapeDtypeStruct((T, D), F32), jax.ShapeDtypeStruct((1, D), F32),
                   jax.ShapeDtypeStruct((1, D), F32)],
        grid=(T // tm,),
        in_specs=[pl.BlockSpec((tm, 2 * D), lambda i: (i, 0)), row, row, row, row, vec, vec, vec],
        out_specs=[row, row, row, vec, vec],
        compiler_params=_cparams(("arbitrary",)),
    )(dmix, y, xbc, proj, yatt, dskip_e, g_ssd.reshape(1, D), g_att.reshape(1, D))


def _head_sum_mat(D):
    i = lax.broadcasted_iota(jnp.int32, (D, LANES), 0)
    c = lax.broadcasted_iota(jnp.int32, (D, LANES), 1)
    return _mask((i >= c * HEAD_DIM) & (i < (c + 1) * HEAD_DIM))


def _collapse_heads(x, *, name):
    R, D = x.shape
    tm = _blk(R, 256)

    def kern(x_ref, o_ref):
        o_ref[...] = _dot_exact_rhs(x_ref[...], _head_sum_mat(D))

    return pl.pallas_call(
        kern, name=name, out_shape=jax.ShapeDtypeStruct((R, LANES), F32), grid=(R // tm,),
        in_specs=[pl.BlockSpec((tm, D), lambda i: (i, 0))],
        out_specs=pl.BlockSpec((tm, LANES), lambda i: (i, 0)),
        compiler_params=_cparams(("parallel",)),
    )(x)


def _adam_math(w, g, m, v):
    m = ADAM_B1 * m + (1.0 - ADAM_B1) * g
    v = ADAM_B2 * v + (1.0 - ADAM_B2) * (g * g)
    m_hat = m / (1.0 - ADAM_B1 ** ADAM_STEP)
    v_hat = v / (1.0 - ADAM_B2 ** ADAM_STEP)
    delta = -ADAM_LR * (m_hat / (jnp.sqrt(v_hat) + ADAM_EPS) + ADAM_WD * w)
    return delta, m, v


def _adamw_sharded(pieces, w, m, v, *, name):
    R, C = w.shape
    tr = _blk(R, 256) if R % 8 == 0 else R

    def kern(p_ref, w_ref, m_ref, v_ref, g_ref, d_ref, nm_ref, nv_ref):
        g = p_ref[0].astype(F32)
        for j in range(1, N_DEV):
            g = g + p_ref[j].astype(F32)
        d, nm, nv = _adam_math(w_ref[...], g, m_ref[...], v_ref[...])
        g_ref[...] = g
        d_ref[...] = d
        nm_ref[...] = nm
        nv_ref[...] = nv

    row = pl.BlockSpec((tr, C), lambda i: (i, 0))
    return pl.pallas_call(
        kern, name=name, out_shape=[jax.ShapeDtypeStruct((R, C), F32)] * 4, grid=(R // tr,),
        in_specs=[pl.BlockSpec((N_DEV, tr, C), lambda i: (0, i, 0)), row, row, row],
        out_specs=[row] * 4,
        compiler_params=_cparams(("parallel",)),
    )(pieces, w, m, v)


def _small_update(parts, wd, md, vd, we, me, ve, nd, ne, D, *, name):
    def kern(p_ref, wd_ref, md_ref, vd_ref, we_ref, me_ref, ve_ref,
             gd_ref, dd_ref, nmd_ref, nvd_ref, ge_ref, de_ref, nme_ref, nve_ref, loss_ref):
        s = p_ref[0]
        for j in range(1, N_DEV):
            s = s + p_ref[j]
        gd = s[:nd]
        d, nm, nv = _adam_math(wd_ref[...], gd, md_ref[...], vd_ref[...])
        gd_ref[...] = gd
        dd_ref[...] = d
        nmd_ref[...] = nm
        nvd_ref[...] = nv
        ge = _dot_exact_rhs(s[nd:nd + ne], _head_sum_mat(D))
        d, nm, nv = _adam_math(we_ref[...], ge, me_ref[...], ve_ref[...])
        ge_ref[...] = ge
        de_ref[...] = d
        nme_ref[...] = nm
        nve_ref[...] = nv
        tot = jnp.sum(jnp.sum(s[nd + ne:], axis=1, keepdims=True), axis=0, keepdims=True)
        loss_ref[...] = jnp.broadcast_to(tot, loss_ref.shape)

    vm = pl.BlockSpec(memory_space=pltpu.VMEM)
    return pl.pallas_call(
        kern, name=name,
        out_shape=[jax.ShapeDtypeStruct((nd, D), F32)] * 4 + [jax.ShapeDtypeStruct((ne, LANES), F32)] * 4
        + [jax.ShapeDtypeStruct((8, LANES), F32)],
        in_specs=[vm] * 7, out_specs=[vm] * 9,
        compiler_params=pltpu.CompilerParams(vmem_limit_bytes=VMEM_LIMIT),
    )(parts, wd, md, vd, we, me, ve)


def _dev_index(p):
    return 4 * p[0] + 2 * p[1] + p[2]


def _all_gather(arrs, *, name):
    n = len(arrs)

    def body(*refs):
        xs, outs = refs[:n], refs[n:2 * n]
        send, recv, loc = refs[2 * n:]
        x, y, c = lax.axis_index("x"), lax.axis_index("y"), lax.axis_index("c")
        me, sib = (x, y, c), (x, y, 1 - c)
        chips = [(1 - x, y), (x, 1 - y), (1 - x, 1 - y)]

        def cp(k, s, block, to, src=None):
            dst = outs[k].at[_dev_index(block)]
            return pltpu.make_async_remote_copy(
                src_ref=dst if src is None else src, dst_ref=dst,
                send_sem=send.at[7 * k + s], recv_sem=recv.at[7 * k + s],
                device_id=to, device_id_type=pl.DeviceIdType.MESH)

        mine = [pltpu.make_async_copy(xs[k], outs[k].at[_dev_index(me)], loc.at[k]) for k in range(n)]
        for m in mine:
            m.start()
        first = []
        for k in range(n):
            first.append(cp(k, 0, me, sib, src=xs[k]))
            for j, ch in enumerate(chips):
                first.append(cp(k, 1 + j, me, (*ch, c), src=xs[k]))
        for f in first:
            f.start()
        passed = []
        for j, ch in enumerate(chips):
            for k in range(n):
                cp(k, 1 + j, (*ch, c), me).wait_recv()
                p = cp(k, 4 + j, (*ch, c), sib)
                p.start()
                passed.append(p)
        for k in range(n):
            cp(k, 0, sib, me).wait_recv()
            for j, ch in enumerate(chips):
                cp(k, 4 + j, (*ch, 1 - c), me).wait_recv()
        for f in first + passed:
            f.wait_send()
        for m in mine:
            m.wait()

    hbm = pl.BlockSpec(memory_space=pl.ANY)
    return pl.pallas_call(
        body, name=name,
        out_shape=[jax.ShapeDtypeStruct((N_DEV,) + a.shape, a.dtype) for a in arrs],
        in_specs=[hbm] * n, out_specs=[hbm] * n,
        scratch_shapes=[pltpu.SemaphoreType.DMA((7 * n,)), pltpu.SemaphoreType.DMA((7 * n,)),
                        pltpu.SemaphoreType.DMA((n,))],
    )(*arrs)


_RELS = [(fx, fy, fc) for fx in (0, 1) for fy in (0, 1) for fc in (0, 1) if fx or fy or fc]


def _direct_ops(src_of, dst_of, n_in):
    def make(ins, outs, send, recv, loc):
        xs, n = ins[:n_in], len(outs)
        x, y, c = lax.axis_index("x"), lax.axis_index("y"), lax.axis_index("c")
        me = _dev_index((x, y, c))
        peers = [(1 - x if r[0] else x, 1 - y if r[1] else y, 1 - c if r[2] else c) for r in _RELS]

        def remote(k, s, src, dst):
            return pltpu.make_async_remote_copy(
                src_ref=src, dst_ref=dst, send_sem=send.at[7 * k + s], recv_sem=recv.at[7 * k + s],
                device_id=peers[s], device_id_type=pl.DeviceIdType.MESH)

        def local(k):
            return pltpu.make_async_copy(src_of(xs, k, me, me), dst_of(outs, k, me), loc.at[k])

        def sends():
            return [remote(k, s, src_of(xs, k, me, _dev_index(p)), dst_of(outs, k, me))
                    for k in range(n) for s, p in enumerate(peers)]

        def start():
            for k in range(n):
                local(k).start()
            for cp in sends():
                cp.start()

        def finish():
            for k in range(n):
                for s, p in enumerate(peers):
                    slot = dst_of(outs, k, _dev_index(p))
                    remote(k, s, slot, slot).wait_recv()
            for cp in sends():
                cp.wait_send()
            for k in range(n):
                local(k).wait()

        return start, finish

    return make


def _gather_comm(arrs):
    make = _direct_ops(lambda xs, k, me, p: xs[k], lambda outs, k, j: outs[k].at[j], len(arrs))
    return make, list(arrs), [jax.ShapeDtypeStruct((N_DEV,) + a.shape, a.dtype) for a in arrs], {}


def _exchange_comm(pieces, bufs, layer):
    n = len(pieces)
    make = _direct_ops(lambda xs, k, me, p: xs[k].at[p], lambda outs, k, j: outs[k].at[j, layer], n)
    return (make, list(pieces) + list(bufs), [jax.ShapeDtypeStruct(b.shape, b.dtype) for b in bufs],
            {n + k: k for k in range(n)})


def _comm_call(comm, *, name):
    make, ins, out_shapes, alias = comm
    n_ci, n_co = len(ins), len(out_shapes)

    def body(*refs):
        start, finish = make(refs[:n_ci], refs[n_ci:n_ci + n_co], *refs[n_ci + n_co:])
        start()
        finish()

    hbm = pl.BlockSpec(memory_space=pl.ANY)
    return pl.pallas_call(
        body, name=name, out_shape=list(out_shapes), in_specs=[hbm] * n_ci, out_specs=[hbm] * n_co,
        scratch_shapes=[pltpu.SemaphoreType.DMA((7 * n_co,)), pltpu.SemaphoreType.DMA((7 * n_co,)),
                        pltpu.SemaphoreType.DMA((n_co,))],
        input_output_aliases=alias,
    )(*ins)


def _layer_fwd(x, P, dims, l, nxt):
    Bl, L, D = dims
    Dc = D + 2 * SSM_GROUPS * SSM_STATE
    qc, kc, vc, dtc = D + Dc, 2 * D + Dc, 3 * D + Dc, 4 * D + Dc
    gat = (lambda key: _gather_comm([nxt[key]])) if nxt is not None else (lambda key: None)
    h = _rmsnorm_fwd(x, P["norm_mix_g"], name=f"norm_mix_fwd_{l}")
    (proj,), g_in = _mm(h, P["w_in"], out_dtypes=[F32], name=f"in_proj_{l}", comm=gat("w_in"))
    xbc = _conv_fwd(proj, P["conv_w"], P["conv_b"], Bl, L, D, Dc, name=f"conv_fwd_{l}")
    y, states = _ssd_fwd(xbc, proj, P["dt_bias_e"], P["a_log_e"], Bl, L, D, dtc, name=f"ssd_fwd_{l}")
    yatt, tot = _attn_fwd(proj, Bl, L, D, qc, kc, vc, name=f"attn_fwd_{l}")
    ymix = _mix_fwd(y, xbc, proj, yatt, P["d_skip_e"], P["ssd_norm_g"], P["att_norm_g"], D, name=f"mix_fwd_{l}")
    (x1,), g_out = _mm(ymix, P["w_out"], out_dtypes=[F32], epilogue=lambda r, res: (r + res,), extras=(x,),
                       name=f"out_proj_{l}", comm=gat("w_out"))
    h2 = _rmsnorm_fwd(x1, P["norm_mlp_g"], name=f"norm_mlp_fwd_{l}")

    def relu2(r):
        a = jnp.maximum(r, 0.0)
        return r, a * a

    (u, act), g_up = _mm(h2, P["w_up"], out_dtypes=[F32, BF16], epilogue=relu2, name=f"mlp_up_{l}",
                         comm=gat("w_up"))
    (x2,), g_down = _mm(act, P["w_down"], out_dtypes=[F32], epilogue=lambda r, res: (r + res,), extras=(x1,),
                        name=f"mlp_down_{l}", comm=gat("w_down"))
    saved = dict(x=x, h=h, proj=proj, xbc=xbc, y=y, states=states, yatt=yatt, tot=tot, ymix=ymix,
                 x1=x1, h2=h2, u=u, act=act)
    gathered = dict(w_in=g_in[0], w_out=g_out[0], w_up=g_up[0], w_down=g_down[0]) if nxt is not None else None
    return x2, saved, gathered


def _layer_bwd(dx2, S, P, dims, l, pend, bufs):
    Bl, L, D = dims
    Dc = D + 2 * SSM_GROUPS * SSM_STATE
    qc, kc, vc, dtc = D + Dc, 2 * D + Dc, 3 * D + Dc, 4 * D + Dc
    G = {}

    def exch(keys):
        if pend is None:
            return None
        return _exchange_comm([pend[k] for k in keys], [bufs[k] for k in keys], l + 1)

    def took(keys, outs):
        if pend is not None:
            bufs.update(zip(keys, outs))

    (du,), o = _mm(dx2, P["w_down_t"], out_dtypes=[BF16], extras=(S["u"],),
                   epilogue=lambda r, u: (r * (2.0 * jnp.maximum(u, 0.0)),), name=f"mlp_down_bwd_{l}",
                   comm=exch(["w_in"]))
    took(["w_in"], o)
    (G["w_down"],), o = _mm(S["act"], dx2, ta=True, out_dtypes=[F32], name=f"mlp_down_dw_{l}",
                            comm=exch(["w_up"]))
    took(["w_up"], o)
    (dh2,), o = _mm(du, P["w_up_t"], out_dtypes=[F32], name=f"mlp_up_bwd_{l}", comm=exch(["w_down"]))
    took(["w_down"], o)
    (G["w_up"],), o = _mm(S["h2"], du, ta=True, out_dtypes=[F32], name=f"mlp_up_dw_{l}",
                          comm=exch(["w_out", "conv_w"]))
    took(["w_out", "conv_w"], o)
    dx1, G["norm_mlp_g"] = _rmsnorm_bwd(dh2, S["x1"], P["norm_mlp_g"], dx2, name=f"norm_mlp_bwd_{l}")
    (dmix,), _ = _mm(dx1, P["w_out_t"], out_dtypes=[F32], name=f"out_proj_bwd_{l}")
    (G["w_out"],), _ = _mm(S["ymix"], dx1, ta=True, out_dtypes=[F32], name=f"out_proj_dw_{l}")
    dz, dy, dyatt, G["ssd_norm_g"], G["att_norm_g"] = _mix_bwd(
        dmix, S["y"], S["xbc"], S["proj"], S["yatt"], P["d_skip_e"], P["ssd_norm_g"], P["att_norm_g"], D,
        name=f"mix_bwd_{l}")
    dq, dk, dv = _attn_bwd(dyatt, S["tot"], S["proj"], Bl, L, D, qc, kc, vc, name=f"attn_bwd_{l}")
    dxs, dB, dC, ddt, G["dt_bias_e"], G["a_log_e"], G["d_skip_e"] = _ssd_bwd(
        dy, S["states"], S["xbc"], S["proj"], P["dt_bias_e"], P["a_log_e"], P["d_skip_e"],
        Bl, L, D, dtc, name=f"ssd_bwd_{l}")
    dact = jnp.concatenate([dxs, dB, dC], axis=1)
    dxbc, G["conv_w"], G["conv_b"] = _conv_bwd(dact, S["proj"], P["conv_w"], P["conv_b"], Bl, L, D, Dc,
                                               name=f"conv_bwd_{l}")
    dproj = jnp.concatenate([dz, dxbc, dq, dk, dv, ddt], axis=1)
    (dh,), _ = _mm(dproj, P["w_in_t"], out_dtypes=[F32], name=f"in_proj_bwd_{l}")
    (G["w_in"],), _ = _mm(S["h"], dproj, ta=True, out_dtypes=[F32], name=f"in_proj_dw_{l}")
    dx, G["norm_mix_g"] = _rmsnorm_bwd(dh, S["x"], P["norm_mix_g"], dx1, name=f"norm_mix_bwd_{l}")
    return dx, G, bufs


def _pad_rows(a, rows):
    return jnp.concatenate([a, jnp.zeros((rows - a.shape[0],) + a.shape[1:], a.dtype)], axis=0)


def kernel(x, norm_mix_g, w_in, conv_w, conv_b, dt_bias, a_log, d_skip, ssd_norm_g, att_norm_g, w_out, norm_mlp_g, w_up, w_down, final_norm_g, loss_target, m_norm_mix_g, m_w_in, m_conv_w, m_conv_b, m_dt_bias, m_a_log, m_d_skip, m_ssd_norm_g, m_att_norm_g, m_w_out, m_norm_mlp_g, m_w_up, m_w_down, m_final_norm_g, v_norm_mix_g, v_w_in, v_conv_w, v_conv_b, v_dt_bias, v_a_log, v_d_skip, v_ssd_norm_g, v_att_norm_g, v_w_out, v_norm_mlp_g, v_w_up, v_w_down, v_final_norm_g):
    Bl, L, D = x.shape
    T = Bl * L
    depth = w_in.shape[0]
    H = D // HEAD_DIM
    Dc = D + 2 * SSM_GROUPS * SSM_STATE
    Dff = w_up.shape[2] * N_DEV
    dims = (Bl, L, D)

    bf = lambda a: a.astype(BF16)
    shards = [dict(w_in=bf(w_in[l]), w_out=bf(w_out[l]), w_up=bf(w_up[l]), w_down=bf(w_down[l]))
              for l in range(depth)]
    g0 = _all_gather([shards[0][k] for k in ("w_in", "w_out", "w_up", "w_down")] + [conv_w],
                     name="gather_weights_0")
    gathered = dict(zip(("w_in", "w_out", "w_up", "w_down"), g0))
    Cw = g0[4].transpose(1, 2, 0, 3).reshape(depth, CONV_WIDTH, Dc)
    rep = lambda a: jnp.repeat(a, HEAD_DIM, axis=1)
    dt_bias_e, a_log_e, d_skip_e = rep(dt_bias), rep(a_log), rep(d_skip)
    o = [0, D, D + Dc, D + Dc + H, 2 * D + Dc + H, 3 * D + Dc + H, 4 * D + Dc + H]

    def layer_params(g, l):
        W_in = g["w_in"].transpose(1, 0, 2).reshape(D, -1)
        wz, wxbc, wdt, wq, wk, wv = [W_in[:, o[i]:o[i + 1]] for i in range(6)]
        W_pack = jnp.concatenate([wz, wxbc, wq, wk, wv, jnp.repeat(wdt, HEAD_DIM, axis=1)], axis=1)
        W_out = g["w_out"].reshape(2 * D, D)
        W_up = g["w_up"].transpose(1, 0, 2).reshape(D, Dff)
        W_down = g["w_down"].reshape(Dff, D)
        return dict(
            norm_mix_g=norm_mix_g[l], w_in=W_pack, w_in_t=W_pack.T, conv_w=Cw[l], conv_b=conv_b[l],
            dt_bias_e=dt_bias_e[l:l + 1], a_log_e=a_log_e[l:l + 1], d_skip_e=d_skip_e[l:l + 1],
            ssd_norm_g=ssd_norm_g[l], att_norm_g=att_norm_g[l], w_out=W_out, w_out_t=W_out.T,
            norm_mlp_g=norm_mlp_g[l], w_up=W_up, w_up_t=W_up.T, w_down=W_down, w_down_t=W_down.T)

    xa = x.reshape(T, D)
    saved, params = [], []
    for l in range(depth):
        params.append(layer_params(gathered, l))
        xa, s, gathered = _layer_fwd(xa, params[l], dims, l, shards[l + 1] if l + 1 < depth else None)
        saved.append(s)
    dx, g_final, loss_part = _loss_head(xa, final_norm_g, loss_target.reshape(T, D), name="loss_head")

    n_in = w_in.shape[2]

    def pieces_of(G, l):
        gW = G["w_in"]
        gdt = _collapse_heads(gW[:, 4 * D + Dc:], name=f"collapse_w_dt_{l}")[:, :H]
        gW_in = jnp.concatenate([gW[:, :D + Dc], gdt, gW[:, D + Dc:4 * D + Dc]], axis=1)
        return dict(
            w_in=bf(gW_in.reshape(D, N_DEV, n_in).transpose(1, 0, 2)),
            w_out=bf(G["w_out"].reshape(N_DEV, 2 * D // N_DEV, D)),
            w_up=bf(G["w_up"].reshape(D, N_DEV, Dff // N_DEV).transpose(1, 0, 2)),
            w_down=bf(G["w_down"].reshape(N_DEV, Dff // N_DEV, D)),
            conv_w=bf(G["conv_w"].reshape(CONV_WIDTH, N_DEV, Dc // N_DEV).transpose(1, 0, 2)))

    keys = ("w_in", "w_out", "w_up", "w_down", "conv_w")
    bufs = {k: lax.empty((N_DEV, depth) + w.shape[1:], BF16)
            for k, w in zip(keys, (w_in, w_out, w_up, w_down, conv_w))}
    grads = [None] * depth
    pend = None
    for l in reversed(range(depth)):
        dx, grads[l], bufs = _layer_bwd(dx, saved[l], params[l], dims, l, pend, bufs)
        pend = pieces_of(grads[l], l)
    grad_x = dx.reshape(Bl, L, D)
    last = _comm_call(_exchange_comm([pend[k] for k in keys], [bufs[k] for k in keys], 0), name="exchange_grads_0")
    r_in, r_out, r_up, r_down, r_conv = last
    st = lambda k: jnp.stack([grads[l][k] for l in range(depth)])

    def sharded(pieces, w, m, v, name):
        shp = w.shape
        C = shp[-1]
        res = _adamw_sharded(pieces.reshape(N_DEV, -1, C), w.reshape(-1, C), m.reshape(-1, C),
                             v.reshape(-1, C), name=name)
        return [a.reshape(shp) for a in res]

    u_in = sharded(r_in, w_in, m_w_in, v_w_in, "adamw_w_in")
    u_out = sharded(r_out, w_out, m_w_out, v_w_out, "adamw_w_out")
    u_up = sharded(r_up, w_up, m_w_up, v_w_up, "adamw_w_up")
    u_down = sharded(r_down, w_down, m_w_down, v_w_down, "adamw_w_down")
    u_conv = sharded(r_conv, conv_w, m_conv_w, v_conv_w, "adamw_conv_w")

    rows = lambda a: a.reshape(-1, D)
    direct_names = ["norm_mix_g", "ssd_norm_g", "att_norm_g", "norm_mlp_g", "conv_b"]
    direct_g = [rows(st(k)) for k in direct_names] + [g_final]
    exp_names = ["dt_bias_e", "a_log_e", "d_skip_e"]
    exp_g = [rows(st(k)) for k in exp_names]
    n_direct = sum(a.shape[0] for a in direct_g)
    nd = -(-n_direct // 8) * 8
    ne = -(-3 * depth // 8) * 8
    part = jnp.concatenate([_pad_rows(jnp.concatenate(direct_g, axis=0), nd),
                            _pad_rows(jnp.concatenate(exp_g, axis=0), ne),
                            _pad_rows(loss_part, 8)], axis=0)
    (parts,) = _all_gather([part], name="gather_small_grads")

    def direct_pack(ws):
        return _pad_rows(jnp.concatenate([rows(a) for a in ws], axis=0), nd)

    def exp_pack(ws):
        a = jnp.concatenate(ws, axis=0)
        a = jnp.concatenate([a, jnp.zeros((a.shape[0], LANES - H), F32)], axis=1)
        return _pad_rows(a, ne)

    res = _small_update(
        parts,
        direct_pack([norm_mix_g, ssd_norm_g, att_norm_g, norm_mlp_g, conv_b, final_norm_g]),
        direct_pack([m_norm_mix_g, m_ssd_norm_g, m_att_norm_g, m_norm_mlp_g, m_conv_b, m_final_norm_g]),
        direct_pack([v_norm_mix_g, v_ssd_norm_g, v_att_norm_g, v_norm_mlp_g, v_conv_b, v_final_norm_g]),
        exp_pack([dt_bias, a_log, d_skip]), exp_pack([m_dt_bias, m_a_log, m_d_skip]),
        exp_pack([v_dt_bias, v_a_log, v_d_skip]), nd, ne, D, name="small_update")
    loss = res[8][0, 0]

    def unpack(kind):
        d, e = res[kind], res[4 + kind]
        out = {}
        r0 = 0
        for nm, shp in [("norm_mix_g", (depth, D)), ("ssd_norm_g", (depth, D)), ("att_norm_g", (depth, D)),
                        ("norm_mlp_g", (depth, D)), ("conv_b", (depth, Dc)), ("final_norm_g", (D,))]:
            n = 1
            for s_ in shp:
                n *= s_
            out[nm] = d[r0:r0 + n // D].reshape(shp)
            r0 += n // D
        for i, nm in enumerate(["dt_bias", "a_log", "d_skip"]):
            out[nm] = e[i * depth:(i + 1) * depth, :H]
        return out

    names = ["norm_mix_g", "w_in", "conv_w", "conv_b", "dt_bias", "a_log", "d_skip", "ssd_norm_g",
             "att_norm_g", "w_out", "norm_mlp_g", "w_up", "w_down", "final_norm_g"]
    big = {"w_in": u_in, "w_out": u_out, "w_up": u_up, "w_down": u_down, "conv_w": u_conv}
    outs = [loss, grad_x]
    for kind in range(4):
        small = unpack(kind)
        for nm in names:
            outs.append(big[nm][kind] if nm in big else small[nm])
    return tuple(outs)
```

```python
import functools

import jax
import jax.numpy as jnp
from jax import lax
from jax.experimental import pallas as pl
from jax.experimental.pallas import tpu as pltpu

F32 = jnp.float32
BF16 = jnp.bfloat16

HEAD_DIM = 64
SSM_STATE = 128
SSM_GROUPS = 4
CONV_WIDTH = 4
CHUNK = 128
ATT_BLOCK = 512
ATT_SUB = 256
SSD_UNROLL = 8
EPS = 1e-5
LANES = 128
N_DEV = 8
VMEM_LIMIT = 56 * 1024 * 1024
MM_VMEM_BUDGET = 44 * 1024 * 1024

ADAM_LR = 0.001
ADAM_B1 = 0.9
ADAM_B2 = 0.999
ADAM_EPS = 1e-08
ADAM_WD = 0.01
ADAM_STEP = 10

_NT =(((1,), (1,)), ((), ()))
_TN = (((0,), (0,)), ((), ()))


def _blk(n, pref):
    if n <= pref:
        return n
    b = (pref // LANES) * LANES
    while b >= LANES:
        if n % b == 0:
            return b
        b -= LANES
    return n


def _cparams(sem):
    return pltpu.CompilerParams(dimension_semantics=sem, vmem_limit_bytes=VMEM_LIMIT)


def _dot(a, b):
    return jnp.dot(a, b, preferred_element_type=F32)


def _dg(a, b, dims):
    return lax.dot_general(a, b, dims, preferred_element_type=F32)


def _split3(x):
    h = x.astype(BF16)
    r = x - h.astype(F32)
    m = r.astype(BF16)
    l = (r - m.astype(F32)).astype(BF16)
    return h, m, l


def _dot_exact_rhs(x, c):
    h, m, l = _split3(x)
    return _dot(h, c) + _dot(m, c) + _dot(l, c)


def _dot_exact_lhs(c, x, dims):
    h, m, l = _split3(x)
    return _dg(c, h, dims) + _dg(c, m, dims) + _dg(c, l, dims)


def _dot_exact_lhs_stacked(c, x):
    return _dot(jnp.concatenate([c, c, c], axis=1), jnp.concatenate(_split3(x), axis=0))


def _mask(cond):
    return jnp.where(cond, 1.0, 0.0).astype(BF16)


def _sigmoid(x):
    return 1.0 / (1.0 + jnp.exp(-x))


def _softplus(x):
    return jnp.maximum(x, 0.0) + jnp.log(1.0 + jnp.exp(-jnp.abs(x)))


def _mm_tiles(M, N, K, a_bytes, b_bytes, mn_bytes, tm, tn, tk):
    tn = _blk(N, tn)
    for tm_try, tk_try in ((2 * tm, tk), (2 * tm, tk // 2), (tm, tk), (tm, tk // 2), (tm // 2, tk // 2)):
        bm, bk = _blk(M, tm_try), _blk(K, tk_try)
        need = 2 * (bm * bk * a_bytes + bk * tn * b_bytes + bm * tn * mn_bytes) + bm * tn * 4
        if need <= MM_VMEM_BUDGET:
            return bm, tn, bk
    return _blk(M, tm // 2), tn, _blk(K, tk // 4)


def _mm(a, b, *, name, out_dtypes, ta=False, epilogue=None, extras=(), tm=1024, tn=1024, tk=2048, comm=None):
    if ta:
        K, M = a.shape
    else:
        M, K = a.shape
    N = b.shape[1]
    assert b.shape[0] == K
    tm, tn, tk = _mm_tiles(M, N, K, a.dtype.itemsize, b.dtype.itemsize,
                           sum(e.dtype.itemsize for e in extras) + sum(jnp.dtype(d).itemsize for d in out_dtypes),
                           tm, tn, tk)
    nj, ni, nk = N // tn, M // tm, K // tk
    n_ex, n_out = len(extras), len(out_dtypes)
    c_make, c_ins, c_outs, c_alias = comm if comm is not None else (None, [], [], {})
    n_ci, n_co = len(c_ins), len(c_outs)

    def kern(*refs):
        a_ref, b_ref = refs[0], refs[1]
        ex = refs[2:2 + n_ex]
        ci = refs[2 + n_ex:2 + n_ex + n_ci]
        o0 = 2 + n_ex + n_ci
        outs = refs[o0:o0 + n_out]
        co = refs[o0 + n_out:o0 + n_out + n_co]
        acc = refs[o0 + n_out + n_co]
        j, i, k = pl.program_id(0), pl.program_id(1), pl.program_id(2)
        if comm is not None:
            start, finish = c_make(ci, co, *refs[o0 + n_out + n_co + 1:])

            @pl.when((j == 0) & (i == 0) & (k == 0))
            def _():
                start()

        av = a_ref[...].astype(BF16)
        bv = b_ref[...].astype(BF16)
        prod = _dg(av, bv, _TN) if ta else _dot(av, bv)

        def write(r):
            vals = epilogue(r, *[e[...] for e in ex]) if epilogue is not None else (r,)
            for o, v in zip(outs, vals):
                o[...] = v.astype(o.dtype)

        if nk == 1:
            write(prod)
        else:
            @pl.when(k == 0)
            def _():
                acc[...] = prod

            @pl.when(k > 0)
            def _():
                acc[...] += prod

            @pl.when(k == nk - 1)
            def _():
                write(acc[...])

        if comm is not None:
            @pl.when((j == nj - 1) & (i == ni - 1) & (k == nk - 1))
            def _():
                finish()

    if ta:
        a_spec = pl.BlockSpec((tk, tm), lambda j, i, k: (k, i))
    else:
        a_spec = pl.BlockSpec((tm, tk), lambda j, i, k: (i, k))
    b_spec = pl.BlockSpec((tk, tn), lambda j, i, k: (k, j))
    o_spec = pl.BlockSpec((tm, tn), lambda j, i, k: (i, j))
    hbm = pl.BlockSpec(memory_space=pl.ANY)
    scratch = [pltpu.VMEM((tm, tn), F32)]
    if comm is not None:
        scratch += [pltpu.SemaphoreType.DMA((7 * n_co,)), pltpu.SemaphoreType.DMA((7 * n_co,)),
                    pltpu.SemaphoreType.DMA((n_co,))]
    res = pl.pallas_call(
        kern, name=name,
        out_shape=[jax.ShapeDtypeStruct((M, N), d) for d in out_dtypes] + list(c_outs),
        grid=(nj, ni, nk),
        in_specs=[a_spec, b_spec] + [o_spec] * n_ex + [hbm] * n_ci,
        out_specs=[o_spec] * n_out + [hbm] * n_co,
        scratch_shapes=scratch,
        input_output_aliases={2 + n_ex + ki: n_out + ko for ki, ko in c_alias.items()},
        compiler_params=_cparams(("parallel", "parallel", "arbitrary") if comm is None
                                 else ("arbitrary", "arbitrary", "arbitrary")),
    )(a, b, *extras, *c_ins)
    return res[:n_out], res[n_out:]


def _rmsnorm_fwd(x, g, *, name):
    T, D = x.shape
    tm = _blk(T, 512)

    def kern(x_ref, g_ref, h_ref):
        xv = x_ref[...]
        r = lax.rsqrt(jnp.mean(xv * xv, axis=-1, keepdims=True) + EPS)
        h_ref[...] = (xv * r * g_ref[...]).astype(h_ref.dtype)

    return pl.pallas_call(
        kern, name=name, out_shape=jax.ShapeDtypeStruct((T, D), BF16), grid=(T // tm,),
        in_specs=[pl.BlockSpec((tm, D), lambda i: (i, 0)), pl.BlockSpec((1, D), lambda i: (0, 0))],
        out_specs=pl.BlockSpec((tm, D), lambda i: (i, 0)),
        compiler_params=_cparams(("parallel",)),
    )(x, g.reshape(1, D))


def _rmsnorm_bwd(dh, x, g, dres, *, name):
    T, D = x.shape
    tm = _blk(T, 512)

    def kern(dh_ref, x_ref, g_ref, dres_ref, dx_ref, dg_ref):
        @pl.when(pl.program_id(0) == 0)
        def _():
            dg_ref[...] = jnp.zeros_like(dg_ref)

        xv = x_ref[...]
        r = lax.rsqrt(jnp.mean(xv * xv, axis=-1, keepdims=True) + EPS)
        xn = xv * r
        dy = dh_ref[...].astype(F32)
        dyg = dy * g_ref[...]
        dx = r * (dyg - xn * jnp.mean(dyg * xn, axis=-1, keepdims=True))
        dx_ref[...] = dres_ref[...] + dx
        dg_ref[...] += jnp.sum(dy * xn, axis=0, keepdims=True)

    row = pl.BlockSpec((tm, D), lambda i: (i, 0))
    vec = pl.BlockSpec((1, D), lambda i: (0, 0))
    return pl.pallas_call(
        kern, name=name,
        out_shape=[jax.ShapeDtypeStruct((T, D), F32), jax.ShapeDtypeStruct((1, D), F32)],
        grid=(T // tm,), in_specs=[row, row, vec, row], out_specs=[row, vec],
        compiler_params=_cparams(("arbitrary",)),
    )(dh, x, g.reshape(1, D), dres)


def _loss_head(x, g, target, *, name):
    T, D = x.shape
    tm = _blk(T, 512)

    def kern(x_ref, g_ref, t_ref, dx_ref, dg_ref, loss_ref):
        @pl.when(pl.program_id(0) == 0)
        def _():
            dg_ref[...] = jnp.zeros_like(dg_ref)
            loss_ref[...] = jnp.zeros_like(loss_ref)

        xv = x_ref[...]
        gv = g_ref[...]
        r = lax.rsqrt(jnp.mean(xv * xv, axis=-1, keepdims=True) + EPS)
        xn = xv * r
        err = xn * gv - t_ref[...]
        loss_ref[...] += jnp.sum(err * err, axis=0, keepdims=True) * (0.5 / D)
        dy = err * (1.0 / D)
        dyg = dy * gv
        dx_ref[...] = r * (dyg - xn * jnp.mean(dyg * xn, axis=-1, keepdims=True))
        dg_ref[...] += jnp.sum(dy * xn, axis=0, keepdims=True)

    row = pl.BlockSpec((tm, D), lambda i: (i, 0))
    vec = pl.BlockSpec((1, D), lambda i: (0, 0))
    return pl.pallas_call(
        kern, name=name,
        out_shape=[jax.ShapeDtypeStruct((T, D), F32), jax.ShapeDtypeStruct((1, D), F32),
                   jax.ShapeDtypeStruct((1, D), F32)],
        grid=(T // tm,), in_specs=[row, vec, row], out_specs=[row, vec, vec],
        compiler_params=_cparams(("arbitrary",)),
    )(x, g.reshape(1, D), target)


def _shift_down(u, s, L):
    t = lax.broadcasted_iota(jnp.int32, u.shape, 0)
    return jnp.where(t >= s, pltpu.roll(u, s, 0), 0.0)


def _shift_up(u, s, L):
    t = lax.broadcasted_iota(jnp.int32, u.shape, 0)
    return jnp.where(t < L - s, pltpu.roll(u, L - s, 0), 0.0)


def _conv_pre(u, w_ref, b_ref, L):
    pre = u * w_ref[CONV_WIDTH - 1:CONV_WIDTH, :] + b_ref[...]
    for s in range(1, CONV_WIDTH):
        pre = pre + _shift_down(u, s, L) * w_ref[CONV_WIDTH - 1 - s:CONV_WIDTH - s, :]
    return pre


def _conv_fwd(proj, w, b, Bl, L, col0, Dc, *, name):
    cb = LANES
    c0 = col0 // cb

    def kern(u_ref, w_ref, b_ref, o_ref):
        pre = _conv_pre(u_ref[...], w_ref, b_ref, L)
        o_ref[...] = pre * _sigmoid(pre)

    return pl.pallas_call(
        kern, name=name, out_shape=jax.ShapeDtypeStruct((Bl * L, Dc), F32), grid=(Bl, Dc // cb),
        in_specs=[pl.BlockSpec((L, cb), lambda bi, j: (bi, c0 + j)),
                  pl.BlockSpec((CONV_WIDTH, cb), lambda bi, j: (0, j)),
                  pl.BlockSpec((1, cb), lambda bi, j: (0, j))],
        out_specs=pl.BlockSpec((L, cb), lambda bi, j: (bi, j)),
        compiler_params=_cparams(("parallel", "parallel")),
    )(proj, w, b.reshape(1, Dc))


def _conv_bwd(dact, proj, w, b, Bl, L, col0, Dc, *, name):
    cb = LANES
    c0 = col0 // cb

    def kern(d_ref, u_ref, w_ref, b_ref, du_ref, dw_ref, db_ref):
        @pl.when(pl.program_id(1) == 0)
        def _():
            dw_ref[...] = jnp.zeros_like(dw_ref)
            db_ref[...] = jnp.zeros_like(db_ref)

        u = u_ref[...]
        us = [u] + [_shift_down(u, s, L) for s in range(1, CONV_WIDTH)]
        pre = b_ref[...] + us[0] * w_ref[CONV_WIDTH - 1:CONV_WIDTH, :]
        for s in range(1, CONV_WIDTH):
            pre = pre + us[s] * w_ref[CONV_WIDTH - 1 - s:CONV_WIDTH - s, :]
        sig = _sigmoid(pre)
        dpre = d_ref[...] * (sig * (1.0 + pre * (1.0 - sig)))
        du = dpre * w_ref[CONV_WIDTH - 1:CONV_WIDTH, :]
        dw_ref[CONV_WIDTH - 1:CONV_WIDTH, :] += jnp.sum(dpre * u, axis=0, keepdims=True)
        for s in range(1, CONV_WIDTH):
            k = CONV_WIDTH - 1 - s
            du = du + _shift_up(dpre, s, L) * w_ref[k:k + 1, :]
            dw_ref[k:k + 1, :] += jnp.sum(dpre * us[s], axis=0, keepdims=True)
        db_ref[...] += jnp.sum(dpre, axis=0, keepdims=True)
        du_ref[...] = du.astype(du_ref.dtype)

    return pl.pallas_call(
        kern, name=name,
        out_shape=[jax.ShapeDtypeStruct((Bl * L, Dc), BF16),
                   jax.ShapeDtypeStruct((CONV_WIDTH, Dc), F32), jax.ShapeDtypeStruct((1, Dc), F32)],
        grid=(Dc // cb, Bl),
        in_specs=[pl.BlockSpec((L, cb), lambda j, bi: (bi, j)),
                  pl.BlockSpec((L, cb), lambda j, bi: (bi, c0 + j)),
                  pl.BlockSpec((CONV_WIDTH, cb), lambda j, bi: (0, j)),
                  pl.BlockSpec((1, cb), lambda j, bi: (0, j))],
        out_specs=[pl.BlockSpec((L, cb), lambda j, bi: (bi, j)),
                   pl.BlockSpec((CONV_WIDTH, cb), lambda j, bi: (0, j)),
                   pl.BlockSpec((1, cb), lambda j, bi: (0, j))],
        compiler_params=_cparams(("parallel", "arbitrary")),
    )(dact, proj, w, b.reshape(1, Dc))


def _tri_consts():
    r = lax.broadcasted_iota(jnp.int32, (CHUNK, CHUNK), 0)
    c = lax.broadcasted_iota(jnp.int32, (CHUNK, CHUNK), 1)
    lane0 = c < HEAD_DIM
    return r, c, lane0


def _ssd_chunk_common(x, dtr, bias, alog):
    r, c, _ = _tri_consts()
    dt = _softplus(dtr + bias)
    a = dt * (-jnp.exp(alog))
    tril = _mask(c <= r)
    E = _dot_exact_lhs_stacked(tril, a)
    return dt, E, E.T


def _ssd_head_decay(E, ET, h):
    r, c, _ = _tri_consts()
    cm = jnp.broadcast_to(E[:, h * HEAD_DIM:h * HEAD_DIM + 1], (CHUNK, CHUNK))
    rm = jnp.broadcast_to(ET[h * HEAD_DIM:h * HEAD_DIM + 1, :], (CHUNK, CHUNK))
    return jnp.where(c <= r, jnp.exp(jnp.minimum(cm - rm, 0.0)), 0.0)


def _ssd_fwd(xbc, proj, bias_e, alog_e, Bl, L, D, dt_col0, *, name):
    nc = L // CHUNK
    npairs = D // LANES
    ppg = npairs // SSM_GROUPS
    gw = ppg * LANES
    b0 = D // LANES
    c0 = (D + SSM_GROUPS * SSM_STATE) // LANES
    d0 = dt_col0 // gw
    unroll = SSD_UNROLL if nc % SSD_UNROLL == 0 else 1

    def kern(x_ref, b_ref, c_ref, dtr_ref, bias_ref, alog_ref, y_ref, st_ref, h_scr):
        _, _, lane0 = _tri_consts()
        h_scr[...] = jnp.zeros_like(h_scr)

        lanes = [pl.ds(p * LANES, LANES) for p in range(ppg)]

        def one_chunk(ci, hs):
            rows = pl.ds(pl.multiple_of(ci * CHUNK, CHUNK), CHUNK)
            Bb = b_ref[rows, :].astype(BF16)
            Cb = c_ref[rows, :].astype(BF16)
            CB = _dg(Cb, Bb, _NT)
            new = []
            for p, ln in enumerate(lanes):
                x, hprev = x_ref[rows, ln], hs[p]
                dt, E, ET = _ssd_chunk_common(x, dtr_ref[rows, ln], bias_ref[:, ln], alog_ref[:, ln])
                xdt = x * dt
                xdtb = xdt.astype(BF16)
                ys = []
                for h in range(2):
                    Wh = (CB * _ssd_head_decay(E, ET, h)).astype(BF16)
                    ys.append(_dot(Wh, xdtb))
                y = jnp.where(lane0, ys[0], ys[1]) + _dg(Cb, hprev.astype(BF16), _NT) * jnp.exp(E)
                ds = jnp.exp(jnp.broadcast_to(E[CHUNK - 1:CHUNK, :], (CHUNK, LANES)) - E)
                states = _dg((xdt * ds).astype(BF16), Bb, _TN)
                cd = jnp.exp(jnp.broadcast_to(ET[:, CHUNK - 1:CHUNK], (LANES, SSM_STATE)))
                st_ref[ci, p] = hprev
                y_ref[rows, ln] = y
                new.append(hprev * cd + states)
            return new

        def chunks(i, carry):
            hs = [h_scr[p] for p in range(ppg)]
            for u in range(unroll):
                hs = one_chunk(i * unroll + u, hs)
            for p in range(ppg):
                h_scr[p] = hs[p]
            return carry

        lax.fori_loop(0, nc // unroll, chunks, 0)

    return pl.pallas_call(
        kern, name=name,
        out_shape=[jax.ShapeDtypeStruct((Bl * L, D), F32),
                   jax.ShapeDtypeStruct((Bl, SSM_GROUPS, nc, ppg, LANES, SSM_STATE), F32)],
        grid=(Bl, SSM_GROUPS),
        in_specs=[pl.BlockSpec((L, gw), lambda bi, g: (bi, g)),
                  pl.BlockSpec((L, SSM_STATE), lambda bi, g: (bi, b0 + g)),
                  pl.BlockSpec((L, SSM_STATE), lambda bi, g: (bi, c0 + g)),
                  pl.BlockSpec((L, gw), lambda bi, g: (bi, d0 + g)),
                  pl.BlockSpec((1, gw), lambda bi, g: (0, g)),
                  pl.BlockSpec((1, gw), lambda bi, g: (0, g))],
        out_specs=[pl.BlockSpec((L, gw), lambda bi, g: (bi, g)),
                   pl.BlockSpec((None, None, nc, ppg, LANES, SSM_STATE), lambda bi, g: (bi, g, 0, 0, 0, 0))],
        scratch_shapes=[pltpu.VMEM((ppg, LANES, SSM_STATE), F32)],
        compiler_params=_cparams(("parallel", "parallel")),
    )(xbc, xbc, xbc, proj, bias_e, alog_e)


def _ssd_bwd(dy, states, xbc, proj, bias_e, alog_e, dskip_e, Bl, L, D, dt_col0, *, name):
    nc = L // CHUNK
    npairs = D // LANES
    ppg = npairs // SSM_GROUPS
    gw = ppg * LANES
    b0 = D // LANES
    c0 = (D + SSM_GROUPS * SSM_STATE) // LANES
    d0 = dt_col0 // gw
    unroll = SSD_UNROLL if nc % SSD_UNROLL == 0 else 1

    def kern(dy_ref, st_ref, x_ref, b_ref, c_ref, dtr_ref, bias_ref, alog_ref, dsk_ref,
             dx_ref, db_ref, dc_ref, ddt_ref, gbias_ref, galog_ref, gdsk_ref, g_scr):
        r, c, lane0 = _tri_consts()
        m0 = lane0.astype(F32)
        masks = (m0, 1.0 - m0)
        triu = _mask(c >= r)
        trils = _mask(c < r)
        ones = jnp.ones((CHUNK, LANES), BF16)
        g_scr[...] = jnp.zeros_like(g_scr)

        @pl.when(pl.program_id(1) == 0)
        def _():
            gbias_ref[...] = jnp.zeros_like(gbias_ref)
            galog_ref[...] = jnp.zeros_like(galog_ref)
            gdsk_ref[...] = jnp.zeros_like(gdsk_ref)

        lanes = [pl.ds(p * LANES, LANES) for p in range(ppg)]

        def one_chunk(ci, gs):
            rows = pl.ds(pl.multiple_of(ci * CHUNK, CHUNK), CHUNK)
            Bb = b_ref[rows, :].astype(BF16)
            Cb = c_ref[rows, :].astype(BF16)
            CB = _dg(Cb, Bb, _NT)
            gsum = jnp.zeros((CHUNK, CHUNK), F32)
            dB = jnp.zeros((CHUNK, SSM_STATE), F32)
            dC = jnp.zeros((CHUNK, SSM_STATE), F32)
            ins = [(x_ref[rows, ln], dtr_ref[rows, ln], bias_ref[:, ln], alog_ref[:, ln], dy_ref[rows, ln],
                    st_ref[ci, p], gs[p], dsk_ref[:, ln]) for p, ln in enumerate(lanes)]
            outs = []
            for x, dtr, bias, alog, dyv, hprev, g, dsk in ins:
                dt, E, ET = _ssd_chunk_common(x, dtr, bias, alog)
                xdt = x * dt
                xdtb = xdt.astype(BF16)
                dyb = dyv.astype(BF16)
                hpb = hprev.astype(BF16)
                gb = g.astype(BF16)
                sd = jnp.exp(E)
                ds = jnp.exp(jnp.broadcast_to(E[CHUNK - 1:CHUNK, :], (CHUNK, LANES)) - E)
                t1 = []
                dE = dyv * (_dg(Cb, hpb, _NT) * sd)
                for h in range(2):
                    Lh = _ssd_head_decay(E, ET, h)
                    Wh = (CB * Lh).astype(BF16)
                    t1.append(_dg(Wh, dyb, _TN))
                    Gh = Lh * _dg((dyv * masks[h]).astype(BF16), xdtb, _NT)
                    gsum = gsum + Gh
                    Qh = Gh * CB
                    dE = dE + _dot_exact_rhs(Qh - Qh.T, _mask(c == h * HEAD_DIM))
                dxst = _dg(Bb, gb, _NT) * ds
                dxdt = jnp.where(lane0, t1[0], t1[1]) + dxst
                dysd = (dyv * sd).astype(BF16)
                dC = dC + _dot(dysd, hpb)
                dB = dB + _dot((xdt * ds).astype(BF16), gb)
                cd = jnp.exp(jnp.broadcast_to(ET[:, CHUNK - 1:CHUNK], (LANES, SSM_STATE)))
                gnew = g * cd + _dg(dysd, Cb, _TN)
                nn = (((1,), (0,)), ((), ()))
                da = (_dot_exact_lhs(triu, dE, nn) + _dot_exact_lhs(trils, xdt * dxst, nn)
                      + _dot_exact_lhs(ones, g * cd * hprev, _NT))
                aneg = -jnp.exp(alog)
                ddt = da * aneg + dxdt * x
                ddtr = ddt * _sigmoid(dtr + bias)
                outs.append((gnew, dxdt * dt + dyv * dsk, ddtr, jnp.sum(ddtr, axis=0, keepdims=True),
                             jnp.sum(da * dt, axis=0, keepdims=True) * aneg,
                             jnp.sum(dyv * x, axis=0, keepdims=True)))
            for p, (gnew, dx, ddtr, sb, sa, sk) in enumerate(outs):
                dx_ref[rows, lanes[p]] = dx
                ddt_ref[rows, lanes[p]] = ddtr.astype(ddt_ref.dtype)
                gbias_ref[:, lanes[p]] += sb
                galog_ref[:, lanes[p]] += sa
                gdsk_ref[:, lanes[p]] += sk
            gsb = gsum.astype(BF16)
            db_ref[rows, :] = dB + _dg(gsb, Cb, _TN)
            dc_ref[rows, :] = dC + _dot(gsb, Bb)
            return [o[0] for o in outs]

        def chunks(i, carry):
            gs = [g_scr[p] for p in range(ppg)]
            for u in range(unroll):
                gs = one_chunk(nc - 1 - (i * unroll + u), gs)
            for p in range(ppg):
                g_scr[p] = gs[p]
            return carry

        lax.fori_loop(0, nc // unroll, chunks, 0)

    T = Bl * L
    xspec = pl.BlockSpec((L, gw), lambda g, bi: (bi, g))
    nspec = pl.BlockSpec((L, SSM_STATE), lambda g, bi: (bi, g))
    vspec = pl.BlockSpec((1, gw), lambda g, bi: (0, g))
    return pl.pallas_call(
        kern, name=name,
        out_shape=[jax.ShapeDtypeStruct((T, D), F32),
                   jax.ShapeDtypeStruct((T, SSM_GROUPS * SSM_STATE), F32),
                   jax.ShapeDtypeStruct((T, SSM_GROUPS * SSM_STATE), F32),
                   jax.ShapeDtypeStruct((T, D), BF16),
                   jax.ShapeDtypeStruct((1, D), F32), jax.ShapeDtypeStruct((1, D), F32),
                   jax.ShapeDtypeStruct((1, D), F32)],
        grid=(SSM_GROUPS, Bl),
        in_specs=[xspec,
                  pl.BlockSpec((None, None, nc, ppg, LANES, SSM_STATE), lambda g, bi: (bi, g, 0, 0, 0, 0)),
                  xspec,
                  pl.BlockSpec((L, SSM_STATE), lambda g, bi: (bi, b0 + g)),
                  pl.BlockSpec((L, SSM_STATE), lambda g, bi: (bi, c0 + g)),
                  pl.BlockSpec((L, gw), lambda g, bi: (bi, d0 + g)),
                  vspec, vspec, vspec],
        out_specs=[xspec, nspec, nspec, xspec, vspec, vspec, vspec],
        scratch_shapes=[pltpu.VMEM((ppg, LANES, SSM_STATE), F32)],
        compiler_params=_cparams(("parallel", "arbitrary")),
    )(dy, states, xbc, xbc, xbc, proj, bias_e, alog_e, dskip_e)


def _att_nl(z):
    return jnp.maximum(z, 0.0) + jnp.log(1.0 + jnp.exp(-jnp.abs(z)))


def _sub_pred(pred):
    r = lax.broadcasted_iota(jnp.int32, (ATT_SUB, ATT_SUB), 0)
    c = lax.broadcasted_iota(jnp.int32, (ATT_SUB, ATT_SUB), 1)
    return pred(r, c)


def _cumsum_mm(x, tri):
    return _dot(x.astype(BF16), tri)


def _att_slices(q0, roff, nrows, k0, nsub):
    qs = pl.ds(pl.multiple_of(q0 + roff, ATT_SUB), nrows)
    ks = pl.ds(pl.multiple_of(k0, ATT_SUB), nsub * ATT_SUB)
    return qs, ks, pl.ds(roff, nrows)


def _attn_fwd(proj, Bl, L, D, qc, kc, vc, *, name):
    npairs = D // LANES
    nq = L // ATT_BLOCK
    nsb = ATT_BLOCK // ATT_SUB
    scale = HEAD_DIM ** -0.5

    def kern(q_ref, k_ref, v_ref, o_ref, tot_ref, qm, kb, vb, carry, acc):
        tri = _sub_pred(lambda t, s: t > s)
        lrow = lax.broadcasted_iota(jnp.int32, (1, LANES), 1) < HEAD_DIM
        q = q_ref[...] * scale
        qm[0] = jnp.where(lrow, q, 0.0).astype(BF16)
        qm[1] = jnp.where(lrow, 0.0, q).astype(BF16)
        kb[...] = k_ref[...].astype(BF16)
        vb[...] = v_ref[...].astype(BF16)
        mat = _mask(_sub_pred(lambda j, s: j >= s))

        def tile(q0, roff, nrows, k0, nsub, diag):
            qs, ks, rows = _att_slices(q0, roff, nrows, k0, nsub)
            for h in range(2):
                z = _dg(qm[h, qs, :], kb[ks, :], _NT)
                nl = _att_nl(z)
                cy = carry[h, rows, :]
                atts = []
                for b in reversed(range(nsub)):
                    cols = slice(b * ATT_SUB, (b + 1) * ATT_SUB)
                    masked = diag and b == nsub - 1
                    x = jnp.where(tri, nl[:, cols], 0.0) if masked else nl[:, cols]
                    inc = _cumsum_mm(x, mat)
                    a = jnp.exp(z[:, cols] - inc - cy)
                    atts.append((jnp.where(tri, a, 0.0) if masked else a).astype(BF16))
                    cy = cy + inc[:, 0:1]
                att = jnp.concatenate(atts[::-1], axis=1) if nsub > 1 else atts[0]
                acc[h, rows, :] += _dot(att, vb[ks, :])
                carry[h, rows, :] = cy

        def qblock(qi, _):
            q0 = qi * ATT_BLOCK
            qs = pl.ds(pl.multiple_of(q0, ATT_BLOCK), ATT_BLOCK)
            carry[...] = jnp.zeros_like(carry)
            acc[...] = jnp.zeros_like(acc)
            for rb in range(nsb):
                tile(q0, rb * ATT_SUB, ATT_SUB, q0, rb + 1, True)

            def kblock(j, _):
                tile(q0, 0, ATT_BLOCK, (qi - 1 - j) * ATT_BLOCK, nsb, False)
                return 0

            lax.fori_loop(0, qi, kblock, 0)
            o_ref[qs, :] = jnp.where(lrow, acc[0], acc[1])
            for h in range(2):
                tot_ref[h, qs, :] = jnp.broadcast_to(carry[h], (ATT_BLOCK, LANES))
            return 0

        lax.fori_loop(0, nq, qblock, 0)

    spec = lambda col: pl.BlockSpec((L, LANES), lambda bi, p: (bi, col // LANES + p))
    return pl.pallas_call(
        kern, name=name,
        out_shape=[jax.ShapeDtypeStruct((Bl * L, D), F32),
                   jax.ShapeDtypeStruct((Bl, npairs, 2, L, LANES), F32)],
        grid=(Bl, npairs),
        in_specs=[spec(qc), spec(kc), spec(vc)],
        out_specs=[pl.BlockSpec((L, LANES), lambda bi, p: (bi, p)),
                   pl.BlockSpec((None, None, 2, L, LANES), lambda bi, p: (bi, p, 0, 0, 0))],
        scratch_shapes=[pltpu.VMEM((2, L, LANES), BF16), pltpu.VMEM((L, LANES), BF16),
                        pltpu.VMEM((L, LANES), BF16), pltpu.VMEM((2, ATT_BLOCK, 1), F32),
                        pltpu.VMEM((2, ATT_BLOCK, LANES), F32)],
        compiler_params=_cparams(("parallel", "parallel")),
    )(proj, proj, proj)


def _attn_bwd(dout, tot, proj, Bl, L, D, qc, kc, vc, *, name):
    npairs = D // LANES
    nq = L // ATT_BLOCK
    nsb = ATT_BLOCK // ATT_SUB
    scale = HEAD_DIM ** -0.5

    def kern(do_ref, tot_ref, q_ref, k_ref, v_ref, dq_ref, dk_ref, dv_ref, qm, km, kb, vb, dom, dkacc, dvacc,
             pre, pde, dqacc):
        tri = _sub_pred(lambda t, s: t > s)
        lrow = lax.broadcasted_iota(jnp.int32, (1, LANES), 1) < HEAD_DIM
        q = q_ref[...] * scale
        qm[0] = jnp.where(lrow, q, 0.0).astype(BF16)
        qm[1] = jnp.where(lrow, 0.0, q).astype(BF16)
        k = k_ref[...]
        kb[...] = k.astype(BF16)
        km[0] = jnp.where(lrow, k, 0.0).astype(BF16)
        km[1] = jnp.where(lrow, 0.0, k).astype(BF16)
        vb[...] = v_ref[...].astype(BF16)
        do = do_ref[...]
        dom[0] = jnp.where(lrow, do, 0.0).astype(BF16)
        dom[1] = jnp.where(lrow, 0.0, do).astype(BF16)
        dkacc[...] = jnp.zeros_like(dkacc)
        dvacc[...] = jnp.zeros_like(dvacc)
        mat = _mask(_sub_pred(lambda j, s: j <= s))

        def tile(q0, roff, nrows, k0, nsub, diag):
            qs, ks, rows = _att_slices(q0, roff, nrows, k0, nsub)
            for h in range(2):
                z = _dg(qm[h, qs, :], kb[ks, :], _NT)
                nl = _att_nl(z)
                lb = z - nl
                datt = _dg(dom[h, qs, :], vb[ks, :], _NT)
                tn = tot_ref[h, qs, 0:1]
                base = pre[h, rows, :] - tn
                pe = pde[h, rows, :]
                atts, dzs = [], []
                for b in range(nsub):
                    cols = slice(b * ATT_SUB, (b + 1) * ATT_SUB)
                    masked = diag and b == nsub - 1
                    x = jnp.where(tri, nl[:, cols], 0.0) if masked else nl[:, cols]
                    pin = _cumsum_mm(x, mat)
                    att = jnp.exp(lb[:, cols] + base + pin)
                    if masked:
                        att = jnp.where(tri, att, 0.0)
                    dE = att * datt[:, cols]
                    pde_in = _cumsum_mm(dE, mat)
                    dz = dE - jnp.exp(lb[:, cols]) * (pe + pde_in)
                    if masked:
                        dz = jnp.where(tri, dz, 0.0)
                    atts.append(att.astype(BF16))
                    dzs.append(dz.astype(BF16))
                    base = base + pin[:, ATT_SUB - 1:ATT_SUB]
                    pe = pe + pde_in[:, ATT_SUB - 1:ATT_SUB]
                pre[h, rows, :] = base + tn
                pde[h, rows, :] = pe
                attb = jnp.concatenate(atts, axis=1) if nsub > 1 else atts[0]
                dzb = jnp.concatenate(dzs, axis=1) if nsub > 1 else dzs[0]
                dqacc[rows, :] += _dot(dzb, km[h, ks, :])
                dkacc[ks, :] += _dg(dzb, qm[h, qs, :], _TN)
                dvacc[ks, :] += _dg(attb, dom[h, qs, :], _TN)

        def qblock(qi, _):
            q0 = qi * ATT_BLOCK
            qs = pl.ds(pl.multiple_of(q0, ATT_BLOCK), ATT_BLOCK)
            pre[...] = jnp.zeros_like(pre)
            pde[...] = jnp.zeros_like(pde)
            dqacc[...] = jnp.zeros_like(dqacc)

            def kblock(kj, _):
                tile(q0, 0, ATT_BLOCK, kj * ATT_BLOCK, nsb, False)
                return 0

            lax.fori_loop(0, qi, kblock, 0)
            for rb in range(nsb):
                tile(q0, rb * ATT_SUB, ATT_SUB, q0, rb + 1, True)
            dq_ref[qs, :] = (dqacc[...] * scale).astype(dq_ref.dtype)
            return 0

        lax.fori_loop(0, nq, qblock, 0)
        dk_ref[...] = dkacc[...].astype(dk_ref.dtype)
        dv_ref[...] = dvacc[...].astype(dv_ref.dtype)

    spec = lambda col: pl.BlockSpec((L, LANES), lambda bi, p: (bi, col // LANES + p))
    ospec = pl.BlockSpec((L, LANES), lambda bi, p: (bi, p))
    T = Bl * L
    return pl.pallas_call(
        kern, name=name,
        out_shape=[jax.ShapeDtypeStruct((T, D), BF16)] * 3,
        grid=(Bl, npairs),
        in_specs=[ospec, pl.BlockSpec((None, None, 2, L, LANES), lambda bi, p: (bi, p, 0, 0, 0)),
                  spec(qc), spec(kc), spec(vc)],
        out_specs=[ospec, ospec, ospec],
        scratch_shapes=[pltpu.VMEM((2, L, LANES), BF16), pltpu.VMEM((2, L, LANES), BF16),
                        pltpu.VMEM((L, LANES), BF16), pltpu.VMEM((L, LANES), BF16),
                        pltpu.VMEM((2, L, LANES), BF16), pltpu.VMEM((L, LANES), F32),
                        pltpu.VMEM((L, LANES), F32), pltpu.VMEM((2, ATT_BLOCK, 1), F32),
                        pltpu.VMEM((2, ATT_BLOCK, 1), F32), pltpu.VMEM((ATT_BLOCK, LANES), F32)],
        compiler_params=_cparams(("parallel", "parallel")),
    )(dout, tot, proj, proj, proj)


def _mix_fwd(y, xbc, proj, yatt, dskip_e, g_ssd, g_att, D, *, name):
    T = y.shape[0]
    tm = _blk(T, 256)

    def kern(y_ref, xs_ref, z_ref, ya_ref, dsk_ref, gs_ref, ga_ref, o_ref):
        z = z_ref[...]
        u = (y_ref[...] + xs_ref[...] * dsk_ref[...]) * (z * _sigmoid(z))
        rs = lax.rsqrt(jnp.mean(u * u, axis=-1, keepdims=True) + EPS)
        o_ref[:, :D] = (u * rs * gs_ref[...]).astype(o_ref.dtype)
        ya = ya_ref[...]
        ra = lax.rsqrt(jnp.mean(ya * ya, axis=-1, keepdims=True) + EPS)
        o_ref[:, D:] = (ya * ra * ga_ref[...]).astype(o_ref.dtype)

    row = pl.BlockSpec((tm, D), lambda i: (i, 0))
    vec = pl.BlockSpec((1, D), lambda i: (0, 0))
    return pl.pallas_call(
        kern, name=name, out_shape=jax.ShapeDtypeStruct((T, 2 * D), BF16), grid=(T // tm,),
        in_specs=[row, row, row, row, vec, vec, vec],
        out_specs=pl.BlockSpec((tm, 2 * D), lambda i: (i, 0)),
        compiler_params=_cparams(("parallel",)),
    )(y, xbc, proj, yatt, dskip_e, g_ssd.reshape(1, D), g_att.reshape(1, D))


def _mix_bwd(dmix, y, xbc, proj, yatt, dskip_e, g_ssd, g_att, D, *, name):
    T = y.shape[0]
    tm = _blk(T, 256)

    def kern(d_ref, y_ref, xs_ref, z_ref, ya_ref, dsk_ref, gs_ref, ga_ref,
             dz_ref, dy_ref, dya_ref, dgs_ref, dga_ref):
        @pl.when(pl.program_id(0) == 0)
        def _():
            dgs_ref[...] = jnp.zeros_like(dgs_ref)
            dga_ref[...] = jnp.zeros_like(dga_ref)

        z = z_ref[...]
        sig = _sigmoid(z)
        sz = z * sig
        ytot = y_ref[...] + xs_ref[...] * dsk_ref[...]
        u = ytot * sz
        rs = lax.rsqrt(jnp.mean(u * u, axis=-1, keepdims=True) + EPS)
        un = u * rs
        do = d_ref[:, :D]
        dog = do * gs_ref[...]
        du = rs * (dog - un * jnp.mean(dog * un, axis=-1, keepdims=True))
        dgs_ref[...] += jnp.sum(do * un, axis=0, keepdims=True)
        dy_ref[...] = du * sz
        dz_ref[...] = (du * ytot * (sig * (1.0 + z * (1.0 - sig)))).astype(dz_ref.dtype)
        ya = ya_ref[...]
        ra = lax.rsqrt(jnp.mean(ya * ya, axis=-1, keepdims=True) + EPS)
        yan = ya * ra
        da = d_ref[:, D:]
        dag = da * ga_ref[...]
        dya_ref[...] = ra * (dag - yan * jnp.mean(dag * yan, axis=-1, keepdims=True))
        dga_ref[...] += jnp.sum(da * yan, axis=0, keepdims=True)

    row = pl.BlockSpec((tm, D), lambda i: (i, 0))
    vec = pl.BlockSpec((1, D), lambda i: (0, 0))
    return pl.pallas_call(
        kern, name=name,
        out_shape=[jax.ShapeDtypeStruct((T, D), BF16), jax.ShapeDtypeStruct((T, D), F32),
                   jax.ShapeDtypeStruct((T, D), F32), jax.ShapeDtypeStruct((1, D), F32),
                   jax.ShapeDtypeStruct((1, D), F32)],
        grid=(T // tm,),
        in_specs=[pl.BlockSpec((tm, 2 * D), lambda i: (i, 0)), row, row, row, row, vec, vec, vec],
        out_specs=[row, row, row, vec, vec],
        compiler_params=_cparams(("arbitrary",)),
    )(dmix, y, xbc, proj, yatt, dskip_e, g_ssd.reshape(1, D), g_att.reshape(1, D))


def _head_sum_mat(D):
    i = lax.broadcasted_iota(jnp.int32, (D, LANES), 0)
    c = lax.broadcasted_iota(jnp.int32, (D, LANES), 1)
    return _mask((i >= c * HEAD_DIM) & (i < (c + 1) * HEAD_DIM))


def _collapse_heads(x, *, name):
    R, D = x.shape
    tm = _blk(R, 256)

    def kern(x_ref, o_ref):
        o_ref[...] = _dot_exact_rhs(x_ref[...], _head_sum_mat(D))

    return pl.pallas_call(
        kern, name=name, out_shape=jax.ShapeDtypeStruct((R, LANES), F32), grid=(R // tm,),
        in_specs=[pl.BlockSpec((tm, D), lambda i: (i, 0))],
        out_specs=pl.BlockSpec((tm, LANES), lambda i: (i, 0)),
        compiler_params=_cparams(("parallel",)),
    )(x)


def _adam_math(w, g, m, v):
    m = ADAM_B1 * m + (1.0 - ADAM_B1) * g
    v = ADAM_B2 * v + (1.0 - ADAM_B2) * (g * g)
    m_hat = m / (1.0 - ADAM_B1 ** ADAM_STEP)
    v_hat = v / (1.0 - ADAM_B2 ** ADAM_STEP)
    delta = -ADAM_LR * (m_hat / (jnp.sqrt(v_hat) + ADAM_EPS) + ADAM_WD * w)
    return delta, m, v


def _adamw_sharded(pieces, w, m, v, *, name):
    R, C = w.shape
    tr = _blk(R, 256) if R % 8 == 0 else R

    def kern(p_ref, w_ref, m_ref, v_ref, g_ref, d_ref, nm_ref, nv_ref):
        g = p_ref[0].astype(F32)
        for j in range(1, N_DEV):
            g = g + p_ref[j].astype(F32)
        d, nm, nv = _adam_math(w_ref[...], g, m_ref[...], v_ref[...])
        g_ref[...] = g
        d_ref[...] = d
        nm_ref[...] = nm
        nv_ref[...] = nv

    row = pl.BlockSpec((tr, C), lambda i: (i, 0))
    return pl.pallas_call(
        kern, name=name, out_shape=[jax.ShapeDtypeStruct((R, C), F32)] * 4, grid=(R // tr,),
        in_specs=[pl.BlockSpec((N_DEV, tr, C), lambda i: (0, i, 0)), row, row, row],
        out_specs=[row] * 4,
        compiler_params=_cparams(("parallel",)),
    )(pieces, w, m, v)


def _small_update(parts, wd, md, vd, we, me, ve, nd, ne, D, *, name):
    def kern(p_ref, wd_ref, md_ref, vd_ref, we_ref, me_ref, ve_ref,
             gd_ref, dd_ref, nmd_ref, nvd_ref, ge_ref, de_ref, nme_ref, nve_ref, loss_ref):
        s = p_ref[0]
        for j in range(1, N_DEV):
            s = s + p_ref[j]
        gd = s[:nd]
        d, nm, nv = _adam_math(wd_ref[...], gd, md_ref[...], vd_ref[...])
        gd_ref[...] = gd
        dd_ref[...] = d
        nmd_ref[...] = nm
        nvd_ref[...] = nv
        ge = _dot_exact_rhs(s[nd:nd + ne], _head_sum_mat(D))
        d, nm, nv = _adam_math(we_ref[...], ge, me_ref[...], ve_ref[...])
        ge_ref[...] = ge
        de_ref[...] = d
        nme_ref[...] = nm
        nve_ref[...] = nv
        tot = jnp.sum(jnp.sum(s[nd + ne:], axis=1, keepdims=True), axis=0, keepdims=True)
        loss_ref[...] = jnp.broadcast_to(tot, loss_ref.shape)

    vm = pl.BlockSpec(memory_space=pltpu.VMEM)
    return pl.pallas_call(
        kern, name=name,
        out_shape=[jax.ShapeDtypeStruct((nd, D), F32)] * 4 + [jax.ShapeDtypeStruct((ne, LANES), F32)] * 4
        + [jax.ShapeDtypeStruct((8, LANES), F32)],
        in_specs=[vm] * 7, out_specs=[vm] * 9,
        compiler_params=pltpu.CompilerParams(vmem_limit_bytes=VMEM_LIMIT),
    )(parts, wd, md, vd, we, me, ve)


def _dev_index(p):
    return 4 * p[0] + 2 * p[1] + p[2]


def _all_gather(arrs, *, name):
    n = len(arrs)

    def body(*refs):
        xs, outs = refs[:n], refs[n:2 * n]
        send, recv, loc = refs[2 * n:]
        x, y, c = lax.axis_index("x"), lax.axis_index("y"), lax.axis_index("c")
        me, sib = (x, y, c), (x, y, 1 - c)
        chips = [(1 - x, y), (x, 1 - y), (1 - x, 1 - y)]

        def cp(k, s, block, to, src=None):
            dst = outs[k].at[_dev_index(block)]
            return pltpu.make_async_remote_copy(
                src_ref=dst if src is None else src, dst_ref=dst,
                send_sem=send.at[7 * k + s], recv_sem=recv.at[7 * k + s],
                device_id=to, device_id_type=pl.DeviceIdType.MESH)

        mine = [pltpu.make_async_copy(xs[k], outs[k].at[_dev_index(me)], loc.at[k]) for k in range(n)]
        for m in mine:
            m.start()
        first = []
        for k in range(n):
            first.append(cp(k, 0, me, sib, src=xs[k]))
            for j, ch in enumerate(chips):
                first.append(cp(k, 1 + j, me, (*ch, c), src=xs[k]))
        for f in first:
            f.start()
        passed = []
        for j, ch in enumerate(chips):
            for k in range(n):
                cp(k, 1 + j, (*ch, c), me).wait_recv()
                p = cp(k, 4 + j, (*ch, c), sib)
                p.start()
                passed.append(p)
        for k in range(n):
            cp(k, 0, sib, me).wait_recv()
            for j, ch in enumerate(chips):
                cp(k, 4 + j, (*ch, 1 - c), me).wait_recv()
        for f in first + passed:
            f.wait_send()
        for m in mine:
            m.wait()

    hbm = pl.BlockSpec(memory_space=pl.ANY)
    return pl.pallas_call(
        body, name=name,
        out_shape=[jax.ShapeDtypeStruct((N_DEV,) + a.shape, a.dtype) for a in arrs],
        in_specs=[hbm] * n, out_specs=[hbm] * n,
        scratch_shapes=[pltpu.SemaphoreType.DMA((7 * n,)), pltpu.SemaphoreType.DMA((7 * n,)),
                        pltpu.SemaphoreType.DMA((n,))],
    )(*arrs)


_RELS = [(fx, fy, fc) for fx in (0, 1) for fy in (0, 1) for fc in (0, 1) if fx or fy or fc]


def _direct_ops(src_of, dst_of, n_in):
    def make(ins, outs, send, recv, loc):
        xs, n = ins[:n_in], len(outs)
        x, y, c = lax.axis_index("x"), lax.axis_index("y"), lax.axis_index("c")
        me = _dev_index((x, y, c))
        peers = [(1 - x if r[0] else x, 1 - y if r[1] else y, 1 - c if r[2] else c) for r in _RELS]

        def remote(k, s, src, dst):
            return pltpu.make_async_remote_copy(
                src_ref=src, dst_ref=dst, send_sem=send.at[7 * k + s], recv_sem=recv.at[7 * k + s],
                device_id=peers[s], device_id_type=pl.DeviceIdType.MESH)

        def local(k):
            return pltpu.make_async_copy(src_of(xs, k, me, me), dst_of(outs, k, me), loc.at[k])

        def sends():
            return [remote(k, s, src_of(xs, k, me, _dev_index(p)), dst_of(outs, k, me))
                    for k in range(n) for s, p in enumerate(peers)]

        def start():
            for k in range(n):
                local(k).start()
            for cp in sends():
                cp.start()

        def finish():
            for k in range(n):
                for s, p in enumerate(peers):
                    slot = dst_of(outs, k, _dev_index(p))
                    remote(k, s, slot, slot).wait_recv()
            for cp in sends():
                cp.wait_send()
            for k in range(n):
                local(k).wait()

        return start, finish

    return make


def _gather_comm(arrs):
    make = _direct_ops(lambda xs, k, me, p: xs[k], lambda outs, k, j: outs[k].at[j], len(arrs))
    return make, list(arrs), [jax.ShapeDtypeStruct((N_DEV,) + a.shape, a.dtype) for a in arrs], {}


def _exchange_comm(pieces, bufs, layer):
    n = len(pieces)
    make = _direct_ops(lambda xs, k, me, p: xs[k].at[p], lambda outs, k, j: outs[k].at[j, layer], n)
    return (make, list(pieces) + list(bufs), [jax.ShapeDtypeStruct(b.shape, b.dtype) for b in bufs],
            {n + k: k for k in range(n)})


def _comm_call(comm, *, name):
    make, ins, out_shapes, alias = comm
    n_ci, n_co = len(ins), len(out_shapes)

    def body(*refs):
        start, finish = make(refs[:n_ci], refs[n_ci:n_ci + n_co], *refs[n_ci + n_co:])
        start()
        finish()

    hbm = pl.BlockSpec(memory_space=pl.ANY)
    return pl.pallas_call(
        body, name=name, out_shape=list(out_shapes), in_specs=[hbm] * n_ci, out_specs=[hbm] * n_co,
        scratch_shapes=[pltpu.SemaphoreType.DMA((7 * n_co,)), pltpu.SemaphoreType.DMA((7 * n_co,)),
                        pltpu.SemaphoreType.DMA((n_co,))],
        input_output_aliases=alias,
    )(*ins)


def _layer_fwd(x, P, dims, l, nxt):
    Bl, L, D = dims
    Dc = D + 2 * SSM_GROUPS * SSM_STATE
    qc, kc, vc, dtc = D + Dc, 2 * D + Dc, 3 * D + Dc, 4 * D + Dc
    gat = (lambda key: _gather_comm([nxt[key]])) if nxt is not None else (lambda key: None)
    h = _rmsnorm_fwd(x, P["norm_mix_g"], name=f"norm_mix_fwd_{l}")
    (proj,), g_in = _mm(h, P["w_in"], out_dtypes=[F32], name=f"in_proj_{l}", comm=gat("w_in"))
    xbc = _conv_fwd(proj, P["conv_w"], P["conv_b"], Bl, L, D, Dc, name=f"conv_fwd_{l}")
    y, states = _ssd_fwd(xbc, proj, P["dt_bias_e"], P["a_log_e"], Bl, L, D, dtc, name=f"ssd_fwd_{l}")
    yatt, tot = _attn_fwd(proj, Bl, L, D, qc, kc, vc, name=f"attn_fwd_{l}")
    ymix = _mix_fwd(y, xbc, proj, yatt, P["d_skip_e"], P["ssd_norm_g"], P["att_norm_g"], D, name=f"mix_fwd_{l}")
    (x1,), g_out = _mm(ymix, P["w_out"], out_dtypes=[F32], epilogue=lambda r, res: (r + res,), extras=(x,),
                       name=f"out_proj_{l}", comm=gat("w_out"))
    h2 = _rmsnorm_fwd(x1, P["norm_mlp_g"], name=f"norm_mlp_fwd_{l}")

    def relu2(r):
        a = jnp.maximum(r, 0.0)
        return r, a * a

    (u, act), g_up = _mm(h2, P["w_up"], out_dtypes=[F32, BF16], epilogue=relu2, name=f"mlp_up_{l}",
                         comm=gat("w_up"))
    (x2,), g_down = _mm(act, P["w_down"], out_dtypes=[F32], epilogue=lambda r, res: (r + res,), extras=(x1,),
                        name=f"mlp_down_{l}", comm=gat("w_down"))
    saved = dict(x=x, h=h, proj=proj, xbc=xbc, y=y, states=states, yatt=yatt, tot=tot, ymix=ymix,
                 x1=x1, h2=h2, u=u, act=act)
    gathered = dict(w_in=g_in[0], w_out=g_out[0], w_up=g_up[0], w_down=g_down[0]) if nxt is not None else None
    return x2, saved, gathered


def _layer_bwd(dx2, S, P, dims, l, pend, bufs):
    Bl, L, D = dims
    Dc = D + 2 * SSM_GROUPS * SSM_STATE
    qc, kc, vc, dtc = D + Dc, 2 * D + Dc, 3 * D + Dc, 4 * D + Dc
    G = {}

    def exch(keys):
        if pend is None:
            return None
        return _exchange_comm([pend[k] for k in keys], [bufs[k] for k in keys], l + 1)

    def took(keys, outs):
        if pend is not None:
            bufs.update(zip(keys, outs))

    (du,), o = _mm(dx2, P["w_down_t"], out_dtypes=[BF16], extras=(S["u"],),
                   epilogue=lambda r, u: (r * (2.0 * jnp.maximum(u, 0.0)),), name=f"mlp_down_bwd_{l}",
                   comm=exch(["w_in"]))
    took(["w_in"], o)
    (G["w_down"],), o = _mm(S["act"], dx2, ta=True, out_dtypes=[F32], name=f"mlp_down_dw_{l}",
                            comm=exch(["w_up"]))
    took(["w_up"], o)
    (dh2,), o = _mm(du, P["w_up_t"], out_dtypes=[F32], name=f"mlp_up_bwd_{l}", comm=exch(["w_down"]))
    took(["w_down"], o)
    (G["w_up"],), o = _mm(S["h2"], du, ta=True, out_dtypes=[F32], name=f"mlp_up_dw_{l}",
                          comm=exch(["w_out", "conv_w"]))
    took(["w_out", "conv_w"], o)
    dx1, G["norm_mlp_g"] = _rmsnorm_bwd(dh2, S["x1"], P["norm_mlp_g"], dx2, name=f"norm_mlp_bwd_{l}")
    (dmix,), _ = _mm(dx1, P["w_out_t"], out_dtypes=[F32], name=f"out_proj_bwd_{l}")
    (G["w_out"],), _ = _mm(S["ymix"], dx1, ta=True, out_dtypes=[F32], name=f"out_proj_dw_{l}")
    dz, dy, dyatt, G["ssd_norm_g"], G["att_norm_g"] = _mix_bwd(
        dmix, S["y"], S["xbc"], S["proj"], S["yatt"], P["d_skip_e"], P["ssd_norm_g"], P["att_norm_g"], D,
        name=f"mix_bwd_{l}")
    dq, dk, dv = _attn_bwd(dyatt, S["tot"], S["proj"], Bl, L, D, qc, kc, vc, name=f"attn_bwd_{l}")
    dxs, dB, dC, ddt, G["dt_bias_e"], G["a_log_e"], G["d_skip_e"] = _ssd_bwd(
        dy, S["states"], S["xbc"], S["proj"], P["dt_bias_e"], P["a_log_e"], P["d_skip_e"],
        Bl, L, D, dtc, name=f"ssd_bwd_{l}")
    dact = jnp.concatenate([dxs, dB, dC], axis=1)
    dxbc, G["conv_w"], G["conv_b"] = _conv_bwd(dact, S["proj"], P["conv_w"], P["conv_b"], Bl, L, D, Dc,
                                               name=f"conv_bwd_{l}")
    dproj = jnp.concatenate([dz, dxbc, dq, dk, dv, ddt], axis=1)
    (dh,), _ = _mm(dproj, P["w_in_t"], out_dtypes=[F32], name=f"in_proj_bwd_{l}")
    (G["w_in"],), _ = _mm(S["h"], dproj, ta=True, out_dtypes=[F32], name=f"in_proj_dw_{l}")
    dx, G["norm_mix_g"] = _rmsnorm_bwd(dh, S["x"], P["norm_mix_g"], dx1, name=f"norm_mix_bwd_{l}")
    return dx, G, bufs


def _pad_rows(a, rows):
    return jnp.concatenate([a, jnp.zeros((rows - a.shape[0],) + a.shape[1:], a.dtype)], axis=0)


def kernel(x, norm_mix_g, w_in, conv_w, conv_b, dt_bias, a_log, d_skip, ssd_norm_g, att_norm_g, w_out, norm_mlp_g, w_up, w_down, final_norm_g, loss_target, m_norm_mix_g, m_w_in, m_conv_w, m_conv_b, m_dt_bias, m_a_log, m_d_skip, m_ssd_norm_g, m_att_norm_g, m_w_out, m_norm_mlp_g, m_w_up, m_w_down, m_final_norm_g, v_norm_mix_g, v_w_in, v_conv_w, v_conv_b, v_dt_bias, v_a_log, v_d_skip, v_ssd_norm_g, v_att_norm_g, v_w_out, v_norm_mlp_g, v_w_up, v_w_down, v_final_norm_g):
    Bl, L, D = x.shape
    T = Bl * L
    depth = w_in.shape[0]
    H = D // HEAD_DIM
    Dc = D + 2 * SSM_GROUPS * SSM_STATE
    Dff = w_up.shape[2] * N_DEV
    dims = (Bl, L, D)

    bf = lambda a: a.astype(BF16)
    shards = [dict(w_in=bf(w_in[l]), w_out=bf(w_out[l]), w_up=bf(w_up[l]), w_down=bf(w_down[l]))
              for l in range(depth)]
    g0 = _all_gather([shards[0][k] for k in ("w_in", "w_out", "w_up", "w_down")] + [conv_w],
                     name="gather_weights_0")
    gathered = dict(zip(("w_in", "w_out", "w_up", "w_down"), g0))
    Cw = g0[4].transpose(1, 2, 0, 3).reshape(depth, CONV_WIDTH, Dc)
    rep = lambda a: jnp.repeat(a, HEAD_DIM, axis=1)
    dt_bias_e, a_log_e, d_skip_e = rep(dt_bias), rep(a_log), rep(d_skip)
    o = [0, D, D + Dc, D + Dc + H, 2 * D + Dc + H, 3 * D + Dc + H, 4 * D + Dc + H]

    def layer_params(g, l):
        W_in = g["w_in"].transpose(1, 0, 2).reshape(D, -1)
        wz, wxbc, wdt, wq, wk, wv = [W_in[:, o[i]:o[i + 1]] for i in range(6)]
        W_pack = jnp.concatenate([wz, wxbc, wq, wk, wv, jnp.repeat(wdt, HEAD_DIM, axis=1)], axis=1)
        W_out = g["w_out"].reshape(2 * D, D)
        W_up = g["w_up"].transpose(1, 0, 2).reshape(D, Dff)
        W_down = g["w_down"].reshape(Dff, D)
        return dict(
            norm_mix_g=norm_mix_g[l], w_in=W_pack, w_in_t=W_pack.T, conv_w=Cw[l], conv_b=conv_b[l],
            dt_bias_e=dt_bias_e[l:l + 1], a_log_e=a_log_e[l:l + 1], d_skip_e=d_skip_e[l:l + 1],
            ssd_norm_g=ssd_norm_g[l], att_norm_g=att_norm_g[l], w_out=W_out, w_out_t=W_out.T,
            norm_mlp_g=norm_mlp_g[l], w_up=W_up, w_up_t=W_up.T, w_down=W_down, w_down_t=W_down.T)

    xa = x.reshape(T, D)
    saved, params = [], []
    for l in range(depth):
        params.append(layer_params(gathered, l))
        xa, s, gathered = _layer_fwd(xa, params[l], dims, l, shards[l + 1] if l + 1 < depth else None)
        saved.append(s)
    dx, g_final, loss_part = _loss_head(xa, final_norm_g, loss_target.reshape(T, D), name="loss_head")

    n_in = w_in.shape[2]

    def pieces_of(G, l):
        gW = G["w_in"]
        gdt = _collapse_heads(gW[:, 4 * D + Dc:], name=f"collapse_w_dt_{l}")[:, :H]
        gW_in = jnp.concatenate([gW[:, :D + Dc], gdt, gW[:, D + Dc:4 * D + Dc]], axis=1)
        return dict(
            w_in=bf(gW_in.reshape(D, N_DEV, n_in).transpose(1, 0, 2)),
            w_out=bf(G["w_out"].reshape(N_DEV, 2 * D // N_DEV, D)),
            w_up=bf(G["w_up"].reshape(D, N_DEV, Dff // N_DEV).transpose(1, 0, 2)),
            w_down=bf(G["w_down"].reshape(N_DEV, Dff // N_DEV, D)),
            conv_w=bf(G["conv_w"].reshape(CONV_WIDTH, N_DEV, Dc // N_DEV).transpose(1, 0, 2)))

    keys = ("w_in", "w_out", "w_up", "w_down", "conv_w")
    bufs = {k: lax.empty((N_DEV, depth) + w.shape[1:], BF16)
            for k, w in zip(keys, (w_in, w_out, w_up, w_down, conv_w))}
    grads = [None] * depth
    pend = None
    for l in reversed(range(depth)):
        dx, grads[l], bufs = _layer_bwd(dx, saved[l], params[l], dims, l, pend, bufs)
        pend = pieces_of(grads[l], l)
    grad_x = dx.reshape(Bl, L, D)
    last = _comm_call(_exchange_comm([pend[k] for k in keys], [bufs[k] for k in keys], 0), name="exchange_grads_0")
    r_in, r_out, r_up, r_down, r_conv = last
    st = lambda k: jnp.stack([grads[l][k] for l in range(depth)])

    def sharded(pieces, w, m, v, name):
        shp = w.shape
        C = shp[-1]
        res = _adamw_sharded(pieces.reshape(N_DEV, -1, C), w.reshape(-1, C), m.reshape(-1, C),
                             v.reshape(-1, C), name=name)
        return [a.reshape(shp) for a in res]

    u_in = sharded(r_in, w_in, m_w_in, v_w_in, "adamw_w_in")
    u_out = sharded(r_out, w_out, m_w_out, v_w_out, "adamw_w_out")
    u_up = sharded(r_up, w_up, m_w_up, v_w_up, "adamw_w_up")
    u_down = sharded(r_down, w_down, m_w_down, v_w_down, "adamw_w_down")
    u_conv = sharded(r_conv, conv_w, m_conv_w, v_conv_w, "adamw_conv_w")

    rows = lambda a: a.reshape(-1, D)
    direct_names = ["norm_mix_g", "ssd_norm_g", "att_norm_g", "norm_mlp_g", "conv_b"]
    direct_g = [rows(st(k)) for k in direct_names] + [g_final]
    exp_names = ["dt_bias_e", "a_log_e", "d_skip_e"]
    exp_g = [rows(st(k)) for k in exp_names]
    n_direct = sum(a.shape[0] for a in direct_g)
    nd = -(-n_direct // 8) * 8
    ne = -(-3 * depth // 8) * 8
    part = jnp.concatenate([_pad_rows(jnp.concatenate(direct_g, axis=0), nd),
                            _pad_rows(jnp.concatenate(exp_g, axis=0), ne),
                            _pad_rows(loss_part, 8)], axis=0)
    (parts,) = _all_gather([part], name="gather_small_grads")

    def direct_pack(ws):
        return _pad_rows(jnp.concatenate([rows(a) for a in ws], axis=0), nd)

    def exp_pack(ws):
        a = jnp.concatenate(ws, axis=0)
        a = jnp.concatenate([a, jnp.zeros((a.shape[0], LANES - H), F32)], axis=1)
        return _pad_rows(a, ne)

    res = _small_update(
        parts,
        direct_pack([norm_mix_g, ssd_norm_g, att_norm_g, norm_mlp_g, conv_b, final_norm_g]),
        direct_pack([m_norm_mix_g, m_ssd_norm_g, m_att_norm_g, m_norm_mlp_g, m_conv_b, m_final_norm_g]),
        direct_pack([v_norm_mix_g, v_ssd_norm_g, v_att_norm_g, v_norm_mlp_g, v_conv_b, v_final_norm_g]),
        exp_pack([dt_bias, a_log, d_skip]), exp_pack([m_dt_bias, m_a_log, m_d_skip]),
        exp_pack([v_dt_bias, v_a_log, v_d_skip]), nd, ne, D, name="small_update")
    loss = res[8][0, 0]

    def unpack(kind):
        d, e = res[kind], res[4 + kind]
        out = {}
        r0 = 0
        for nm, shp in [("norm_mix_g", (depth, D)), ("ssd_norm_g", (depth, D)), ("att_norm_g", (depth, D)),
                        ("norm_mlp_g", (depth, D)), ("conv_b", (depth, Dc)), ("final_norm_g", (D,))]:
            n = 1
            for s_ in shp:
                n *= s_
            out[nm] = d[r0:r0 + n // D].reshape(shp)
            r0 += n // D
        for i, nm in enumerate(["dt_bias", "a_log", "d_skip"]):
            out[nm] = e[i * depth:(i + 1) * depth, :H]
        return out

    names = ["norm_mix_g", "w_in", "conv_w", "conv_b", "dt_bias", "a_log", "d_skip", "ssd_norm_g",
             "att_norm_g", "w_out", "norm_mlp_g", "w_up", "w_down", "final_norm_g"]
    big = {"w_in": u_in, "w_out": u_out, "w_up": u_up, "w_down": u_down, "conv_w": u_conv}
    outs = [loss, grad_x]
    for kind in range(4):
        small = unpack(kind)
        for nm in names:
            outs.append(big[nm][kind] if nm in big else small[nm])
    return tuple(outs)
```

```python
import functools

import jax
import jax.numpy as jnp
from jax import lax
from jax.experimental import pallas as pl
from jax.experimental.pallas import tpu as pltpu

F32 = jnp.float32
BF16 = jnp.bfloat16

HEAD_DIM = 64
SSM_STATE = 128
SSM_GROUPS = 4
CONV_WIDTH = 4
CHUNK = 128
ATT_BLOCK = 512
ATT_SUB = 256
SSD_UNROLL = 8
EPS = 1e-5
LANES = 128
N_DEV = 8
VMEM_LIMIT = 56 * 1024 * 1024
MM_VMEM_BUDGET = 44 * 1024 * 1024

ADAM_LR = 0.001
ADAM_B1 = 0.9
ADAM_B2 = 0.999
ADAM_EPS = 1e-08
ADAM_WD = 0.01
ADAM_STEP = 10

_NT =(((1,), (1,)), ((), ()))
_TN = (((0,), (0,)), ((), ()))


def _blk(n, pref):
    if n <= pref:
        return n
    b = (pref // LANES) * LANES
    while b >= LANES:
        if n % b == 0:
            return b
        b -= LANES
    return n


def _cparams(sem):
    return pltpu.CompilerParams(dimension_semantics=sem, vmem_limit_bytes=VMEM_LIMIT)


def _dot(a, b):
    return jnp.dot(a, b, preferred_element_type=F32)


def _dg(a, b, dims):
    return lax.dot_general(a, b, dims, preferred_element_type=F32)


def _split3(x):
    h = x.astype(BF16)
    r = x - h.astype(F32)
    m = r.astype(BF16)
    l = (r - m.astype(F32)).astype(BF16)
    return h, m, l


def _dot_exact_rhs(x, c):
    h, m, l = _split3(x)
    return _dot(h, c) + _dot(m, c) + _dot(l, c)


def _dot_exact_lhs(c, x, dims):
    h, m, l = _split3(x)
    return _dg(c, h, dims) + _dg(c, m, dims) + _dg(c, l, dims)


def _dot_exact_lhs_stacked(c, x):
    return _dot(jnp.concatenate([c, c, c], axis=1), jnp.concatenate(_split3(x), axis=0))


def _mask(cond):
    return jnp.where(cond, 1.0, 0.0).astype(BF16)


def _sigmoid(x):
    return 1.0 / (1.0 + jnp.exp(-x))


def _softplus(x):
    return jnp.maximum(x, 0.0) + jnp.log(1.0 + jnp.exp(-jnp.abs(x)))


def _mm_tiles(M, N, K, a_bytes, b_bytes, mn_bytes, tm, tn, tk):
    tn = _blk(N, tn)
    for tm_try, tk_try in ((2 * tm, tk), (2 * tm, tk // 2), (tm, tk), (tm, tk // 2), (tm // 2, tk // 2)):
        bm, bk = _blk(M, tm_try), _blk(K, tk_try)
        need = 2 * (bm * bk * a_bytes + bk * tn * b_bytes + bm * tn * mn_bytes) + bm * tn * 4
        if need <= MM_VMEM_BUDGET:
            return bm, tn, bk
    return _blk(M, tm // 2), tn, _blk(K, tk // 4)


def _mm(a, b, *, name, out_dtypes, ta=False, epilogue=None, extras=(), tm=1024, tn=1024, tk=2048, comm=None):
    if ta:
        K, M = a.shape
    else:
        M, K = a.shape
    N = b.shape[1]
    assert b.shape[0] == K
    tm, tn, tk = _mm_tiles(M, N, K, a.dtype.itemsize, b.dtype.itemsize,
                           sum(e.dtype.itemsize for e in extras) + sum(jnp.dtype(d).itemsize for d in out_dtypes),
                           tm, tn, tk)
    nj, ni, nk = N // tn, M // tm, K // tk
    n_ex, n_out = len(extras), len(out_dtypes)
    c_make, c_ins, c_outs, c_alias = comm if comm is not None else (None, [], [], {})
    n_ci, n_co = len(c_ins), len(c_outs)

    def kern(*refs):
        a_ref, b_ref = refs[0], refs[1]
        ex = refs[2:2 + n_ex]
        ci = refs[2 + n_ex:2 + n_ex + n_ci]
        o0 = 2 + n_ex + n_ci
        outs = refs[o0:o0 + n_out]
        co = refs[o0 + n_out:o0 + n_out + n_co]
        acc = refs[o0 + n_out + n_co]
        j, i, k = pl.program_id(0), pl.program_id(1), pl.program_id(2)
        if comm is not None:
            start, finish = c_make(ci, co, *refs[o0 + n_out + n_co + 1:])

            @pl.when((j == 0) & (i == 0) & (k == 0))
            def _():
                start()

        av = a_ref[...].astype(BF16)
        bv = b_ref[...].astype(BF16)
        prod = _dg(av, bv, _TN) if ta else _dot(av, bv)

        def write(r):
            vals = epilogue(r, *[e[...] for e in ex]) if epilogue is not None else (r,)
            for o, v in zip(outs, vals):
                o[...] = v.astype(o.dtype)

        if nk == 1:
            write(prod)
        else:
            @pl.when(k == 0)
            def _():
                acc[...] = prod

            @pl.when(k > 0)
            def _():
                acc[...] += prod

            @pl.when(k == nk - 1)
            def _():
                write(acc[...])

        if comm is not None:
            @pl.when((j == nj - 1) & (i == ni - 1) & (k == nk - 1))
            def _():
                finish()

    if ta:
        a_spec = pl.BlockSpec((tk, tm), lambda j, i, k: (k, i))
    else:
        a_spec = pl.BlockSpec((tm, tk), lambda j, i, k: (i, k))
    b_spec = pl.BlockSpec((tk, tn), lambda j, i, k: (k, j))
    o_spec = pl.BlockSpec((tm, tn), lambda j, i, k: (i, j))
    hbm = pl.BlockSpec(memory_space=pl.ANY)
    scratch = [pltpu.VMEM((tm, tn), F32)]
    if comm is not None:
        scratch += [pltpu.SemaphoreType.DMA((7 * n_co,)), pltpu.SemaphoreType.DMA((7 * n_co,)),
                    pltpu.SemaphoreType.DMA((n_co,))]
    res = pl.pallas_call(
        kern, name=name,
        out_shape=[jax.ShapeDtypeStruct((M, N), d) for d in out_dtypes] + list(c_outs),
        grid=(nj, ni, nk),
        in_specs=[a_spec, b_spec] + [o_spec] * n_ex + [hbm] * n_ci,
        out_specs=[o_spec] * n_out + [hbm] * n_co,
        scratch_shapes=scratch,
        input_output_aliases={2 + n_ex + ki: n_out + ko for ki, ko in c_alias.items()},
        compiler_params=_cparams(("parallel", "parallel", "arbitrary") if comm is None
                                 else ("arbitrary", "arbitrary", "arbitrary")),
    )(a, b, *extras, *c_ins)
    return res[:n_out], res[n_out:]


def _rmsnorm_fwd(x, g, *, name):
    T, D = x.shape
    tm = _blk(T, 512)

    def kern(x_ref, g_ref, h_ref):
        xv = x_ref[...]
        r = lax.rsqrt(jnp.mean(xv * xv, axis=-1, keepdims=True) + EPS)
        h_ref[...] = (xv * r * g_ref[...]).astype(h_ref.dtype)

    return pl.pallas_call(
        kern, name=name, out_shape=jax.ShapeDtypeStruct((T, D), BF16), grid=(T // tm,),
        in_specs=[pl.BlockSpec((tm, D), lambda i: (i, 0)), pl.BlockSpec((1, D), lambda i: (0, 0))],
        out_specs=pl.BlockSpec((tm, D), lambda i: (i, 0)),
        compiler_params=_cparams(("parallel",)),
    )(x, g.reshape(1, D))


def _rmsnorm_bwd(dh, x, g, dres, *, name):
    T, D = x.shape
    tm = _blk(T, 512)

    def kern(dh_ref, x_ref, g_ref, dres_ref, dx_ref, dg_ref):
        @pl.when(pl.program_id(0) == 0)
        def _():
            dg_ref[...] = jnp.zeros_like(dg_ref)

        xv = x_ref[...]
        r = lax.rsqrt(jnp.mean(xv * xv, axis=-1, keepdims=True) + EPS)
        xn = xv * r
        dy = dh_ref[...].astype(F32)
        dyg = dy * g_ref[...]
        dx = r * (dyg - xn * jnp.mean(dyg * xn, axis=-1, keepdims=True))
        dx_ref[...] = dres_ref[...] + dx
        dg_ref[...] += jnp.sum(dy * xn, axis=0, keepdims=True)

    row = pl.BlockSpec((tm, D), lambda i: (i, 0))
    vec = pl.BlockSpec((1, D), lambda i: (0, 0))
    return pl.pallas_call(
        kern, name=name,
        out_shape=[jax.ShapeDtypeStruct((T, D), F32), jax.ShapeDtypeStruct((1, D), F32)],
        grid=(T // tm,), in_specs=[row, row, vec, row], out_specs=[row, vec],
        compiler_params=_cparams(("arbitrary",)),
    )(dh, x, g.reshape(1, D), dres)


def _loss_head(x, g, target, *, name):
    T, D = x.shape
    tm = _blk(T, 512)

    def kern(x_ref, g_ref, t_ref, dx_ref, dg_ref, loss_ref):
        @pl.when(pl.program_id(0) == 0)
        def _():
            dg_ref[...] = jnp.zeros_like(dg_ref)
            loss_ref[...] = jnp.zeros_like(loss_ref)

        xv = x_ref[...]
        gv = g_ref[...]
        r = lax.rsqrt(jnp.mean(xv * xv, axis=-1, keepdims=True) + EPS)
        xn = xv * r
        err = xn * gv - t_ref[...]
        loss_ref[...] += jnp.sum(err * err, axis=0, keepdims=True) * (0.5 / D)
        dy = err * (1.0 / D)
        dyg = dy * gv
        dx_ref[...] = r * (dyg - xn * jnp.mean(dyg * xn, axis=-1, keepdims=True))
        dg_ref[...] += jnp.sum(dy * xn, axis=0, keepdims=True)

    row = pl.BlockSpec((tm, D), lambda i: (i, 0))
    vec = pl.BlockSpec((1, D), lambda i: (0, 0))
    return pl.pallas_call(
        kern, name=name,
        out_shape=[jax.ShapeDtypeStruct((T, D), F32), jax.ShapeDtypeStruct((1, D), F32),
                   jax.ShapeDtypeStruct((1, D), F32)],
        grid=(T // tm,), in_specs=[row, vec, row], out_specs=[row, vec, vec],
        compiler_params=_cparams(("arbitrary",)),
    )(x, g.reshape(1, D), target)


def _shift_down(u, s, L):
    t = lax.broadcasted_iota(jnp.int32, u.shape, 0)
    return jnp.where(t >= s, pltpu.roll(u, s, 0), 0.0)


def _shift_up(u, s, L):
    t = lax.broadcasted_iota(jnp.int32, u.shape, 0)
    return jnp.where(t < L - s, pltpu.roll(u, L - s, 0), 0.0)


def _conv_pre(u, w_ref, b_ref, L):
    pre = u * w_ref[CONV_WIDTH - 1:CONV_WIDTH, :] + b_ref[...]
    for s in range(1, CONV_WIDTH):
        pre = pre + _shift_down(u, s, L) * w_ref[CONV_WIDTH - 1 - s:CONV_WIDTH - s, :]
    return pre


def _conv_fwd(proj, w, b, Bl, L, col0, Dc, *, name):
    cb = LANES
    c0 = col0 // cb

    def kern(u_ref, w_ref, b_ref, o_ref):
        pre = _conv_pre(u_ref[...], w_ref, b_ref, L)
        o_ref[...] = pre * _sigmoid(pre)

    return pl.pallas_call(
        kern, name=name, out_shape=jax.ShapeDtypeStruct((Bl * L, Dc), F32), grid=(Bl, Dc // cb),
        in_specs=[pl.BlockSpec((L, cb), lambda bi, j: (bi, c0 + j)),
                  pl.BlockSpec((CONV_WIDTH, cb), lambda bi, j: (0, j)),
                  pl.BlockSpec((1, cb), lambda bi, j: (0, j))],
        out_specs=pl.BlockSpec((L, cb), lambda bi, j: (bi, j)),
        compiler_params=_cparams(("parallel", "parallel")),
    )(proj, w, b.reshape(1, Dc))


def _conv_bwd(dact, proj, w, b, Bl, L, col0, Dc, *, name):
    cb = LANES
    c0 = col0 // cb

    def kern(d_ref, u_ref, w_ref, b_ref, du_ref, dw_ref, db_ref):
        @pl.when(pl.program_id(1) == 0)
        def _():
            dw_ref[...] = jnp.zeros_like(dw_ref)
            db_ref[...] = jnp.zeros_like(db_ref)

        u = u_ref[...]
        us = [u] + [_shift_down(u, s, L) for s in range(1, CONV_WIDTH)]
        pre = b_ref[...] + us[0] * w_ref[CONV_WIDTH - 1:CONV_WIDTH, :]
        for s in range(1, CONV_WIDTH):
            pre = pre + us[s] * w_ref[CONV_WIDTH - 1 - s:CONV_WIDTH - s, :]
        sig = _sigmoid(pre)
        dpre = d_ref[...] * (sig * (1.0 + pre * (1.0 - sig)))
        du = dpre * w_ref[CONV_WIDTH - 1:CONV_WIDTH, :]
        dw_ref[CONV_WIDTH - 1:CONV_WIDTH, :] += jnp.sum(dpre * u, axis=0, keepdims=True)
        for s in range(1, CONV_WIDTH):
            k = CONV_WIDTH - 1 - s
            du = du + _shift_up(dpre, s, L) * w_ref[k:k + 1, :]
            dw_ref[k:k + 1, :] += jnp.sum(dpre * us[s], axis=0, keepdims=True)
        db_ref[...] += jnp.sum(dpre, axis=0, keepdims=True)
        du_ref[...] = du.astype(du_ref.dtype)

    return pl.pallas_call(
        kern, name=name,
        out_shape=[jax.ShapeDtypeStruct((Bl * L, Dc), BF16),
                   jax.ShapeDtypeStruct((CONV_WIDTH, Dc), F32), jax.ShapeDtypeStruct((1, Dc), F32)],
        grid=(Dc // cb, Bl),
        in_specs=[pl.BlockSpec((L, cb), lambda j, bi: (bi, j)),
                  pl.BlockSpec((L, cb), lambda j, bi: (bi, c0 + j)),
                  pl.BlockSpec((CONV_WIDTH, cb), lambda j, bi: (0, j)),
                  pl.BlockSpec((1, cb), lambda j, bi: (0, j))],
        out_specs=[pl.BlockSpec((L, cb), lambda j, bi: (bi, j)),
                   pl.BlockSpec((CONV_WIDTH, cb), lambda j, bi: (0, j)),
                   pl.BlockSpec((1, cb), lambda j, bi: (0, j))],
        compiler_params=_cparams(("parallel", "arbitrary")),
    )(dact, proj, w, b.reshape(1, Dc))


def _tri_consts():
    r = lax.broadcasted_iota(jnp.int32, (CHUNK, CHUNK), 0)
    c = lax.broadcasted_iota(jnp.int32, (CHUNK, CHUNK), 1)
    lane0 = c < HEAD_DIM
    return r, c, lane0


def _ssd_chunk_common(x, dtr, bias, alog):
    r, c, _ = _tri_consts()
    dt = _softplus(dtr + bias)
    a = dt * (-jnp.exp(alog))
    tril = _mask(c <= r)
    E = _dot_exact_lhs_stacked(tril, a)
    return dt, E, E.T


def _ssd_head_decay(E, ET, h):
    r, c, _ = _tri_consts()
    cm = jnp.broadcast_to(E[:, h * HEAD_DIM:h * HEAD_DIM + 1], (CHUNK, CHUNK))
    rm = jnp.broadcast_to(ET[h * HEAD_DIM:h * HEAD_DIM + 1, :], (CHUNK, CHUNK))
    return jnp.where(c <= r, jnp.exp(jnp.minimum(cm - rm, 0.0)), 0.0)


def _ssd_fwd(xbc, proj, bias_e, alog_e, Bl, L, D, dt_col0, *, name):
    nc = L // CHUNK
    npairs = D // LANES
    ppg = npairs // SSM_GROUPS
    gw = ppg * LANES
    b0 = D // LANES
    c0 = (D + SSM_GROUPS * SSM_STATE) // LANES
    d0 = dt_col0 // gw
    unroll = SSD_UNROLL if nc % SSD_UNROLL == 0 else 1

    def kern(x_ref, b_ref, c_ref, dtr_ref, bias_ref, alog_ref, y_ref, st_ref, h_scr):
        _, _, lane0 = _tri_consts()
        h_scr[...] = jnp.zeros_like(h_scr)

        lanes = [pl.ds(p * LANES, LANES) for p in range(ppg)]

        def one_chunk(ci, hs):
            rows = pl.ds(pl.multiple_of(ci * CHUNK, CHUNK), CHUNK)
            Bb = b_ref[rows, :].astype(BF16)
            Cb = c_ref[rows, :].astype(BF16)
            CB = _dg(Cb, Bb, _NT)
            new = []
            for p, ln in enumerate(lanes):
                x, hprev = x_ref[rows, ln], hs[p]
                dt, E, ET = _ssd_chunk_common(x, dtr_ref[rows, ln], bias_ref[:, ln], alog_ref[:, ln])
                xdt = x * dt
                xdtb = xdt.astype(BF16)
                ys = []
                for h in range(2):
                    Wh = (CB * _ssd_head_decay(E, ET, h)).astype(BF16)
                    ys.append(_dot(Wh, xdtb))
                y = jnp.where(lane0, ys[0], ys[1]) + _dg(Cb, hprev.astype(BF16), _NT) * jnp.exp(E)
                ds = jnp.exp(jnp.broadcast_to(E[CHUNK - 1:CHUNK, :], (CHUNK, LANES)) - E)
                states = _dg((xdt * ds).astype(BF16), Bb, _TN)
                cd = jnp.exp(jnp.broadcast_to(ET[:, CHUNK - 1:CHUNK], (LANES, SSM_STATE)))
                st_ref[ci, p] = hprev
                y_ref[rows, ln] = y
                new.append(hprev * cd + states)
            return new

        def chunks(i, carry):
            hs = [h_scr[p] for p in range(ppg)]
            for u in range(unroll):
                hs = one_chunk(i * unroll + u, hs)
            for p in range(ppg):
                h_scr[p] = hs[p]
            return carry

        lax.fori_loop(0, nc // unroll, chunks, 0)

    return pl.pallas_call(
        kern, name=name,
        out_shape=[jax.ShapeDtypeStruct((Bl * L, D), F32),
                   jax.ShapeDtypeStruct((Bl, SSM_GROUPS, nc, ppg, LANES, SSM_STATE), F32)],
        grid=(Bl, SSM_GROUPS),
        in_specs=[pl.BlockSpec((L, gw), lambda bi, g: (bi, g)),
                  pl.BlockSpec((L, SSM_STATE), lambda bi, g: (bi, b0 + g)),
                  pl.BlockSpec((L, SSM_STATE), lambda bi, g: (bi, c0 + g)),
                  pl.BlockSpec((L, gw), lambda bi, g: (bi, d0 + g)),
                  pl.BlockSpec((1, gw), lambda bi, g: (0, g)),
                  pl.BlockSpec((1, gw), lambda bi, g: (0, g))],
        out_specs=[pl.BlockSpec((L, gw), lambda bi, g: (bi, g)),
                   pl.BlockSpec((None, None, nc, ppg, LANES, SSM_STATE), lambda bi, g: (bi, g, 0, 0, 0, 0))],
        scratch_shapes=[pltpu.VMEM((ppg, LANES, SSM_STATE), F32)],
        compiler_params=_cparams(("parallel", "parallel")),
    )(xbc, xbc, xbc, proj, bias_e, alog_e)


def _ssd_bwd(dy, states, xbc, proj, bias_e, alog_e, dskip_e, Bl, L, D, dt_col0, *, name):
    nc = L // CHUNK
    npairs = D // LANES
    ppg = npairs // SSM_GROUPS
    gw = ppg * LANES
    b0 = D // LANES
    c0 = (D + SSM_GROUPS * SSM_STATE) // LANES
    d0 = dt_col0 // gw
    unroll = SSD_UNROLL if nc % SSD_UNROLL == 0 else 1

    def kern(dy_ref, st_ref, x_ref, b_ref, c_ref, dtr_ref, bias_ref, alog_ref, dsk_ref,
             dx_ref, db_ref, dc_ref, ddt_ref, gbias_ref, galog_ref, gdsk_ref, g_scr):
        r, c, lane0 = _tri_consts()
        m0 = lane0.astype(F32)
        masks = (m0, 1.0 - m0)
        triu = _mask(c >= r)
        trils = _mask(c < r)
        ones = jnp.ones((CHUNK, LANES), BF16)
        g_scr[...] = jnp.zeros_like(g_scr)

        @pl.when(pl.program_id(1) == 0)
        def _():
            gbias_ref[...] = jnp.zeros_like(gbias_ref)
            galog_ref[...] = jnp.zeros_like(galog_ref)
            gdsk_ref[...] = jnp.zeros_like(gdsk_ref)

        lanes = [pl.ds(p * LANES, LANES) for p in range(ppg)]

        def one_chunk(ci, gs):
            rows = pl.ds(pl.multiple_of(ci * CHUNK, CHUNK), CHUNK)
            Bb = b_ref[rows, :].astype(BF16)
            Cb = c_ref[rows, :].astype(BF16)
            CB = _dg(Cb, Bb, _NT)
            gsum = jnp.zeros((CHUNK, CHUNK), F32)
            dB = jnp.zeros((CHUNK, SSM_STATE), F32)
            dC = jnp.zeros((CHUNK, SSM_STATE), F32)
            ins = [(x_ref[rows, ln], dtr_ref[rows, ln], bias_ref[:, ln], alog_ref[:, ln], dy_ref[rows, ln],
                    st_ref[ci, p], gs[p], dsk_ref[:, ln]) for p, ln in enumerate(lanes)]
            outs = []
            for x, dtr, bias, alog, dyv, hprev, g, dsk in ins:
                dt, E, ET = _ssd_chunk_common(x, dtr, bias, alog)
                xdt = x * dt
                xdtb = xdt.astype(BF16)
                dyb = dyv.astype(BF16)
                hpb = hprev.astype(BF16)
                gb = g.astype(BF16)
                sd = jnp.exp(E)
                ds = jnp.exp(jnp.broadcast_to(E[CHUNK - 1:CHUNK, :], (CHUNK, LANES)) - E)
                t1 = []
                dE = dyv * (_dg(Cb, hpb, _NT) * sd)
                for h in range(2):
                    Lh = _ssd_head_decay(E, ET, h)
                    Wh = (CB * Lh).astype(BF16)
                    t1.append(_dg(Wh, dyb, _TN))
                    Gh = Lh * _dg((dyv * masks[h]).astype(BF16), xdtb, _NT)
                    gsum = gsum + Gh
                    Qh = Gh * CB
                    dE = dE + _dot_exact_rhs(Qh - Qh.T, _mask(c == h * HEAD_DIM))
                dxst = _dg(Bb, gb, _NT) * ds
                dxdt = jnp.where(lane0, t1[0], t1[1]) + dxst
                dysd = (dyv * sd).astype(BF16)
                dC = dC + _dot(dysd, hpb)
                dB = dB + _dot((xdt * ds).astype(BF16), gb)
                cd = jnp.exp(jnp.broadcast_to(ET[:, CHUNK - 1:CHUNK], (LANES, SSM_STATE)))
                gnew = g * cd + _dg(dysd, Cb, _TN)
                nn = (((1,), (0,)), ((), ()))
                da = (_dot_exact_lhs(triu, dE, nn) + _dot_exact_lhs(trils, xdt * dxst, nn)
                      + _dot_exact_lhs(ones, g * cd * hprev, _NT))
                aneg = -jnp.exp(alog)
                ddt = da * aneg + dxdt * x
                ddtr = ddt * _sigmoid(dtr + bias)
                outs.append((gnew, dxdt * dt + dyv * dsk, ddtr, jnp.sum(ddtr, axis=0, keepdims=True),
                             jnp.sum(da * dt, axis=0, keepdims=True) * aneg,
                             jnp.sum(dyv * x, axis=0, keepdims=True)))
            for p, (gnew, dx, ddtr, sb, sa, sk) in enumerate(outs):
                dx_ref[rows, lanes[p]] = dx
                ddt_ref[rows, lanes[p]] = ddtr.astype(ddt_ref.dtype)
                gbias_ref[:, lanes[p]] += sb
                galog_ref[:, lanes[p]] += sa
                gdsk_ref[:, lanes[p]] += sk
            gsb = gsum.astype(BF16)
            db_ref[rows, :] = dB + _dg(gsb, Cb, _TN)
            dc_ref[rows, :] = dC + _dot(gsb, Bb)
            return [o[0] for o in outs]

        def chunks(i, carry):
            gs = [g_scr[p] for p in range(ppg)]
            for u in range(unroll):
                gs = one_chunk(nc - 1 - (i * unroll + u), gs)
            for p in range(ppg):
                g_scr[p] = gs[p]
            return carry

        lax.fori_loop(0, nc // unroll, chunks, 0)

    T = Bl * L
    xspec = pl.BlockSpec((L, gw), lambda g, bi: (bi, g))
    nspec = pl.BlockSpec((L, SSM_STATE), lambda g, bi: (bi, g))
    vspec = pl.BlockSpec((1, gw), lambda g, bi: (0, g))
    return pl.pallas_call(
        kern, name=name,
        out_shape=[jax.ShapeDtypeStruct((T, D), F32),
                   jax.ShapeDtypeStruct((T, SSM_GROUPS * SSM_STATE), F32),
                   jax.ShapeDtypeStruct((T, SSM_GROUPS * SSM_STATE), F32),
                   jax.ShapeDtypeStruct((T, D), BF16),
                   jax.ShapeDtypeStruct((1, D), F32), jax.ShapeDtypeStruct((1, D), F32),
                   jax.ShapeDtypeStruct((1, D), F32)],
        grid=(SSM_GROUPS, Bl),
        in_specs=[xspec,
                  pl.BlockSpec((None, None, nc, ppg, LANES, SSM_STATE), lambda g, bi: (bi, g, 0, 0, 0, 0)),
                  xspec,
                  pl.BlockSpec((L, SSM_STATE), lambda g, bi: (bi, b0 + g)),
                  pl.BlockSpec((L, SSM_STATE), lambda g, bi: (bi, c0 + g)),
                  pl.BlockSpec((L, gw), lambda g, bi: (bi, d0 + g)),
                  vspec, vspec, vspec],
        out_specs=[xspec, nspec, nspec, xspec, vspec, vspec, vspec],
        scratch_shapes=[pltpu.VMEM((ppg, LANES, SSM_STATE), F32)],
        compiler_params=_cparams(("parallel", "arbitrary")),
    )(dy, states, xbc, xbc, xbc, proj, bias_e, alog_e, dskip_e)


def _att_nl(z):
    return jnp.maximum(z, 0.0) + jnp.log(1.0 + jnp.exp(-jnp.abs(z)))


def _sub_pred(pred):
    r = lax.broadcasted_iota(jnp.int32, (ATT_SUB, ATT_SUB), 0)
    c = lax.broadcasted_iota(jnp.int32, (ATT_SUB, ATT_SUB), 1)
    return pred(r, c)


def _cumsum_mm(x, tri):
    return _dot(x.astype(BF16), tri)


def _att_slices(q0, roff, nrows, k0, nsub):
    qs = pl.ds(pl.multiple_of(q0 + roff, ATT_SUB), nrows)
    ks = pl.ds(pl.multiple_of(k0, ATT_SUB), nsub * ATT_SUB)
    return qs, ks, pl.ds(roff, nrows)


def _attn_fwd(proj, Bl, L, D, qc, kc, vc, *, name):
    npairs = D // LANES
    nq = L // ATT_BLOCK
    nsb = ATT_BLOCK // ATT_SUB
    scale = HEAD_DIM ** -0.5

    def kern(q_ref, k_ref, v_ref, o_ref, tot_ref, qm, kb, vb, carry, acc):
        tri = _sub_pred(lambda t, s: t > s)
        lrow = lax.broadcasted_iota(jnp.int32, (1, LANES), 1) < HEAD_DIM
        q = q_ref[...] * scale
        qm[0] = jnp.where(lrow, q, 0.0).astype(BF16)
        qm[1] = jnp.where(lrow, 0.0, q).astype(BF16)
        kb[...] = k_ref[...].astype(BF16)
        vb[...] = v_ref[...].astype(BF16)
        mat = _mask(_sub_pred(lambda j, s: j >= s))

        def tile(q0, roff, nrows, k0, nsub, diag):
            qs, ks, rows = _att_slices(q0, roff, nrows, k0, nsub)
            for h in range(2):
                z = _dg(qm[h, qs, :], kb[ks, :], _NT)
                nl = _att_nl(z)
                cy = carry[h, rows, :]
                atts = []
                for b in reversed(range(nsub)):
                    cols = slice(b * ATT_SUB, (b + 1) * ATT_SUB)
                    masked = diag and b == nsub - 1
                    x = jnp.where(tri, nl[:, cols], 0.0) if masked else nl[:, cols]
                    inc = _cumsum_mm(x, mat)
                    a = jnp.exp(z[:, cols] - inc - cy)
                    atts.append((jnp.where(tri, a, 0.0) if masked else a).astype(BF16))
                    cy = cy + inc[:, 0:1]
                att = jnp.concatenate(atts[::-1], axis=1) if nsub > 1 else atts[0]
                acc[h, rows, :] += _dot(att, vb[ks, :])
                carry[h, rows, :] = cy

        def qblock(qi, _):
            q0 = qi * ATT_BLOCK
            qs = pl.ds(pl.multiple_of(q0, ATT_BLOCK), ATT_BLOCK)
            carry[...] = jnp.zeros_like(carry)
            acc[...] = jnp.zeros_like(acc)
            for rb in range(nsb):
                tile(q0, rb * ATT_SUB, ATT_SUB, q0, rb + 1, True)

            def kblock(j, _):
                tile(q0, 0, ATT_BLOCK, (qi - 1 - j) * ATT_BLOCK, nsb, False)
                return 0

            lax.fori_loop(0, qi, kblock, 0)
            o_ref[qs, :] = jnp.where(lrow, acc[0], acc[1])
            for h in range(2):
                tot_ref[h, qs, :] = jnp.broadcast_to(carry[h], (ATT_BLOCK, LANES))
            return 0

        lax.fori_loop(0, nq, qblock, 0)

    spec = lambda col: pl.BlockSpec((L, LANES), lambda bi, p: (bi, col // LANES + p))
    return pl.pallas_call(
        kern, name=name,
        out_shape=[jax.ShapeDtypeStruct((Bl * L, D), F32),
                   jax.ShapeDtypeStruct((Bl, npairs, 2, L, LANES), F32)],
        grid=(Bl, npairs),
        in_specs=[spec(qc), spec(kc), spec(vc)],
        out_specs=[pl.BlockSpec((L, LANES), lambda bi, p: (bi, p)),
                   pl.BlockSpec((None, None, 2, L, LANES), lambda bi, p: (bi, p, 0, 0, 0))],
        scratch_shapes=[pltpu.VMEM((2, L, LANES), BF16), pltpu.VMEM((L, LANES), BF16),
                        pltpu.VMEM((L, LANES), BF16), pltpu.VMEM((2, ATT_BLOCK, 1), F32),
                        pltpu.VMEM((2, ATT_BLOCK, LANES), F32)],
        compiler_params=_cparams(("parallel", "parallel")),
    )(proj, proj, proj)


def _attn_bwd(dout, tot, proj, Bl, L, D, qc, kc, vc, *, name):
    npairs = D // LANES
    nq = L // ATT_BLOCK
    nsb = ATT_BLOCK // ATT_SUB
    scale = HEAD_DIM ** -0.5

    def kern(do_ref, tot_ref, q_ref, k_ref, v_ref, dq_ref, dk_ref, dv_ref, qm, km, kb, vb, dom, dkacc, dvacc,
             pre, pde, dqacc):
        tri = _sub_pred(lambda t, s: t > s)
        lrow = lax.broadcasted_iota(jnp.int32, (1, LANES), 1) < HEAD_DIM
        q = q_ref[...] * scale
        qm[0] = jnp.where(lrow, q, 0.0).astype(BF16)
        qm[1] = jnp.where(lrow, 0.0, q).astype(BF16)
        k = k_ref[...]
        kb[...] = k.astype(BF16)
        km[0] = jnp.where(lrow, k, 0.0).astype(BF16)
        km[1] = jnp.where(lrow, 0.0, k).astype(BF16)
        vb[...] = v_ref[...].astype(BF16)
        do = do_ref[...]
        dom[0] = jnp.where(lrow, do, 0.0).astype(BF16)
        dom[1] = jnp.where(lrow, 0.0, do).astype(BF16)
        dkacc[...] = jnp.zeros_like(dkacc)
        dvacc[...] = jnp.zeros_like(dvacc)
        mat = _mask(_sub_pred(lambda j, s: j <= s))

        def tile(q0, roff, nrows, k0, nsub, diag):
            qs, ks, rows = _att_slices(q0, roff, nrows, k0, nsub)
            for h in range(2):
                z = _dg(qm[h, qs, :], kb[ks, :], _NT)
                nl = _att_nl(z)
                lb = z - nl
                datt = _dg(dom[h, qs, :], vb[ks, :], _NT)
                tn = tot_ref[h, qs, 0:1]
                base = pre[h, rows, :] - tn
                pe = pde[h, rows, :]
                atts, dzs = [], []
                for b in range(nsub):
                    cols = slice(b * ATT_SUB, (b + 1) * ATT_SUB)
                    masked = diag and b == nsub - 1
                    x = jnp.where(tri, nl[:, cols], 0.0) if masked else nl[:, cols]
                    pin = _cumsum_mm(x, mat)
                    att = jnp.exp(lb[:, cols] + base + pin)
                    if masked:
                        att = jnp.where(tri, att, 0.0)
                    dE = att * datt[:, cols]
                    pde_in = _cumsum_mm(dE, mat)
                    dz = dE - jnp.exp(lb[:, cols]) * (pe + pde_in)
                    if masked:
                        dz = jnp.where(tri, dz, 0.0)
                    atts.append(att.astype(BF16))
                    dzs.append(dz.astype(BF16))
                    base = base + pin[:, ATT_SUB - 1:ATT_SUB]
                    pe = pe + pde_in[:, ATT_SUB - 1:ATT_SUB]
                pre[h, rows, :] = base + tn
                pde[h, rows, :] = pe
                attb = jnp.concatenate(atts, axis=1) if nsub > 1 else atts[0]
                dzb = jnp.concatenate(dzs, axis=1) if nsub > 1 else dzs[0]
                dqacc[rows, :] += _dot(dzb, km[h, ks, :])
                dkacc[ks, :] += _dg(dzb, qm[h, qs, :], _TN)
                dvacc[ks, :] += _dg(attb, dom[h, qs, :], _TN)

        def qblock(qi, _):
            q0 = qi * ATT_BLOCK
            qs = pl.ds(pl.multiple_of(q0, ATT_BLOCK), ATT_BLOCK)
            pre[...] = jnp.zeros_like(pre)
            pde[...] = jnp.zeros_like(pde)
            dqacc[...] = jnp.zeros_like(dqacc)

            def kblock(kj, _):
                tile(q0, 0, ATT_BLOCK, kj * ATT_BLOCK, nsb, False)
                return 0

            lax.fori_loop(0, qi, kblock, 0)
            for rb in range(nsb):
                tile(q0, rb * ATT_SUB, ATT_SUB, q0, rb + 1, True)
            dq_ref[qs, :] = (dqacc[...] * scale).astype(dq_ref.dtype)
            return 0

        lax.fori_loop(0, nq, qblock, 0)
        dk_ref[...] = dkacc[...].astype(dk_ref.dtype)
        dv_ref[...] = dvacc[...].astype(dv_ref.dtype)

    spec = lambda col: pl.BlockSpec((L, LANES), lambda bi, p: (bi, col // LANES + p))
    ospec = pl.BlockSpec((L, LANES), lambda bi, p: (bi, p))
    T = Bl * L
    return pl.pallas_call(
        kern, name=name,
        out_shape=[jax.ShapeDtypeStruct((T, D), BF16)] * 3,
        grid=(Bl, npairs),
        in_specs=[ospec, pl.BlockSpec((None, None, 2, L, LANES), lambda bi, p: (bi, p, 0, 0, 0)),
                  spec(qc), spec(kc), spec(vc)],
        out_specs=[ospec, ospec, ospec],
        scratch_shapes=[pltpu.VMEM((2, L, LANES), BF16), pltpu.VMEM((2, L, LANES), BF16),
                        pltpu.VMEM((L, LANES), BF16), pltpu.VMEM((L, LANES), BF16),
                        pltpu.VMEM((2, L, LANES), BF16), pltpu.VMEM((L, LANES), F32),
                        pltpu.VMEM((L, LANES), F32), pltpu.VMEM((2, ATT_BLOCK, 1), F32),
                        pltpu.VMEM((2, ATT_BLOCK, 1), F32), pltpu.VMEM((ATT_BLOCK, LANES), F32)],
        compiler_params=_cparams(("parallel", "parallel")),
    )(dout, tot, proj, proj, proj)


def _mix_fwd(y, xbc, proj, yatt, dskip_e, g_ssd, g_att, D, *, name):
    T = y.shape[0]
    tm = _blk(T, 256)

    def kern(y_ref, xs_ref, z_ref, ya_ref, dsk_ref, gs_ref, ga_ref, o_ref):
        z = z_ref[...]
        u = (y_ref[...] + xs_ref[...] * dsk_ref[...]) * (z * _sigmoid(z))
        rs = lax.rsqrt(jnp.mean(u * u, axis=-1, keepdims=True) + EPS)
        o_ref[:, :D] = (u * rs * gs_ref[...]).astype(o_ref.dtype)
        ya = ya_ref[...]
        ra = lax.rsqrt(jnp.mean(ya * ya, axis=-1, keepdims=True) + EPS)
        o_ref[:, D:] = (ya * ra * ga_ref[...]).astype(o_ref.dtype)

    row = pl.BlockSpec((tm, D), lambda i: (i, 0))
    vec = pl.BlockSpec((1, D), lambda i: (0, 0))
    return pl.pallas_call(
        kern, name=name, out_shape=jax.ShapeDtypeStruct((T, 2 * D), BF16), grid=(T // tm,),
        in_specs=[row, row, row, row, vec, vec, vec],
        out_specs=pl.BlockSpec((tm, 2 * D), lambda i: (i, 0)),
        compiler_params=_cparams(("parallel",)),
    )(y, xbc, proj, yatt, dskip_e, g_ssd.reshape(1, D), g_att.reshape(1, D))


def _mix_bwd(dmix, y, xbc, proj, yatt, dskip_e, g_ssd, g_att, D, *, name):
    T = y.shape[0]
    tm = _blk(T, 256)

    def kern(d_ref, y_ref, xs_ref, z_ref, ya_ref, dsk_ref, gs_ref, ga_ref,
             dz_ref, dy_ref, dya_ref, dgs_ref, dga_ref):
        @pl.when(pl.program_id(0) == 0)
        def _():
            dgs_ref[...] = jnp.zeros_like(dgs_ref)
            dga_ref[...] = jnp.zeros_like(dga_ref)

        z = z_ref[...]
        sig = _sigmoid(z)
        sz = z * sig
        ytot = y_ref[...] + xs_ref[...] * dsk_ref[...]
        u = ytot * sz
        rs = lax.rsqrt(jnp.mean(u * u, axis=-1, keepdims=True) + EPS)
        un = u * rs
        do = d_ref[:, :D]
        dog = do * gs_ref[...]
        du = rs * (dog - un * jnp.mean(dog * un, axis=-1, keepdims=True))
        dgs_ref[...] += jnp.sum(do * un, axis=0, keepdims=True)
        dy_ref[...] = du * sz
        dz_ref[...] = (du * ytot * (sig * (1.0 + z * (1.0 - sig)))).astype(dz_ref.dtype)
        ya = ya_ref[...]
        ra = lax.rsqrt(jnp.mean(ya * ya, axis=-1, keepdims=True) + EPS)
        yan = ya * ra
        da = d_ref[:, D:]
        dag = da * ga_ref[...]
        dya_ref[...] = ra * (dag - yan * jnp.mean(dag * yan, axis=-1, keepdims=True))
        dga_ref[...] += jnp.sum(da * yan, axis=0, keepdims=True)

    row = pl.BlockSpec((tm, D), lambda i: (i, 0))
    vec = pl.BlockSpec((1, D), lambda i: (0, 0))
    return pl.pallas_call(
        kern, name=name,
        out_shape=[jax.ShapeDtypeStruct((T, D), BF16), jax.ShapeDtypeStruct((T, D), F32),
                   jax.ShapeDtypeStruct((T, D), F32), jax.ShapeDtypeStruct((1, D), F32),
                   jax.ShapeDtypeStruct((1, D), F32)],
        grid=(T // tm,),
        in_specs=[pl.BlockSpec((tm, 2 * D), lambda i: (i, 0)), row, row, row, row, vec, vec, vec],
        out_specs=[row, row, row, vec, vec],
        compiler_params=_cparams(("arbitrary",)),
    )(dmix, y, xbc, proj, yatt, dskip_e, g_ssd.reshape(1, D), g_att.reshape(1, D))


def _head_sum_mat(D):
    i = lax.broadcasted_iota(jnp.int32, (D, LANES), 0)
    c = lax.broadcasted_iota(jnp.int32, (D, LANES), 1)
    return _mask((i >= c * HEAD_DIM) & (i < (c + 1) * HEAD_DIM))


def _collapse_heads(x, *, name):
    R, D = x.shape
    tm = _blk(R, 256)

    def kern(x_ref, o_ref):
        o_ref[...] = _dot_exact_rhs(x_ref[...], _head_sum_mat(D))

    return pl.pallas_call(
        kern, name=name, out_shape=jax.ShapeDtypeStruct((R, LANES), F32), grid=(R // tm,),
        in_specs=[pl.BlockSpec((tm, D), lambda i: (i, 0))],
        out_specs=pl.BlockSpec((tm, LANES), lambda i: (i, 0)),
        compiler_params=_cparams(("parallel",)),
    )(x)


def _adam_math(w, g, m, v):
    m = ADAM_B1 * m + (1.0 - ADAM_B1) * g
    v = ADAM_B2 * v + (1.0 - ADAM_B2) * (g * g)
    m_hat = m / (1.0 - ADAM_B1 ** ADAM_STEP)
    v_hat = v / (1.0 - ADAM_B2 ** ADAM_STEP)
    delta = -ADAM_LR * (m_hat / (jnp.sqrt(v_hat) + ADAM_EPS) + ADAM_WD * w)
    return delta, m, v


def _adamw_sharded(pieces, w, m, v, *, name):
    R, C = w.shape
    tr = _blk(R, 256) if R % 8 == 0 else R

    def kern(p_ref, w_ref, m_ref, v_ref, g_ref, d_ref, nm_ref, nv_ref):
        g = p_ref[0].astype(F32)
        for j in range(1, N_DEV):
            g = g + p_ref[j].astype(F32)
        d, nm, nv = _adam_math(w_ref[...], g, m_ref[...], v_ref[...])
        g_ref[...] = g
        d_ref[...] = d
        nm_ref[...] = nm
        nv_ref[...] = nv

    row = pl.BlockSpec((tr, C), lambda i: (i, 0))
    return pl.pallas_call(
        kern, name=name, out_shape=[jax.ShapeDtypeStruct((R, C), F32)] * 4, grid=(R // tr,),
        in_specs=[pl.BlockSpec((N_DEV, tr, C), lambda i: (0, i, 0)), row, row, row],
        out_specs=[row] * 4,
        compiler_params=_cparams(("parallel",)),
    )(pieces, w, m, v)


def _small_update(parts, wd, md, vd, we, me, ve, nd, ne, D, *, name):
    def kern(p_ref, wd_ref, md_ref, vd_ref, we_ref, me_ref, ve_ref,
             gd_ref, dd_ref, nmd_ref, nvd_ref, ge_ref, de_ref, nme_ref, nve_ref, loss_ref):
        s = p_ref[0]
        for j in range(1, N_DEV):
            s = s + p_ref[j]
        gd = s[:nd]
        d, nm, nv = _adam_math(wd_ref[...], gd, md_ref[...], vd_ref[...])
        gd_ref[...] = gd
        dd_ref[...] = d
        nmd_ref[...] = nm
        nvd_ref[...] = nv
        ge = _dot_exact_rhs(s[nd:nd + ne], _head_sum_mat(D))
        d, nm, nv = _adam_math(we_ref[...], ge, me_ref[...], ve_ref[...])
        ge_ref[...] = ge
        de_ref[...] = d
        nme_ref[...] = nm
        nve_ref[...] = nv
        tot = jnp.sum(jnp.sum(s[nd + ne:], axis=1, keepdims=True), axis=0, keepdims=True)
        loss_ref[...] = jnp.broadcast_to(tot, loss_ref.shape)

    vm = pl.BlockSpec(memory_space=pltpu.VMEM)
    return pl.pallas_call(
        kern, name=name,
        out_shape=[jax.ShapeDtypeStruct((nd, D), F32)] * 4 + [jax.ShapeDtypeStruct((ne, LANES), F32)] * 4
        + [jax.ShapeDtypeStruct((8, LANES), F32)],
        in_specs=[vm] * 7, out_specs=[vm] * 9,
        compiler_params=pltpu.CompilerParams(vmem_limit_bytes=VMEM_LIMIT),
    )(parts, wd, md, vd, we, me, ve)


def _dev_index(p):
    return 4 * p[0] + 2 * p[1] + p[2]


def _all_gather(arrs, *, name):
    n = len(arrs)

    def body(*refs):
        xs, outs = refs[:n], refs[n:2 * n]
        send, recv, loc = refs[2 * n:]
        x, y, c = lax.axis_index("x"), lax.axis_index("y"), lax.axis_index("c")
        me, sib = (x, y, c), (x, y, 1 - c)
        chips = [(1 - x, y), (x, 1 - y), (1 - x, 1 - y)]

        def cp(k, s, block, to, src=None):
            dst = outs[k].at[_dev_index(block)]
            return pltpu.make_async_remote_copy(
                src_ref=dst if src is None else src, dst_ref=dst,
                send_sem=send.at[7 * k + s], recv_sem=recv.at[7 * k + s],
                device_id=to, device_id_type=pl.DeviceIdType.MESH)

        mine = [pltpu.make_async_copy(xs[k], outs[k].at[_dev_index(me)], loc.at[k]) for k in range(n)]
        for m in mine:
            m.start()
        first = []
        for k in range(n):
            first.append(cp(k, 0, me, sib, src=xs[k]))
            for j, ch in enumerate(chips):
                first.append(cp(k, 1 + j, me, (*ch, c), src=xs[k]))
        for f in first:
            f.start()
        passed = []
        for j, ch in enumerate(chips):
            for k in range(n):
                cp(k, 1 + j, (*ch, c), me).wait_recv()
                p = cp(k, 4 + j, (*ch, c), sib)
                p.start()
                passed.append(p)
        for k in range(n):
            cp(k, 0, sib, me).wait_recv()
            for j, ch in enumerate(chips):
                cp(k, 4 + j, (*ch, 1 - c), me).wait_recv()
        for f in first + passed:
            f.wait_send()
        for m in mine:
            m.wait()

    hbm = pl.BlockSpec(memory_space=pl.ANY)
    return pl.pallas_call(
        body, name=name,
        out_shape=[jax.ShapeDtypeStruct((N_DEV,) + a.shape, a.dtype) for a in arrs],
        in_specs=[hbm] * n, out_specs=[hbm] * n,
        scratch_shapes=[pltpu.SemaphoreType.DMA((7 * n,)), pltpu.SemaphoreType.DMA((7 * n,)),
                        pltpu.SemaphoreType.DMA((n,))],
    )(*arrs)


_RELS = [(fx, fy, fc) for fx in (0, 1) for fy in (0, 1) for fc in (0, 1) if fx or fy or fc]


def _direct_ops(src_of, dst_of, n_in):
    def make(ins, outs, send, recv, loc):
        xs, n = ins[:n_in], len(outs)
        x, y, c = lax.axis_index("x"), lax.axis_index("y"), lax.axis_index("c")
        me = _dev_index((x, y, c))
        peers = [(1 - x if r[0] else x, 1 - y if r[1] else y, 1 - c if r[2] else c) for r in _RELS]

        def remote(k, s, src, dst):
            return pltpu.make_async_remote_copy(
                src_ref=src, dst_ref=dst, send_sem=send.at[7 * k + s], recv_sem=recv.at[7 * k + s],
                device_id=peers[s], device_id_type=pl.DeviceIdType.MESH)

        def local(k):
            return pltpu.make_async_copy(src_of(xs, k, me, me), dst_of(outs, k, me), loc.at[k])

        def sends():
            return [remote(k, s, src_of(xs, k, me, _dev_index(p)), dst_of(outs, k, me))
                    for k in range(n) for s, p in enumerate(peers)]

        def start():
            for k in range(n):
                local(k).start()
            for cp in sends():
                cp.start()

        def finish():
            for k in range(n):
                for s, p in enumerate(peers):
                    slot = dst_of(outs, k, _dev_index(p))
                    remote(k, s, slot, slot).wait_recv()
            for cp in sends():
                cp.wait_send()
            for k in range(n):
                local(k).wait()

        return start, finish

    return make


def _gather_comm(arrs):
    make = _direct_ops(lambda xs, k, me, p: xs[k], lambda outs, k, j: outs[k].at[j], len(arrs))
    return make, list(arrs), [jax.ShapeDtypeStruct((N_DEV,) + a.shape, a.dtype) for a in arrs], {}


def _exchange_comm(pieces, bufs, layer):
    n = len(pieces)
    make = _direct_ops(lambda xs, k, me, p: xs[k].at[p], lambda outs, k, j: outs[k].at[j, layer], n)
    return (make, list(pieces) + list(bufs), [jax.ShapeDtypeStruct(b.shape, b.dtype) for b in bufs],
            {n + k: k for k in range(n)})


def _comm_call(comm, *, name):
    make, ins, out_shapes, alias = comm
    n_ci, n_co = len(ins), len(out_shapes)

    def body(*refs):
        start, finish = make(refs[:n_ci], refs[n_ci:n_ci + n_co], *refs[n_ci + n_co:])
        start()
        finish()

    hbm = pl.BlockSpec(memory_space=pl.ANY)
    return pl.pallas_call(
        body, name=name, out_shape=list(out_shapes), in_specs=[hbm] * n_ci, out_specs=[hbm] * n_co,
        scratch_shapes=[pltpu.SemaphoreType.DMA((7 * n_co,)), pltpu.SemaphoreType.DMA((7 * n_co,)),
                        pltpu.SemaphoreType.DMA((n_co,))],
        input_output_aliases=alias,
    )(*ins)


def _layer_fwd(x, P, dims, l, nxt):
    Bl, L, D = dims
    Dc = D + 2 * SSM_GROUPS * SSM_STATE
    qc, kc, vc, dtc = D + Dc, 2 * D + Dc, 3 * D + Dc, 4 * D + Dc
    gat = (lambda key: _gather_comm([nxt[key]])) if nxt is not None else (lambda key: None)
    h = _rmsnorm_fwd(x, P["norm_mix_g"], name=f"norm_mix_fwd_{l}")
    (proj,), g_in = _mm(h, P["w_in"], out_dtypes=[F32], name=f"in_proj_{l}", comm=gat("w_in"))
    xbc = _conv_fwd(proj, P["conv_w"], P["conv_b"], Bl, L, D, Dc, name=f"conv_fwd_{l}")
    y, states = _ssd_fwd(xbc, proj, P["dt_bias_e"], P["a_log_e"], Bl, L, D, dtc, name=f"ssd_fwd_{l}")
    yatt, tot = _attn_fwd(proj, Bl, L, D, qc, kc, vc, name=f"attn_fwd_{l}")
    ymix = _mix_fwd(y, xbc, proj, yatt, P["d_skip_e"], P["ssd_norm_g"], P["att_norm_g"], D, name=f"mix_fwd_{l}")
    (x1,), g_out = _mm(ymix, P["w_out"], out_dtypes=[F32], epilogue=lambda r, res: (r + res,), extras=(x,),
                       name=f"out_proj_{l}", comm=gat("w_out"))
    h2 = _rmsnorm_fwd(x1, P["norm_mlp_g"], name=f"norm_mlp_fwd_{l}")

    def relu2(r):
        a = jnp.maximum(r, 0.0)
        return r, a * a

    (u, act), g_up = _mm(h2, P["w_up"], out_dtypes=[F32, BF16], epilogue=relu2, name=f"mlp_up_{l}",
                         comm=gat("w_up"))
    (x2,), g_down = _mm(act, P["w_down"], out_dtypes=[F32], epilogue=lambda r, res: (r + res,), extras=(x1,),
                        name=f"mlp_down_{l}", comm=gat("w_down"))
    saved = dict(x=x, h=h, proj=proj, xbc=xbc, y=y, states=states, yatt=yatt, tot=tot, ymix=ymix,
                 x1=x1, h2=h2, u=u, act=act)
    gathered = dict(w_in=g_in[0], w_out=g_out[0], w_up=g_up[0], w_down=g_down[0]) if nxt is not None else None
    return x2, saved, gathered


def _layer_bwd(dx2, S, P, dims, l, pend, bufs, own_pieces=None):
    Bl, L, D = dims
    Dc = D + 2 * SSM_GROUPS * SSM_STATE
    qc, kc, vc, dtc = D + Dc, 2 * D + Dc, 3 * D + Dc, 4 * D + Dc
    G = {}

    def exch(keys):
        if pend is None:
            return None
        return _exchange_comm([pend[k] for k in keys], [bufs[k] for k in keys], l + 1)

    def took(keys, outs):
        if pend is not None:
            bufs.update(zip(keys, outs))

    (du,), o = _mm(dx2, P["w_down_t"], out_dtypes=[BF16], extras=(S["u"],),
                   epilogue=lambda r, u: (r * (2.0 * jnp.maximum(u, 0.0)),), name=f"mlp_down_bwd_{l}",
                   comm=exch(["w_in"]))
    took(["w_in"], o)
    (G["w_down"],), o = _mm(S["act"], dx2, ta=True, out_dtypes=[F32], name=f"mlp_down_dw_{l}",
                            comm=exch(["w_up"]))
    took(["w_up"], o)
    (dh2,), o = _mm(du, P["w_up_t"], out_dtypes=[F32], name=f"mlp_up_bwd_{l}", comm=exch(["w_down"]))
    took(["w_down"], o)
    (G["w_up"],), o = _mm(S["h2"], du, ta=True, out_dtypes=[F32], name=f"mlp_up_dw_{l}",
                          comm=exch(["w_out", "conv_w"]))
    took(["w_out", "conv_w"], o)
    dx1, G["norm_mlp_g"] = _rmsnorm_bwd(dh2, S["x1"], P["norm_mlp_g"], dx2, name=f"norm_mlp_bwd_{l}")
    (dmix,), _ = _mm(dx1, P["w_out_t"], out_dtypes=[F32], name=f"out_proj_bwd_{l}")
    (G["w_out"],), _ = _mm(S["ymix"], dx1, ta=True, out_dtypes=[F32], name=f"out_proj_dw_{l}")
    dz, dy, dyatt, G["ssd_norm_g"], G["att_norm_g"] = _mix_bwd(
        dmix, S["y"], S["xbc"], S["proj"], S["yatt"], P["d_skip_e"], P["ssd_norm_g"], P["att_norm_g"], D,
        name=f"mix_bwd_{l}")
    dq, dk, dv = _attn_bwd(dyatt, S["tot"], S["proj"], Bl, L, D, qc, kc, vc, name=f"attn_bwd_{l}")
    dxs, dB, dC, ddt, G["dt_bias_e"], G["a_log_e"], G["d_skip_e"] = _ssd_bwd(
        dy, S["states"], S["xbc"], S["proj"], P["dt_bias_e"], P["a_log_e"], P["d_skip_e"],
        Bl, L, D, dtc, name=f"ssd_bwd_{l}")
    dact = jnp.concatenate([dxs, dB, dC], axis=1)
    dxbc, G["conv_w"], G["conv_b"] = _conv_bwd(dact, S["proj"], P["conv_w"], P["conv_b"], Bl, L, D, Dc,
                                               name=f"conv_bwd_{l}")
    dproj = jnp.concatenate([dz, dxbc, dq, dk, dv, ddt], axis=1)
    own = own_pieces(G) if own_pieces is not None else None

    def exch_own(keys):
        return None if own is None else _exchange_comm([own[k] for k in keys], [bufs[k] for k in keys], l)

    (dh,), o = _mm(dproj, P["w_in_t"], out_dtypes=[F32], name=f"in_proj_bwd_{l}", comm=exch_own(["w_down"]))
    if own is not None:
        bufs.update(zip(["w_down"], o))
    (G["w_in"],), o = _mm(S["h"], dproj, ta=True, out_dtypes=[F32], name=f"in_proj_dw_{l}",
                          comm=exch_own(["w_up", "w_out", "conv_w"]))
    if own is not None:
        bufs.update(zip(["w_up", "w_out", "conv_w"], o))
    dx, G["norm_mix_g"] = _rmsnorm_bwd(dh, S["x"], P["norm_mix_g"], dx1, name=f"norm_mix_bwd_{l}")
    return dx, G, bufs


def _pad_rows(a, rows):
    return jnp.concatenate([a, jnp.zeros((rows - a.shape[0],) + a.shape[1:], a.dtype)], axis=0)


def kernel(x, norm_mix_g, w_in, conv_w, conv_b, dt_bias, a_log, d_skip, ssd_norm_g, att_norm_g, w_out, norm_mlp_g, w_up, w_down, final_norm_g, loss_target, m_norm_mix_g, m_w_in, m_conv_w, m_conv_b, m_dt_bias, m_a_log, m_d_skip, m_ssd_norm_g, m_att_norm_g, m_w_out, m_norm_mlp_g, m_w_up, m_w_down, m_final_norm_g, v_norm_mix_g, v_w_in, v_conv_w, v_conv_b, v_dt_bias, v_a_log, v_d_skip, v_ssd_norm_g, v_att_norm_g, v_w_out, v_norm_mlp_g, v_w_up, v_w_down, v_final_norm_g):
    Bl, L, D = x.shape
    T = Bl * L
    depth = w_in.shape[0]
    H = D // HEAD_DIM
    Dc = D + 2 * SSM_GROUPS * SSM_STATE
    Dff = w_up.shape[2] * N_DEV
    dims = (Bl, L, D)

    bf = lambda a: a.astype(BF16)
    shards = [dict(w_in=bf(w_in[l]), w_out=bf(w_out[l]), w_up=bf(w_up[l]), w_down=bf(w_down[l]))
              for l in range(depth)]
    g0 = _all_gather([shards[0][k] for k in ("w_in", "w_out", "w_up", "w_down")] + [conv_w],
                     name="gather_weights_0")
    gathered = dict(zip(("w_in", "w_out", "w_up", "w_down"), g0))
    Cw = g0[4].transpose(1, 2, 0, 3).reshape(depth, CONV_WIDTH, Dc)
    rep = lambda a: jnp.repeat(a, HEAD_DIM, axis=1)
    dt_bias_e, a_log_e, d_skip_e = rep(dt_bias), rep(a_log), rep(d_skip)
    o = [0, D, D + Dc, D + Dc + H, 2 * D + Dc + H, 3 * D + Dc + H, 4 * D + Dc + H]

    def layer_params(g, l):
        W_in = g["w_in"].transpose(1, 0, 2).reshape(D, -1)
        wz, wxbc, wdt, wq, wk, wv = [W_in[:, o[i]:o[i + 1]] for i in range(6)]
        W_pack = jnp.concatenate([wz, wxbc, wq, wk, wv, jnp.repeat(wdt, HEAD_DIM, axis=1)], axis=1)
        W_out = g["w_out"].reshape(2 * D, D)
        W_up = g["w_up"].transpose(1, 0, 2).reshape(D, Dff)
        W_down = g["w_down"].reshape(Dff, D)
        return dict(
            norm_mix_g=norm_mix_g[l], w_in=W_pack, w_in_t=W_pack.T, conv_w=Cw[l], conv_b=conv_b[l],
            dt_bias_e=dt_bias_e[l:l + 1], a_log_e=a_log_e[l:l + 1], d_skip_e=d_skip_e[l:l + 1],
            ssd_norm_g=ssd_norm_g[l], att_norm_g=att_norm_g[l], w_out=W_out, w_out_t=W_out.T,
            norm_mlp_g=norm_mlp_g[l], w_up=W_up, w_up_t=W_up.T, w_down=W_down, w_down_t=W_down.T)

    xa = x.reshape(T, D)
    saved, params = [], []
    for l in range(depth):
        params.append(layer_params(gathered, l))
        xa, s, gathered = _layer_fwd(xa, params[l], dims, l, shards[l + 1] if l + 1 < depth else None)
        saved.append(s)
    dx, g_final, loss_part = _loss_head(xa, final_norm_g, loss_target.reshape(T, D), name="loss_head")

    n_in = w_in.shape[2]

    def pieces_but_w_in(G):
        return dict(
            w_out=bf(G["w_out"].reshape(N_DEV, 2 * D // N_DEV, D)),
            w_up=bf(G["w_up"].reshape(D, N_DEV, Dff // N_DEV).transpose(1, 0, 2)),
            w_down=bf(G["w_down"].reshape(N_DEV, Dff // N_DEV, D)),
            conv_w=bf(G["conv_w"].reshape(CONV_WIDTH, N_DEV, Dc // N_DEV).transpose(1, 0, 2)))

    def piece_w_in(G, l):
        gW = G["w_in"]
        gdt = _collapse_heads(gW[:, 4 * D + Dc:], name=f"collapse_w_dt_{l}")[:, :H]
        gW_in = jnp.concatenate([gW[:, :D + Dc], gdt, gW[:, D + Dc:4 * D + Dc]], axis=1)
        return bf(gW_in.reshape(D, N_DEV, n_in).transpose(1, 0, 2))

    def pieces_of(G, l):
        return dict(pieces_but_w_in(G), w_in=piece_w_in(G, l))

    keys = ("w_in", "w_out", "w_up", "w_down", "conv_w")
    bufs = {k: lax.empty((N_DEV, depth) + w.shape[1:], BF16)
            for k, w in zip(keys, (w_in, w_out, w_up, w_down, conv_w))}
    grads = [None] * depth
    pend = None
    for l in reversed(range(depth)):
        dx, grads[l], bufs = _layer_bwd(dx, saved[l], params[l], dims, l, pend, bufs,
                                        own_pieces=pieces_but_w_in if l == 0 else None)
        pend = pieces_of(grads[l], l) if l > 0 else None
    grad_x = dx.reshape(Bl, L, D)
    (bufs["w_in"],) = _comm_call(_exchange_comm([piece_w_in(grads[0], 0)], [bufs["w_in"]], 0),
                                 name="exchange_grads_0")
    r_in, r_out, r_up, r_down, r_conv = [bufs[k] for k in keys]
    st = lambda k: jnp.stack([grads[l][k] for l in range(depth)])

    def sharded(pieces, w, m, v, name):
        shp = w.shape
        C = shp[-1]
        res = _adamw_sharded(pieces.reshape(N_DEV, -1, C), w.reshape(-1, C), m.reshape(-1, C),
                             v.reshape(-1, C), name=name)
        return [a.reshape(shp) for a in res]

    u_in = sharded(r_in, w_in, m_w_in, v_w_in, "adamw_w_in")
    u_out = sharded(r_out, w_out, m_w_out, v_w_out, "adamw_w_out")
    u_up = sharded(r_up, w_up, m_w_up, v_w_up, "adamw_w_up")
    u_down = sharded(r_down, w_down, m_w_down, v_w_down, "adamw_w_down")
    u_conv = sharded(r_conv, conv_w, m_conv_w, v_conv_w, "adamw_conv_w")

    rows = lambda a: a.reshape(-1, D)
    direct_names = ["norm_mix_g", "ssd_norm_g", "att_norm_g", "norm_mlp_g", "conv_b"]
    direct_g = [rows(st(k)) for k in direct_names] + [g_final]
    exp_names = ["dt_bias_e", "a_log_e", "d_skip_e"]
    exp_g = [rows(st(k)) for k in exp_names]
    n_direct = sum(a.shape[0] for a in direct_g)
    nd = -(-n_direct // 8) * 8
    ne = -(-3 * depth // 8) * 8
    part = jnp.concatenate([_pad_rows(jnp.concatenate(direct_g, axis=0), nd),
                            _pad_rows(jnp.concatenate(exp_g, axis=0), ne),
                            _pad_rows(loss_part, 8)], axis=0)
    (parts,) = _all_gather([part], name="gather_small_grads")

    def direct_pack(ws):
        return _pad_rows(jnp.concatenate([rows(a) for a in ws], axis=0), nd)

    def exp_pack(ws):
        a = jnp.concatenate(ws, axis=0)
        a = jnp.concatenate([a, jnp.zeros((a.shape[0], LANES - H), F32)], axis=1)
        return _pad_rows(a, ne)

    res = _small_update(
        parts,
        direct_pack([norm_mix_g, ssd_norm_g, att_norm_g, norm_mlp_g, conv_b, final_norm_g]),
        direct_pack([m_norm_mix_g, m_ssd_norm_g, m_att_norm_g, m_norm_mlp_g, m_conv_b, m_final_norm_g]),
        direct_pack([v_norm_mix_g, v_ssd_norm_g, v_att_norm_g, v_norm_mlp_g, v_conv_b, v_final_norm_g]),
        exp_pack([dt_bias, a_log, d_skip]), exp_pack([m_dt_bias, m_a_log, m_d_skip]),
        exp_pack([v_dt_bias, v_a_log, v_d_skip]), nd, ne, D, name="small_update")
    loss = res[8][0, 0]

    def unpack(kind):
        d, e = res[kind], res[4 + kind]
        out = {}
        r0 = 0
        for nm, shp in [("norm_mix_g", (depth, D)), ("ssd_norm_g", (depth, D)), ("att_norm_g", (depth, D)),
                        ("norm_mlp_g", (depth, D)), ("conv_b", (depth, Dc)), ("final_norm_g", (D,))]:
            n = 1
            for s_ in shp:
                n *= s_
            out[nm] = d[r0:r0 + n // D].reshape(shp)
            r0 += n // D
        for i, nm in enumerate(["dt_bias", "a_log", "d_skip"]):
            out[nm] = e[i * depth:(i + 1) * depth, :H]
        return out

    names = ["norm_mix_g", "w_in", "conv_w", "conv_b", "dt_bias", "a_log", "d_skip", "ssd_norm_g",
             "att_norm_g", "w_out", "norm_mlp_g", "w_up", "w_down", "final_norm_g"]
    big = {"w_in": u_in, "w_out": u_out, "w_up": u_up, "w_down": u_down, "conv_w": u_conv}
    outs = [loss, grad_x]
    for kind in range(4):
        small = unpack(kind)
        for nm in names:
            outs.append(big[nm][kind] if nm in big else small[nm])
    return tuple(outs)
```
